```python
import math
import jax, jax.numpy as jnp
from jax import lax
import numpy as np

D_MODEL = 1024
BATCH = 4
SEQ = 4096
DEPTH = 2
DEC_BATCH = 128
DEC_SEQ = 1
PAST_LEN = 2048
PAGE_SIZE = 128

N_EVEN = (DEPTH + 1) // 2
N_ODD = DEPTH // 2
A_HEADS = 8
A_HEAD_DIM = 64
A_WIDTH = A_HEADS * A_HEAD_DIM
IDX_HEADS = 4
IDX_DIM = 64
TOPK_MAX = 256
ATTN_QBLOCK = 64
REL_BUCKETS = 32
REL_MAX_DIST = 128
RG_WIDTH = D_MODEL // 2
RG_BLOCKS = 8
RG_BLOCK_DIM = RG_WIDTH // RG_BLOCKS
RG_CONV = 4
RG_C = 8.0
EVEN_SPLITS = (A_WIDTH, A_WIDTH, A_WIDTH, IDX_HEADS * IDX_DIM, IDX_DIM, IDX_HEADS, RG_WIDTH, RG_WIDTH)
EVEN_IN = sum(EVEN_SPLITS)
ML_HEADS = 4
ML_HEAD_DIM = D_MODEL // ML_HEADS
ML_WIDTH = ML_HEADS * ML_HEAD_DIM
ML_CHUNK = 128
ODD_SPLITS = (ML_WIDTH, ML_WIDTH, ML_WIDTH, ML_WIDTH, ML_HEADS, ML_HEADS)
ODD_IN = sum(ODD_SPLITS)
MEM_TOKENS = 256
X_HEADS = 4
X_HEAD_DIM = D_MODEL // X_HEADS
D_FF = 2816
N_EXPERTS = 8
TOP_K = 2
D_FF_EXPERT = 3584
DN_ALPHA = (2.0 * DEPTH) ** 0.25
DN_BETA = (8.0 * DEPTH) ** -0.25
LN_EPS = 1e-5
F32 = jnp.float32

kernel_name = 'hybrid_dsa_rglru_mlstm_decode_step'


def _split(z, sizes):
    offs, acc = [], 0
    for s in sizes[:-1]:
        acc += s
        offs.append(acc)
    return jnp.split(z, offs, axis=-1)


def _layernorm(x, g, b):
    xf = x.astype(F32)
    mu = xf.mean(-1, keepdims=True)
    var = jnp.square(xf - mu).mean(-1, keepdims=True)
    return ((xf - mu) * lax.rsqrt(var + LN_EPS) * g + b).astype(x.dtype)


def _t5_bucket(dist):
    max_exact = REL_BUCKETS // 2
    d = jnp.maximum(dist.astype(F32), 1.0)
    large = max_exact + (jnp.log(d / max_exact) / math.log(REL_MAX_DIST / max_exact)
                         * (REL_BUCKETS - max_exact)).astype(jnp.int32)
    large = jnp.minimum(large, REL_BUCKETS - 1)
    return jnp.where(dist < max_exact, dist, large)


def _dsa_attention(q, k, v, q_idx, k_idx, w_idx, q_pos, rel_bias):
    B, T = q.shape[:2]
    L = k.shape[1]
    topk = min(TOPK_MAX, L // 4)
    qb = math.gcd(T, ATTN_QBLOCK)
    nb = T // qb
    k_pos = jnp.arange(L, dtype=jnp.int32)
    k_idx32 = k_idx.astype(F32)

    def to_blocks(a):
        return jnp.moveaxis(a.reshape((B, nb, qb) + a.shape[2:]), 1, 0)

    def block(args):
        qblk, qiblk, wblk, pblk = args
        s = jnp.einsum('bqhd,bkd->bqhk', qiblk.astype(F32), k_idx32) * IDX_DIM ** -0.5
        score = jnp.einsum('bqh,bqhk->bqk', wblk.astype(F32), jax.nn.relu(s))
        admissible = k_pos[None, None, :] <= pblk[None, :, None]
        score = jnp.where(admissible, score, -jnp.inf)
        _, idx = lax.top_k(score, topk)
        kg = jax.vmap(lambda kk, ii: kk[ii])(k, idx)
        vg = jax.vmap(lambda vv, ii: vv[ii])(v, idx)
        dist = pblk[None, :, None] - idx
        bias = rel_bias[_t5_bucket(jnp.maximum(dist, 0))].astype(F32)
        logits = (jnp.einsum('bqhd,bqkhd->bqhk', qblk.astype(F32), kg.astype(F32)) * A_HEAD_DIM ** -0.5
                  + jnp.swapaxes(bias, -1, -2))
        logits = jnp.where((dist >= 0)[:, :, None, :], logits, -jnp.inf)
        prob = jax.nn.softmax(logits, axis=-1)
        return jnp.einsum('bqhk,bqkhd->bqhd', prob, vg.astype(F32)).astype(q.dtype)

    out = lax.map(block, (to_blocks(q), to_blocks(q_idx), to_blocks(w_idx), q_pos.reshape(nb, qb)))
    return jnp.moveaxis(out, 0, 1).reshape(B, T, A_HEADS, A_HEAD_DIM)


def _rglru(xr, gate, conv_buf, h0, conv_w, conv_b, w_a, b_a, w_i, b_i, lam):
    B, T, W = xr.shape
    xp = jnp.concatenate([conv_buf.astype(xr.dtype), xr], axis=1)
    xc = conv_b.astype(F32)
    for j in range(RG_CONV):
        xc = xc + conv_w[j] * xp[:, j:j + T].astype(F32)
    new_buf = xp[:, T:]
    xb = xc.reshape(B, T, RG_BLOCKS, RG_BLOCK_DIM)
    r = jax.nn.sigmoid(jnp.einsum('btnd,nde->btne', xb, w_a).reshape(B, T, W) + b_a)
    i = jax.nn.sigmoid(jnp.einsum('btnd,nde->btne', xb, w_i).reshape(B, T, W) + b_i)
    log_a = -RG_C * r * jax.nn.softplus(-lam.astype(F32))
    a = jnp.exp(log_a)
    u = jnp.sqrt(-jnp.expm1(2.0 * log_a)) * (i * xc)
    u = u.at[:, 0].add(a[:, 0] * h0.astype(F32))

    def combine(c1, c2):
        a1, b1 = c1
        a2, b2 = c2
        return a1 * a2, a2 * b1 + b2

    _, h = lax.associative_scan(combine, (a, u), axis=1)
    y = h * jax.nn.gelu(gate.astype(F32))
    return y.astype(xr.dtype), h[:, -1].astype(h0.dtype), new_buf.astype(conv_buf.dtype)


def _mlstm(q, k, v, i_pre, logf, c0, n0, m0):
    B, T, H, Dh = q.shape
    cl = math.gcd(T, ML_CHUNK)
    nc = T // cl
    causal = jnp.tril(jnp.ones((cl, cl), dtype=bool))

    def chunks(a):
        return jnp.moveaxis(a.reshape((B, nc, cl) + a.shape[2:]), 1, 0)

    def step(carry, xs):
        c, n, m = carry
        qc, kc, vc, ic, fc = xs
        b = jnp.moveaxis(jnp.cumsum(fc, axis=1), 1, 2)
        ih = jnp.moveaxis(ic, 1, 2)
        dmat = jnp.where(causal, b[..., :, None] - b[..., None, :] + ih[..., None, :], -jnp.inf)
        inter = b + m[..., None]
        m_t = jnp.maximum(inter, dmat.max(-1))
        g_inter = jnp.exp(inter - m_t)
        ws = jnp.exp(dmat - m_t[..., None]) * jnp.einsum('bthd,bshd->bhts', qc, kc)
        num = (g_inter[..., None] * jnp.einsum('bhvk,bthk->bhtv', c, qc)
               + jnp.einsum('bhts,bshv->bhtv', ws, vc))
        den = g_inter * jnp.einsum('bhk,bthk->bht', n, qc) + ws.sum(-1)
        h = num / jnp.maximum(jnp.abs(den), jnp.exp(-m_t))[..., None]
        m_new = m_t[..., -1]
        g_c = jnp.exp(b[..., -1] + m - m_new)
        w_end = jnp.exp(b[..., -1:] - b + ih - m_new[..., None])
        c_new = g_c[..., None, None] * c + jnp.einsum('bhs,bshv,bshk->bhvk', w_end, vc, kc)
        n_new = g_c[..., None] * n + jnp.einsum('bhs,bshk->bhk', w_end, kc)
        return (c_new, n_new, m_new), jnp.moveaxis(h, 1, 2)

    init = (c0.astype(F32), n0.astype(F32), m0.astype(F32))
    (c, n, m), hs = lax.scan(step, init, (chunks(q), chunks(k), chunks(v), chunks(i_pre), chunks(logf)))
    h = jnp.moveaxis(hs, 0, 1).reshape(B, T, H, Dh)
    return h, c.astype(c0.dtype), n.astype(n0.dtype), m.astype(m0.dtype)


def _moe(x, router, wg, wu, wd):
    B, T, D = x.shape
    xt = x.reshape(B * T, D)
    probs = jax.nn.softmax((xt @ router).astype(F32), axis=-1)
    topv, topi = lax.top_k(probs, TOP_K)
    topv = topv / topv.sum(-1, keepdims=True)
    gates = jnp.sum(jax.nn.one_hot(topi, N_EXPERTS, dtype=F32) * topv[..., None], axis=1)
    y = jnp.zeros((B * T, D), F32)
    for e in range(N_EXPERTS):
        h = jax.nn.silu(xt @ wg[e]) * (xt @ wu[e])
        y = y + gates[:, e:e + 1] * (h @ wd[e])
    return y.reshape(B, T, D).astype(x.dtype)


def _trunk(x, q_pos, dsa_past, rg_state, ml_state, mem_kv, p):
    B, T, _ = x.shape
    new_dsa, new_rg, new_ml = [], [], []
    for l in range(DEPTH):
        if l % 2 == 0:
            e = l // 2
            z = x @ p['w_in_even'][e]
            q, k, v, qi, ki, wi, xr, gr = _split(z, EVEN_SPLITS)
            q = q.reshape(B, T, A_HEADS, A_HEAD_DIM)
            k = k.reshape(B, T, A_HEADS, A_HEAD_DIM)
            v = v.reshape(B, T, A_HEADS, A_HEAD_DIM)
            qi = qi.reshape(B, T, IDX_HEADS, IDX_DIM)
            k_past, v_past, ki_past = dsa_past[e]
            k_all = jnp.concatenate([k_past.astype(k.dtype), k], axis=1)
            v_all = jnp.concatenate([v_past.astype(v.dtype), v], axis=1)
            ki_all = jnp.concatenate([ki_past.astype(ki.dtype), ki], axis=1)
            att = _dsa_attention(q, k_all, v_all, qi, ki_all, wi * IDX_HEADS ** -0.5, q_pos, p['rel_bias'])
            h0, buf0 = rg_state[e]
            rg_y, h_last, buf = _rglru(xr, gr, buf0, h0, p['rg_conv_w'][e], p['rg_conv_b'][e],
                                       p['rg_w_a'][e], p['rg_b_a'][e], p['rg_w_i'][e], p['rg_b_i'][e],
                                       p['rg_lambda'][e])
            mix = jnp.concatenate([att.reshape(B, T, A_WIDTH), rg_y], axis=-1) @ p['w_out_even'][e]
            new_dsa.append((k, v, ki))
            new_rg.append((h_last, buf))
        else:
            o = l // 2
            z = x @ p['w_in_odd'][o]
            q, k, v, og, gi, gf = _split(z, ODD_SPLITS)
            shp = (B, T, ML_HEADS, ML_HEAD_DIM)
            bg = p['b_gate_odd'][o].astype(F32)
            i_pre = gi.astype(F32) + bg[:ML_HEADS]
            logf = jax.nn.log_sigmoid(gf.astype(F32) + bg[ML_HEADS:])
            c0, n0, m0 = ml_state[o]
            h, c, n, m = _mlstm(q.reshape(shp).astype(F32), k.reshape(shp).astype(F32) * ML_HEAD_DIM ** -0.5,
                                v.reshape(shp).astype(F32), i_pre, logf, c0, n0, m0)
            mu = h.mean(-1, keepdims=True)
            var = jnp.square(h - mu).mean(-1, keepdims=True)
            hn = ((h - mu) * lax.rsqrt(var + LN_EPS)).reshape(B, T, ML_WIDTH) * p['ml_norm_g'][o]
            mix = (hn * jax.nn.sigmoid(og.astype(F32))).astype(x.dtype) @ p['w_out_odd'][o]
            new_ml.append((c, n, m))
        x = _layernorm(DN_ALPHA * x + mix, p['ln_g'][l, 0], p['ln_b'][l, 0])
        mk, mv = mem_kv[l]
        xq = (x @ p['w_xq'][l]).reshape(B, T, X_HEADS, X_HEAD_DIM)
        lg = jnp.einsum('bthd,bmhd->bhtm', xq.astype(F32), mk.astype(F32)) * X_HEAD_DIM ** -0.5
        pr = jax.nn.softmax(lg, axis=-1)
        xo = jnp.einsum('bhtm,bmhd->bthd', pr, mv.astype(F32)).reshape(B, T, D_MODEL).astype(x.dtype) @ p['w_xo'][l]
        x = _layernorm(DN_ALPHA * x + xo, p['ln_g'][l, 1], p['ln_b'][l, 1])
        if l % 2 == 0:
            e = l // 2
            f = (jax.nn.silu(x @ p['ffn_w_gate'][e]) * (x @ p['ffn_w_up'][e])) @ p['ffn_w_down'][e]
        else:
            o = l // 2
            f = _moe(x, p['moe_router'][o], p['moe_w_gate'][o], p['moe_w_up'][o], p['moe_w_down'][o])
        x = _layernorm(DN_ALPHA * x + f, p['ln_g'][l, 2], p['ln_b'][l, 2])
    return x, new_dsa, new_rg, new_ml


def setup_inputs(seed: int = 0) -> dict:
    key = jax.random.key(seed)
    ks = iter(jax.random.split(key, 64))

    def nrm(shape, scale):
        return jax.random.normal(next(ks), shape, F32) * scale

    n_pages = PAST_LEN // PAGE_SIZE
    n_pool = (DEC_BATCH * n_pages * 5) // 4
    d = D_MODEL
    u = jax.random.uniform(next(ks), (N_EVEN, RG_WIDTH), F32, 0.9, 0.999)
    s = u ** (1.0 / RG_C)
    rg_lambda = jnp.log(s) - jnp.log1p(-s)
    b_gate_odd = jnp.concatenate([nrm((N_ODD, ML_HEADS), 0.1),
                                  3.0 + 3.0 * jax.random.uniform(next(ks), (N_ODD, ML_HEADS), F32)], axis=-1)
    page_table = jax.random.permutation(next(ks), n_pool)[:DEC_BATCH * n_pages].reshape(DEC_BATCH, n_pages).astype(jnp.int32)
    return {
        'x_prompt': nrm((BATCH, SEQ, d), 1.0),
        'x_sample': nrm((DEC_BATCH, DEC_SEQ, d), 1.0),
        'cache_k': nrm((N_EVEN, n_pool, PAGE_SIZE, A_HEADS, A_HEAD_DIM), 1.0),
        'cache_v': nrm((N_EVEN, n_pool, PAGE_SIZE, A_HEADS, A_HEAD_DIM), 1.0),
        'cache_kidx': nrm((N_EVEN, n_pool, PAGE_SIZE, IDX_DIM), 1.0),
        'state_rglru_h': nrm((N_EVEN, DEC_BATCH, RG_WIDTH), 0.5),
        'state_rglru_conv': nrm((N_EVEN, DEC_BATCH, RG_CONV - 1, RG_WIDTH), 1.0),
        'state_mlstm_c': nrm((N_ODD, DEC_BATCH, ML_HEADS, ML_HEAD_DIM, ML_HEAD_DIM), 1.0),
        'state_mlstm_n': nrm((N_ODD, DEC_BATCH, ML_HEADS, ML_HEAD_DIM), 1.0),
        'state_mlstm_m': nrm((N_ODD, DEC_BATCH, ML_HEADS), 1.0),
        'cache_mem_k': nrm((DEPTH, DEC_BATCH, MEM_TOKENS, X_HEADS, X_HEAD_DIM), 1.0),
        'cache_mem_v': nrm((DEPTH, DEC_BATCH, MEM_TOKENS, X_HEADS, X_HEAD_DIM), 1.0),
        'page_table': page_table,
        'mem_prompt': nrm((BATCH, MEM_TOKENS, d), 1.0),
        'ln_g': 1.0 + nrm((DEPTH, 3, d), 0.02),
        'ln_b': nrm((DEPTH, 3, d), 0.02),
        'rel_bias': nrm((REL_BUCKETS, A_HEADS), 0.5),
        'w_in_even': nrm((N_EVEN, d, EVEN_IN), d ** -0.5),
        'w_out_even': nrm((N_EVEN, A_WIDTH + RG_WIDTH, d), DN_BETA * (A_WIDTH + RG_WIDTH) ** -0.5),
        'rg_conv_w': nrm((N_EVEN, RG_CONV, RG_WIDTH), RG_CONV ** -0.5),
        'rg_conv_b': nrm((N_EVEN, RG_WIDTH), 0.02),
        'rg_w_a': nrm((N_EVEN, RG_BLOCKS, RG_BLOCK_DIM, RG_BLOCK_DIM), RG_BLOCK_DIM ** -0.5),
        'rg_b_a': nrm((N_EVEN, RG_WIDTH), 0.1),
        'rg_w_i': nrm((N_EVEN, RG_BLOCKS, RG_BLOCK_DIM, RG_BLOCK_DIM), RG_BLOCK_DIM ** -0.5),
        'rg_b_i': nrm((N_EVEN, RG_WIDTH), 0.1),
        'rg_lambda': rg_lambda,
        'w_in_odd': nrm((N_ODD, d, ODD_IN), d ** -0.5),
        'b_gate_odd': b_gate_odd,
        'ml_norm_g': 1.0 + nrm((N_ODD, ML_WIDTH), 0.02),
        'w_out_odd': nrm((N_ODD, ML_WIDTH, d), DN_BETA * ML_WIDTH ** -0.5),
        'w_xq': nrm((DEPTH, d, d), d ** -0.5),
        'w_xk': nrm((DEPTH, d, d), d ** -0.5),
        'w_xv': nrm((DEPTH, d, d), d ** -0.5),
        'w_xo': nrm((DEPTH, d, d), DN_BETA * d ** -0.5),
        'ffn_w_gate': nrm((N_EVEN, d, D_FF), d ** -0.5),
        'ffn_w_up': nrm((N_EVEN, d, D_FF), d ** -0.5),
        'ffn_w_down': nrm((N_EVEN, D_FF, d), DN_BETA * D_FF ** -0.5),
        'moe_router': nrm((N_ODD, d, N_EXPERTS), d ** -0.5),
        'moe_w_gate': nrm((N_ODD, N_EXPERTS, d, D_FF_EXPERT), d ** -0.5),
        'moe_w_up': nrm((N_ODD, N_EXPERTS, d, D_FF_EXPERT), d ** -0.5),
        'moe_w_down': nrm((N_ODD, N_EXPERTS, D_FF_EXPERT, d), DN_BETA * D_FF_EXPERT ** -0.5),
    }


def reference(x_prompt, x_sample, cache_k, cache_v, cache_kidx, state_rglru_h, state_rglru_conv,
              state_mlstm_c, state_mlstm_n, state_mlstm_m, cache_mem_k, cache_mem_v, page_table, mem_prompt,
              ln_g, ln_b, rel_bias, w_in_even, w_out_even, rg_conv_w, rg_conv_b, rg_w_a, rg_b_a, rg_w_i, rg_b_i,
              rg_lambda, w_in_odd, b_gate_odd, ml_norm_g, w_out_odd, w_xq, w_xk, w_xv, w_xo,
              ffn_w_gate, ffn_w_up, ffn_w_down, moe_router, moe_w_gate, moe_w_up, moe_w_down):
    p = {'ln_g': ln_g, 'ln_b': ln_b, 'rel_bias': rel_bias, 'w_in_even': w_in_even, 'w_out_even': w_out_even,
         'rg_conv_w': rg_conv_w, 'rg_conv_b': rg_conv_b, 'rg_w_a': rg_w_a, 'rg_b_a': rg_b_a,
         'rg_w_i': rg_w_i, 'rg_b_i': rg_b_i, 'rg_lambda': rg_lambda, 'w_in_odd': w_in_odd,
         'b_gate_odd': b_gate_odd, 'ml_norm_g': ml_norm_g, 'w_out_odd': w_out_odd, 'w_xq': w_xq,
         'w_xo': w_xo, 'ffn_w_gate': ffn_w_gate, 'ffn_w_up': ffn_w_up, 'ffn_w_down': ffn_w_down,
         'moe_router': moe_router, 'moe_w_gate': moe_w_gate, 'moe_w_up': moe_w_up, 'moe_w_down': moe_w_down}
    B, T, _ = x_prompt.shape
    Bs, Ts, _ = x_sample.shape
    M = mem_prompt.shape[1]
    past_len = page_table.shape[1] * PAGE_SIZE
    dt = x_prompt.dtype

    dsa_past_p = [(jnp.zeros((B, 0, A_HEADS, A_HEAD_DIM), dt), jnp.zeros((B, 0, A_HEADS, A_HEAD_DIM), dt),
                   jnp.zeros((B, 0, IDX_DIM), dt)) for _ in range(N_EVEN)]
    rg_p = [(jnp.zeros((B, RG_WIDTH), state_rglru_h.dtype),
             jnp.zeros((B, RG_CONV - 1, RG_WIDTH), state_rglru_conv.dtype)) for _ in range(N_EVEN)]
    ml_p = [(jnp.zeros((B, ML_HEADS, ML_HEAD_DIM, ML_HEAD_DIM), state_mlstm_c.dtype),
             jnp.zeros((B, ML_HEADS, ML_HEAD_DIM), state_mlstm_n.dtype),
             jnp.zeros((B, ML_HEADS), state_mlstm_m.dtype)) for _ in range(N_ODD)]
    mem_kv_p = [((mem_prompt @ w_xk[l]).reshape(B, M, X_HEADS, X_HEAD_DIM),
                 (mem_prompt @ w_xv[l]).reshape(B, M, X_HEADS, X_HEAD_DIM)) for l in range(DEPTH)]
    y_prompt, dsa_p, rgs_p, mls_p = _trunk(x_prompt, jnp.arange(T, dtype=jnp.int32),
                                           dsa_past_p, rg_p, ml_p, mem_kv_p, p)

    def pages(pool):
        g = pool[page_table]
        return g.reshape((Bs, past_len) + pool.shape[2:])

    dsa_past_s = [(pages(cache_k[e]), pages(cache_v[e]), pages(cache_kidx[e])) for e in range(N_EVEN)]
    rg_s = [(state_rglru_h[e], state_rglru_conv[e]) for e in range(N_EVEN)]
    ml_s = [(state_mlstm_c[o], state_mlstm_n[o], state_mlstm_m[o]) for o in range(N_ODD)]
    mem_kv_s = [(cache_mem_k[l], cache_mem_v[l]) for l in range(DEPTH)]
    y_sample, dsa_s, rgs_s, mls_s = _trunk(x_sample, past_len + jnp.arange(Ts, dtype=jnp.int32),
                                           dsa_past_s, rg_s, ml_s, mem_kv_s, p)

    k_prompt = jnp.stack([t[0] for t in dsa_p])
    v_prompt = jnp.stack([t[1] for t in dsa_p])
    kidx_prompt = jnp.stack([t[2] for t in dsa_p])
    k_sample = jnp.stack([t[0] for t in dsa_s])
    v_sample = jnp.stack([t[1] for t in dsa_s])
    kidx_sample = jnp.stack([t[2] for t in dsa_s])
    rg_h_prompt = jnp.stack([t[0] for t in rgs_p])
    rg_conv_prompt = jnp.stack([t[1] for t in rgs_p])
    rg_h_sample = jnp.stack([t[0] for t in rgs_s])
    rg_conv_sample = jnp.stack([t[1] for t in rgs_s])
    ml_c_prompt = jnp.stack([t[0] for t in mls_p])
    ml_n_prompt = jnp.stack([t[1] for t in mls_p])
    ml_m_prompt = jnp.stack([t[2] for t in mls_p])
    ml_c_sample = jnp.stack([t[0] for t in mls_s])
    ml_n_sample = jnp.stack([t[1] for t in mls_s])
    ml_m_sample = jnp.stack([t[2] for t in mls_s])
    mem_k_prompt = jnp.stack([t[0] for t in mem_kv_p])
    mem_v_prompt = jnp.stack([t[1] for t in mem_kv_p])
    return (y_prompt, y_sample, k_prompt, v_prompt, kidx_prompt, k_sample, v_sample, kidx_sample,
            rg_h_prompt, rg_conv_prompt, rg_h_sample, rg_conv_sample,
            ml_c_prompt, ml_n_prompt, ml_m_prompt, ml_c_sample, ml_n_sample, ml_m_sample,
            mem_k_prompt, mem_v_prompt)
```

```python
import functools
import math

import jax
import jax.numpy as jnp
from jax import lax
from jax.experimental import pallas as pl
from jax.experimental.pallas import tpu as pltpu

F32 = jnp.float32
BF16 = jnp.bfloat16
I32 = jnp.int32

PAGE_SIZE = 128
A_HEADS = 8
A_HEAD_DIM = 64
A_WIDTH = A_HEADS * A_HEAD_DIM
IDX_HEADS = 4
IDX_DIM = 64
TOPK_MAX = 256
REL_BUCKETS = 32
REL_MAX_DIST = 128
RG_BLOCKS = 8
RG_CONV = 4
RG_C = 8.0
ML_HEADS = 4
ML_CHUNK = 128
X_HEADS = 4
TOP_K = 2
LN_EPS = 1e-5

LANES = 128
SUBLANES = 8
VMEM_LIMIT = 48 * 1024 * 1024
NEG = -1e30
INT_MIN = -2 ** 31
TQ = 128

NT = (((1,), (1,)), ((), ()))
TN = (((0,), (0,)), ((), ()))
HI = lax.Precision.HIGHEST


def _cp(*sem):
    return pltpu.CompilerParams(dimension_semantics=sem, vmem_limit_bytes=VMEM_LIMIT)


def _full(shape):
    n = len(shape)
    return pl.BlockSpec(shape, lambda *_: (0,) * n)


def _row_tile(m, cap):
    t = min(m, cap)
    assert m % t == 0
    return t


def _proj_kernel(x_ref, *refs, n_w, out_counts):
    w_refs, o_refs = refs[:n_w], refs[n_w:]
    x = x_ref[...]
    k = 0
    for w_ref, cnt in zip(w_refs, out_counts):
        y = jnp.dot(x, w_ref[...], preferred_element_type=F32)
        for _ in range(cnt):
            o_refs[k][...] = y.astype(o_refs[k].dtype)
            k += 1


def multi_proj(x, ws, out_dtypes, tm_cap=512):
    m, kdim = x.shape
    tm = _row_tile(m, tm_cap)
    in_specs = [pl.BlockSpec((tm, kdim), lambda i: (i, 0))] + [_full(w.shape) for w in ws]
    out_shape, out_specs = [], []
    for w, dts in zip(ws, out_dtypes):
        for dt in dts:
            out_shape.append(jax.ShapeDtypeStruct((m, w.shape[1]), dt))
            out_specs.append(pl.BlockSpec((tm, w.shape[1]), lambda i: (i, 0)))
    kern = functools.partial(_proj_kernel, n_w=len(ws), out_counts=tuple(len(d) for d in out_dtypes))
    return pl.pallas_call(kern, grid=(m // tm,), in_specs=in_specs, out_specs=out_specs,
                          out_shape=out_shape, compiler_params=_cp("parallel"), name="multi_proj")(x, *ws)


def _layernorm_rows(z, g, b):
    mu = jnp.mean(z, axis=-1, keepdims=True)
    zc = z - mu
    var = jnp.mean(zc * zc, axis=-1, keepdims=True)
    return zc * lax.rsqrt(var + LN_EPS) * g + b


def _mm_ln_kernel(*refs, n_a, alpha):
    a_refs, w_refs = refs[:n_a], refs[n_a:2 * n_a]
    r_ref, g_ref, b_ref, o_ref, obf_ref = refs[2 * n_a:]
    y = alpha * r_ref[...]
    for a_ref, w_ref in zip(a_refs, w_refs):
        y = y + jnp.dot(a_ref[...], w_ref[...], preferred_element_type=F32)
    out = _layernorm_rows(y, g_ref[...], b_ref[...])
    o_ref[...] = out
    obf_ref[...] = out.astype(BF16)


def mm_res_ln(a_list, w_list, resid, g, b, alpha, tm_cap=512):
    m, d = resid.shape
    tm = _row_tile(m, tm_cap)
    in_specs = ([pl.BlockSpec((tm, a.shape[1]), lambda i: (i, 0)) for a in a_list]
                + [_full(w.shape) for w in w_list]
                + [pl.BlockSpec((tm, d), lambda i: (i, 0)), _full((1, d)), _full((1, d))])
    out_specs = [pl.BlockSpec((tm, d), lambda i: (i, 0))] * 2
    out_shape = [jax.ShapeDtypeStruct((m, d), F32), jax.ShapeDtypeStruct((m, d), BF16)]
    kern = functools.partial(_mm_ln_kernel, n_a=len(a_list), alpha=alpha)
    return pl.pallas_call(kern, grid=(m // tm,), in_specs=in_specs, out_specs=out_specs, out_shape=out_shape,
                          compiler_params=_cp("parallel"), name="mm_res_ln")(
                              *a_list, *w_list, resid, g.reshape(1, d), b.reshape(1, d))


def _ffn_kernel(x_ref, r_ref, wg_ref, wu_ref, wd_ref, g_ref, b_ref, o_ref, obf_ref, acc_sc, *, alpha):
    f = pl.program_id(1)

    @pl.when(f == 0)
    def _():
        acc_sc[...] = alpha * r_ref[...]

    x = x_ref[...]
    hg = jnp.dot(x, wg_ref[...], preferred_element_type=F32)
    hu = jnp.dot(x, wu_ref[...], preferred_element_type=F32)
    h = (jax.nn.silu(hg) * hu).astype(BF16)
    acc_sc[...] += jnp.dot(h, wd_ref[...], preferred_element_type=F32)

    @pl.when(f == pl.num_programs(1) - 1)
    def _():
        out = _layernorm_rows(acc_sc[...], g_ref[...], b_ref[...])
        o_ref[...] = out
        obf_ref[...] = out.astype(BF16)


def ffn_ln(x_bf, resid, wg, wu, wd, g, b, alpha, tm_cap=512, tf=256):
    m, d = resid.shape
    dff = wg.shape[1]
    tm = _row_tile(m, tm_cap)
    assert dff % tf == 0
    in_specs = [pl.BlockSpec((tm, d), lambda i, f: (i, 0)), pl.BlockSpec((tm, d), lambda i, f: (i, 0)),
                pl.BlockSpec((d, tf), lambda i, f: (0, f)), pl.BlockSpec((d, tf), lambda i, f: (0, f)),
                pl.BlockSpec((tf, d), lambda i, f: (f, 0)), _full((1, d)), _full((1, d))]
    out_specs = [pl.BlockSpec((tm, d), lambda i, f: (i, 0))] * 2
    out_shape = [jax.ShapeDtypeStruct((m, d), F32), jax.ShapeDtypeStruct((m, d), BF16)]
    return pl.pallas_call(functools.partial(_ffn_kernel, alpha=alpha), grid=(m // tm, dff // tf),
                          in_specs=in_specs, out_specs=out_specs, out_shape=out_shape,
                          scratch_shapes=[pltpu.VMEM((tm, d), F32)],
                          compiler_params=_cp("parallel", "arbitrary"), name="ffn_ln")(
                              x_bf, resid, wg, wu, wd, g.reshape(1, d), b.reshape(1, d))


def _router_kernel(x_ref, w_ref, o_ref, *, n_experts):
    logits = jnp.dot(x_ref[...], w_ref[...], preferred_element_type=F32, precision=HI)
    lane = lax.broadcasted_iota(I32, logits.shape, 1)
    valid = lane < n_experts
    logits = jnp.where(valid, logits, NEG)
    mx = jnp.max(logits, axis=-1, keepdims=True)
    e = jnp.where(valid, jnp.exp(logits - mx), 0.0)
    p = e / jnp.sum(e, axis=-1, keepdims=True)
    p = jnp.where(valid, p, -1.0)
    big = jnp.int32(LANES)
    m1 = jnp.max(p, axis=-1, keepdims=True)
    i1 = jnp.min(jnp.where(p == m1, lane, big), axis=-1, keepdims=True)
    p2 = jnp.where(lane == i1, -1.0, p)
    m2 = jnp.max(p2, axis=-1, keepdims=True)
    i2 = jnp.min(jnp.where(p2 == m2, lane, big), axis=-1, keepdims=True)
    tot = m1 + m2
    o_ref[...] = jnp.where(lane == i1, m1 / tot, jnp.where(lane == i2, m2 / tot, 0.0))


def router_gates(x, w_pad, n_experts, tm_cap=512):
    m, d = x.shape
    tm = _row_tile(m, tm_cap)
    return pl.pallas_call(functools.partial(_router_kernel, n_experts=n_experts), grid=(m // tm,),
                          in_specs=[pl.BlockSpec((tm, d), lambda i: (i, 0)), _full(w_pad.shape)],
                          out_specs=pl.BlockSpec((tm, LANES), lambda i: (i, 0)),
                          out_shape=jax.ShapeDtypeStruct((m, LANES), F32),
                          compiler_params=_cp("parallel"), name="router")(x, w_pad)


def _moe_kernel(x_ref, r_ref, gt_ref, wg_ref, wu_ref, wd_ref, g_ref, b_ref, o_ref, obf_ref, acc_sc, *, alpha):
    e = pl.program_id(1)
    f = pl.program_id(2)

    @pl.when((e == 0) & (f == 0))
    def _():
        acc_sc[...] = alpha * r_ref[...]

    gt = gt_ref[...]
    lane = lax.broadcasted_iota(I32, gt.shape, 1)
    ge = jnp.sum(jnp.where(lane == e, gt, 0.0), axis=-1, keepdims=True)
    x = x_ref[...]
    hg = jnp.dot(x, wg_ref[0], preferred_element_type=F32)
    hu = jnp.dot(x, wu_ref[0], preferred_element_type=F32)
    h = (jax.nn.silu(hg) * hu * ge).astype(BF16)
    acc_sc[...] += jnp.dot(h, wd_ref[0], preferred_element_type=F32)

    @pl.when((e == pl.num_programs(1) - 1) & (f == pl.num_programs(2) - 1))
    def _():
        out = _layernorm_rows(acc_sc[...], g_ref[...], b_ref[...])
        o_ref[...] = out
        obf_ref[...] = out.astype(BF16)


def moe_ln(x_bf, resid, gates, wg, wu, wd, g, b, alpha, tm_cap=512, tf=512):
    m, d = resid.shape
    ne, _, dff = wg.shape
    tm = _row_tile(m, tm_cap)
    assert dff % tf == 0
    in_specs = [pl.BlockSpec((tm, d), lambda i, e, f: (i, 0)), pl.BlockSpec((tm, d), lambda i, e, f: (i, 0)),
                pl.BlockSpec((tm, LANES), lambda i, e, f: (i, 0)),
                pl.BlockSpec((1, d, tf), lambda i, e, f: (e, 0, f)), pl.BlockSpec((1, d, tf), lambda i, e, f: (e, 0, f)),
                pl.BlockSpec((1, tf, d), lambda i, e, f: (e, f, 0)), _full((1, d)), _full((1, d))]
    out_specs = [pl.BlockSpec((tm, d), lambda i, e, f: (i, 0))] * 2
    out_shape = [jax.ShapeDtypeStruct((m, d), F32), jax.ShapeDtypeStruct((m, d), BF16)]
    return pl.pallas_call(functools.partial(_moe_kernel, alpha=alpha), grid=(m // tm, ne, dff // tf),
                          in_specs=in_specs, out_specs=out_specs, out_shape=out_shape,
                          scratch_shapes=[pltpu.VMEM((tm, d), F32)],
                          compiler_params=_cp("parallel", "arbitrary", "arbitrary"), name="moe_ln")(
                              x_bf, resid, gates, wg, wu, wd, g.reshape(1, d), b.reshape(1, d))


def _sortable_key(score):
    bits = pltpu.bitcast(score, I32)
    return jnp.where(bits < 0, bits ^ jnp.int32(0x7FFFFFFF), bits)


def _topk_threshold(count, topk, n_idx_bits, rows):
    kk = jnp.int32(topk)
    c0 = count(lambda key, idx: jnp.where(key >= 0, 1, 0))
    t = jnp.where(c0 >= kk, jnp.int32(0), jnp.int32(INT_MIN)) + jnp.zeros((rows, 1), I32)

    def bit_body(j, t):
        cand = t | (jnp.int32(1) << (30 - j))
        cnt = count(lambda key, idx: jnp.where(key >= cand, 1, 0))
        return jnp.where(cnt >= kk, cand, t)

    t = lax.fori_loop(0, 31, bit_body, t)
    need = kk - count(lambda key, idx: jnp.where(key > t, 1, 0))

    def tie_body(j, p):
        cand = p | (jnp.int32(1) << (n_idx_bits - 1 - j))
        got = count(lambda key, idx: jnp.where(key == t, jnp.where(idx < cand, 1, 0), 0))
        return jnp.where(got <= need, cand, p)

    p = lax.fori_loop(0, n_idx_bits, tie_body, jnp.zeros((rows, 1), I32))
    p = jnp.where(t == INT_MIN, 0, p)
    return t, p


def _dsa_prompt_kernel(q_ref, k_ref, v_ref, qi_ref, ki_ref, wi_ref, bias_ref, o_ref,
                       key_sc, m_sc, l_sc, acc_sc, *, topk, n_idx_bits):
    i = pl.program_id(1)
    nk = i + 1
    row = lax.broadcasted_iota(I32, (TQ, TQ), 0)
    col = lax.broadcasted_iota(I32, (TQ, TQ), 1)
    qi = qi_ref[0]
    w = wi_ref[0] * (IDX_HEADS ** -0.5)

    def score_body(c, carry):
        off = pl.multiple_of(c * TQ, TQ)
        kc = ki_ref[0, pl.ds(off, TQ), :]
        sc = jnp.zeros((TQ, TQ), F32)
        for h in range(IDX_HEADS):
            s = lax.dot_general(qi[:, h * IDX_DIM:(h + 1) * IDX_DIM], kc, NT, preferred_element_type=F32)
            sc = sc + w[:, h:h + 1] * jnp.maximum(s * (IDX_DIM ** -0.5), 0.0)
        key = _sortable_key(sc)
        key_sc[c] = jnp.where(off + col <= i * TQ + row, key, INT_MIN)
        return carry

    lax.fori_loop(0, nk, score_body, 0)

    def count(pred):
        def body(c, acc):
            return acc + pred(key_sc[c], c * TQ + col)
        acc = lax.fori_loop(0, nk, body, jnp.zeros((TQ, TQ), I32))
        return jnp.sum(acc, axis=1, keepdims=True)

    t, p = _topk_threshold(count, topk, n_idx_bits, TQ)

    m_sc[...] = jnp.full(m_sc.shape, NEG, F32)
    l_sc[...] = jnp.zeros(l_sc.shape, F32)
    acc_sc[...] = jnp.zeros(acc_sc.shape, F32)
    q = q_ref[0]

    def att_body(c, carry):
        off = pl.multiple_of(c * TQ, TQ)
        key = key_sc[c]
        sel = jnp.where(key > t, 1, jnp.where(key == t, jnp.where(off + col < p, 1, 0), 0)) > 0
        kc = k_ref[0, pl.ds(off, TQ), :]
        vc = v_ref[0, pl.ds(off, TQ), :]
        bsel = jnp.minimum(i - c, 2)
        for h in range(A_HEADS):
            sl = slice(h * A_HEAD_DIM, (h + 1) * A_HEAD_DIM)
            s = lax.dot_general(q[:, sl], kc[:, sl], NT, preferred_element_type=F32)
            s = jnp.where(sel, s * (A_HEAD_DIM ** -0.5) + bias_ref[bsel, h], NEG)
            m_prev = m_sc[h]
            m_new = jnp.maximum(m_prev, jnp.max(s, axis=1, keepdims=True))
            alpha = jnp.exp(m_prev - m_new)
            pe = jnp.exp(s - m_new)
            l_sc[h] = alpha * l_sc[h] + jnp.sum(pe, axis=1, keepdims=True)
            acc_sc[h] = alpha[:, :A_HEAD_DIM] * acc_sc[h] + jnp.dot(pe.astype(BF16), vc[:, sl],
                                                                     preferred_element_type=F32)
            m_sc[h] = m_new
        return carry

    lax.fori_loop(0, nk, att_body, 0)
    for h in range(A_HEADS):
        o_ref[0, :, h * A_HEAD_DIM:(h + 1) * A_HEAD_DIM] = (acc_sc[h] / l_sc[h][:, :A_HEAD_DIM]).astype(o_ref.dtype)


def dsa_prompt(q, k, v, qi, ki, wi, bias_tab):
    bsz, t, _ = q.shape
    assert t % TQ == 0
    topk = min(TOPK_MAX, t // 4)
    nq = t // TQ
    kern = functools.partial(_dsa_prompt_kernel, topk=topk, n_idx_bits=t.bit_length())
    per_q = lambda w: pl.BlockSpec((1, TQ, w), lambda b, i: (b, i, 0))
    per_b = lambda w: pl.BlockSpec((1, t, w), lambda b, i: (b, 0, 0))
    return pl.pallas_call(
        kern, grid=(bsz, nq),
        in_specs=[per_q(A_WIDTH), per_b(A_WIDTH), per_b(A_WIDTH), per_q(IDX_HEADS * IDX_DIM), per_b(IDX_DIM),
                  per_q(LANES), _full(bias_tab.shape)],
        out_specs=per_q(A_WIDTH),
        out_shape=jax.ShapeDtypeStruct((bsz, t, A_WIDTH), BF16),
        scratch_shapes=[pltpu.VMEM((nq, TQ, TQ), I32), pltpu.VMEM((A_HEADS, TQ, LANES), F32),
                        pltpu.VMEM((A_HEADS, TQ, LANES), F32), pltpu.VMEM((A_HEADS, TQ, A_HEAD_DIM), F32)],
        compiler_params=_cp("parallel", "arbitrary"), name="dsa_prompt")(q, k, v, qi, ki, wi, bias_tab)


def _dsa_s_score_kernel(pt_ref, qi_ref, *refs, n_pages):
    page_refs, o_ref = refs[:n_pages], refs[n_pages]
    qi = qi_ref[0]
    for p in range(n_pages):
        o_ref[0, :, p * PAGE_SIZE:(p + 1) * PAGE_SIZE] = lax.dot_general(
            qi, page_refs[p][0], NT, preferred_element_type=F32, precision=HI)


def dsa_sample_scores(page_table, qi8, kidx_pool):
    s, n_pages = page_table.shape
    pt = page_table.reshape(-1)
    page_spec = lambda p: pl.BlockSpec((1, PAGE_SIZE, IDX_DIM), lambda i, pt_ref: (pt_ref[i * n_pages + p], 0, 0))
    grid_spec = pltpu.PrefetchScalarGridSpec(
        num_scalar_prefetch=1, grid=(s,),
        in_specs=[pl.BlockSpec((1, SUBLANES, IDX_DIM), lambda i, pt_ref: (i, 0, 0))] + [page_spec(p) for p in range(n_pages)],
        out_specs=pl.BlockSpec((1, SUBLANES, n_pages * PAGE_SIZE), lambda i, pt_ref: (i, 0, 0)))
    return pl.pallas_call(functools.partial(_dsa_s_score_kernel, n_pages=n_pages), grid_spec=grid_spec,
                          out_shape=jax.ShapeDtypeStruct((s, SUBLANES, n_pages * PAGE_SIZE), F32),
                          compiler_params=_cp("arbitrary"), name="dsa_sample_scores")(pt, qi8, *([kidx_pool] * n_pages))


def _dsa_s_select_kernel(raw_ref, qi_ref, kn_ref, wi_ref, mask_ref, mnew_ref, key_sc, *, topk, n_idx_bits):
    s, lp = mask_ref.shape
    w = wi_ref[...] * (IDX_HEADS ** -0.5)
    sc = jnp.zeros((s, lp), F32)
    sc_new = jnp.zeros((s, 1), F32)
    kn = kn_ref[...]
    for h in range(IDX_HEADS):
        sc = sc + w[:, h:h + 1] * jnp.maximum(raw_ref[:, h, :] * (IDX_DIM ** -0.5), 0.0)
        s_new = jnp.sum(qi_ref[:, h * IDX_DIM:(h + 1) * IDX_DIM] * kn, axis=-1, keepdims=True)
        sc_new = sc_new + w[:, h:h + 1] * jnp.maximum(s_new * (IDX_DIM ** -0.5), 0.0)
    key_sc[...] = _sortable_key(sc)
    key_new = _sortable_key(sc_new)
    col = lax.broadcasted_iota(I32, (s, lp), 1)

    def count(pred):
        main = jnp.sum(pred(key_sc[...], col), axis=1, keepdims=True)
        return main + pred(key_new, jnp.int32(lp))

    t, p = _topk_threshold(count, topk, n_idx_bits, s)
    key = key_sc[...]
    mask_ref[...] = jnp.where(key > t, 1.0, jnp.where(key == t, jnp.where(col < p, 1.0, 0.0), 0.0))
    sel_new = jnp.where(key_new > t, 1.0, jnp.where(key_new == t, jnp.where(lp < p, 1.0, 0.0), 0.0))
    mnew_ref[...] = jnp.broadcast_to(sel_new, mnew_ref.shape)


def dsa_sample_select(raw, qi, ki_new, wi):
    s, _, lp = raw.shape
    length = lp + 1
    topk = min(TOPK_MAX, length // 4)
    kern = functools.partial(_dsa_s_select_kernel, topk=topk, n_idx_bits=length.bit_length())
    return pl.pallas_call(kern, out_shape=[jax.ShapeDtypeStruct((s, lp), F32), jax.ShapeDtypeStruct((s, LANES), F32)],
                          scratch_shapes=[pltpu.VMEM((s, lp), I32)],
                          compiler_params=pltpu.CompilerParams(vmem_limit_bytes=VMEM_LIMIT),
                          name="dsa_sample_select")(raw, qi, ki_new, wi)


def _dot_split(x, w_bf):
    hi = x.astype(BF16)
    lo = (x - hi.astype(F32)).astype(BF16)
    return (jnp.dot(hi, w_bf, preferred_element_type=F32) + jnp.dot(lo, w_bf, preferred_element_type=F32))


def _dsa_s_attn_kernel(pt_ref, q_ref, kn_ref, vn_ref, mask_ref, mnew_ref, bias_ref, bnew_ref, ss_ref, *refs, n_pages):
    k_refs, v_refs, o_ref = refs[:n_pages], refs[n_pages:2 * n_pages], refs[2 * n_pages]
    q = q_ref[0]
    ss = ss_ref[...]
    row = lax.broadcasted_iota(I32, (PAGE_SIZE, PAGE_SIZE), 0)
    col = lax.broadcasted_iota(I32, (PAGE_SIZE, PAGE_SIZE), 1)
    scale = A_HEAD_DIM ** -0.5
    lg_new = _dot_split(jnp.broadcast_to(kn_ref[0] * q, (SUBLANES, A_WIDTH)), ss)[0:1] * scale + bnew_ref[...]
    lg_new = jnp.where(mnew_ref[0][:, 0:1] > 0.0, lg_new, NEG)
    m = lg_new
    l = jnp.ones((1, A_WIDTH), F32)
    acc = vn_ref[0]
    for p in range(n_pages):
        lg = _dot_split(k_refs[p][0] * q, ss) * scale + bias_ref[p]
        mrow = mask_ref[0, p:p + 1, :]
        mcol = jnp.sum(jnp.where(row == col, jnp.broadcast_to(mrow, (PAGE_SIZE, PAGE_SIZE)), 0.0),
                       axis=1, keepdims=True)
        lg = jnp.where(mcol > 0.0, lg, NEG)
        m_new = jnp.maximum(m, jnp.max(lg, axis=0, keepdims=True))
        alpha = jnp.exp(m - m_new)
        pe = jnp.exp(lg - m_new)
        l = alpha * l + jnp.sum(pe, axis=0, keepdims=True)
        acc = alpha * acc + jnp.sum(pe * v_refs[p][0], axis=0, keepdims=True)
        m = m_new
    o_ref[0] = (acc / l).astype(o_ref.dtype)


def dsa_sample_attn(page_table, q, k_new, v_new, mask, mask_new, bias_exp, bias_new, k_pool, v_pool):
    s, n_pages = page_table.shape
    pt = page_table.reshape(-1)
    ss = (lax.broadcasted_iota(I32, (A_WIDTH, A_WIDTH), 0) // A_HEAD_DIM
          == lax.broadcasted_iota(I32, (A_WIDTH, A_WIDTH), 1) // A_HEAD_DIM).astype(BF16)
    per_s = lambda r, w: pl.BlockSpec((1, r, w), lambda i, pt_ref: (i, 0, 0))
    const = lambda shape: pl.BlockSpec(shape, lambda i, pt_ref: (0,) * len(shape))
    page_spec = lambda p: pl.BlockSpec((1, PAGE_SIZE, A_WIDTH), lambda i, pt_ref: (pt_ref[i * n_pages + p], 0, 0))
    grid_spec = pltpu.PrefetchScalarGridSpec(
        num_scalar_prefetch=1, grid=(s,),
        in_specs=[per_s(1, A_WIDTH), per_s(1, A_WIDTH), per_s(1, A_WIDTH), per_s(n_pages, PAGE_SIZE), per_s(1, LANES),
                  const(bias_exp.shape), const(bias_new.shape), const(ss.shape)]
                 + [page_spec(p) for p in range(n_pages)] * 2,
        out_specs=per_s(1, A_WIDTH))
    return pl.pallas_call(functools.partial(_dsa_s_attn_kernel, n_pages=n_pages), grid_spec=grid_spec,
                          out_shape=jax.ShapeDtypeStruct((s, 1, A_WIDTH), BF16),
                          compiler_params=_cp("arbitrary"), name="dsa_sample_attn")(
                              pt, q, k_new, v_new, mask, mask_new, bias_exp, bias_new, ss,
                              *([k_pool] * n_pages), *([v_pool] * n_pages))


def _rglru_gates(xc, wa, ba, wi, bi, lam):
    xb = xc.astype(BF16)
    r = jax.nn.sigmoid(jnp.dot(xb, wa, preferred_element_type=F32) + ba)
    ig = jax.nn.sigmoid(jnp.dot(xb, wi, preferred_element_type=F32) + bi)
    log_a = -RG_C * r * jax.nn.softplus(-lam)
    a = jnp.exp(log_a)
    u = jnp.sqrt(1.0 - jnp.exp(2.0 * log_a)) * (ig * xc)
    return a, u


def _rglru_prompt_kernel(xr_ref, gr_ref, cw_ref, cb_ref, wa_ref, ba_ref, wi_ref, bi_ref, lam_ref,
                         y_ref, hl_ref, xs_sc, h_sc, *, tb):
    t = pl.program_id(1)
    pad = SUBLANES

    @pl.when(t == 0)
    def _():
        xs_sc[0:pad, :] = jnp.zeros((pad, xs_sc.shape[1]), F32)
        h_sc[...] = jnp.zeros(h_sc.shape, F32)

    xs_sc[pad:pad + tb, :] = xr_ref[0]
    xc = cb_ref[...] + cw_ref[RG_CONV - 1:RG_CONV, :] * xs_sc[pad:pad + tb, :]
    for j in range(1, RG_CONV):
        xc = xc + cw_ref[RG_CONV - 1 - j:RG_CONV - j, :] * xs_sc[pad - j:pad - j + tb, :]
    a, u = _rglru_gates(xc, wa_ref[...], ba_ref[...], wi_ref[...], bi_ref[...], lam_ref[...])
    rowi = lax.broadcasted_iota(I32, a.shape, 0)
    s = 1
    while s < tb:
        keep = rowi >= s
        a_sh = jnp.where(keep, pltpu.roll(a, s, 0), 1.0)
        u_sh = jnp.where(keep, pltpu.roll(u, s, 0), 0.0)
        u = a * u_sh + u
        a = a * a_sh
        s *= 2
    h = a * h_sc[0:1, :] + u
    y_ref[0] = (h * jax.nn.gelu(gr_ref[0])).astype(y_ref.dtype)
    h_sc[0:1, :] = h[tb - 1:tb, :]
    xs_sc[0:pad, :] = xs_sc[tb:tb + pad, :]

    @pl.when(t == pl.num_programs(1) - 1)
    def _():
        hl_ref[0] = h[tb - 1:tb, :]


def rglru_prompt(xr, gr, cw, cb, wa, ba, wi, bi, lam, tb_cap=256):
    bsz, t, w = xr.shape
    tb = _row_tile(t, tb_cap)
    blk = pl.BlockSpec((1, tb, w), lambda b, i: (b, i, 0))
    vec = _full((1, w))
    return pl.pallas_call(
        functools.partial(_rglru_prompt_kernel, tb=tb), grid=(bsz, t // tb),
        in_specs=[blk, blk, _full((RG_CONV, w)), vec, _full((w, w)), vec, _full((w, w)), vec, vec],
        out_specs=[blk, pl.BlockSpec((1, 1, w), lambda b, i: (b, 0, 0))],
        out_shape=[jax.ShapeDtypeStruct((bsz, t, w), BF16), jax.ShapeDtypeStruct((bsz, 1, w), F32)],
        scratch_shapes=[pltpu.VMEM((tb + SUBLANES, w), F32), pltpu.VMEM((SUBLANES, w), F32)],
        compiler_params=_cp("parallel", "arbitrary"), name="rglru_prompt")(
            xr, gr, cw, cb.reshape(1, w), wa, ba.reshape(1, w), wi, bi.reshape(1, w), lam.reshape(1, w))


def _rglru_step_kernel(xr_ref, gr_ref, b0_ref, b1_ref, b2_ref, h0_ref, cw_ref, cb_ref, wa_ref, ba_ref, wi_ref,
                       bi_ref, lam_ref, y_ref, h_ref):
    xc = (cb_ref[...] + cw_ref[0:1, :] * b0_ref[...] + cw_ref[1:2, :] * b1_ref[...]
          + cw_ref[2:3, :] * b2_ref[...] + cw_ref[3:4, :] * xr_ref[...])
    a, u = _rglru_gates(xc, wa_ref[...], ba_ref[...], wi_ref[...], bi_ref[...], lam_ref[...])
    h = a * h0_ref[...] + u
    h_ref[...] = h
    y_ref[...] = (h * jax.nn.gelu(gr_ref[...])).astype(y_ref.dtype)


def rglru_step(xr, gr, buf, h0, cw, cb, wa, ba, wi, bi, lam):
    s, w = xr.shape
    return pl.pallas_call(
        _rglru_step_kernel,
        out_shape=[jax.ShapeDtypeStruct((s, w), BF16), jax.ShapeDtypeStruct((s, w), F32)],
        compiler_params=pltpu.CompilerParams(vmem_limit_bytes=VMEM_LIMIT), name="rglru_step")(
            xr, gr, buf[:, 0], buf[:, 1], buf[:, 2], h0, cw, cb.reshape(1, w), wa, ba.reshape(1, w),
            wi, bi.reshape(1, w), lam.reshape(1, w))


def _headnorm_gate(h, g, og):
    mu = jnp.mean(h, axis=-1, keepdims=True)
    hc = h - mu
    var = jnp.mean(hc * hc, axis=-1, keepdims=True)
    return hc * lax.rsqrt(var + LN_EPS) * g * jax.nn.sigmoid(og)


def _mlstm_prompt_kernel(q_ref, k_ref, v_ref, og_ref, gt_ref, bg_ref, ng_ref, tri_ref,
                         y_ref, c_ref, n_ref, m_ref, c_sc, n_sc, m_sc, *, cl, dh):
    ci = pl.program_id(1)

    @pl.when(ci == 0)
    def _():
        c_sc[...] = jnp.zeros(c_sc.shape, F32)
        n_sc[...] = jnp.zeros(n_sc.shape, F32)
        m_sc[...] = jnp.zeros(m_sc.shape, F32)

    pre = gt_ref[0] + bg_ref[...]
    lane = lax.broadcasted_iota(I32, pre.shape, 1)
    gv = jnp.where(lane < ML_HEADS, pre, jax.nn.log_sigmoid(pre))
    csum = jnp.dot(tri_ref[...], gv, preferred_element_type=F32, precision=HI)
    gv_t = gv.T
    csum_t = csum.T
    row = lax.broadcasted_iota(I32, (cl, cl), 0)
    col = lax.broadcasted_iota(I32, (cl, cl), 1)
    causal = col <= row
    kscale = dh ** -0.5
    for h in range(ML_HEADS):
        sl = slice(h * dh, (h + 1) * dh)
        qh = q_ref[0, :, sl]
        kh = k_ref[0, :, sl] * jnp.asarray(kscale, BF16)
        vh = v_ref[0, :, sl]
        b_col = csum[:, ML_HEADS + h:ML_HEADS + h + 1]
        i_col = gv[:, h:h + 1]
        b_row = csum_t[ML_HEADS + h:ML_HEADS + h + 1, :]
        i_row = gv_t[h:h + 1, :]
        m_prev = m_sc[h:h + 1, 0:1]
        c_prev = c_sc[h]
        n_prev = n_sc[h:h + 1, :]
        dmat = jnp.where(causal, b_col - b_row + i_row, NEG)
        inter = b_col + m_prev
        m_t = jnp.maximum(inter, jnp.max(dmat, axis=1, keepdims=True))
        g_inter = jnp.exp(inter - m_t)
        qk = lax.dot_general(qh, kh, NT, preferred_element_type=F32)
        ws = jnp.exp(dmat - m_t) * qk
        num = (g_inter * lax.dot_general(qh, c_prev.astype(BF16), NT, preferred_element_type=F32)
               + jnp.dot(ws.astype(BF16), vh, preferred_element_type=F32))
        qf = qh.astype(F32)
        den = g_inter * jnp.sum(qf * n_prev, axis=1, keepdims=True) + jnp.sum(ws, axis=1, keepdims=True)
        hout = num / jnp.maximum(jnp.abs(den), jnp.exp(-m_t))
        m_new = m_t[cl - 1:cl, :]
        b_last = b_col[cl - 1:cl, :]
        g_c = jnp.exp(b_last + m_prev - m_new)
        w_end = jnp.exp(b_last - b_col + i_col - m_new)
        kf = kh.astype(F32)
        vw = (vh.astype(F32) * w_end).astype(BF16)
        c_sc[h] = g_c * c_prev + lax.dot_general(vw, kh, TN, preferred_element_type=F32)
        n_sc[h:h + 1, :] = g_c * n_prev + jnp.sum(kf * w_end, axis=0, keepdims=True)
        m_sc[h:h + 1, :] = jnp.broadcast_to(m_new, (1, m_sc.shape[1]))
        y_ref[0, :, sl] = _headnorm_gate(hout, ng_ref[:, sl], og_ref[0, :, sl]).astype(y_ref.dtype)

    @pl.when(ci == pl.num_programs(1) - 1)
    def _():
        c_ref[0] = c_sc[...]
        n_ref[0] = n_sc[...]
        m_ref[0] = m_sc[...]


def mlstm_prompt(q, k, v, og, gates, bgate, norm_g):
    bsz, t, width = q.shape
    dh = width // ML_HEADS
    cl = math.gcd(t, ML_CHUNK)
    tri = (lax.broadcasted_iota(I32, (cl, cl), 1) <= lax.broadcasted_iota(I32, (cl, cl), 0)).astype(F32)
    blk = lambda w: pl.BlockSpec((1, cl, w), lambda b, i: (b, i, 0))
    return pl.pallas_call(
        functools.partial(_mlstm_prompt_kernel, cl=cl, dh=dh), grid=(bsz, t // cl),
        in_specs=[blk(width), blk(width), blk(width), blk(width), blk(LANES), _full((1, LANES)), _full((1, width)),
                  _full((cl, cl))],
        out_specs=[blk(width), pl.BlockSpec((1, ML_HEADS, dh, dh), lambda b, i: (b, 0, 0, 0)),
                   pl.BlockSpec((1, SUBLANES, dh), lambda b, i: (b, 0, 0)),
                   pl.BlockSpec((1, SUBLANES, LANES), lambda b, i: (b, 0, 0))],
        out_shape=[jax.ShapeDtypeStruct((bsz, t, width), BF16), jax.ShapeDtypeStruct((bsz, ML_HEADS, dh, dh), F32),
                   jax.ShapeDtypeStruct((bsz, SUBLANES, dh), F32), jax.ShapeDtypeStruct((bsz, SUBLANES, LANES), F32)],
        scratch_shapes=[pltpu.VMEM((ML_HEADS, dh, dh), F32), pltpu.VMEM((SUBLANES, dh), F32),
                        pltpu.VMEM((SUBLANES, LANES), F32)],
        compiler_params=_cp("parallel", "arbitrary"), name="mlstm_prompt")(
            q, k, v, og, gates, bgate, norm_g.reshape(1, width), tri)


def _mlstm_step_kernel(q_ref, k_ref, v_ref, og_ref, gt_ref, bg_ref, ng_ref, c0_ref, n0_ref, m0_ref,
                       y_ref, c_ref, n_ref, m_ref, *, sb, dh):
    kscale = dh ** -0.5
    m_ref[...] = jnp.zeros(m_ref.shape, F32)

    def seq_body(s, carry):
        rs = pl.ds(s, 1)
        pre = gt_ref[rs, :] + bg_ref[...]
        for h in range(ML_HEADS):
            sl = slice(h * dh, (h + 1) * dh)
            qh = q_ref[rs, sl]
            kh = k_ref[rs, sl] * kscale
            vh = v_ref[rs, sl]
            i_pre = pre[:, h:h + 1]
            logf = jax.nn.log_sigmoid(pre[:, ML_HEADS + h:ML_HEADS + h + 1])
            m_prev = m0_ref[rs, h:h + 1]
            c_prev = c0_ref[s, h]
            n_prev = n0_ref[s, h:h + 1, :]
            inter = logf + m_prev
            m_t = jnp.maximum(inter, i_pre)
            g_inter = jnp.exp(inter - m_t)
            w_in = jnp.exp(i_pre - m_t)
            qk = jnp.sum(qh * kh, axis=1, keepdims=True)
            q8 = jnp.broadcast_to(qh, (SUBLANES, dh))
            cq = lax.dot_general(q8, c_prev, NT, preferred_element_type=F32, precision=HI)[0:1]
            num = g_inter * cq + (w_in * qk) * vh
            den = g_inter * jnp.sum(n_prev * qh, axis=1, keepdims=True) + w_in * qk
            hout = num / jnp.maximum(jnp.abs(den), jnp.exp(-m_t))
            rsel = lax.broadcasted_iota(I32, (SUBLANES, dh), 0) == 0
            v8 = jnp.where(rsel, jnp.broadcast_to(w_in * vh, (SUBLANES, dh)), 0.0)
            k8 = jnp.broadcast_to(kh, (SUBLANES, dh))
            c_ref[s, h] = g_inter * c_prev + lax.dot_general(v8, k8, TN, preferred_element_type=F32, precision=HI)
            n_ref[s, h:h + 1, :] = g_inter * n_prev + w_in * kh
            m_ref[rs, h:h + 1] = m_t
            y_ref[rs, sl] = _headnorm_gate(hout, ng_ref[:, sl], og_ref[rs, sl]).astype(y_ref.dtype)
        return carry

    lax.fori_loop(0, sb, seq_body, 0)


def mlstm_step(q, k, v, og, gates, bgate, norm_g, c0, n0, m0, sb=SUBLANES):
    s, width = q.shape
    dh = width // ML_HEADS
    assert s % sb == 0
    row = lambda w: pl.BlockSpec((sb, w), lambda i: (i, 0))
    m0p = jnp.pad(m0, ((0, 0), (0, LANES - ML_HEADS)))
    outs = pl.pallas_call(
        functools.partial(_mlstm_step_kernel, sb=sb, dh=dh), grid=(s // sb,),
        in_specs=[row(width), row(width), row(width), row(width), row(LANES), _full((1, LANES)), _full((1, width)),
                  pl.BlockSpec((sb, ML_HEADS, dh, dh), lambda i: (i, 0, 0, 0)),
                  pl.BlockSpec((sb, ML_HEADS, dh), lambda i: (i, 0, 0)), row(LANES)],
        out_specs=[row(width), pl.BlockSpec((sb, ML_HEADS, dh, dh), lambda i: (i, 0, 0, 0)),
                   pl.BlockSpec((sb, ML_HEADS, dh), lambda i: (i, 0, 0)), row(LANES)],
        out_shape=[jax.ShapeDtypeStruct((s, width), F32), jax.ShapeDtypeStruct((s, ML_HEADS, dh, dh), F32),
                   jax.ShapeDtypeStruct((s, ML_HEADS, dh), F32), jax.ShapeDtypeStruct((s, LANES), F32)],
        compiler_params=_cp("parallel"), name="mlstm_step")(
            q, k, v, og, gates, bgate, norm_g.reshape(1, width), c0, n0, m0p)
    y, c, n, m = outs
    return y, c, n, m[:, :ML_HEADS]


def _xattn_prompt_kernel(q_ref, k_ref, v_ref, o_ref, *, dh):
    scale = dh ** -0.5
    for h in range(X_HEADS):
        sl = slice(h * dh, (h + 1) * dh)
        lg = lax.dot_general(q_ref[0, :, sl], k_ref[0, :, sl], NT, preferred_element_type=F32) * scale
        mx = jnp.max(lg, axis=-1, keepdims=True)
        e = jnp.exp(lg - mx)
        pr = e / jnp.sum(e, axis=-1, keepdims=True)
        o_ref[0, :, sl] = jnp.dot(pr.astype(BF16), v_ref[0, :, sl], preferred_element_type=F32).astype(o_ref.dtype)


def xattn_prompt(xq, mk, mv, tq_cap=512):
    bsz, t, d = xq.shape
    mt = mk.shape[1]
    tq = _row_tile(t, tq_cap)
    return pl.pallas_call(
        functools.partial(_xattn_prompt_kernel, dh=d // X_HEADS), grid=(bsz, t // tq),
        in_specs=[pl.BlockSpec((1, tq, d), lambda b, i: (b, i, 0)), pl.BlockSpec((1, mt, d), lambda b, i: (b, 0, 0)),
                  pl.BlockSpec((1, mt, d), lambda b, i: (b, 0, 0))],
        out_specs=pl.BlockSpec((1, tq, d), lambda b, i: (b, i, 0)),
        out_shape=jax.ShapeDtypeStruct((bsz, t, d), BF16),
        compiler_params=_cp("parallel", "parallel"), name="xattn_prompt")(xq, mk, mv)


def _xattn_step_kernel(q_ref, k_ref, v_ref, o_ref, *, sb, dh):
    scale = dh ** -0.5

    def seq_body(s, carry):
        rs = pl.ds(s, 1)
        for h in range(X_HEADS):
            sl = slice(h * dh, (h + 1) * dh)
            qh = q_ref[rs, sl]
            lg = jnp.sum(k_ref[s, :, sl] * qh, axis=1, keepdims=True) * scale
            mx = jnp.max(lg, axis=0, keepdims=True)
            e = jnp.exp(lg - mx)
            pr = e / jnp.sum(e, axis=0, keepdims=True)
            o_ref[rs, sl] = jnp.sum(pr * v_ref[s, :, sl], axis=0, keepdims=True).astype(o_ref.dtype)
        return carry

    lax.fori_loop(0, sb, seq_body, 0)


def xattn_step(xq, mk, mv, sb=8):
    s, d = xq.shape
    mt = mk.shape[1]
    assert s % sb == 0
    return pl.pallas_call(
        functools.partial(_xattn_step_kernel, sb=sb, dh=d // X_HEADS), grid=(s // sb,),
        in_specs=[pl.BlockSpec((sb, d), lambda i: (i, 0)), pl.BlockSpec((sb, mt, d), lambda i: (i, 0, 0)),
                  pl.BlockSpec((sb, mt, d), lambda i: (i, 0, 0))],
        out_specs=pl.BlockSpec((sb, d), lambda i: (i, 0)),
        out_shape=jax.ShapeDtypeStruct((s, d), F32),
        compiler_params=_cp("parallel"), name="xattn_step")(xq, mk, mv)


def _t5_bucket(dist):
    max_exact = REL_BUCKETS // 2
    d = jnp.maximum(dist.astype(F32), 1.0)
    large = max_exact + (jnp.log(d / max_exact) / math.log(REL_MAX_DIST / max_exact)
                         * (REL_BUCKETS - max_exact)).astype(I32)
    large = jnp.minimum(large, REL_BUCKETS - 1)
    return jnp.where(dist < max_exact, dist, large)


def _pad_cols(w, n):
    return jnp.pad(w, ((0, 0), (0, n - w.shape[1])))


def _block_diag(w):
    nb, di, do = w.shape
    eye = jnp.eye(nb, dtype=w.dtype)
    return jnp.einsum('nde,nm->ndme', w, eye).reshape(nb * di, nb * do)


def kernel(x_prompt, x_sample, cache_k, cache_v, cache_kidx, state_rglru_h, state_rglru_conv, state_mlstm_c,
           state_mlstm_n, state_mlstm_m, cache_mem_k, cache_mem_v, page_table, mem_prompt, ln_g, ln_b, rel_bias,
           w_in_even, w_out_even, rg_conv_w, rg_conv_b, rg_w_a, rg_b_a, rg_w_i, rg_b_i, rg_lambda, w_in_odd,
           b_gate_odd, ml_norm_g, w_out_odd, w_xq, w_xk, w_xv, w_xo, ffn_w_gate, ffn_w_up, ffn_w_down,
           moe_router, moe_w_gate, moe_w_up, moe_w_down):
    bsz, t, d = x_prompt.shape
    s = x_sample.shape[0]
    depth = ln_g.shape[0]
    n_pages = page_table.shape[1]
    past_len = n_pages * PAGE_SIZE
    mt = mem_prompt.shape[1]
    rg_w = rg_lambda.shape[1]
    ml_w = ml_norm_g.shape[1]
    n_experts = moe_router.shape[2]
    alpha = (2.0 * depth) ** 0.25
    bf = lambda a: a.astype(BF16)

    ii = lax.broadcasted_iota(I32, (TQ, TQ), 0)
    jj = lax.broadcasted_iota(I32, (TQ, TQ), 1)
    dist3 = jnp.stack([jnp.maximum(ii - jj, 0), TQ + ii - jj, jnp.full((TQ, TQ), 2 * TQ, I32)])
    bias_tab = jnp.transpose(rel_bias[_t5_bucket(dist3)], (0, 3, 1, 2)).astype(F32)
    dist_s = past_len - jnp.arange(past_len, dtype=I32)
    bias_s = jnp.repeat(rel_bias[_t5_bucket(dist_s)], A_HEAD_DIM, axis=-1).reshape(n_pages, PAGE_SIZE, A_WIDTH)
    bias_s_new = jnp.repeat(rel_bias[_t5_bucket(jnp.zeros((1,), I32))], A_HEAD_DIM, axis=-1)

    mem_bf = bf(mem_prompt.reshape(bsz * mt, d))
    kv_w = [bf(w_xk[l]) for l in range(depth)] + [bf(w_xv[l]) for l in range(depth)]
    kv = multi_proj(mem_bf, kv_w, [(F32, BF16)] * (2 * depth))
    mem_k_f32 = [kv[2 * l] for l in range(depth)]
    mem_k_bf = [kv[2 * l + 1].reshape(bsz, mt, d) for l in range(depth)]
    mem_v_f32 = [kv[2 * depth + 2 * l] for l in range(depth)]
    mem_v_bf = [kv[2 * depth + 2 * l + 1].reshape(bsz, mt, d) for l in range(depth)]

    xp = x_prompt.reshape(bsz * t, d)
    xs = x_sample.reshape(s, d)
    xp_bf, xs_bf = bf(xp), bf(xs)
    outs = {}
    k_p, v_p, ki_p, k_s, v_s, ki_s = [], [], [], [], [], []
    rgh_p, rgc_p, rgh_s, rgc_s = [], [], [], []
    mlc_p, mln_p, mlm_p, mlc_s, mln_s, mlm_s = [], [], [], [], [], []

    for l in range(depth):
        if l % 2 == 0:
            e = l // 2
            w_in = w_in_even[e]
            o = 0
            cols = {}
            for name, width in (("q", A_WIDTH), ("k", A_WIDTH), ("v", A_WIDTH), ("qi", IDX_HEADS * IDX_DIM),
                                ("ki", IDX_DIM), ("wi", IDX_HEADS), ("xr", rg_w), ("gr", rg_w)):
                cols[name] = bf(w_in[:, o:o + width])
                o += width
            cols["wi"] = _pad_cols(cols["wi"], LANES)
            names = ("q", "k", "v", "qi", "ki", "wi", "xr", "gr")
            ws = [cols[n] for n in names]
            wa = bf(_block_diag(rg_w_a[e]))
            wi_g = bf(_block_diag(rg_w_i[e]))
            w_out = bf(w_out_even[e])

            q, kf, kb, vf, vb, qi, kif, kib, wi, xr, gr = multi_proj(
                xp_bf, ws, [(BF16,), (F32, BF16), (F32, BF16), (BF16,), (F32, BF16), (F32,), (F32,), (F32,)])
            r3 = lambda a: a.reshape(bsz, t, a.shape[-1])
            att = dsa_prompt(r3(q), r3(kb), r3(vb), r3(qi), r3(kib), r3(wi), bias_tab)
            rg_y, h_last = rglru_prompt(r3(xr), r3(gr), rg_conv_w[e], rg_conv_b[e], wa, rg_b_a[e], wi_g, rg_b_i[e],
                                        rg_lambda[e])
            xp, xp_bf = mm_res_ln([att.reshape(bsz * t, A_WIDTH), rg_y.reshape(bsz * t, rg_w)],
                                  [w_out[:A_WIDTH], w_out[A_WIDTH:]], xp, ln_g[l, 0], ln_b[l, 0], alpha)
            k_p.append(kf.reshape(bsz, t, A_HEADS, A_HEAD_DIM))
            v_p.append(vf.reshape(bsz, t, A_HEADS, A_HEAD_DIM))
            ki_p.append(kif.reshape(bsz, t, IDX_DIM))
            rgh_p.append(h_last.reshape(bsz, rg_w))
            rgc_p.append(r3(xr)[:, t - (RG_CONV - 1):, :])

            q, kf, vf, qi, kif, wi, xr, gr = multi_proj(
                xs_bf, ws, [(F32,), (F32,), (F32,), (F32,), (F32,), (F32,), (F32,), (F32,)])
            qi8 = jnp.pad(qi.reshape(s, IDX_HEADS, IDX_DIM), ((0, 0), (0, SUBLANES - IDX_HEADS), (0, 0)))
            pool = cache_k.shape[1]
            raw = dsa_sample_scores(page_table, qi8, cache_kidx[e])
            mask, mask_new = dsa_sample_select(raw, qi, kif, wi)
            att = dsa_sample_attn(page_table, q.reshape(s, 1, A_WIDTH), kf.reshape(s, 1, A_WIDTH),
                                  vf.reshape(s, 1, A_WIDTH), mask.reshape(s, n_pages, PAGE_SIZE),
                                  mask_new.reshape(s, 1, LANES), bias_s, bias_s_new,
                                  cache_k[e].reshape(pool, PAGE_SIZE, A_WIDTH),
                                  cache_v[e].reshape(pool, PAGE_SIZE, A_WIDTH))
            rg_y, h_new = rglru_step(xr, gr, state_rglru_conv[e], state_rglru_h[e], rg_conv_w[e], rg_conv_b[e],
                                     wa, rg_b_a[e], wi_g, rg_b_i[e], rg_lambda[e])
            xs, xs_bf = mm_res_ln([att.reshape(s, A_WIDTH), rg_y], [w_out[:A_WIDTH], w_out[A_WIDTH:]], xs,
                                  ln_g[l, 0], ln_b[l, 0], alpha)
            k_s.append(kf.reshape(s, 1, A_HEADS, A_HEAD_DIM))
            v_s.append(vf.reshape(s, 1, A_HEADS, A_HEAD_DIM))
            ki_s.append(kif.reshape(s, 1, IDX_DIM))
            rgh_s.append(h_new)
            rgc_s.append(jnp.concatenate([state_rglru_conv[e][:, 1:], xr[:, None, :]], axis=1))
        else:
            o_ = l // 2
            w_in = w_in_odd[o_]
            ws = [bf(w_in[:, i * ml_w:(i + 1) * ml_w]) for i in range(4)]
            ws.append(_pad_cols(bf(w_in[:, 4 * ml_w:]), LANES))
            bgate = jnp.pad(b_gate_odd[o_].astype(F32), (0, LANES - 2 * ML_HEADS)).reshape(1, LANES)
            w_out = bf(w_out_odd[o_])

            q, k, v, og, gates = multi_proj(xp_bf, ws, [(BF16,), (BF16,), (BF16,), (F32,), (F32,)])
            r3 = lambda a: a.reshape(bsz, t, a.shape[-1])
            hn, c, n, m = mlstm_prompt(r3(q), r3(k), r3(v), r3(og), r3(gates), bgate, ml_norm_g[o_])
            xp, xp_bf = mm_res_ln([hn.reshape(bsz * t, ml_w)], [w_out], xp, ln_g[l, 0], ln_b[l, 0], alpha)
            mlc_p.append(c)
            mln_p.append(n[:, :ML_HEADS])
            mlm_p.append(m[:, :ML_HEADS, 0])

            q, k, v, og, gates = multi_proj(xs_bf, ws, [(F32,), (F32,), (F32,), (F32,), (F32,)])
            hn, c, n, m = mlstm_step(q, k, v, og, gates, bgate, ml_norm_g[o_], state_mlstm_c[o_], state_mlstm_n[o_],
                                     state_mlstm_m[o_])
            xs, xs_bf = mm_res_ln([bf(hn)], [w_out], xs, ln_g[l, 0], ln_b[l, 0], alpha)
            mlc_s.append(c)
            mln_s.append(n)
            mlm_s.append(m)

        wq, wo = bf(w_xq[l]), bf(w_xo[l])
        (xq,) = multi_proj(xp_bf, [wq], [(BF16,)])
        xo = xattn_prompt(xq.reshape(bsz, t, d), mem_k_bf[l], mem_v_bf[l])
        xp, xp_bf = mm_res_ln([xo.reshape(bsz * t, d)], [wo], xp, ln_g[l, 1], ln_b[l, 1], alpha)
        (xq,) = multi_proj(xs_bf, [wq], [(F32,)])
        xo = xattn_step(xq, cache_mem_k[l].reshape(s, mt, d), cache_mem_v[l].reshape(s, mt, d))
        xs, xs_bf = mm_res_ln([bf(xo)], [wo], xs, ln_g[l, 1], ln_b[l, 1], alpha)

        if l % 2 == 0:
            e = l // 2
            wg, wu, wd = bf(ffn_w_gate[e]), bf(ffn_w_up[e]), bf(ffn_w_down[e])
            xp, xp_bf = ffn_ln(xp_bf, xp, wg, wu, wd, ln_g[l, 2], ln_b[l, 2], alpha)
            xs, xs_bf = ffn_ln(xs_bf, xs, wg, wu, wd, ln_g[l, 2], ln_b[l, 2], alpha)
        else:
            o_ = l // 2
            wr = _pad_cols(moe_router[o_].astype(F32), LANES)
            wg, wu, wd = bf(moe_w_gate[o_]), bf(moe_w_up[o_]), bf(moe_w_down[o_])
            gp = router_gates(xp, wr, n_experts)
            xp, xp_bf = moe_ln(xp_bf, xp, gp, wg, wu, wd, ln_g[l, 2], ln_b[l, 2], alpha)
            gs = router_gates(xs, wr, n_experts)
            xs, xs_bf = moe_ln(xs_bf, xs, gs, wg, wu, wd, ln_g[l, 2], ln_b[l, 2], alpha)

    xh = mem_k_f32[0].shape[-1] // X_HEADS
    mem_shape = (bsz, mt, X_HEADS, xh)
    return (xp.reshape(bsz, t, d), xs.reshape(s, 1, d),
            jnp.stack(k_p), jnp.stack(v_p), jnp.stack(ki_p), jnp.stack(k_s), jnp.stack(v_s), jnp.stack(ki_s),
            jnp.stack(rgh_p), jnp.stack(rgc_p), jnp.stack(rgh_s), jnp.stack(rgc_s),
            jnp.stack(mlc_p), jnp.stack(mln_p), jnp.stack(mlm_p), jnp.stack(mlc_s), jnp.stack(mln_s), jnp.stack(mlm_s),
            jnp.stack([a.reshape(mem_shape) for a in mem_k_f32]), jnp.stack([a.reshape(mem_shape) for a in mem_v_f32]))
```

```python
import functools
import math

import jax
import jax.numpy as jnp
from jax import lax
from jax.experimental import pallas as pl
from jax.experimental.pallas import tpu as pltpu

F32 = jnp.float32
BF16 = jnp.bfloat16
I32 = jnp.int32

PAGE_SIZE = 128
A_HEADS = 8
A_HEAD_DIM = 64
A_WIDTH = A_HEADS * A_HEAD_DIM
IDX_HEADS = 4
IDX_DIM = 64
TOPK_MAX = 256
REL_BUCKETS = 32
REL_MAX_DIST = 128
RG_BLOCKS = 8
RG_CONV = 4
RG_C = 8.0
ML_HEADS = 4
ML_CHUNK = 128
X_HEADS = 4
TOP_K = 2
LN_EPS = 1e-5

LANES = 128
SUBLANES = 8
VMEM_LIMIT = 48 * 1024 * 1024
NEG = -1e30
INT_MIN = -2 ** 31
TQ = 128

NT = (((1,), (1,)), ((), ()))
TN = (((0,), (0,)), ((), ()))
HI = lax.Precision.HIGHEST


def _cp(*sem):
    return pltpu.CompilerParams(dimension_semantics=sem, vmem_limit_bytes=VMEM_LIMIT)


def _full(shape):
    n = len(shape)
    return pl.BlockSpec(shape, lambda *_: (0,) * n)


def _row_tile(m, cap):
    t = min(m, cap)
    assert m % t == 0
    return t


def _mm(a, b, dims=None):
    prec = HI if a.dtype == F32 else None
    if dims is None:
        return jnp.dot(a, b, preferred_element_type=F32, precision=prec)
    return lax.dot_general(a, b, dims, preferred_element_type=F32, precision=prec)


def _proj_kernel(x_ref, *refs, n_w, out_counts, transposed):
    w_refs, o_refs = refs[:n_w], refs[n_w:]
    x = x_ref[...]
    k = 0
    for w_ref, cnt, tr in zip(w_refs, out_counts, transposed):
        if tr:
            y = _mm(w_ref[...], x, NT)
        else:
            y = _mm(x, w_ref[...])
        for _ in range(cnt):
            if tr:
                for j in range(y.shape[1] // LANES):
                    o_refs[k][j] = y[:, j * LANES:(j + 1) * LANES].astype(o_refs[k].dtype)
            else:
                o_refs[k][...] = y.astype(o_refs[k].dtype)
            k += 1


def multi_proj(x, ws, out_dtypes, transposed=None, tm_cap=512):
    m, kdim = x.shape
    tm = _row_tile(m, tm_cap)
    transposed = transposed or (False,) * len(ws)
    in_specs = [pl.BlockSpec((tm, kdim), lambda i: (i, 0))] + [_full(w.shape) for w in ws]
    out_shape, out_specs = [], []
    for w, dts, tr in zip(ws, out_dtypes, transposed):
        for dt in dts:
            if tr:
                assert tm % LANES == 0
                out_shape.append(jax.ShapeDtypeStruct((m // LANES, w.shape[0], LANES), dt))
                out_specs.append(pl.BlockSpec((tm // LANES, w.shape[0], LANES), lambda i: (i, 0, 0)))
            else:
                out_shape.append(jax.ShapeDtypeStruct((m, w.shape[1]), dt))
                out_specs.append(pl.BlockSpec((tm, w.shape[1]), lambda i: (i, 0)))
    kern = functools.partial(_proj_kernel, n_w=len(ws), out_counts=tuple(len(d) for d in out_dtypes),
                             transposed=tuple(transposed))
    return pl.pallas_call(kern, grid=(m // tm,), in_specs=in_specs, out_specs=out_specs,
                          out_shape=out_shape, compiler_params=_cp("parallel"), name="multi_proj")(x, *ws)


def _layernorm_rows(z, g, b):
    mu = jnp.mean(z, axis=-1, keepdims=True)
    zc = z - mu
    var = jnp.mean(zc * zc, axis=-1, keepdims=True)
    return zc * lax.rsqrt(var + LN_EPS) * g + b


def _mm_ln_kernel(*refs, n_a, alpha):
    a_refs, w_refs = refs[:n_a], refs[n_a:2 * n_a]
    r_ref, g_ref, b_ref, o_ref, obf_ref = refs[2 * n_a:]
    y = alpha * r_ref[...]
    for a_ref, w_ref in zip(a_refs, w_refs):
        y = y + _mm(a_ref[...], w_ref[...])
    out = _layernorm_rows(y, g_ref[...], b_ref[...])
    o_ref[...] = out
    obf_ref[...] = out.astype(BF16)


def mm_res_ln(a_list, w_list, resid, g, b, alpha, tm_cap=512):
    m, d = resid.shape
    tm = _row_tile(m, tm_cap)
    in_specs = ([pl.BlockSpec((tm, a.shape[1]), lambda i: (i, 0)) for a in a_list]
                + [_full(w.shape) for w in w_list]
                + [pl.BlockSpec((tm, d), lambda i: (i, 0)), _full((1, d)), _full((1, d))])
    out_specs = [pl.BlockSpec((tm, d), lambda i: (i, 0))] * 2
    out_shape = [jax.ShapeDtypeStruct((m, d), F32), jax.ShapeDtypeStruct((m, d), BF16)]
    kern = functools.partial(_mm_ln_kernel, n_a=len(a_list), alpha=alpha)
    return pl.pallas_call(kern, grid=(m // tm,), in_specs=in_specs, out_specs=out_specs, out_shape=out_shape,
                          compiler_params=_cp("parallel"), name="mm_res_ln")(
                              *a_list, *w_list, resid, g.reshape(1, d), b.reshape(1, d))


def _ffn_kernel(x_ref, r_ref, wg_ref, wu_ref, wd_ref, g_ref, b_ref, o_ref, obf_ref, acc_sc, *, alpha):
    f = pl.program_id(1)

    @pl.when(f == 0)
    def _():
        acc_sc[...] = alpha * r_ref[...]

    x = x_ref[...]
    hg = _mm(x, wg_ref[...])
    hu = _mm(x, wu_ref[...])
    h = (jax.nn.silu(hg) * hu).astype(x.dtype)
    acc_sc[...] += _mm(h, wd_ref[...])

    @pl.when(f == pl.num_programs(1) - 1)
    def _():
        out = _layernorm_rows(acc_sc[...], g_ref[...], b_ref[...])
        o_ref[...] = out
        obf_ref[...] = out.astype(BF16)


def ffn_ln(x_bf, resid, wg, wu, wd, g, b, alpha, tm_cap=512, tf=256):
    m, d = resid.shape
    dff = wg.shape[1]
    tm = _row_tile(m, tm_cap)
    assert dff % tf == 0
    in_specs = [pl.BlockSpec((tm, d), lambda i, f: (i, 0)), pl.BlockSpec((tm, d), lambda i, f: (i, 0)),
                pl.BlockSpec((d, tf), lambda i, f: (0, f)), pl.BlockSpec((d, tf), lambda i, f: (0, f)),
                pl.BlockSpec((tf, d), lambda i, f: (f, 0)), _full((1, d)), _full((1, d))]
    out_specs = [pl.BlockSpec((tm, d), lambda i, f: (i, 0))] * 2
    out_shape = [jax.ShapeDtypeStruct((m, d), F32), jax.ShapeDtypeStruct((m, d), BF16)]
    return pl.pallas_call(functools.partial(_ffn_kernel, alpha=alpha), grid=(m // tm, dff // tf),
                          in_specs=in_specs, out_specs=out_specs, out_shape=out_shape,
                          scratch_shapes=[pltpu.VMEM((tm, d), F32)],
                          compiler_params=_cp("parallel", "arbitrary"), name="ffn_ln")(
                              x_bf, resid, wg, wu, wd, g.reshape(1, d), b.reshape(1, d))


def _router_kernel(x_ref, w_ref, o_ref, *, n_experts):
    logits = _mm(x_ref[...], w_ref[...])
    lane = lax.broadcasted_iota(I32, logits.shape, 1)
    valid = lane < n_experts
    logits = jnp.where(valid, logits, NEG)
    mx = jnp.max(logits, axis=-1, keepdims=True)
    e = jnp.where(valid, jnp.exp(logits - mx), 0.0)
    p = e / jnp.sum(e, axis=-1, keepdims=True)
    p = jnp.where(valid, p, -1.0)
    big = jnp.int32(LANES)
    m1 = jnp.max(p, axis=-1, keepdims=True)
    i1 = jnp.min(jnp.where(p == m1, lane, big), axis=-1, keepdims=True)
    p2 = jnp.where(lane == i1, -1.0, p)
    m2 = jnp.max(p2, axis=-1, keepdims=True)
    i2 = jnp.min(jnp.where(p2 == m2, lane, big), axis=-1, keepdims=True)
    tot = m1 + m2
    o_ref[...] = jnp.where(lane == i1, m1 / tot, jnp.where(lane == i2, m2 / tot, 0.0))


def router_gates(x, w_pad, n_experts, tm_cap=512):
    m, d = x.shape
    tm = _row_tile(m, tm_cap)
    return pl.pallas_call(functools.partial(_router_kernel, n_experts=n_experts), grid=(m // tm,),
                          in_specs=[pl.BlockSpec((tm, d), lambda i: (i, 0)), _full(w_pad.shape)],
                          out_specs=pl.BlockSpec((tm, LANES), lambda i: (i, 0)),
                          out_shape=jax.ShapeDtypeStruct((m, LANES), F32),
                          compiler_params=_cp("parallel"), name="router")(x, w_pad)


def _moe_kernel(x_ref, r_ref, gt_ref, wg_ref, wu_ref, wd_ref, g_ref, b_ref, o_ref, obf_ref, acc_sc, *, alpha):
    e = pl.program_id(1)
    f = pl.program_id(2)

    @pl.when((e == 0) & (f == 0))
    def _():
        acc_sc[...] = alpha * r_ref[...]

    gt = gt_ref[...]
    lane = lax.broadcasted_iota(I32, gt.shape, 1)
    ge = jnp.sum(jnp.where(lane == e, gt, 0.0), axis=-1, keepdims=True)
    x = x_ref[...]
    hg = jnp.dot(x, wg_ref[0], preferred_element_type=F32)
    hu = jnp.dot(x, wu_ref[0], preferred_element_type=F32)
    h = (jax.nn.silu(hg) * hu * ge).astype(BF16)
    acc_sc[...] += jnp.dot(h, wd_ref[0], preferred_element_type=F32)

    @pl.when((e == pl.num_programs(1) - 1) & (f == pl.num_programs(2) - 1))
    def _():
        out = _layernorm_rows(acc_sc[...], g_ref[...], b_ref[...])
        o_ref[...] = out
        obf_ref[...] = out.astype(BF16)


def moe_ln(x_bf, resid, gates, wg, wu, wd, g, b, alpha, tm_cap=512, tf=512):
    m, d = resid.shape
    ne, _, dff = wg.shape
    tm = _row_tile(m, tm_cap)
    assert dff % tf == 0
    in_specs = [pl.BlockSpec((tm, d), lambda i, e, f: (i, 0)), pl.BlockSpec((tm, d), lambda i, e, f: (i, 0)),
                pl.BlockSpec((tm, LANES), lambda i, e, f: (i, 0)),
                pl.BlockSpec((1, d, tf), lambda i, e, f: (e, 0, f)), pl.BlockSpec((1, d, tf), lambda i, e, f: (e, 0, f)),
                pl.BlockSpec((1, tf, d), lambda i, e, f: (e, f, 0)), _full((1, d)), _full((1, d))]
    out_specs = [pl.BlockSpec((tm, d), lambda i, e, f: (i, 0))] * 2
    out_shape = [jax.ShapeDtypeStruct((m, d), F32), jax.ShapeDtypeStruct((m, d), BF16)]
    return pl.pallas_call(functools.partial(_moe_kernel, alpha=alpha), grid=(m // tm, ne, dff // tf),
                          in_specs=in_specs, out_specs=out_specs, out_shape=out_shape,
                          scratch_shapes=[pltpu.VMEM((tm, d), F32)],
                          compiler_params=_cp("parallel", "arbitrary", "arbitrary"), name="moe_ln")(
                              x_bf, resid, gates, wg, wu, wd, g.reshape(1, d), b.reshape(1, d))


def _sortable_key(score):
    bits = pltpu.bitcast(score, I32)
    return jnp.where(bits < 0, bits ^ jnp.int32(0x7FFFFFFF), bits)


def _threshold_search(count, topk, shape):
    kk = jnp.int32(topk)
    c0 = count(lambda key, idx: jnp.where(key >= 0, 1, 0))
    t = jnp.where(c0 >= kk, jnp.int32(0), jnp.int32(INT_MIN)) + jnp.zeros(shape, I32)

    def bit_body(j, t):
        cand = t | (jnp.int32(1) << (30 - j))
        cnt = count(lambda key, idx: jnp.where(key >= cand, 1, 0))
        return jnp.where(cnt >= kk, cand, t)

    t = lax.fori_loop(0, 31, bit_body, t)
    need = kk - count(lambda key, idx: jnp.where(key > t, 1, 0))
    return t, jnp.where(t == INT_MIN, 0, need)


def _topk_threshold(count, topk, n_idx_bits, rows):
    t, need = _threshold_search(count, topk, (rows, 1))

    def tie_body(j, p):
        cand = p | (jnp.int32(1) << (n_idx_bits - 1 - j))
        got = count(lambda key, idx: jnp.where(key == t, jnp.where(idx < cand, 1, 0), 0))
        return jnp.where(got <= need, cand, p)

    p = lax.fori_loop(0, n_idx_bits, tie_body, jnp.zeros((rows, 1), I32))
    p = jnp.where(t == INT_MIN, 0, p)
    return t, p


CHUNK_GROUP = 4


def _dsa_prompt_kernel(qt_ref, k_ref, vt_ref, qit_ref, ki_ref, wit_ref, bias_ref, ltri_ref, o_ref,
                       key_sc, m_sc, l_sc, acc_sc, *, topk):
    i = pl.program_id(1)
    n_groups = (i + CHUNK_GROUP) // CHUNK_GROUP
    krow = lax.broadcasted_iota(I32, (TQ, TQ), 0)
    qcol = lax.broadcasted_iota(I32, (TQ, TQ), 1)
    qit = qit_ref[0, 0] * jnp.asarray(IDX_DIM ** -0.5, BF16)
    wt = wit_ref[0, 0] * (IDX_HEADS ** -0.5)

    def score_body(c, carry):
        off = pl.multiple_of(c * TQ, TQ)
        kc = ki_ref[0, pl.ds(off, TQ), :]
        sc = jnp.zeros((TQ, TQ), F32)
        for h in range(IDX_HEADS):
            s = jnp.dot(kc, qit[h * IDX_DIM:(h + 1) * IDX_DIM, :], preferred_element_type=F32)
            sc = sc + wt[h:h + 1, :] * jnp.maximum(s, 0.0)
        key = _sortable_key(sc)
        key_sc[pl.ds(off, TQ), :] = jnp.where(off + krow <= i * TQ + qcol, key, INT_MIN)
        return carry

    lax.fori_loop(0, n_groups * CHUNK_GROUP, score_body, 0)

    def count(pred):
        def body(g, acc):
            base = pl.multiple_of(g * (CHUNK_GROUP * TQ), CHUNK_GROUP * TQ)
            for j in range(CHUNK_GROUP):
                acc = acc + pred(key_sc[pl.ds(base + j * TQ, TQ), :], None)
            return acc
        acc = lax.fori_loop(0, n_groups, body, jnp.zeros((TQ, TQ), I32))
        return jnp.sum(acc, axis=0, keepdims=True)

    t, need = _threshold_search(count, topk, (1, TQ))
    need = need.astype(F32)

    m_sc[...] = jnp.full(m_sc.shape, NEG, F32)
    l_sc[...] = jnp.zeros(l_sc.shape, F32)
    acc_sc[...] = jnp.zeros(acc_sc.shape, F32)
    qt = qt_ref[0, 0] * jnp.asarray(A_HEAD_DIM ** -0.5, BF16)

    def att_body(c, ties_before):
        off = pl.multiple_of(c * TQ, TQ)
        key = key_sc[pl.ds(off, TQ), :]
        tie = key == t
        rank = ties_before + jnp.dot(ltri_ref[...], jnp.where(tie, 1.0, 0.0).astype(BF16),
                                     preferred_element_type=F32)
        sel = jnp.where(key > t, 1, jnp.where(tie, jnp.where(rank <= need, 1, 0), 0)) > 0
        kc = k_ref[0, pl.ds(off, TQ), :]
        bsel = jnp.minimum(i - c, 2)
        for h in range(A_HEADS):
            sl = slice(h * A_HEAD_DIM, (h + 1) * A_HEAD_DIM)
            s = jnp.dot(kc[:, sl], qt[sl, :], preferred_element_type=F32)
            s = jnp.where(sel, s + bias_ref[bsel, h], NEG)
            m_prev = m_sc[h, 0:1, :]
            m_new = jnp.maximum(m_prev, jnp.max(s, axis=0, keepdims=True))
            alpha = jnp.exp(m_prev - m_new)
            pe = jnp.exp(s - m_new)
            l_sc[h, 0:1, :] = alpha * l_sc[h, 0:1, :] + jnp.sum(pe, axis=0, keepdims=True)
            acc_sc[sl, :] = alpha * acc_sc[sl, :] + jnp.dot(vt_ref[0, c, sl, :], pe.astype(BF16),
                                                            preferred_element_type=F32)
            m_sc[h, 0:1, :] = m_new
        return rank[TQ - 1:TQ, :]

    lax.fori_loop(0, i + 1, att_body, jnp.zeros((1, TQ), F32))
    for h in range(A_HEADS):
        sl = slice(h * A_HEAD_DIM, (h + 1) * A_HEAD_DIM)
        acc_sc[sl, :] = acc_sc[sl, :] / l_sc[h, 0:1, :]
    o_ref[0] = acc_sc[...].T.astype(o_ref.dtype)


def dsa_prompt(qt, k, vt, qit, ki, wit, bias_tab):
    bsz, t, _ = k.shape
    assert TQ == LANES and t % (CHUNK_GROUP * TQ) == 0
    topk = min(TOPK_MAX, t // 4)
    nq = t // TQ
    ltri = (lax.broadcasted_iota(I32, (TQ, TQ), 1) <= lax.broadcasted_iota(I32, (TQ, TQ), 0)).astype(BF16)
    per_q = lambda rows: pl.BlockSpec((1, 1, rows, TQ), lambda b, i: (b, i, 0, 0))
    per_b = lambda w: pl.BlockSpec((1, t, w), lambda b, i: (b, 0, 0))
    stat = pltpu.VMEM((A_HEADS, SUBLANES, TQ), F32)
    return pl.pallas_call(
        functools.partial(_dsa_prompt_kernel, topk=topk), grid=(bsz, nq),
        in_specs=[per_q(A_WIDTH), per_b(A_WIDTH), pl.BlockSpec((1, nq, A_WIDTH, TQ), lambda b, i: (b, 0, 0, 0)),
                  per_q(IDX_HEADS * IDX_DIM), per_b(IDX_DIM), per_q(SUBLANES), _full(bias_tab.shape),
                  _full((TQ, TQ))],
        out_specs=pl.BlockSpec((1, TQ, A_WIDTH), lambda b, i: (b, i, 0)),
        out_shape=jax.ShapeDtypeStruct((bsz, t, A_WIDTH), BF16),
        scratch_shapes=[pltpu.VMEM((t, TQ), I32), stat, stat, pltpu.VMEM((A_WIDTH, TQ), F32)],
        compiler_params=_cp("parallel", "arbitrary"), name="dsa_prompt")(qt, k, vt, qit, ki, wit, bias_tab, ltri)


def _row_to_col(row):
    n = row.shape[1]
    eye = lax.broadcasted_iota(I32, (n, n), 0) == lax.broadcasted_iota(I32, (n, n), 1)
    return jnp.sum(jnp.where(eye, jnp.broadcast_to(row, (n, n)), 0.0), axis=1, keepdims=True)


def _col_to_row(col):
    n = col.shape[0]
    eye = lax.broadcasted_iota(I32, (n, n), 0) == lax.broadcasted_iota(I32, (n, n), 1)
    return jnp.sum(jnp.where(eye, jnp.broadcast_to(col, (n, n)), 0.0), axis=0, keepdims=True)


def _dsa_s_score_kernel(pt_ref, qi_ref, w_ref, *refs, n_pages):
    page_refs, o_ref = refs[:n_pages], refs[n_pages]
    w = w_ref[0] * (IDX_HEADS ** -0.5)
    qcols = [_row_to_col(qi_ref[0, h:h + 1, :] * (IDX_DIM ** -0.5)) for h in range(IDX_HEADS)]
    for p in range(n_pages):
        page = page_refs[p][0]
        sc = jnp.zeros((1, PAGE_SIZE), F32)
        for h in range(IDX_HEADS):
            s = jnp.sum(page * qcols[h], axis=0, keepdims=True)
            sc = sc + w[:, h:h + 1] * jnp.maximum(s, 0.0)
        o_ref[0, :, p * PAGE_SIZE:(p + 1) * PAGE_SIZE] = sc


def dsa_sample_scores(page_table, qi8, wi, kidx_pool_t):
    s, n_pages = page_table.shape
    pt = page_table.reshape(-1)
    page_spec = lambda p: pl.BlockSpec((1, IDX_DIM, PAGE_SIZE), lambda i, pt_ref: (pt_ref[i * n_pages + p], 0, 0))
    grid_spec = pltpu.PrefetchScalarGridSpec(
        num_scalar_prefetch=1, grid=(s,),
        in_specs=[pl.BlockSpec((1, SUBLANES, IDX_DIM), lambda i, pt_ref: (i, 0, 0)),
                  pl.BlockSpec((1, 1, LANES), lambda i, pt_ref: (i, 0, 0))] + [page_spec(p) for p in range(n_pages)],
        out_specs=pl.BlockSpec((1, 1, n_pages * PAGE_SIZE), lambda i, pt_ref: (i, 0, 0)))
    return pl.pallas_call(functools.partial(_dsa_s_score_kernel, n_pages=n_pages), grid_spec=grid_spec,
                          out_shape=jax.ShapeDtypeStruct((s, 1, n_pages * PAGE_SIZE), F32),
                          compiler_params=_cp("arbitrary"), name="dsa_sample_scores")(
                              pt, qi8, wi, *([kidx_pool_t] * n_pages))


def _dsa_s_select_kernel(sc_ref, qi_ref, kn_ref, wi_ref, mask_ref, mnew_ref, key_sc, *, topk, n_idx_bits):
    s, lp = mask_ref.shape
    w = wi_ref[...] * (IDX_HEADS ** -0.5)
    sc = sc_ref[...]
    sc_new = jnp.zeros((s, 1), F32)
    kn = kn_ref[...]
    for h in range(IDX_HEADS):
        s_new = jnp.sum(qi_ref[:, h * IDX_DIM:(h + 1) * IDX_DIM] * kn, axis=-1, keepdims=True)
        sc_new = sc_new + w[:, h:h + 1] * jnp.maximum(s_new * (IDX_DIM ** -0.5), 0.0)
    key_sc[...] = _sortable_key(sc)
    key_new = _sortable_key(sc_new)
    col = lax.broadcasted_iota(I32, (s, lp), 1)

    def count(pred):
        main = jnp.sum(pred(key_sc[...], col), axis=1, keepdims=True)
        return main + pred(key_new, jnp.int32(lp))

    t, p = _topk_threshold(count, topk, n_idx_bits, s)
    key = key_sc[...]
    mask_ref[...] = jnp.where(key > t, 1.0, jnp.where(key == t, jnp.where(col < p, 1.0, 0.0), 0.0))
    sel_new = jnp.where(key_new > t, 1.0, jnp.where(key_new == t, jnp.where(lp < p, 1.0, 0.0), 0.0))
    mnew_ref[...] = jnp.broadcast_to(sel_new, mnew_ref.shape)


def dsa_sample_select(raw, qi, ki_new, wi):
    s, lp = raw.shape
    length = lp + 1
    topk = min(TOPK_MAX, length // 4)
    kern = functools.partial(_dsa_s_select_kernel, topk=topk, n_idx_bits=length.bit_length())
    return pl.pallas_call(kern, out_shape=[jax.ShapeDtypeStruct((s, lp), F32), jax.ShapeDtypeStruct((s, LANES), F32)],
                          scratch_shapes=[pltpu.VMEM((s, lp), I32)],
                          compiler_params=pltpu.CompilerParams(vmem_limit_bytes=VMEM_LIMIT),
                          name="dsa_sample_select")(raw, qi, ki_new, wi)


def _dsa_s_attn_kernel(pt_ref, q_ref, kn_ref, vn_ref, mask_ref, mnew_ref, bias_ref, bnew_ref, *refs, n_pages):
    k_refs, v_refs, o_ref = refs[:n_pages], refs[n_pages:2 * n_pages], refs[2 * n_pages]
    hd = A_HEAD_DIM
    q = q_ref[0] * (hd ** -0.5)
    qcols = [_row_to_col(q[h:h + 1, :]) for h in range(A_HEADS)]
    hrow = lax.broadcasted_iota(I32, (A_HEADS, PAGE_SIZE), 0)
    lg_new = jnp.sum(q * kn_ref[0], axis=1, keepdims=True) + bnew_ref[:, 0:1]
    lg_new = jnp.where(mnew_ref[0][:, 0:1] > 0.0, lg_new, NEG)
    logits = []
    mx = lg_new
    for p in range(n_pages):
        lg = jnp.zeros((A_HEADS, PAGE_SIZE), F32)
        for h in range(A_HEADS):
            s = jnp.sum(k_refs[p][0, h * hd:(h + 1) * hd, :] * qcols[h], axis=0, keepdims=True)
            lg = jnp.where(hrow == h, jnp.broadcast_to(s, lg.shape), lg)
        lg = jnp.where(mask_ref[0, p:p + 1, :] > 0.0, lg + bias_ref[p], NEG)
        logits.append(lg)
        mx = jnp.maximum(mx, jnp.max(lg, axis=1, keepdims=True))
    pe_new = jnp.exp(lg_new - mx)
    pes = [jnp.exp(lg - mx) for lg in logits]
    den = pe_new
    for pe in pes:
        den = den + jnp.sum(pe, axis=1, keepdims=True)
    for h in range(A_HEADS):
        acc = jnp.zeros((hd, PAGE_SIZE), F32)
        for p in range(n_pages):
            acc = acc + pes[p][h:h + 1, :] * v_refs[p][0, h * hd:(h + 1) * hd, :]
        out = _col_to_row(jnp.sum(acc, axis=1, keepdims=True)) + pe_new[h:h + 1, :] * vn_ref[0, h:h + 1, :]
        o_ref[0, h:h + 1, :] = out / den[h:h + 1, :]


def dsa_sample_attn(page_table, q, k_new, v_new, mask, mask_new, bias_pages, bias_new, k_pool_t, v_pool_t):
    s, n_pages = page_table.shape
    pt = page_table.reshape(-1)
    per_s = lambda r, w: pl.BlockSpec((1, r, w), lambda i, pt_ref: (i, 0, 0))
    const = lambda shape: pl.BlockSpec(shape, lambda i, pt_ref: (0,) * len(shape))
    page_spec = lambda p: pl.BlockSpec((1, A_WIDTH, PAGE_SIZE), lambda i, pt_ref: (pt_ref[i * n_pages + p], 0, 0))
    hd = (A_HEADS, A_HEAD_DIM)
    grid_spec = pltpu.PrefetchScalarGridSpec(
        num_scalar_prefetch=1, grid=(s,),
        in_specs=[per_s(*hd), per_s(*hd), per_s(*hd), per_s(n_pages, PAGE_SIZE), per_s(1, LANES),
                  const(bias_pages.shape), const(bias_new.shape)]
                 + [page_spec(p) for p in range(n_pages)] * 2,
        out_specs=per_s(*hd))
    return pl.pallas_call(functools.partial(_dsa_s_attn_kernel, n_pages=n_pages), grid_spec=grid_spec,
                          out_shape=jax.ShapeDtypeStruct((s,) + hd, F32),
                          compiler_params=_cp("arbitrary"), name="dsa_sample_attn")(
                              pt, q, k_new, v_new, mask, mask_new, bias_pages, bias_new,
                              *([k_pool_t] * n_pages), *([v_pool_t] * n_pages))


def _rglru_gates(xc, wa, ba, wi, bi, lam):
    xb = xc.astype(wa.dtype)
    r = jax.nn.sigmoid(_mm(xb, wa) + ba)
    ig = jax.nn.sigmoid(_mm(xb, wi) + bi)
    log_a = -RG_C * r * jax.nn.softplus(-lam)
    a = jnp.exp(log_a)
    u = jnp.sqrt(1.0 - jnp.exp(2.0 * log_a)) * (ig * xc)
    return a, u


def _rglru_prompt_kernel(xr_ref, gr_ref, cw_ref, cb_ref, wa_ref, ba_ref, wi_ref, bi_ref, lam_ref,
                         y_ref, hl_ref, xs_sc, h_sc, *, tb):
    t = pl.program_id(1)
    pad = SUBLANES

    @pl.when(t == 0)
    def _():
        xs_sc[0:pad, :] = jnp.zeros((pad, xs_sc.shape[1]), F32)
        h_sc[...] = jnp.zeros(h_sc.shape, F32)

    xs_sc[pad:pad + tb, :] = xr_ref[0]
    xc = cb_ref[...] + cw_ref[RG_CONV - 1:RG_CONV, :] * xs_sc[pad:pad + tb, :]
    for j in range(1, RG_CONV):
        xc = xc + cw_ref[RG_CONV - 1 - j:RG_CONV - j, :] * xs_sc[pad - j:pad - j + tb, :]
    a, u = _rglru_gates(xc, wa_ref[...], ba_ref[...], wi_ref[...], bi_ref[...], lam_ref[...])
    rowi = lax.broadcasted_iota(I32, a.shape, 0)
    s = 1
    while s < tb:
        keep = rowi >= s
        a_sh = jnp.where(keep, pltpu.roll(a, s, 0), 1.0)
        u_sh = jnp.where(keep, pltpu.roll(u, s, 0), 0.0)
        u = a * u_sh + u
        a = a * a_sh
        s *= 2
    h = a * h_sc[0:1, :] + u
    y_ref[0] = (h * jax.nn.gelu(gr_ref[0])).astype(y_ref.dtype)
    h_sc[0:1, :] = h[tb - 1:tb, :]
    xs_sc[0:pad, :] = xs_sc[tb:tb + pad, :]

    @pl.when(t == pl.num_programs(1) - 1)
    def _():
        hl_ref[0] = h[tb - 1:tb, :]


def rglru_prompt(xr, gr, cw, cb, wa, ba, wi, bi, lam, tb_cap=256):
    bsz, t, w = xr.shape
    tb = _row_tile(t, tb_cap)
    blk = pl.BlockSpec((1, tb, w), lambda b, i: (b, i, 0))
    vec = _full((1, w))
    return pl.pallas_call(
        functools.partial(_rglru_prompt_kernel, tb=tb), grid=(bsz, t // tb),
        in_specs=[blk, blk, _full((RG_CONV, w)), vec, _full((w, w)), vec, _full((w, w)), vec, vec],
        out_specs=[blk, pl.BlockSpec((1, 1, w), lambda b, i: (b, 0, 0))],
        out_shape=[jax.ShapeDtypeStruct((bsz, t, w), BF16), jax.ShapeDtypeStruct((bsz, 1, w), F32)],
        scratch_shapes=[pltpu.VMEM((tb + SUBLANES, w), F32), pltpu.VMEM((SUBLANES, w), F32)],
        compiler_params=_cp("parallel", "arbitrary"), name="rglru_prompt")(
            xr, gr, cw, cb.reshape(1, w), wa, ba.reshape(1, w), wi, bi.reshape(1, w), lam.reshape(1, w))


def _rglru_step_kernel(xr_ref, gr_ref, b0_ref, b1_ref, b2_ref, h0_ref, cw_ref, cb_ref, wa_ref, ba_ref, wi_ref,
                       bi_ref, lam_ref, y_ref, h_ref):
    xc = (cb_ref[...] + cw_ref[0:1, :] * b0_ref[...] + cw_ref[1:2, :] * b1_ref[...]
          + cw_ref[2:3, :] * b2_ref[...] + cw_ref[3:4, :] * xr_ref[...])
    a, u = _rglru_gates(xc, wa_ref[...], ba_ref[...], wi_ref[...], bi_ref[...], lam_ref[...])
    h = a * h0_ref[...] + u
    h_ref[...] = h
    y_ref[...] = (h * jax.nn.gelu(gr_ref[...])).astype(y_ref.dtype)


def rglru_step(xr, gr, buf, h0, cw, cb, wa, ba, wi, bi, lam):
    s, w = xr.shape
    return pl.pallas_call(
        _rglru_step_kernel,
        out_shape=[jax.ShapeDtypeStruct((s, w), F32), jax.ShapeDtypeStruct((s, w), F32)],
        compiler_params=pltpu.CompilerParams(vmem_limit_bytes=VMEM_LIMIT), name="rglru_step")(
            xr, gr, buf[:, 0], buf[:, 1], buf[:, 2], h0, cw, cb.reshape(1, w), wa, ba.reshape(1, w),
            wi, bi.reshape(1, w), lam.reshape(1, w))


def _headnorm_gate(h, g, og):
    mu = jnp.mean(h, axis=-1, keepdims=True)
    hc = h - mu
    var = jnp.mean(hc * hc, axis=-1, keepdims=True)
    return hc * lax.rsqrt(var + LN_EPS) * g * jax.nn.sigmoid(og)


def _mlstm_prompt_kernel(q_ref, k_ref, v_ref, og_ref, gt_ref, bg_ref, ng_ref, tri_ref,
                         y_ref, c_ref, n_ref, m_ref, c_sc, n_sc, m_sc, *, cl, dh):
    ci = pl.program_id(1)

    @pl.when(ci == 0)
    def _():
        c_sc[...] = jnp.zeros(c_sc.shape, F32)
        n_sc[...] = jnp.zeros(n_sc.shape, F32)
        m_sc[...] = jnp.zeros(m_sc.shape, F32)

    pre = gt_ref[0] + bg_ref[...]
    lane = lax.broadcasted_iota(I32, pre.shape, 1)
    gv = jnp.where(lane < ML_HEADS, pre, jax.nn.log_sigmoid(pre))
    csum = jnp.dot(tri_ref[...], gv, preferred_element_type=F32, precision=HI)
    gv_t = gv.T
    csum_t = csum.T
    row = lax.broadcasted_iota(I32, (cl, cl), 0)
    col = lax.broadcasted_iota(I32, (cl, cl), 1)
    causal = col <= row
    kscale = dh ** -0.5
    for h in range(ML_HEADS):
        sl = slice(h * dh, (h + 1) * dh)
        qh = q_ref[0, :, sl]
        kh = k_ref[0, :, sl] * jnp.asarray(kscale, BF16)
        vh = v_ref[0, :, sl]
        b_col = csum[:, ML_HEADS + h:ML_HEADS + h + 1]
        i_col = gv[:, h:h + 1]
        b_row = csum_t[ML_HEADS + h:ML_HEADS + h + 1, :]
        i_row = gv_t[h:h + 1, :]
        m_prev = m_sc[h:h + 1, 0:1]
        c_prev = c_sc[h]
        n_prev = n_sc[h:h + 1, :]
        dmat = jnp.where(causal, b_col - b_row + i_row, NEG)
        inter = b_col + m_prev
        m_t = jnp.maximum(inter, jnp.max(dmat, axis=1, keepdims=True))
        g_inter = jnp.exp(inter - m_t)
        qk = lax.dot_general(qh, kh, NT, preferred_element_type=F32)
        ws = jnp.exp(dmat - m_t) * qk
        num = (g_inter * lax.dot_general(qh, c_prev.astype(BF16), NT, preferred_element_type=F32)
               + jnp.dot(ws.astype(BF16), vh, preferred_element_type=F32))
        qf = qh.astype(F32)
        den = g_inter * jnp.sum(qf * n_prev, axis=1, keepdims=True) + jnp.sum(ws, axis=1, keepdims=True)
        hout = num / jnp.maximum(jnp.abs(den), jnp.exp(-m_t))
        m_new = m_t[cl - 1:cl, :]
        b_last = b_col[cl - 1:cl, :]
        g_c = jnp.exp(b_last + m_prev - m_new)
        w_end = jnp.exp(b_last - b_col + i_col - m_new)
        kf = kh.astype(F32)
        vw = (vh.astype(F32) * w_end).astype(BF16)
        c_sc[h] = g_c * c_prev + lax.dot_general(vw, kh, TN, preferred_element_type=F32)
        n_sc[h:h + 1, :] = g_c * n_prev + jnp.sum(kf * w_end, axis=0, keepdims=True)
        m_sc[h:h + 1, :] = jnp.broadcast_to(m_new, (1, m_sc.shape[1]))
        y_ref[0, :, sl] = _headnorm_gate(hout, ng_ref[:, sl], og_ref[0, :, sl]).astype(y_ref.dtype)

    @pl.when(ci == pl.num_programs(1) - 1)
    def _():
        c_ref[0] = c_sc[...]
        n_ref[0] = n_sc[...]
        m_ref[0] = m_sc[...]


def mlstm_prompt(q, k, v, og, gates, bgate, norm_g):
    bsz, t, width = q.shape
    dh = width // ML_HEADS
    cl = math.gcd(t, ML_CHUNK)
    tri = (lax.broadcasted_iota(I32, (cl, cl), 1) <= lax.broadcasted_iota(I32, (cl, cl), 0)).astype(F32)
    blk = lambda w: pl.BlockSpec((1, cl, w), lambda b, i: (b, i, 0))
    return pl.pallas_call(
        functools.partial(_mlstm_prompt_kernel, cl=cl, dh=dh), grid=(bsz, t // cl),
        in_specs=[blk(width), blk(width), blk(width), blk(width), blk(LANES), _full((1, LANES)), _full((1, width)),
                  _full((cl, cl))],
        out_specs=[blk(width), pl.BlockSpec((1, ML_HEADS, dh, dh), lambda b, i: (b, 0, 0, 0)),
                   pl.BlockSpec((1, SUBLANES, dh), lambda b, i: (b, 0, 0)),
                   pl.BlockSpec((1, SUBLANES, LANES), lambda b, i: (b, 0, 0))],
        out_shape=[jax.ShapeDtypeStruct((bsz, t, width), BF16), jax.ShapeDtypeStruct((bsz, ML_HEADS, dh, dh), F32),
                   jax.ShapeDtypeStruct((bsz, SUBLANES, dh), F32), jax.ShapeDtypeStruct((bsz, SUBLANES, LANES), F32)],
        scratch_shapes=[pltpu.VMEM((ML_HEADS, dh, dh), F32), pltpu.VMEM((SUBLANES, dh), F32),
                        pltpu.VMEM((SUBLANES, LANES), F32)],
        compiler_params=_cp("parallel", "arbitrary"), name="mlstm_prompt")(
            q, k, v, og, gates, bgate, norm_g.reshape(1, width), tri)


def _mlstm_step_kernel(q_ref, k_ref, v_ref, og_ref, gt_ref, bg_ref, ng_ref, c0_ref, n0_ref, m0_ref,
                       y_ref, c_ref, n_ref, m_ref, *, sb, dh):
    kscale = dh ** -0.5
    m_ref[...] = jnp.zeros(m_ref.shape, F32)

    def seq_body(s, carry):
        rs = pl.ds(s, 1)
        pre = gt_ref[rs, :] + bg_ref[...]
        for h in range(ML_HEADS):
            sl = slice(h * dh, (h + 1) * dh)
            qh = q_ref[rs, sl]
            kh = k_ref[rs, sl] * kscale
            vh = v_ref[rs, sl]
            i_pre = pre[:, h:h + 1]
            logf = jax.nn.log_sigmoid(pre[:, ML_HEADS + h:ML_HEADS + h + 1])
            m_prev = m0_ref[rs, h:h + 1]
            c_prev = c0_ref[s, h]
            n_prev = n0_ref[s, h:h + 1, :]
            inter = logf + m_prev
            m_t = jnp.maximum(inter, i_pre)
            g_inter = jnp.exp(inter - m_t)
            w_in = jnp.exp(i_pre - m_t)
            qk = jnp.sum(qh * kh, axis=1, keepdims=True)
            q8 = jnp.broadcast_to(qh, (SUBLANES, dh))
            cq = lax.dot_general(q8, c_prev, NT, preferred_element_type=F32, precision=HI)[0:1]
            num = g_inter * cq + (w_in * qk) * vh
            den = g_inter * jnp.sum(n_prev * qh, axis=1, keepdims=True) + w_in * qk
            hout = num / jnp.maximum(jnp.abs(den), jnp.exp(-m_t))
            rsel = lax.broadcasted_iota(I32, (SUBLANES, dh), 0) == 0
            v8 = jnp.where(rsel, jnp.broadcast_to(w_in * vh, (SUBLANES, dh)), 0.0)
            k8 = jnp.broadcast_to(kh, (SUBLANES, dh))
            c_ref[s, h] = g_inter * c_prev + lax.dot_general(v8, k8, TN, preferred_element_type=F32, precision=HI)
            n_ref[s, h:h + 1, :] = g_inter * n_prev + w_in * kh
            m_ref[rs, h:h + 1] = m_t
            y_ref[rs, sl] = _headnorm_gate(hout, ng_ref[:, sl], og_ref[rs, sl]).astype(y_ref.dtype)
        return carry

    lax.fori_loop(0, sb, seq_body, 0)


def mlstm_step(q, k, v, og, gates, bgate, norm_g, c0, n0, m0, sb=SUBLANES):
    s, width = q.shape
    dh = width // ML_HEADS
    assert s % sb == 0
    row = lambda w: pl.BlockSpec((sb, w), lambda i: (i, 0))
    m0p = jnp.pad(m0, ((0, 0), (0, LANES - ML_HEADS)))
    outs = pl.pallas_call(
        functools.partial(_mlstm_step_kernel, sb=sb, dh=dh), grid=(s // sb,),
        in_specs=[row(width), row(width), row(width), row(width), row(LANES), _full((1, LANES)), _full((1, width)),
                  pl.BlockSpec((sb, ML_HEADS, dh, dh), lambda i: (i, 0, 0, 0)),
                  pl.BlockSpec((sb, ML_HEADS, dh), lambda i: (i, 0, 0)), row(LANES)],
        out_specs=[row(width), pl.BlockSpec((sb, ML_HEADS, dh, dh), lambda i: (i, 0, 0, 0)),
                   pl.BlockSpec((sb, ML_HEADS, dh), lambda i: (i, 0, 0)), row(LANES)],
        out_shape=[jax.ShapeDtypeStruct((s, width), F32), jax.ShapeDtypeStruct((s, ML_HEADS, dh, dh), F32),
                   jax.ShapeDtypeStruct((s, ML_HEADS, dh), F32), jax.ShapeDtypeStruct((s, LANES), F32)],
        compiler_params=_cp("parallel"), name="mlstm_step")(
            q, k, v, og, gates, bgate, norm_g.reshape(1, width), c0, n0, m0p)
    y, c, n, m = outs
    return y, c, n, m[:, :ML_HEADS]


def _xattn_prompt_kernel(q_ref, k_ref, v_ref, o_ref, *, dh):
    scale = dh ** -0.5
    for h in range(X_HEADS):
        sl = slice(h * dh, (h + 1) * dh)
        lg = lax.dot_general(q_ref[0, :, sl], k_ref[0, :, sl], NT, preferred_element_type=F32) * scale
        mx = jnp.max(lg, axis=-1, keepdims=True)
        e = jnp.exp(lg - mx)
        pr = e / jnp.sum(e, axis=-1, keepdims=True)
        o_ref[0, :, sl] = jnp.dot(pr.astype(BF16), v_ref[0, :, sl], preferred_element_type=F32).astype(o_ref.dtype)


def xattn_prompt(xq, mk, mv, tq_cap=512):
    bsz, t, d = xq.shape
    mt = mk.shape[1]
    tq = _row_tile(t, tq_cap)
    return pl.pallas_call(
        functools.partial(_xattn_prompt_kernel, dh=d // X_HEADS), grid=(bsz, t // tq),
        in_specs=[pl.BlockSpec((1, tq, d), lambda b, i: (b, i, 0)), pl.BlockSpec((1, mt, d), lambda b, i: (b, 0, 0)),
                  pl.BlockSpec((1, mt, d), lambda b, i: (b, 0, 0))],
        out_specs=pl.BlockSpec((1, tq, d), lambda b, i: (b, i, 0)),
        out_shape=jax.ShapeDtypeStruct((bsz, t, d), BF16),
        compiler_params=_cp("parallel", "parallel"), name="xattn_prompt")(xq, mk, mv)


def _xattn_step_kernel(q_ref, k_ref, v_ref, o_ref, *, sb, mt, dh):
    scale = dh ** -0.5
    parts = dh // LANES
    rows = X_HEADS * parts

    def seq_body(s, carry):
        rs = pl.ds(s, 1)
        qrow = q_ref[rs, :]
        pieces = []
        for h in range(X_HEADS):
            lg = jnp.zeros((mt, 1), F32)
            for c in range(parts):
                qp = qrow[:, h * dh + c * LANES:h * dh + (c + 1) * LANES]
                kp = k_ref[0, s, pl.ds(c * X_HEADS + h, mt, stride=rows), :]
                lg = lg + jnp.sum(kp * qp, axis=1, keepdims=True)
            lg = lg * scale
            e = jnp.exp(lg - jnp.max(lg, axis=0, keepdims=True))
            pr = e / jnp.sum(e, axis=0, keepdims=True)
            for c in range(parts):
                vp = v_ref[0, s, pl.ds(c * X_HEADS + h, mt, stride=rows), :]
                pieces.append(jnp.sum(pr * vp, axis=0, keepdims=True))
        o_ref[rs, :] = jnp.concatenate(pieces, axis=1)
        return carry

    lax.fori_loop(0, sb, seq_body, 0)


def _lane_chunk_view(mem):
    nl, s, mt, nh, dh = mem.shape
    v = mem.reshape(nl, s, mt, nh, dh // LANES, LANES)
    return jnp.transpose(v, (0, 1, 2, 4, 3, 5)).reshape(nl, s, mt * (dh // LANES) * nh, LANES)


def xattn_step(xq, mk_view, mv_view, layer, mt, sb=SUBLANES):
    s, d = xq.shape
    dh = d // X_HEADS
    assert s % sb == 0 and dh % LANES == 0
    nrows = mk_view.shape[2]
    mem_spec = pl.BlockSpec((1, sb, nrows, LANES), lambda i: (layer, i, 0, 0))
    return pl.pallas_call(
        functools.partial(_xattn_step_kernel, sb=sb, mt=mt, dh=dh), grid=(s // sb,),
        in_specs=[pl.BlockSpec((sb, d), lambda i: (i, 0)), mem_spec, mem_spec],
        out_specs=pl.BlockSpec((sb, d), lambda i: (i, 0)),
        out_shape=jax.ShapeDtypeStruct((s, d), F32),
        compiler_params=_cp("parallel"), name="xattn_step")(xq, mk_view, mv_view)


def _t5_bucket(dist):
    max_exact = REL_BUCKETS // 2
    d = jnp.maximum(dist.astype(F32), 1.0)
    large = max_exact + (jnp.log(d / max_exact) / math.log(REL_MAX_DIST / max_exact)
                         * (REL_BUCKETS - max_exact)).astype(I32)
    large = jnp.minimum(large, REL_BUCKETS - 1)
    return jnp.where(dist < max_exact, dist, large)


def _pad_cols(w, n):
    return jnp.pad(w, ((0, 0), (0, n - w.shape[1])))


def _block_diag(w):
    nb, di, do = w.shape
    eye = jnp.eye(nb, dtype=w.dtype)
    return jnp.einsum('nde,nm->ndme', w, eye).reshape(nb * di, nb * do)


def kernel(x_prompt, x_sample, cache_k, cache_v, cache_kidx, state_rglru_h, state_rglru_conv, state_mlstm_c,
           state_mlstm_n, state_mlstm_m, cache_mem_k, cache_mem_v, page_table, mem_prompt, ln_g, ln_b, rel_bias,
           w_in_even, w_out_even, rg_conv_w, rg_conv_b, rg_w_a, rg_b_a, rg_w_i, rg_b_i, rg_lambda, w_in_odd,
           b_gate_odd, ml_norm_g, w_out_odd, w_xq, w_xk, w_xv, w_xo, ffn_w_gate, ffn_w_up, ffn_w_down,
           moe_router, moe_w_gate, moe_w_up, moe_w_down):
    bsz, t, d = x_prompt.shape
    s = x_sample.shape[0]
    depth = ln_g.shape[0]
    n_pages = page_table.shape[1]
    past_len = n_pages * PAGE_SIZE
    mt = mem_prompt.shape[1]
    rg_w = rg_lambda.shape[1]
    ml_w = ml_norm_g.shape[1]
    n_experts = moe_router.shape[2]
    alpha = (2.0 * depth) ** 0.25
    bf = lambda a: a.astype(BF16)

    def bias_at(dist):
        onehot = jax.nn.one_hot(_t5_bucket(dist), REL_BUCKETS, dtype=F32)
        return jnp.einsum('...b,bh->...h', onehot, rel_bias.astype(F32), precision=HI)

    kk_ = lax.broadcasted_iota(I32, (TQ, TQ), 0)
    qq_ = lax.broadcasted_iota(I32, (TQ, TQ), 1)
    dist3 = jnp.stack([jnp.maximum(qq_ - kk_, 0), TQ + qq_ - kk_, jnp.full((TQ, TQ), 2 * TQ, I32)])
    bias_tab = jnp.transpose(bias_at(dist3), (0, 3, 1, 2))
    dist_s = (past_len - jnp.arange(past_len, dtype=I32)).reshape(n_pages, PAGE_SIZE)
    bias_s = jnp.transpose(bias_at(dist_s), (0, 2, 1))
    bias_s_new = jnp.broadcast_to(bias_at(jnp.zeros((1,), I32)).reshape(A_HEADS, 1), (A_HEADS, LANES))
    mem_k_view, mem_v_view = _lane_chunk_view(cache_mem_k), _lane_chunk_view(cache_mem_v)

    mem_bf = bf(mem_prompt.reshape(bsz * mt, d))
    kv_w = [bf(w_xk[l]) for l in range(depth)] + [bf(w_xv[l]) for l in range(depth)]
    kv = multi_proj(mem_bf, kv_w, [(F32, BF16)] * (2 * depth))
    mem_k_f32 = [kv[2 * l] for l in range(depth)]
    mem_k_bf = [kv[2 * l + 1].reshape(bsz, mt, d) for l in range(depth)]
    mem_v_f32 = [kv[2 * depth + 2 * l] for l in range(depth)]
    mem_v_bf = [kv[2 * depth + 2 * l + 1].reshape(bsz, mt, d) for l in range(depth)]

    xp = x_prompt.reshape(bsz * t, d)
    xs = x_sample.reshape(s, d)
    xp_bf, xs_bf = bf(xp), bf(xs)
    outs = {}
    k_p, v_p, ki_p, k_s, v_s, ki_s = [], [], [], [], [], []
    rgh_p, rgc_p, rgh_s, rgc_s = [], [], [], []
    mlc_p, mln_p, mlm_p, mlc_s, mln_s, mlm_s = [], [], [], [], [], []

    for l in range(depth):
        if l % 2 == 0:
            e = l // 2
            w_in = w_in_even[e]
            o = 0
            cols = {}
            for name, width in (("q", A_WIDTH), ("k", A_WIDTH), ("v", A_WIDTH), ("qi", IDX_HEADS * IDX_DIM),
                                ("ki", IDX_DIM), ("wi", IDX_HEADS), ("xr", rg_w), ("gr", rg_w)):
                cols[name] = bf(w_in[:, o:o + width])
                o += width
            cols["wi"] = _pad_cols(cols["wi"], LANES)
            names = ("q", "k", "v", "qi", "ki", "wi", "xr", "gr")
            wq_t, wv_t, wqi_t = cols["q"].T, cols["v"].T, cols["qi"].T
            wi_t = jnp.pad(cols["wi"][:, :IDX_HEADS].T, ((0, SUBLANES - IDX_HEADS), (0, 0)))
            wa = bf(_block_diag(rg_w_a[e]))
            wi_g = bf(_block_diag(rg_w_i[e]))
            w_out = bf(w_out_even[e])

            nq = t // TQ
            pnames = ("k", "v", "ki", "xr", "gr")
            kf, kb, vf, kif, kib, xr, gr, qt, vt, qit, wit = multi_proj(
                xp_bf, [cols[n] for n in pnames] + [wq_t, wv_t, wqi_t, wi_t],
                [(F32, BF16), (F32,), (F32, BF16), (F32,), (F32,), (BF16,), (BF16,), (BF16,), (F32,)],
                transposed=(False,) * len(pnames) + (True,) * 4)
            r3 = lambda a: a.reshape(bsz, t, a.shape[-1])
            r4 = lambda a: a.reshape(bsz, nq, a.shape[-2], TQ)
            att = dsa_prompt(r4(qt), r3(kb), r4(vt), r4(qit), r3(kib), r4(wit), bias_tab)
            rg_y, h_last = rglru_prompt(r3(xr), r3(gr), rg_conv_w[e], rg_conv_b[e], wa, rg_b_a[e], wi_g, rg_b_i[e],
                                        rg_lambda[e])
            xp, xp_bf = mm_res_ln([att.reshape(bsz * t, A_WIDTH), rg_y.reshape(bsz * t, rg_w)],
                                  [w_out[:A_WIDTH], w_out[A_WIDTH:]], xp, ln_g[l, 0], ln_b[l, 0], alpha)
            k_p.append(kf.reshape(bsz, t, A_HEADS, A_HEAD_DIM))
            v_p.append(vf.reshape(bsz, t, A_HEADS, A_HEAD_DIM))
            ki_p.append(kif.reshape(bsz, t, IDX_DIM))
            rgh_p.append(h_last.reshape(bsz, rg_w))
            rgc_p.append(r3(xr)[:, t - (RG_CONV - 1):, :])

            q, kf, vf, qi, kif, wi, xr, gr = multi_proj(xs_bf, [cols[n] for n in names], [(F32,)] * len(names))
            qi8 = jnp.pad(qi.reshape(s, IDX_HEADS, IDX_DIM), ((0, 0), (0, SUBLANES - IDX_HEADS), (0, 0)))
            pool = cache_k.shape[1]
            kidx_t = jnp.transpose(cache_kidx[e], (0, 2, 1))
            k_t = jnp.transpose(cache_k[e], (0, 2, 3, 1)).reshape(pool, A_WIDTH, PAGE_SIZE)
            v_t = jnp.transpose(cache_v[e], (0, 2, 3, 1)).reshape(pool, A_WIDTH, PAGE_SIZE)
            raw = dsa_sample_scores(page_table, qi8, wi.reshape(s, 1, LANES), kidx_t)
            mask, mask_new = dsa_sample_select(raw.reshape(s, past_len), qi, kif, wi)
            hd3 = lambda a: a.reshape(s, A_HEADS, A_HEAD_DIM)
            att = dsa_sample_attn(page_table, hd3(q), hd3(kf), hd3(vf), mask.reshape(s, n_pages, PAGE_SIZE),
                                  mask_new.reshape(s, 1, LANES), bias_s, bias_s_new, k_t, v_t)
            rg_y, h_new = rglru_step(xr, gr, state_rglru_conv[e], state_rglru_h[e], rg_conv_w[e], rg_conv_b[e],
                                     wa, rg_b_a[e], wi_g, rg_b_i[e], rg_lambda[e])
            xs, xs_bf = mm_res_ln([bf(att.reshape(s, A_WIDTH)), bf(rg_y)], [w_out[:A_WIDTH], w_out[A_WIDTH:]], xs,
                                  ln_g[l, 0], ln_b[l, 0], alpha)
            k_s.append(kf.reshape(s, 1, A_HEADS, A_HEAD_DIM))
            v_s.append(vf.reshape(s, 1, A_HEADS, A_HEAD_DIM))
            ki_s.append(kif.reshape(s, 1, IDX_DIM))
            rgh_s.append(h_new)
            rgc_s.append(jnp.concatenate([state_rglru_conv[e][:, 1:], xr[:, None, :]], axis=1))
        else:
            o_ = l // 2
            w_in = w_in_odd[o_]
            ws = [bf(w_in[:, i * ml_w:(i + 1) * ml_w]) for i in range(4)]
            ws.append(_pad_cols(bf(w_in[:, 4 * ml_w:]), LANES))
            bgate = jnp.pad(b_gate_odd[o_].astype(F32), (0, LANES - 2 * ML_HEADS)).reshape(1, LANES)
            w_out = bf(w_out_odd[o_])

            q, k, v, og, gates = multi_proj(xp_bf, ws, [(BF16,), (BF16,), (BF16,), (F32,), (F32,)])
            r3 = lambda a: a.reshape(bsz, t, a.shape[-1])
            hn, c, n, m = mlstm_prompt(r3(q), r3(k), r3(v), r3(og), r3(gates), bgate, ml_norm_g[o_])
            xp, xp_bf = mm_res_ln([hn.reshape(bsz * t, ml_w)], [w_out], xp, ln_g[l, 0], ln_b[l, 0], alpha)
            mlc_p.append(c)
            mln_p.append(n[:, :ML_HEADS])
            mlm_p.append(m[:, :ML_HEADS, 0])

            q, k, v, og, gates = multi_proj(xs_bf, ws, [(F32,)] * len(ws))
            hn, c, n, m = mlstm_step(q, k, v, og, gates, bgate, ml_norm_g[o_], state_mlstm_c[o_], state_mlstm_n[o_],
                                     state_mlstm_m[o_])
            xs, xs_bf = mm_res_ln([bf(hn)], [w_out], xs, ln_g[l, 0], ln_b[l, 0], alpha)
            mlc_s.append(c)
            mln_s.append(n)
            mlm_s.append(m)

        wq, wo = bf(w_xq[l]), bf(w_xo[l])
        (xq,) = multi_proj(xp_bf, [wq], [(BF16,)])
        xo = xattn_prompt(xq.reshape(bsz, t, d), mem_k_bf[l], mem_v_bf[l])
        xp, xp_bf = mm_res_ln([xo.reshape(bsz * t, d)], [wo], xp, ln_g[l, 1], ln_b[l, 1], alpha)
        (xq,) = multi_proj(xs_bf, [wq], [(F32,)])
        xo = xattn_step(xq, mem_k_view, mem_v_view, l, mt)
        xs, xs_bf = mm_res_ln([bf(xo)], [wo], xs, ln_g[l, 1], ln_b[l, 1], alpha)

        if l % 2 == 0:
            e = l // 2
            wg, wu, wd = bf(ffn_w_gate[e]), bf(ffn_w_up[e]), bf(ffn_w_down[e])
            xp, xp_bf = ffn_ln(xp_bf, xp, wg, wu, wd, ln_g[l, 2], ln_b[l, 2], alpha)
            xs, xs_bf = ffn_ln(xs_bf, xs, wg, wu, wd, ln_g[l, 2], ln_b[l, 2], alpha)
        else:
            o_ = l // 2
            wr = _pad_cols(bf(moe_router[o_]), LANES)
            wg, wu, wd = bf(moe_w_gate[o_]), bf(moe_w_up[o_]), bf(moe_w_down[o_])
            gp = router_gates(xp_bf, wr, n_experts)
            xp, xp_bf = moe_ln(xp_bf, xp, gp, wg, wu, wd, ln_g[l, 2], ln_b[l, 2], alpha)
            gs = router_gates(xs_bf, wr, n_experts)
            xs, xs_bf = moe_ln(xs_bf, xs, gs, wg, wu, wd, ln_g[l, 2], ln_b[l, 2], alpha)

    xh = mem_k_f32[0].shape[-1] // X_HEADS
    mem_shape = (bsz, mt, X_HEADS, xh)
    return (xp.reshape(bsz, t, d), xs.reshape(s, 1, d),
            jnp.stack(k_p), jnp.stack(v_p), jnp.stack(ki_p), jnp.stack(k_s), jnp.stack(v_s), jnp.stack(ki_s),
            jnp.stack(rgh_p), jnp.stack(rgc_p), jnp.stack(rgh_s), jnp.stack(rgc_s),
            jnp.stack(mlc_p), jnp.stack(mln_p), jnp.stack(mlm_p), jnp.stack(mlc_s), jnp.stack(mln_s), jnp.stack(mlm_s),
            jnp.stack([a.reshape(mem_shape) for a in mem_k_f32]), jnp.stack([a.reshape(mem_shape) for a in mem_v_f32]))
```

```python
import functools
import math

import jax
import jax.numpy as jnp
from jax import lax
from jax.experimental import pallas as pl
from jax.experimental.pallas import tpu as pltpu

F32 = jnp.float32
BF16 = jnp.bfloat16
I32 = jnp.int32

PAGE_SIZE = 128
A_HEADS = 8
A_HEAD_DIM = 64
A_WIDTH = A_HEADS * A_HEAD_DIM
IDX_HEADS = 4
IDX_DIM = 64
TOPK_MAX = 256
REL_BUCKETS = 32
REL_MAX_DIST = 128
RG_BLOCKS = 8
RG_CONV = 4
RG_C = 8.0
ML_HEADS = 4
ML_CHUNK = 128
X_HEADS = 4
TOP_K = 2
LN_EPS = 1e-5

LANES = 128
SUBLANES = 8
VMEM_LIMIT = 48 * 1024 * 1024
VMEM_LIMIT_RESIDENT = 56 * 1024 * 1024
NEG = -1e30
INT_MIN = -2 ** 31
TQ = 128

NT = (((1,), (1,)), ((), ()))
TN = (((0,), (0,)), ((), ()))
HI = lax.Precision.HIGHEST


def _cp(*sem):
    return pltpu.CompilerParams(dimension_semantics=sem, vmem_limit_bytes=VMEM_LIMIT)


def _full(shape):
    n = len(shape)
    return pl.BlockSpec(shape, lambda *_: (0,) * n)


def _row_tile(m, cap):
    t = min(m, cap)
    assert m % t == 0
    return t


def _as_mxu_operand(a):
    return a.astype(BF16).astype(F32)


def _mm(a, b, dims=None):
    prec = HI if a.dtype == F32 else None
    if dims is None:
        return jnp.dot(a, b, preferred_element_type=F32, precision=prec)
    return lax.dot_general(a, b, dims, preferred_element_type=F32, precision=prec)


def _proj_kernel(x_ref, *refs, n_w, out_counts, transposed):
    w_refs, o_refs = refs[:n_w], refs[n_w:]
    x = x_ref[...]
    k = 0
    for w_ref, cnt, tr in zip(w_refs, out_counts, transposed):
        if tr:
            y = _mm(w_ref[...], x, NT)
        else:
            y = _mm(x, w_ref[...])
        for _ in range(cnt):
            if tr:
                for j in range(y.shape[1] // LANES):
                    o_refs[k][j] = y[:, j * LANES:(j + 1) * LANES].astype(o_refs[k].dtype)
            else:
                o_refs[k][...] = y.astype(o_refs[k].dtype)
            k += 1


def multi_proj(x, ws, out_dtypes, transposed=None, tm_cap=512):
    m, kdim = x.shape
    tm = _row_tile(m, tm_cap)
    transposed = transposed or (False,) * len(ws)
    in_specs = [pl.BlockSpec((tm, kdim), lambda i: (i, 0))] + [_full(w.shape) for w in ws]
    out_shape, out_specs = [], []
    for w, dts, tr in zip(ws, out_dtypes, transposed):
        for dt in dts:
            if tr:
                assert tm % LANES == 0
                out_shape.append(jax.ShapeDtypeStruct((m // LANES, w.shape[0], LANES), dt))
                out_specs.append(pl.BlockSpec((tm // LANES, w.shape[0], LANES), lambda i: (i, 0, 0)))
            else:
                out_shape.append(jax.ShapeDtypeStruct((m, w.shape[1]), dt))
                out_specs.append(pl.BlockSpec((tm, w.shape[1]), lambda i: (i, 0)))
    kern = functools.partial(_proj_kernel, n_w=len(ws), out_counts=tuple(len(d) for d in out_dtypes),
                             transposed=tuple(transposed))
    return pl.pallas_call(kern, grid=(m // tm,), in_specs=in_specs, out_specs=out_specs,
                          out_shape=out_shape, compiler_params=_cp("parallel"), name="multi_proj")(x, *ws)


def _layernorm_rows(z, g, b):
    mu = jnp.mean(z, axis=-1, keepdims=True)
    zc = z - mu
    var = jnp.mean(zc * zc, axis=-1, keepdims=True)
    return zc * lax.rsqrt(var + LN_EPS) * g + b


def _mm_ln_kernel(*refs, n_a, alpha):
    a_refs, w_refs = refs[:n_a], refs[n_a:2 * n_a]
    r_ref, g_ref, b_ref, o_ref, obf_ref = refs[2 * n_a:]
    y = alpha * r_ref[...]
    for a_ref, w_ref in zip(a_refs, w_refs):
        y = y + _mm(a_ref[...], w_ref[...])
    out = _layernorm_rows(y, g_ref[...], b_ref[...])
    o_ref[...] = out
    obf_ref[...] = out.astype(BF16)


def mm_res_ln(a_list, w_list, resid, g, b, alpha, tm_cap=512):
    m, d = resid.shape
    tm = _row_tile(m, tm_cap)
    in_specs = ([pl.BlockSpec((tm, a.shape[1]), lambda i: (i, 0)) for a in a_list]
                + [_full(w.shape) for w in w_list]
                + [pl.BlockSpec((tm, d), lambda i: (i, 0)), _full((1, d)), _full((1, d))])
    out_specs = [pl.BlockSpec((tm, d), lambda i: (i, 0))] * 2
    out_shape = [jax.ShapeDtypeStruct((m, d), F32), jax.ShapeDtypeStruct((m, d), BF16)]
    kern = functools.partial(_mm_ln_kernel, n_a=len(a_list), alpha=alpha)
    return pl.pallas_call(kern, grid=(m // tm,), in_specs=in_specs, out_specs=out_specs, out_shape=out_shape,
                          compiler_params=_cp("parallel"), name="mm_res_ln")(
                              *a_list, *w_list, resid, g.reshape(1, d), b.reshape(1, d))


def _ffn_kernel(x_ref, r_ref, wg_ref, wu_ref, wd_ref, g_ref, b_ref, o_ref, obf_ref, acc_sc, *, alpha):
    f = pl.program_id(1)

    @pl.when(f == 0)
    def _():
        acc_sc[...] = alpha * r_ref[...]

    x = x_ref[...]
    hg = _mm(x, wg_ref[...])
    hu = _mm(x, wu_ref[...])
    h = (jax.nn.silu(hg) * hu).astype(x.dtype)
    acc_sc[...] += _mm(h, wd_ref[...])

    @pl.when(f == pl.num_programs(1) - 1)
    def _():
        out = _layernorm_rows(acc_sc[...], g_ref[...], b_ref[...])
        o_ref[...] = out
        obf_ref[...] = out.astype(BF16)


def ffn_ln(x_bf, resid, wg, wu, wd, g, b, alpha, tm_cap=512, tf=256):
    m, d = resid.shape
    dff = wg.shape[1]
    tm = _row_tile(m, tm_cap)
    assert dff % tf == 0
    in_specs = [pl.BlockSpec((tm, d), lambda i, f: (i, 0)), pl.BlockSpec((tm, d), lambda i, f: (i, 0)),
                pl.BlockSpec((d, tf), lambda i, f: (0, f)), pl.BlockSpec((d, tf), lambda i, f: (0, f)),
                pl.BlockSpec((tf, d), lambda i, f: (f, 0)), _full((1, d)), _full((1, d))]
    out_specs = [pl.BlockSpec((tm, d), lambda i, f: (i, 0))] * 2
    out_shape = [jax.ShapeDtypeStruct((m, d), F32), jax.ShapeDtypeStruct((m, d), BF16)]
    return pl.pallas_call(functools.partial(_ffn_kernel, alpha=alpha), grid=(m // tm, dff // tf),
                          in_specs=in_specs, out_specs=out_specs, out_shape=out_shape,
                          scratch_shapes=[pltpu.VMEM((tm, d), F32)],
                          compiler_params=_cp("parallel", "arbitrary"), name="ffn_ln")(
                              x_bf, resid, wg, wu, wd, g.reshape(1, d), b.reshape(1, d))


def _top2(logits, n_experts):
    lane = lax.broadcasted_iota(I32, logits.shape, 1)
    valid = lane < n_experts
    logits = jnp.where(valid, logits, NEG)
    mx = jnp.max(logits, axis=-1, keepdims=True)
    e = jnp.where(valid, jnp.exp(logits - mx), 0.0)
    p = e / jnp.sum(e, axis=-1, keepdims=True)
    p = jnp.where(valid, p, -1.0)
    big = jnp.int32(LANES)
    m1 = jnp.max(p, axis=-1, keepdims=True)
    i1 = jnp.min(jnp.where(p == m1, lane, big), axis=-1, keepdims=True)
    p2 = jnp.where(lane == i1, -1.0, p)
    m2 = jnp.max(p2, axis=-1, keepdims=True)
    i2 = jnp.min(jnp.where(p2 == m2, lane, big), axis=-1, keepdims=True)
    tot = m1 + m2
    return lane, i1, i2, m1 / tot, m2 / tot


def _router_kernel(x_ref, w_ref, o_ref, *, n_experts):
    lane, i1, i2, g1, g2 = _top2(_mm(x_ref[...], w_ref[...]), n_experts)
    o_ref[...] = jnp.where(lane == i1, g1, jnp.where(lane == i2, g2, 0.0))


def router_gates(x, w_pad, n_experts, tm_cap=512):
    m, d = x.shape
    tm = _row_tile(m, tm_cap)
    return pl.pallas_call(functools.partial(_router_kernel, n_experts=n_experts), grid=(m // tm,),
                          in_specs=[pl.BlockSpec((tm, d), lambda i: (i, 0)), _full(w_pad.shape)],
                          out_specs=pl.BlockSpec((tm, LANES), lambda i: (i, 0)),
                          out_shape=jax.ShapeDtypeStruct((m, LANES), F32),
                          compiler_params=_cp("parallel"), name="router")(x, w_pad)


def _moe_kernel(x_ref, r_ref, gt_ref, wg_ref, wu_ref, wd_ref, g_ref, b_ref, o_ref, obf_ref, acc_sc, *, alpha):
    e = pl.program_id(1)
    f = pl.program_id(2)

    @pl.when((e == 0) & (f == 0))
    def _():
        acc_sc[...] = alpha * r_ref[...]

    gt = gt_ref[...]
    lane = lax.broadcasted_iota(I32, gt.shape, 1)
    ge = jnp.sum(jnp.where(lane == e, gt, 0.0), axis=-1, keepdims=True)
    x = x_ref[...]
    hg = jnp.dot(x, wg_ref[0], preferred_element_type=F32)
    hu = jnp.dot(x, wu_ref[0], preferred_element_type=F32)
    h = (jax.nn.silu(hg) * hu).astype(BF16)
    acc_sc[...] += ge * jnp.dot(h, wd_ref[0], preferred_element_type=F32)

    @pl.when((e == pl.num_programs(1) - 1) & (f == pl.num_programs(2) - 1))
    def _():
        out = _layernorm_rows(acc_sc[...], g_ref[...], b_ref[...])
        o_ref[...] = out
        obf_ref[...] = out.astype(BF16)


def moe_ln(x_bf, resid, gates, wg, wu, wd, g, b, alpha, tm_cap=512, tf=512):
    m, d = resid.shape
    ne, _, dff = wg.shape
    tm = _row_tile(m, tm_cap)
    assert dff % tf == 0
    in_specs = [pl.BlockSpec((tm, d), lambda i, e, f: (i, 0)), pl.BlockSpec((tm, d), lambda i, e, f: (i, 0)),
                pl.BlockSpec((tm, LANES), lambda i, e, f: (i, 0)),
                pl.BlockSpec((1, d, tf), lambda i, e, f: (e, 0, f)), pl.BlockSpec((1, d, tf), lambda i, e, f: (e, 0, f)),
                pl.BlockSpec((1, tf, d), lambda i, e, f: (e, f, 0)), _full((1, d)), _full((1, d))]
    out_specs = [pl.BlockSpec((tm, d), lambda i, e, f: (i, 0))] * 2
    out_shape = [jax.ShapeDtypeStruct((m, d), F32), jax.ShapeDtypeStruct((m, d), BF16)]
    return pl.pallas_call(functools.partial(_moe_kernel, alpha=alpha), grid=(m // tm, ne, dff // tf),
                          in_specs=in_specs, out_specs=out_specs, out_shape=out_shape,
                          scratch_shapes=[pltpu.VMEM((tm, d), F32)],
                          compiler_params=_cp("parallel", "arbitrary", "arbitrary"), name="moe_ln")(
                              x_bf, resid, gates, wg, wu, wd, g.reshape(1, d), b.reshape(1, d))


MOE_TILE = 512
COMBINE_SLACK = LANES
ROUTE_LANES = ("e1", "e2", "g1", "g2", "rank1", "rank2")


def _router_plan_kernel(x_ref, w_ref, ltri_ref, route_ref, before_ref, count_ref, carry_sc, *, n_experts):
    @pl.when(pl.program_id(0) == 0)
    def _():
        carry_sc[...] = jnp.zeros(carry_sc.shape, F32)

    lane, i1, i2, g1, g2 = _top2(_mm(x_ref[...], w_ref[...]), n_experts)
    chosen = jnp.where(lane == i1, 1.0, jnp.where(lane == i2, 1.0, 0.0))
    before = carry_sc[0:1, :]
    rank = before + jnp.dot(ltri_ref[...], chosen.astype(BF16), preferred_element_type=F32)
    r1 = jnp.sum(jnp.where(lane == i1, rank, 0.0), axis=-1, keepdims=True)
    r2 = jnp.sum(jnp.where(lane == i2, rank, 0.0), axis=-1, keepdims=True)
    vals = (i1.astype(F32), i2.astype(F32), g1, g2, r1, r2)
    route = jnp.zeros(lane.shape, F32)
    for k, v in enumerate(vals):
        route = jnp.where(lane == k, v, route)
    route_ref[...] = route
    before_ref[0] = jnp.broadcast_to(before, before_ref.shape[1:])
    after = before + jnp.sum(chosen, axis=0, keepdims=True)
    carry_sc[0:1, :] = after
    count_ref[...] = jnp.broadcast_to(after, count_ref.shape)


def router_plan(x_bf, w_pad, n_experts):
    m, d = x_bf.shape
    tb = MOE_TILE
    assert m % tb == 0
    ltri = (lax.broadcasted_iota(I32, (tb, tb), 1) < lax.broadcasted_iota(I32, (tb, tb), 0)).astype(BF16)
    return pl.pallas_call(
        functools.partial(_router_plan_kernel, n_experts=n_experts), grid=(m // tb,),
        in_specs=[pl.BlockSpec((tb, d), lambda i: (i, 0)), _full(w_pad.shape), _full((tb, tb))],
        out_specs=[pl.BlockSpec((tb, LANES), lambda i: (i, 0)), pl.BlockSpec((1, SUBLANES, LANES), lambda i: (i, 0, 0)),
                   _full((SUBLANES, LANES))],
        out_shape=[jax.ShapeDtypeStruct((m, LANES), F32), jax.ShapeDtypeStruct((m // tb, SUBLANES, LANES), F32),
                   jax.ShapeDtypeStruct((SUBLANES, LANES), F32)],
        scratch_shapes=[pltpu.VMEM((SUBLANES, LANES), F32)],
        compiler_params=_cp("arbitrary"), name="router_plan")(x_bf, w_pad, ltri)


def _moe_gather_kernel(blo_ref, bhi_ref, x_ref, pos_ref, g_ref, xs_ref, gs_ref, acc_sc, gacc_sc, *, tile):
    j = pl.program_id(0)
    acc_sc[...] = jnp.zeros(acc_sc.shape, F32)
    gacc_sc[...] = jnp.zeros(gacc_sc.shape, F32)
    rid = j * tile + lax.broadcasted_iota(I32, (tile, tile), 0)
    glane = lax.broadcasted_iota(I32, gacc_sc.shape, 1)

    def body(b, carry):
        off = pl.multiple_of(b * tile, tile)
        m1 = jnp.where(pos_ref[b, 0:1, :] == rid, 1.0, 0.0).astype(BF16)
        m2 = jnp.where(pos_ref[b, 1:2, :] == rid, 1.0, 0.0).astype(BF16)
        acc_sc[...] += jnp.dot(m1 + m2, x_ref[pl.ds(off, tile), :], preferred_element_type=F32)
        gb = g_ref[pl.ds(off, tile), :]
        gacc_sc[...] += jnp.where(glane < 3, jnp.dot(m1, gb, preferred_element_type=F32),
                                  jnp.dot(m2, gb, preferred_element_type=F32))
        return carry

    lax.fori_loop(blo_ref[j], bhi_ref[j] + 1, body, 0)
    xs_ref[...] = acc_sc[...].astype(xs_ref.dtype)
    gs_ref[...] = gacc_sc[...]


def moe_gather(x_bf, pos, gsplit, b_lo, b_hi, n_tiles):
    m, d = x_bf.shape
    tile = MOE_TILE
    resident = lambda shape: pl.BlockSpec(shape, lambda j, lo, hi: (0,) * len(shape), pipeline_mode=pl.Buffered(1))
    grid_spec = pltpu.PrefetchScalarGridSpec(
        num_scalar_prefetch=2, grid=(n_tiles,),
        in_specs=[resident((m, d)), resident(pos.shape), resident(gsplit.shape)],
        out_specs=[pl.BlockSpec((tile, d), lambda j, lo, hi: (j, 0)),
                   pl.BlockSpec((tile, LANES), lambda j, lo, hi: (j, 0))],
        scratch_shapes=[pltpu.VMEM((tile, d), F32), pltpu.VMEM((tile, LANES), F32)])
    return pl.pallas_call(
        functools.partial(_moe_gather_kernel, tile=tile), grid_spec=grid_spec,
        out_shape=[jax.ShapeDtypeStruct((n_tiles * tile, d), BF16),
                   jax.ShapeDtypeStruct((n_tiles * tile, LANES), F32)],
        compiler_params=pltpu.CompilerParams(dimension_semantics=("arbitrary",), vmem_limit_bytes=VMEM_LIMIT_RESIDENT),
        name="moe_gather")(b_lo, b_hi, x_bf, pos, gsplit)


def _moe_group_ffn_kernel(te_ref, nv_ref, x_ref, gs_ref, wg_ref, wu_ref, wd_ref, o_ref, acc_sc):
    j = pl.program_id(0)
    f = pl.program_id(1)

    @pl.when(f == 0)
    def _():
        acc_sc[...] = jnp.zeros(acc_sc.shape, F32)

    @pl.when(j < nv_ref[0])
    def _():
        x = x_ref[...]
        hg = jnp.dot(x, wg_ref[0], preferred_element_type=F32)
        hu = jnp.dot(x, wu_ref[0], preferred_element_type=F32)
        h = (jax.nn.silu(hg) * hu).astype(BF16)
        acc_sc[...] += jnp.dot(h, wd_ref[0], preferred_element_type=F32)

    @pl.when(f == pl.num_programs(1) - 1)
    def _():
        gs = gs_ref[...]
        lane = lax.broadcasted_iota(I32, gs.shape, 1)
        gate = jnp.sum(jnp.where(lane < 6, gs, 0.0), axis=-1, keepdims=True)
        o_ref[...] = acc_sc[...] * gate


def moe_group_ffn(tile_expert, n_valid, x_sorted, gterms, wg, wu, wd, n_out_tiles, tf=512):
    tile = MOE_TILE
    n_tiles = x_sorted.shape[0] // tile
    d = x_sorted.shape[1]
    dff = wg.shape[2]
    assert dff % tf == 0
    src = lambda j: jnp.minimum(j, n_tiles - 1)
    grid_spec = pltpu.PrefetchScalarGridSpec(
        num_scalar_prefetch=2, grid=(n_out_tiles, dff // tf),
        in_specs=[pl.BlockSpec((tile, d), lambda j, f, te, nv: (src(j), 0)),
                  pl.BlockSpec((tile, LANES), lambda j, f, te, nv: (src(j), 0)),
                  pl.BlockSpec((1, d, tf), lambda j, f, te, nv: (te[src(j)], 0, f)),
                  pl.BlockSpec((1, d, tf), lambda j, f, te, nv: (te[src(j)], 0, f)),
                  pl.BlockSpec((1, tf, d), lambda j, f, te, nv: (te[src(j)], f, 0))],
        out_specs=pl.BlockSpec((tile, d), lambda j, f, te, nv: (j, 0)),
        scratch_shapes=[pltpu.VMEM((tile, d), F32)])
    return pl.pallas_call(_moe_group_ffn_kernel, grid_spec=grid_spec,
                          out_shape=jax.ShapeDtypeStruct((n_out_tiles * tile, d), F32),
                          compiler_params=_cp("arbitrary", "arbitrary"), name="moe_group_ffn")(
                              tile_expert, n_valid, x_sorted, gterms, wg, wu, wd)


def _moe_combine_kernel(rlo_ref, short_ref, route_ref, r_ref, os_hbm, g_ref, b_ref, o_ref, obf_ref, buf, sem, acc_sc,
                        *, n_experts, windows, alpha):
    i = pl.program_id(0)
    acc_sc[...] = alpha * r_ref[...]
    route = route_ref[...]
    e1 = route[:, 0:1].astype(I32)
    e2 = route[:, 1:2].astype(I32)
    p1 = route[:, 2:3].astype(I32)
    p2 = route[:, 3:4].astype(I32)

    def accumulate(window):
        def window_copy(e, slot):
            lo = pl.multiple_of((rlo_ref[i * n_experts + e] // SUBLANES) * SUBLANES, SUBLANES)
            return lo, pltpu.make_async_copy(os_hbm.at[pl.ds(lo, window), :], buf.at[slot, pl.ds(0, window), :],
                                             sem.at[slot])

        window_copy(0, 0)[1].start()
        col = lax.broadcasted_iota(I32, (route.shape[0], window), 1)
        for e in range(n_experts):
            slot = e % 2
            if e + 1 < n_experts:
                window_copy(e + 1, 1 - slot)[1].start()
            lo, cp = window_copy(e, slot)
            cp.wait()
            c1 = jnp.where(e1 == e, p1 - lo, -1)
            c2 = jnp.where(e2 == e, p2 - lo, -1)
            pick = jnp.where(col == c1, 1.0, jnp.where(col == c2, 1.0, 0.0)).astype(BF16)
            rows = buf[slot, pl.ds(0, window), :]
            hi = rows.astype(BF16)
            lo_part = (rows - hi.astype(F32)).astype(BF16)
            acc_sc[...] += (jnp.dot(pick, hi, preferred_element_type=F32)
                            + jnp.dot(pick, lo_part, preferred_element_type=F32))

    short, full = windows

    @pl.when(short_ref[i] == 1)
    def _():
        accumulate(short)

    @pl.when(short_ref[i] != 1)
    def _():
        accumulate(full)

    out = _layernorm_rows(acc_sc[...], g_ref[...], b_ref[...])
    o_ref[...] = out
    obf_ref[...] = out.astype(BF16)


def moe_combine_ln(row_lo, short, route_pos, resid, out_sorted, g, b, alpha, n_experts, windows):
    m, d = resid.shape
    tile = MOE_TILE
    grid_spec = pltpu.PrefetchScalarGridSpec(
        num_scalar_prefetch=2, grid=(m // tile,),
        in_specs=[pl.BlockSpec((tile, LANES), lambda i, rlo, sh: (i, 0)),
                  pl.BlockSpec((tile, d), lambda i, rlo, sh: (i, 0)),
                  pl.BlockSpec(memory_space=pl.ANY),
                  pl.BlockSpec((1, d), lambda i, rlo, sh: (0, 0)), pl.BlockSpec((1, d), lambda i, rlo, sh: (0, 0))],
        out_specs=[pl.BlockSpec((tile, d), lambda i, rlo, sh: (i, 0))] * 2,
        scratch_shapes=[pltpu.VMEM((2, max(windows), d), F32), pltpu.SemaphoreType.DMA((2,)),
                        pltpu.VMEM((tile, d), F32)])
    return pl.pallas_call(
        functools.partial(_moe_combine_kernel, n_experts=n_experts, windows=windows, alpha=alpha),
        grid_spec=grid_spec,
        out_shape=[jax.ShapeDtypeStruct((m, d), F32), jax.ShapeDtypeStruct((m, d), BF16)],
        compiler_params=_cp("arbitrary"), name="moe_combine_ln")(
            row_lo, short, route_pos, resid, out_sorted, g.reshape(1, d), b.reshape(1, d))


def moe_top2_ln(x_bf, resid, w_router, wg, wu, wd, g, b, alpha):
    m, d = resid.shape
    ne = wg.shape[0]
    tile = MOE_TILE
    nb = m // tile
    n_tiles = (TOP_K * m) // tile + ne
    tail_tiles = -(-(tile + COMBINE_SLACK) // tile)
    route, before, count = router_plan(x_bf, w_router, ne)

    eid = jnp.arange(ne, dtype=I32)
    lookup = lambda table, idx: jnp.sum(jnp.where(idx[..., None] == eid, table, 0), axis=-1)
    cnt = count[0, :ne].astype(I32)
    tiles_e = (cnt + tile - 1) // tile
    first_tile = jnp.cumsum(tiles_e) - tiles_e
    n_valid = jnp.sum(tiles_e)
    row_start = first_tile * tile
    tid = jnp.arange(n_tiles, dtype=I32)
    tile_expert = jnp.minimum(jnp.sum((tid[:, None] >= (first_tile + tiles_e)[None, :]).astype(I32), axis=1), ne - 1)
    rank0 = (tid - lookup(first_tile, tile_expert)) * tile
    bef = before[:, 0, :ne].astype(I32)
    aft = jnp.concatenate([bef[1:], cnt[None, :]], axis=0)
    bef_t = lookup(bef[:, None, :], tile_expert[None, :])
    aft_t = lookup(aft[:, None, :], tile_expert[None, :])
    valid_t = tid < n_valid
    b_lo = jnp.where(valid_t, jnp.sum((aft_t <= rank0[None, :]).astype(I32), axis=0), 0)
    b_hi = jnp.where(valid_t, jnp.sum((bef_t < (rank0 + tile)[None, :]).astype(I32), axis=0) - 1, -1)
    e1, e2 = route[:, 0].astype(I32), route[:, 1].astype(I32)
    pos1 = lookup(row_start, e1) + route[:, 4].astype(I32)
    pos2 = lookup(row_start, e2) + route[:, 5].astype(I32)
    pos = jnp.stack([pos1.reshape(nb, tile), pos2.reshape(nb, tile)]
                    + [jnp.zeros((nb, tile), I32)] * (SUBLANES - 2), axis=1)

    def three_terms(v):
        hi = v.astype(BF16)
        mid = (v - hi.astype(F32)).astype(BF16)
        low = (v - hi.astype(F32) - mid.astype(F32)).astype(BF16)
        return [hi, mid, low]

    gsplit = jnp.stack(three_terms(route[:, 2]) + three_terms(route[:, 3]), axis=1)
    gsplit = jnp.pad(gsplit, ((0, 0), (0, LANES - gsplit.shape[1])))
    route_pos = jnp.stack([route[:, 0], route[:, 1], pos1.astype(F32), pos2.astype(F32)], axis=1)
    route_pos = jnp.pad(route_pos, ((0, 0), (0, LANES - route_pos.shape[1])))
    row_lo = row_start[None, :] + bef
    windows = (tile // 2, tile + COMBINE_SLACK)
    short = jnp.all(row_lo % SUBLANES + (aft - bef) <= windows[0], axis=1).astype(I32)

    x_sorted, gterms = moe_gather(x_bf, pos, gsplit, b_lo, b_hi, n_tiles)
    out_sorted = moe_group_ffn(tile_expert, n_valid.reshape(1), x_sorted, gterms, wg, wu, wd, n_tiles + tail_tiles)
    return moe_combine_ln(row_lo.reshape(-1), short, route_pos, resid, out_sorted, g, b, alpha, ne, windows)


NEG_INF_CODE = INT_MIN + 0x7FFFFF


def _code_to_f32(code):
    return pltpu.bitcast(jnp.where(code < 0, code ^ jnp.int32(0x7FFFFFFF), code), F32)


def _threshold_search(fold, topk, shape):
    kk = jnp.int32(topk)
    count = lambda pred: fold(lambda s, i: jnp.where(pred(s), 1, 0), "sum")
    c0 = count(lambda s: s >= 0.0)
    code = jnp.where(c0 >= kk, jnp.int32(0), jnp.int32(INT_MIN)) + jnp.zeros(shape, I32)

    def bit_body(j, code):
        cand = code | (jnp.int32(1) << (30 - j))
        cand_f = _code_to_f32(cand)
        return jnp.where(count(lambda s: s >= cand_f) >= kk, cand, code)

    code = lax.fori_loop(0, 31, bit_body, code)
    enough = code > NEG_INF_CODE
    floor_t = _code_to_f32(code)
    t = jnp.where(enough, fold(lambda s, i: jnp.where(s >= floor_t, s, jnp.inf), "min"), -jnp.inf)
    need = kk - count(lambda s: s > t)
    return t, jnp.where(enough, need, 0)


def _topk_threshold(fold, topk, n_idx_bits, rows):
    t, need = _threshold_search(fold, topk, (rows, 1))

    def tie_body(j, p):
        cand = p | (jnp.int32(1) << (n_idx_bits - 1 - j))
        got = fold(lambda s, i: jnp.where(s == t, jnp.where(i < cand, 1, 0), 0), "sum")
        return jnp.where(got <= need, cand, p)

    p = lax.fori_loop(0, n_idx_bits, tie_body, jnp.zeros((rows, 1), I32))
    return t, jnp.where(need > 0, p, 0)


CHUNK_GROUP = 4


def _dsa_prompt_kernel(qt_ref, k_ref, vt_ref, qit_ref, ki_ref, wit_ref, bias_ref, ltri_ref, o_ref,
                       score_sc, m_sc, l_sc, acc_sc, *, topk):
    i = pl.program_id(1)
    n_groups = (i + CHUNK_GROUP) // CHUNK_GROUP
    krow = lax.broadcasted_iota(I32, (TQ, TQ), 0)
    qcol = lax.broadcasted_iota(I32, (TQ, TQ), 1)
    qit = qit_ref[0, 0] * jnp.asarray(IDX_DIM ** -0.5, BF16)
    wt = wit_ref[0, 0] * (IDX_HEADS ** -0.5)

    def score_body(c, carry):
        off = pl.multiple_of(c * TQ, TQ)
        kc = ki_ref[0, pl.ds(off, TQ), :]
        sc = jnp.zeros((TQ, TQ), F32)
        for h in range(IDX_HEADS):
            s = jnp.dot(kc, qit[h * IDX_DIM:(h + 1) * IDX_DIM, :], preferred_element_type=F32)
            sc = sc + wt[h:h + 1, :] * jnp.maximum(s, 0.0)
        score_sc[pl.ds(off, TQ), :] = jnp.where(off + krow <= i * TQ + qcol, sc, -jnp.inf)
        return carry

    lax.fori_loop(0, n_groups * CHUNK_GROUP, score_body, 0)

    def fold(fn, kind):
        init, join, finish = {"sum": (0, jnp.add, jnp.sum), "min": (jnp.inf, jnp.minimum, jnp.min)}[kind]

        def body(g, acc):
            base = pl.multiple_of(g * (CHUNK_GROUP * TQ), CHUNK_GROUP * TQ)
            for j in range(CHUNK_GROUP):
                acc = join(acc, fn(score_sc[pl.ds(base + j * TQ, TQ), :], None))
            return acc

        acc = lax.fori_loop(0, n_groups, body, jnp.full((TQ, TQ), init, I32 if kind == "sum" else F32))
        return finish(acc, axis=0, keepdims=True)

    t, need = _threshold_search(fold, topk, (1, TQ))
    need = need.astype(F32)

    m_sc[...] = jnp.full(m_sc.shape, NEG, F32)
    l_sc[...] = jnp.zeros(l_sc.shape, F32)
    acc_sc[...] = jnp.zeros(acc_sc.shape, F32)
    qt = qt_ref[0, 0] * jnp.asarray(A_HEAD_DIM ** -0.5, BF16)

    def att_body(c, ties_before):
        off = pl.multiple_of(c * TQ, TQ)
        key = score_sc[pl.ds(off, TQ), :]
        tie = key == t
        rank = ties_before + jnp.dot(ltri_ref[...], jnp.where(tie, 1.0, 0.0).astype(BF16),
                                     preferred_element_type=F32)
        sel = jnp.where(key > t, 1, jnp.where(tie, jnp.where(rank <= need, 1, 0), 0)) > 0
        kc = k_ref[0, pl.ds(off, TQ), :]
        bsel = jnp.minimum(i - c, 2)
        for h in range(A_HEADS):
            sl = slice(h * A_HEAD_DIM, (h + 1) * A_HEAD_DIM)
            s = jnp.dot(kc[:, sl], qt[sl, :], preferred_element_type=F32)
            s = jnp.where(sel, s + bias_ref[bsel, h], NEG)
            m_prev = m_sc[h, 0:1, :]
            m_new = jnp.maximum(m_prev, jnp.max(s, axis=0, keepdims=True))
            alpha = jnp.exp(m_prev - m_new)
            pe = jnp.exp(s - m_new)
            l_sc[h, 0:1, :] = alpha * l_sc[h, 0:1, :] + jnp.sum(pe, axis=0, keepdims=True)
            acc_sc[sl, :] = alpha * acc_sc[sl, :] + jnp.dot(vt_ref[0, c, sl, :], pe.astype(BF16),
                                                            preferred_element_type=F32)
            m_sc[h, 0:1, :] = m_new
        return rank[TQ - 1:TQ, :]

    lax.fori_loop(0, i + 1, att_body, jnp.zeros((1, TQ), F32))
    for h in range(A_HEADS):
        sl = slice(h * A_HEAD_DIM, (h + 1) * A_HEAD_DIM)
        acc_sc[sl, :] = acc_sc[sl, :] / l_sc[h, 0:1, :]
    o_ref[0] = acc_sc[...].T.astype(o_ref.dtype)


def dsa_prompt(qt, k, vt, qit, ki, wit, bias_tab):
    bsz, t, _ = k.shape
    assert TQ == LANES and t % (CHUNK_GROUP * TQ) == 0
    topk = min(TOPK_MAX, t // 4)
    nq = t // TQ
    ltri = (lax.broadcasted_iota(I32, (TQ, TQ), 1) <= lax.broadcasted_iota(I32, (TQ, TQ), 0)).astype(BF16)
    per_q = lambda rows: pl.BlockSpec((1, 1, rows, TQ), lambda b, i: (b, i, 0, 0))
    per_b = lambda w: pl.BlockSpec((1, t, w), lambda b, i: (b, 0, 0))
    stat = pltpu.VMEM((A_HEADS, SUBLANES, TQ), F32)
    return pl.pallas_call(
        functools.partial(_dsa_prompt_kernel, topk=topk), grid=(bsz, nq),
        in_specs=[per_q(A_WIDTH), per_b(A_WIDTH), pl.BlockSpec((1, nq, A_WIDTH, TQ), lambda b, i: (b, 0, 0, 0)),
                  per_q(IDX_HEADS * IDX_DIM), per_b(IDX_DIM), per_q(SUBLANES), _full(bias_tab.shape),
                  _full((TQ, TQ))],
        out_specs=pl.BlockSpec((1, TQ, A_WIDTH), lambda b, i: (b, i, 0)),
        out_shape=jax.ShapeDtypeStruct((bsz, t, A_WIDTH), BF16),
        scratch_shapes=[pltpu.VMEM((t, TQ), F32), stat, stat, pltpu.VMEM((A_WIDTH, TQ), F32)],
        compiler_params=_cp("parallel", "arbitrary"), name="dsa_prompt")(qt, k, vt, qit, ki, wit, bias_tab, ltri)


def _row_to_col(row):
    n = row.shape[1]
    eye = lax.broadcasted_iota(I32, (n, n), 0) == lax.broadcasted_iota(I32, (n, n), 1)
    return jnp.sum(jnp.where(eye, jnp.broadcast_to(row, (n, n)), 0.0), axis=1, keepdims=True)


def _col_to_row(col):
    n = col.shape[0]
    eye = lax.broadcasted_iota(I32, (n, n), 0) == lax.broadcasted_iota(I32, (n, n), 1)
    return jnp.sum(jnp.where(eye, jnp.broadcast_to(col, (n, n)), 0.0), axis=0, keepdims=True)


def _dsa_s_score_kernel(pt_ref, qi_ref, w_ref, *refs, n_pages):
    page_refs, o_ref = refs[:n_pages], refs[n_pages]
    w = _as_mxu_operand(w_ref[0] * (IDX_HEADS ** -0.5))
    qcols = [_row_to_col(_as_mxu_operand(qi_ref[0, h:h + 1, :] * (IDX_DIM ** -0.5))) for h in range(IDX_HEADS)]
    for p in range(n_pages):
        page = _as_mxu_operand(page_refs[p][0])
        sc = jnp.zeros((1, PAGE_SIZE), F32)
        for h in range(IDX_HEADS):
            s = jnp.sum(page * qcols[h], axis=0, keepdims=True)
            sc = sc + w[:, h:h + 1] * _as_mxu_operand(jnp.maximum(s, 0.0))
        o_ref[0, :, p * PAGE_SIZE:(p + 1) * PAGE_SIZE] = sc


def dsa_sample_scores(page_table, qi8, wi, kidx_pool_t):
    s, n_pages = page_table.shape
    pt = page_table.reshape(-1)
    page_spec = lambda p: pl.BlockSpec((1, IDX_DIM, PAGE_SIZE), lambda i, pt_ref: (pt_ref[i * n_pages + p], 0, 0))
    grid_spec = pltpu.PrefetchScalarGridSpec(
        num_scalar_prefetch=1, grid=(s,),
        in_specs=[pl.BlockSpec((1, SUBLANES, IDX_DIM), lambda i, pt_ref: (i, 0, 0)),
                  pl.BlockSpec((1, 1, LANES), lambda i, pt_ref: (i, 0, 0))] + [page_spec(p) for p in range(n_pages)],
        out_specs=pl.BlockSpec((1, 1, n_pages * PAGE_SIZE), lambda i, pt_ref: (i, 0, 0)))
    return pl.pallas_call(functools.partial(_dsa_s_score_kernel, n_pages=n_pages), grid_spec=grid_spec,
                          out_shape=jax.ShapeDtypeStruct((s, 1, n_pages * PAGE_SIZE), F32),
                          compiler_params=_cp("arbitrary"), name="dsa_sample_scores")(
                              pt, qi8, wi, *([kidx_pool_t] * n_pages))


def _dsa_s_select_kernel(sc_ref, qi_ref, kn_ref, wi_ref, mask_ref, mnew_ref, *, topk, n_idx_bits):
    s, lp = mask_ref.shape
    w = _as_mxu_operand(wi_ref[...] * (IDX_HEADS ** -0.5))
    sc_new = jnp.zeros((s, 1), F32)
    kn = _as_mxu_operand(kn_ref[...])
    for h in range(IDX_HEADS):
        s_new = jnp.sum(_as_mxu_operand(qi_ref[:, h * IDX_DIM:(h + 1) * IDX_DIM]) * kn, axis=-1, keepdims=True)
        sc_new = sc_new + w[:, h:h + 1] * _as_mxu_operand(jnp.maximum(s_new * (IDX_DIM ** -0.5), 0.0))
    col = lax.broadcasted_iota(I32, (s, lp), 1)

    def fold(fn, kind):
        join, finish = {"sum": (jnp.add, jnp.sum), "min": (jnp.minimum, jnp.min)}[kind]
        main = finish(fn(sc_ref[...], col), axis=1, keepdims=True)
        return join(main, fn(sc_new, jnp.int32(lp)))

    t, p = _topk_threshold(fold, topk, n_idx_bits, s)
    sc = sc_ref[...]
    mask_ref[...] = jnp.where(sc > t, 1.0, jnp.where(sc == t, jnp.where(col < p, 1.0, 0.0), 0.0))
    sel_new = jnp.where(sc_new > t, 1.0, jnp.where(sc_new == t, jnp.where(lp < p, 1.0, 0.0), 0.0))
    mnew_ref[...] = jnp.broadcast_to(sel_new, mnew_ref.shape)


def dsa_sample_select(raw, qi, ki_new, wi):
    s, lp = raw.shape
    length = lp + 1
    topk = min(TOPK_MAX, length // 4)
    kern = functools.partial(_dsa_s_select_kernel, topk=topk, n_idx_bits=length.bit_length())
    return pl.pallas_call(kern, out_shape=[jax.ShapeDtypeStruct((s, lp), F32), jax.ShapeDtypeStruct((s, LANES), F32)],
                          compiler_params=pltpu.CompilerParams(vmem_limit_bytes=VMEM_LIMIT),
                          name="dsa_sample_select")(raw, qi, ki_new, wi)


def _dsa_s_attn_kernel(pt_ref, q_ref, kn_ref, vn_ref, mask_ref, mnew_ref, bias_ref, bnew_ref, *refs, n_pages):
    k_refs, v_refs, o_ref = refs[:n_pages], refs[n_pages:2 * n_pages], refs[2 * n_pages]
    hd = A_HEAD_DIM
    q = _as_mxu_operand(q_ref[0] * (hd ** -0.5))
    qcols = [_row_to_col(q[h:h + 1, :]) for h in range(A_HEADS)]
    hrow = lax.broadcasted_iota(I32, (A_HEADS, PAGE_SIZE), 0)
    lg_new = jnp.sum(q * _as_mxu_operand(kn_ref[0]), axis=1, keepdims=True) + bnew_ref[:, 0:1]
    lg_new = jnp.where(mnew_ref[0][:, 0:1] > 0.0, lg_new, NEG)
    logits = []
    mx = lg_new
    for p in range(n_pages):
        lg = jnp.zeros((A_HEADS, PAGE_SIZE), F32)
        for h in range(A_HEADS):
            kp = _as_mxu_operand(k_refs[p][0, h * hd:(h + 1) * hd, :])
            s = jnp.sum(kp * qcols[h], axis=0, keepdims=True)
            lg = jnp.where(hrow == h, jnp.broadcast_to(s, lg.shape), lg)
        lg = jnp.where(mask_ref[0, p:p + 1, :] > 0.0, lg + bias_ref[p], NEG)
        logits.append(lg)
        mx = jnp.maximum(mx, jnp.max(lg, axis=1, keepdims=True))
    pe_new = jnp.exp(lg_new - mx)
    pes = [jnp.exp(lg - mx) for lg in logits]
    den = pe_new
    for pe in pes:
        den = den + jnp.sum(pe, axis=1, keepdims=True)
    probs = [_as_mxu_operand(pe / den) for pe in pes]
    p_new = _as_mxu_operand(pe_new / den)
    for h in range(A_HEADS):
        acc = jnp.zeros((hd, PAGE_SIZE), F32)
        for p in range(n_pages):
            acc = acc + probs[p][h:h + 1, :] * _as_mxu_operand(v_refs[p][0, h * hd:(h + 1) * hd, :])
        o_ref[0, h:h + 1, :] = (_col_to_row(jnp.sum(acc, axis=1, keepdims=True))
                                + p_new[h:h + 1, :] * _as_mxu_operand(vn_ref[0, h:h + 1, :]))


def dsa_sample_attn(page_table, q, k_new, v_new, mask, mask_new, bias_pages, bias_new, k_pool_t, v_pool_t):
    s, n_pages = page_table.shape
    pt = page_table.reshape(-1)
    per_s = lambda r, w: pl.BlockSpec((1, r, w), lambda i, pt_ref: (i, 0, 0))
    const = lambda shape: pl.BlockSpec(shape, lambda i, pt_ref: (0,) * len(shape))
    page_spec = lambda p: pl.BlockSpec((1, A_WIDTH, PAGE_SIZE), lambda i, pt_ref: (pt_ref[i * n_pages + p], 0, 0))
    hd = (A_HEADS, A_HEAD_DIM)
    grid_spec = pltpu.PrefetchScalarGridSpec(
        num_scalar_prefetch=1, grid=(s,),
        in_specs=[per_s(*hd), per_s(*hd), per_s(*hd), per_s(n_pages, PAGE_SIZE), per_s(1, LANES),
                  const(bias_pages.shape), const(bias_new.shape)]
                 + [page_spec(p) for p in range(n_pages)] * 2,
        out_specs=per_s(*hd))
    return pl.pallas_call(functools.partial(_dsa_s_attn_kernel, n_pages=n_pages), grid_spec=grid_spec,
                          out_shape=jax.ShapeDtypeStruct((s,) + hd, F32),
                          compiler_params=_cp("arbitrary"), name="dsa_sample_attn")(
                              pt, q, k_new, v_new, mask, mask_new, bias_pages, bias_new,
                              *([k_pool_t] * n_pages), *([v_pool_t] * n_pages))


def _rglru_gates(xc, wa, ba, wi, bi, lam):
    xb = xc.astype(wa.dtype)
    r = jax.nn.sigmoid(_mm(xb, wa) + ba)
    ig = jax.nn.sigmoid(_mm(xb, wi) + bi)
    log_a = -RG_C * r * jax.nn.softplus(-lam)
    a = jnp.exp(log_a)
    u = jnp.sqrt(1.0 - jnp.exp(2.0 * log_a)) * (ig * xc)
    return a, u


def _rglru_prompt_kernel(xr_ref, gr_ref, cw_ref, cb_ref, wa_ref, ba_ref, wi_ref, bi_ref, lam_ref,
                         y_ref, hl_ref, xs_sc, h_sc, *, tb):
    t = pl.program_id(1)
    pad = SUBLANES

    @pl.when(t == 0)
    def _():
        xs_sc[0:pad, :] = jnp.zeros((pad, xs_sc.shape[1]), F32)
        h_sc[...] = jnp.zeros(h_sc.shape, F32)

    xs_sc[pad:pad + tb, :] = xr_ref[0]
    xc = cb_ref[...] + cw_ref[RG_CONV - 1:RG_CONV, :] * xs_sc[pad:pad + tb, :]
    for j in range(1, RG_CONV):
        xc = xc + cw_ref[RG_CONV - 1 - j:RG_CONV - j, :] * xs_sc[pad - j:pad - j + tb, :]
    a, u = _rglru_gates(xc, wa_ref[...], ba_ref[...], wi_ref[...], bi_ref[...], lam_ref[...])
    rowi = lax.broadcasted_iota(I32, a.shape, 0)
    s = 1
    while s < tb:
        keep = rowi >= s
        a_sh = jnp.where(keep, pltpu.roll(a, s, 0), 1.0)
        u_sh = jnp.where(keep, pltpu.roll(u, s, 0), 0.0)
        u = a * u_sh + u
        a = a * a_sh
        s *= 2
    h = a * h_sc[0:1, :] + u
    y_ref[0] = (h * jax.nn.gelu(gr_ref[0])).astype(y_ref.dtype)
    h_sc[0:1, :] = h[tb - 1:tb, :]
    xs_sc[0:pad, :] = xs_sc[tb:tb + pad, :]

    @pl.when(t == pl.num_programs(1) - 1)
    def _():
        hl_ref[0] = h[tb - 1:tb, :]


def rglru_prompt(xr, gr, cw, cb, wa, ba, wi, bi, lam, tb_cap=256):
    bsz, t, w = xr.shape
    tb = _row_tile(t, tb_cap)
    blk = pl.BlockSpec((1, tb, w), lambda b, i: (b, i, 0))
    vec = _full((1, w))
    return pl.pallas_call(
        functools.partial(_rglru_prompt_kernel, tb=tb), grid=(bsz, t // tb),
        in_specs=[blk, blk, _full((RG_CONV, w)), vec, _full((w, w)), vec, _full((w, w)), vec, vec],
        out_specs=[blk, pl.BlockSpec((1, 1, w), lambda b, i: (b, 0, 0))],
        out_shape=[jax.ShapeDtypeStruct((bsz, t, w), BF16), jax.ShapeDtypeStruct((bsz, 1, w), F32)],
        scratch_shapes=[pltpu.VMEM((tb + SUBLANES, w), F32), pltpu.VMEM((SUBLANES, w), F32)],
        compiler_params=_cp("parallel", "arbitrary"), name="rglru_prompt")(
            xr, gr, cw, cb.reshape(1, w), wa, ba.reshape(1, w), wi, bi.reshape(1, w), lam.reshape(1, w))


def _rglru_step_kernel(xr_ref, gr_ref, b0_ref, b1_ref, b2_ref, h0_ref, cw_ref, cb_ref, wa_ref, ba_ref, wi_ref,
                       bi_ref, lam_ref, y_ref, h_ref):
    xc = (cb_ref[...] + cw_ref[0:1, :] * b0_ref[...] + cw_ref[1:2, :] * b1_ref[...]
          + cw_ref[2:3, :] * b2_ref[...] + cw_ref[3:4, :] * xr_ref[...])
    a, u = _rglru_gates(xc, wa_ref[...], ba_ref[...], wi_ref[...], bi_ref[...], lam_ref[...])
    h = a * h0_ref[...] + u
    h_ref[...] = h
    y_ref[...] = (h * jax.nn.gelu(gr_ref[...])).astype(y_ref.dtype)


def rglru_step(xr, gr, buf, h0, cw, cb, wa, ba, wi, bi, lam):
    s, w = xr.shape
    return pl.pallas_call(
        _rglru_step_kernel,
        out_shape=[jax.ShapeDtypeStruct((s, w), F32), jax.ShapeDtypeStruct((s, w), F32)],
        compiler_params=pltpu.CompilerParams(vmem_limit_bytes=VMEM_LIMIT), name="rglru_step")(
            xr, gr, buf[:, 0], buf[:, 1], buf[:, 2], h0, cw, cb.reshape(1, w), wa, ba.reshape(1, w),
            wi, bi.reshape(1, w), lam.reshape(1, w))


def _headnorm_gate(h, g, og):
    mu = jnp.mean(h, axis=-1, keepdims=True)
    hc = h - mu
    var = jnp.mean(hc * hc, axis=-1, keepdims=True)
    return hc * lax.rsqrt(var + LN_EPS) * g * jax.nn.sigmoid(og)


def _mlstm_prompt_kernel(q_ref, k_ref, v_ref, og_ref, gt_ref, bg_ref, ng_ref, tri_ref,
                         y_ref, c_ref, n_ref, m_ref, c_sc, n_sc, m_sc, *, cl, dh):
    ci = pl.program_id(1)

    @pl.when(ci == 0)
    def _():
        c_sc[...] = jnp.zeros(c_sc.shape, F32)
        n_sc[...] = jnp.zeros(n_sc.shape, F32)
        m_sc[...] = jnp.zeros(m_sc.shape, F32)

    pre = gt_ref[0] + bg_ref[...]
    lane = lax.broadcasted_iota(I32, pre.shape, 1)
    gv = jnp.where(lane < ML_HEADS, pre, jax.nn.log_sigmoid(pre))
    csum = jnp.dot(tri_ref[...], gv, preferred_element_type=F32, precision=HI)
    gv_t = gv.T
    csum_t = csum.T
    row = lax.broadcasted_iota(I32, (cl, cl), 0)
    col = lax.broadcasted_iota(I32, (cl, cl), 1)
    causal = col <= row
    kscale = dh ** -0.5
    for h in range(ML_HEADS):
        sl = slice(h * dh, (h + 1) * dh)
        qh = q_ref[0, :, sl]
        kh = k_ref[0, :, sl] * jnp.asarray(kscale, BF16)
        vh = v_ref[0, :, sl]
        b_col = csum[:, ML_HEADS + h:ML_HEADS + h + 1]
        i_col = gv[:, h:h + 1]
        b_row = csum_t[ML_HEADS + h:ML_HEADS + h + 1, :]
        i_row = gv_t[h:h + 1, :]
        m_prev = m_sc[h:h + 1, 0:1]
        c_prev = c_sc[h]
        n_prev = n_sc[h:h + 1, :]
        dmat = jnp.where(causal, b_col - b_row + i_row, NEG)
        inter = b_col + m_prev
        m_t = jnp.maximum(inter, jnp.max(dmat, axis=1, keepdims=True))
        g_inter = jnp.exp(inter - m_t)
        qk = lax.dot_general(qh, kh, NT, preferred_element_type=F32)
        ws = jnp.exp(dmat - m_t) * qk
        num = (g_inter * lax.dot_general(qh, c_prev.astype(BF16), NT, preferred_element_type=F32)
               + jnp.dot(ws.astype(BF16), vh, preferred_element_type=F32))
        qf = qh.astype(F32)
        den = g_inter * jnp.sum(qf * n_prev, axis=1, keepdims=True) + jnp.sum(ws, axis=1, keepdims=True)
        hout = num / jnp.maximum(jnp.abs(den), jnp.exp(-m_t))
        m_new = m_t[cl - 1:cl, :]
        b_last = b_col[cl - 1:cl, :]
        g_c = jnp.exp(b_last + m_prev - m_new)
        w_end = jnp.exp(b_last - b_col + i_col - m_new)
        kf = kh.astype(F32)
        vw = (vh.astype(F32) * w_end).astype(BF16)
        c_sc[h] = g_c * c_prev + lax.dot_general(vw, kh, TN, preferred_element_type=F32)
        n_sc[h:h + 1, :] = g_c * n_prev + jnp.sum(kf * w_end, axis=0, keepdims=True)
        m_sc[h:h + 1, :] = jnp.broadcast_to(m_new, (1, m_sc.shape[1]))
        y_ref[0, :, sl] = _headnorm_gate(hout, ng_ref[:, sl], og_ref[0, :, sl]).astype(y_ref.dtype)

    @pl.when(ci == pl.num_programs(1) - 1)
    def _():
        c_ref[0] = c_sc[...]
        n_ref[0] = n_sc[...]
        m_ref[0] = m_sc[...]


def mlstm_prompt(q, k, v, og, gates, bgate, norm_g):
    bsz, t, width = q.shape
    dh = width // ML_HEADS
    cl = math.gcd(t, ML_CHUNK)
    tri = (lax.broadcasted_iota(I32, (cl, cl), 1) <= lax.broadcasted_iota(I32, (cl, cl), 0)).astype(F32)
    blk = lambda w: pl.BlockSpec((1, cl, w), lambda b, i: (b, i, 0))
    return pl.pallas_call(
        functools.partial(_mlstm_prompt_kernel, cl=cl, dh=dh), grid=(bsz, t // cl),
        in_specs=[blk(width), blk(width), blk(width), blk(width), blk(LANES), _full((1, LANES)), _full((1, width)),
                  _full((cl, cl))],
        out_specs=[blk(width), pl.BlockSpec((1, ML_HEADS, dh, dh), lambda b, i: (b, 0, 0, 0)),
                   pl.BlockSpec((1, SUBLANES, dh), lambda b, i: (b, 0, 0)),
                   pl.BlockSpec((1, SUBLANES, LANES), lambda b, i: (b, 0, 0))],
        out_shape=[jax.ShapeDtypeStruct((bsz, t, width), BF16), jax.ShapeDtypeStruct((bsz, ML_HEADS, dh, dh), F32),
                   jax.ShapeDtypeStruct((bsz, SUBLANES, dh), F32), jax.ShapeDtypeStruct((bsz, SUBLANES, LANES), F32)],
        scratch_shapes=[pltpu.VMEM((ML_HEADS, dh, dh), F32), pltpu.VMEM((SUBLANES, dh), F32),
                        pltpu.VMEM((SUBLANES, LANES), F32)],
        compiler_params=_cp("parallel", "arbitrary"), name="mlstm_prompt")(
            q, k, v, og, gates, bgate, norm_g.reshape(1, width), tri)


def _mlstm_step_kernel(q_ref, k_ref, v_ref, og_ref, gt_ref, bg_ref, ng_ref, c0_ref, n0_ref, m0_ref,
                       y_ref, c_ref, n_ref, m_ref, *, sb, dh):
    kscale = dh ** -0.5
    m_ref[...] = jnp.zeros(m_ref.shape, F32)

    def seq_body(s, carry):
        rs = pl.ds(s, 1)
        pre = gt_ref[rs, :] + bg_ref[...]
        for h in range(ML_HEADS):
            sl = slice(h * dh, (h + 1) * dh)
            qh = q_ref[rs, sl]
            kh = k_ref[rs, sl] * kscale
            vh = v_ref[rs, sl]
            i_pre = pre[:, h:h + 1]
            logf = jax.nn.log_sigmoid(pre[:, ML_HEADS + h:ML_HEADS + h + 1])
            m_prev = m0_ref[rs, h:h + 1]
            c_prev = c0_ref[s, h]
            n_prev = n0_ref[s, h:h + 1, :]
            inter = logf + m_prev
            m_t = jnp.maximum(inter, i_pre)
            g_inter = jnp.exp(inter - m_t)
            w_in = jnp.exp(i_pre - m_t)
            q16 = _as_mxu_operand(qh)
            qk = jnp.sum(q16 * _as_mxu_operand(kh), axis=1, keepdims=True)
            q8 = jnp.broadcast_to(qh, (SUBLANES, dh)).astype(BF16)
            cq = lax.dot_general(q8, c_prev.astype(BF16), NT, preferred_element_type=F32)[0:1]
            num = g_inter * cq + (w_in * qk) * vh
            den = g_inter * jnp.sum(_as_mxu_operand(n_prev) * q16, axis=1, keepdims=True) + w_in * qk
            hout = num / jnp.maximum(jnp.abs(den), jnp.exp(-m_t))
            rsel = lax.broadcasted_iota(I32, (SUBLANES, dh), 0) == 0
            v8 = jnp.where(rsel, jnp.broadcast_to(w_in * vh, (SUBLANES, dh)), 0.0)
            k8 = jnp.broadcast_to(kh, (SUBLANES, dh))
            c_ref[s, h] = g_inter * c_prev + lax.dot_general(v8, k8, TN, preferred_element_type=F32, precision=HI)
            n_ref[s, h:h + 1, :] = g_inter * n_prev + w_in * kh
            m_ref[rs, h:h + 1] = m_t
            y_ref[rs, sl] = _headnorm_gate(hout, ng_ref[:, sl], og_ref[rs, sl]).astype(y_ref.dtype)
        return carry

    lax.fori_loop(0, sb, seq_body, 0)


def mlstm_step(q, k, v, og, gates, bgate, norm_g, c0, n0, m0, sb=SUBLANES):
    s, width = q.shape
    dh = width // ML_HEADS
    assert s % sb == 0
    row = lambda w: pl.BlockSpec((sb, w), lambda i: (i, 0))
    m0p = jnp.pad(m0, ((0, 0), (0, LANES - ML_HEADS)))
    outs = pl.pallas_call(
        functools.partial(_mlstm_step_kernel, sb=sb, dh=dh), grid=(s // sb,),
        in_specs=[row(width), row(width), row(width), row(width), row(LANES), _full((1, LANES)), _full((1, width)),
                  pl.BlockSpec((sb, ML_HEADS, dh, dh), lambda i: (i, 0, 0, 0)),
                  pl.BlockSpec((sb, ML_HEADS, dh), lambda i: (i, 0, 0)), row(LANES)],
        out_specs=[row(width), pl.BlockSpec((sb, ML_HEADS, dh, dh), lambda i: (i, 0, 0, 0)),
                   pl.BlockSpec((sb, ML_HEADS, dh), lambda i: (i, 0, 0)), row(LANES)],
        out_shape=[jax.ShapeDtypeStruct((s, width), F32), jax.ShapeDtypeStruct((s, ML_HEADS, dh, dh), F32),
                   jax.ShapeDtypeStruct((s, ML_HEADS, dh), F32), jax.ShapeDtypeStruct((s, LANES), F32)],
        compiler_params=_cp("parallel"), name="mlstm_step")(
            q, k, v, og, gates, bgate, norm_g.reshape(1, width), c0, n0, m0p)
    y, c, n, m = outs
    return y, c, n, m[:, :ML_HEADS]


def _xattn_prompt_kernel(q_ref, k_ref, v_ref, o_ref, *, dh):
    scale = dh ** -0.5
    for h in range(X_HEADS):
        sl = slice(h * dh, (h + 1) * dh)
        lg = lax.dot_general(q_ref[0, :, sl], k_ref[0, :, sl], NT, preferred_element_type=F32) * scale
        mx = jnp.max(lg, axis=-1, keepdims=True)
        e = jnp.exp(lg - mx)
        pr = e / jnp.sum(e, axis=-1, keepdims=True)
        o_ref[0, :, sl] = jnp.dot(pr.astype(BF16), v_ref[0, :, sl], preferred_element_type=F32).astype(o_ref.dtype)


def xattn_prompt(xq, mk, mv, tq_cap=512):
    bsz, t, d = xq.shape
    mt = mk.shape[1]
    tq = _row_tile(t, tq_cap)
    return pl.pallas_call(
        functools.partial(_xattn_prompt_kernel, dh=d // X_HEADS), grid=(bsz, t // tq),
        in_specs=[pl.BlockSpec((1, tq, d), lambda b, i: (b, i, 0)), pl.BlockSpec((1, mt, d), lambda b, i: (b, 0, 0)),
                  pl.BlockSpec((1, mt, d), lambda b, i: (b, 0, 0))],
        out_specs=pl.BlockSpec((1, tq, d), lambda b, i: (b, i, 0)),
        out_shape=jax.ShapeDtypeStruct((bsz, t, d), BF16),
        compiler_params=_cp("parallel", "parallel"), name="xattn_prompt")(xq, mk, mv)


def _xattn_step_kernel(q_ref, k_ref, v_ref, o_ref, *, sb, mt, dh):
    scale = dh ** -0.5
    parts = dh // LANES
    rows = X_HEADS * parts

    def seq_body(s, carry):
        rs = pl.ds(s, 1)
        qrow = _as_mxu_operand(q_ref[rs, :])
        pieces = []
        for h in range(X_HEADS):
            lg = jnp.zeros((mt, 1), F32)
            for c in range(parts):
                qp = qrow[:, h * dh + c * LANES:h * dh + (c + 1) * LANES]
                kp = _as_mxu_operand(k_ref[0, s, pl.ds(c * X_HEADS + h, mt, stride=rows), :])
                lg = lg + jnp.sum(kp * qp, axis=1, keepdims=True)
            lg = lg * scale
            e = jnp.exp(lg - jnp.max(lg, axis=0, keepdims=True))
            pr = _as_mxu_operand(e / jnp.sum(e, axis=0, keepdims=True))
            for c in range(parts):
                vp = _as_mxu_operand(v_ref[0, s, pl.ds(c * X_HEADS + h, mt, stride=rows), :])
                pieces.append(jnp.sum(pr * vp, axis=0, keepdims=True))
        o_ref[rs, :] = jnp.concatenate(pieces, axis=1)
        return carry

    lax.fori_loop(0, sb, seq_body, 0)


def _lane_chunk_view(mem):
    nl, s, mt, nh, dh = mem.shape
    v = mem.reshape(nl, s, mt, nh, dh // LANES, LANES)
    return jnp.transpose(v, (0, 1, 2, 4, 3, 5)).reshape(nl, s, mt * (dh // LANES) * nh, LANES)


def xattn_step(xq, mk_view, mv_view, layer, mt, sb=SUBLANES):
    s, d = xq.shape
    dh = d // X_HEADS
    assert s % sb == 0 and dh % LANES == 0
    nrows = mk_view.shape[2]
    mem_spec = pl.BlockSpec((1, sb, nrows, LANES), lambda i: (layer, i, 0, 0))
    return pl.pallas_call(
        functools.partial(_xattn_step_kernel, sb=sb, mt=mt, dh=dh), grid=(s // sb,),
        in_specs=[pl.BlockSpec((sb, d), lambda i: (i, 0)), mem_spec, mem_spec],
        out_specs=pl.BlockSpec((sb, d), lambda i: (i, 0)),
        out_shape=jax.ShapeDtypeStruct((s, d), F32),
        compiler_params=_cp("parallel"), name="xattn_step")(xq, mk_view, mv_view)


def _t5_bucket(dist):
    max_exact = REL_BUCKETS // 2
    d = jnp.maximum(dist.astype(F32), 1.0)
    large = max_exact + (jnp.log(d / max_exact) / math.log(REL_MAX_DIST / max_exact)
                         * (REL_BUCKETS - max_exact)).astype(I32)
    large = jnp.minimum(large, REL_BUCKETS - 1)
    return jnp.where(dist < max_exact, dist, large)


def _pad_cols(w, n):
    return jnp.pad(w, ((0, 0), (0, n - w.shape[1])))


def _block_diag(w):
    nb, di, do = w.shape
    eye = jnp.eye(nb, dtype=w.dtype)
    return jnp.einsum('nde,nm->ndme', w, eye).reshape(nb * di, nb * do)


def kernel(x_prompt, x_sample, cache_k, cache_v, cache_kidx, state_rglru_h, state_rglru_conv, state_mlstm_c,
           state_mlstm_n, state_mlstm_m, cache_mem_k, cache_mem_v, page_table, mem_prompt, ln_g, ln_b, rel_bias,
           w_in_even, w_out_even, rg_conv_w, rg_conv_b, rg_w_a, rg_b_a, rg_w_i, rg_b_i, rg_lambda, w_in_odd,
           b_gate_odd, ml_norm_g, w_out_odd, w_xq, w_xk, w_xv, w_xo, ffn_w_gate, ffn_w_up, ffn_w_down,
           moe_router, moe_w_gate, moe_w_up, moe_w_down):
    bsz, t, d = x_prompt.shape
    s = x_sample.shape[0]
    depth = ln_g.shape[0]
    n_pages = page_table.shape[1]
    past_len = n_pages * PAGE_SIZE
    mt = mem_prompt.shape[1]
    rg_w = rg_lambda.shape[1]
    ml_w = ml_norm_g.shape[1]
    n_experts = moe_router.shape[2]
    alpha = (2.0 * depth) ** 0.25
    bf = lambda a: a.astype(BF16)

    assert REL_MAX_DIST <= TQ
    rel32 = rel_bias.astype(F32)
    by_dist = rel32[_t5_bucket(jnp.arange(2 * TQ + 1, dtype=I32))]

    def spread(dist):
        out = jnp.broadcast_to(by_dist[2 * TQ], dist.shape + (A_HEADS,))
        for dd in range(2 * TQ):
            out = jnp.where((dist == dd)[..., None], by_dist[dd], out)
        return out

    kk_ = lax.broadcasted_iota(I32, (TQ, TQ), 0)
    qq_ = lax.broadcasted_iota(I32, (TQ, TQ), 1)
    bias_tab = jnp.stack([spread(jnp.maximum(qq_ - kk_, 0)), spread(TQ + qq_ - kk_),
                          spread(jnp.full((TQ, TQ), 2 * TQ, I32))])
    bias_tab = jnp.transpose(bias_tab, (0, 3, 1, 2))
    dist_s = (past_len - jnp.arange(past_len, dtype=I32)).reshape(n_pages, PAGE_SIZE)
    bias_s = jnp.transpose(rel32[_t5_bucket(dist_s)], (0, 2, 1))
    bias_s_new = jnp.broadcast_to(rel32[_t5_bucket(jnp.zeros((1,), I32))].reshape(A_HEADS, 1), (A_HEADS, LANES))
    mem_k_view, mem_v_view = _lane_chunk_view(cache_mem_k), _lane_chunk_view(cache_mem_v)

    mem_bf = bf(mem_prompt.reshape(bsz * mt, d))
    kv_w = [bf(w_xk[l]) for l in range(depth)] + [bf(w_xv[l]) for l in range(depth)]
    kv = multi_proj(mem_bf, kv_w, [(F32, BF16)] * (2 * depth))
    mem_k_f32 = [kv[2 * l] for l in range(depth)]
    mem_k_bf = [kv[2 * l + 1].reshape(bsz, mt, d) for l in range(depth)]
    mem_v_f32 = [kv[2 * depth + 2 * l] for l in range(depth)]
    mem_v_bf = [kv[2 * depth + 2 * l + 1].reshape(bsz, mt, d) for l in range(depth)]

    xp = x_prompt.reshape(bsz * t, d)
    xs = x_sample.reshape(s, d)
    xp_bf, xs_bf = bf(xp), bf(xs)
    outs = {}
    k_p, v_p, ki_p, k_s, v_s, ki_s = [], [], [], [], [], []
    rgh_p, rgc_p, rgh_s, rgc_s = [], [], [], []
    mlc_p, mln_p, mlm_p, mlc_s, mln_s, mlm_s = [], [], [], [], [], []

    for l in range(depth):
        if l % 2 == 0:
            e = l // 2
            w_in = w_in_even[e]
            o = 0
            cols = {}
            for name, width in (("q", A_WIDTH), ("k", A_WIDTH), ("v", A_WIDTH), ("qi", IDX_HEADS * IDX_DIM),
                                ("ki", IDX_DIM), ("wi", IDX_HEADS), ("xr", rg_w), ("gr", rg_w)):
                cols[name] = bf(w_in[:, o:o + width])
                o += width
            cols["wi"] = _pad_cols(cols["wi"], LANES)
            names = ("q", "k", "v", "qi", "ki", "wi", "xr", "gr")
            wq_t, wv_t, wqi_t = cols["q"].T, cols["v"].T, cols["qi"].T
            wi_t = jnp.pad(cols["wi"][:, :IDX_HEADS].T, ((0, SUBLANES - IDX_HEADS), (0, 0)))
            wa = bf(_block_diag(rg_w_a[e]))
            wi_g = bf(_block_diag(rg_w_i[e]))
            w_out = bf(w_out_even[e])

            nq = t // TQ
            pnames = ("k", "v", "ki", "xr", "gr")
            kf, kb, vf, kif, kib, xr, gr, qt, vt, qit, wit = multi_proj(
                xp_bf, [cols[n] for n in pnames] + [wq_t, wv_t, wqi_t, wi_t],
                [(F32, BF16), (F32,), (F32, BF16), (F32,), (F32,), (BF16,), (BF16,), (BF16,), (F32,)],
                transposed=(False,) * len(pnames) + (True,) * 4)
            r3 = lambda a: a.reshape(bsz, t, a.shape[-1])
            r4 = lambda a: a.reshape(bsz, nq, a.shape[-2], TQ)
            att = dsa_prompt(r4(qt), r3(kb), r4(vt), r4(qit), r3(kib), r4(wit), bias_tab)
            rg_y, h_last = rglru_prompt(r3(xr), r3(gr), rg_conv_w[e], rg_conv_b[e], wa, rg_b_a[e], wi_g, rg_b_i[e],
                                        rg_lambda[e])
            xp, xp_bf = mm_res_ln([att.reshape(bsz * t, A_WIDTH), rg_y.reshape(bsz * t, rg_w)],
                                  [w_out[:A_WIDTH], w_out[A_WIDTH:]], xp, ln_g[l, 0], ln_b[l, 0], alpha)
            k_p.append(kf.reshape(bsz, t, A_HEADS, A_HEAD_DIM))
            v_p.append(vf.reshape(bsz, t, A_HEADS, A_HEAD_DIM))
            ki_p.append(kif.reshape(bsz, t, IDX_DIM))
            rgh_p.append(h_last.reshape(bsz, rg_w))
            rgc_p.append(r3(xr)[:, t - (RG_CONV - 1):, :])

            q, kf, vf, qi, kif, wi, xr, gr = multi_proj(xs_bf, [cols[n] for n in names], [(F32,)] * len(names))
            qi8 = jnp.pad(qi.reshape(s, IDX_HEADS, IDX_DIM), ((0, 0), (0, SUBLANES - IDX_HEADS), (0, 0)))
            pool = cache_k.shape[1]
            kidx_t = jnp.transpose(cache_kidx[e], (0, 2, 1))
            k_t = jnp.transpose(cache_k[e], (0, 2, 3, 1)).reshape(pool, A_WIDTH, PAGE_SIZE)
            v_t = jnp.transpose(cache_v[e], (0, 2, 3, 1)).reshape(pool, A_WIDTH, PAGE_SIZE)
            raw = dsa_sample_scores(page_table, qi8, wi.reshape(s, 1, LANES), kidx_t)
            mask, mask_new = dsa_sample_select(raw.reshape(s, past_len), qi, kif, wi)
            hd3 = lambda a: a.reshape(s, A_HEADS, A_HEAD_DIM)
            att = dsa_sample_attn(page_table, hd3(q), hd3(kf), hd3(vf), mask.reshape(s, n_pages, PAGE_SIZE),
                                  mask_new.reshape(s, 1, LANES), bias_s, bias_s_new, k_t, v_t)
            rg_y, h_new = rglru_step(xr, gr, state_rglru_conv[e], state_rglru_h[e], rg_conv_w[e], rg_conv_b[e],
                                     wa, rg_b_a[e], wi_g, rg_b_i[e], rg_lambda[e])
            xs, xs_bf = mm_res_ln([bf(att.reshape(s, A_WIDTH)), bf(rg_y)], [w_out[:A_WIDTH], w_out[A_WIDTH:]], xs,
                                  ln_g[l, 0], ln_b[l, 0], alpha)
            k_s.append(kf.reshape(s, 1, A_HEADS, A_HEAD_DIM))
            v_s.append(vf.reshape(s, 1, A_HEADS, A_HEAD_DIM))
            ki_s.append(kif.reshape(s, 1, IDX_DIM))
            rgh_s.append(h_new)
            rgc_s.append(jnp.concatenate([state_rglru_conv[e][:, 1:], xr[:, None, :]], axis=1))
        else:
            o_ = l // 2
            w_in = w_in_odd[o_]
            ws = [bf(w_in[:, i * ml_w:(i + 1) * ml_w]) for i in range(4)]
            ws.append(_pad_cols(bf(w_in[:, 4 * ml_w:]), LANES))
            bgate = jnp.pad(b_gate_odd[o_].astype(F32), (0, LANES - 2 * ML_HEADS)).reshape(1, LANES)
            w_out = bf(w_out_odd[o_])

            q, k, v, og, gates = multi_proj(xp_bf, ws, [(BF16,), (BF16,), (BF16,), (F32,), (F32,)])
            r3 = lambda a: a.reshape(bsz, t, a.shape[-1])
            hn, c, n, m = mlstm_prompt(r3(q), r3(k), r3(v), r3(og), r3(gates), bgate, ml_norm_g[o_])
            xp, xp_bf = mm_res_ln([hn.reshape(bsz * t, ml_w)], [w_out], xp, ln_g[l, 0], ln_b[l, 0], alpha)
            mlc_p.append(c)
            mln_p.append(n[:, :ML_HEADS])
            mlm_p.append(m[:, :ML_HEADS, 0])

            q, k, v, og, gates = multi_proj(xs_bf, ws, [(F32,)] * len(ws))
            hn, c, n, m = mlstm_step(q, k, v, og, gates, bgate, ml_norm_g[o_], state_mlstm_c[o_], state_mlstm_n[o_],
                                     state_mlstm_m[o_])
            xs, xs_bf = mm_res_ln([bf(hn)], [w_out], xs, ln_g[l, 0], ln_b[l, 0], alpha)
            mlc_s.append(c)
            mln_s.append(n)
            mlm_s.append(m)

        wq, wo = bf(w_xq[l]), bf(w_xo[l])
        (xq,) = multi_proj(xp_bf, [wq], [(BF16,)])
        xo = xattn_prompt(xq.reshape(bsz, t, d), mem_k_bf[l], mem_v_bf[l])
        xp, xp_bf = mm_res_ln([xo.reshape(bsz * t, d)], [wo], xp, ln_g[l, 1], ln_b[l, 1], alpha)
        (xq,) = multi_proj(xs_bf, [wq], [(F32,)])
        xo = xattn_step(xq, mem_k_view, mem_v_view, l, mt)
        xs, xs_bf = mm_res_ln([bf(xo)], [wo], xs, ln_g[l, 1], ln_b[l, 1], alpha)

        if l % 2 == 0:
            e = l // 2
            wg, wu, wd = bf(ffn_w_gate[e]), bf(ffn_w_up[e]), bf(ffn_w_down[e])
            xp, xp_bf = ffn_ln(xp_bf, xp, wg, wu, wd, ln_g[l, 2], ln_b[l, 2], alpha)
            xs, xs_bf = ffn_ln(xs_bf, xs, wg, wu, wd, ln_g[l, 2], ln_b[l, 2], alpha)
        else:
            o_ = l // 2
            wr = _pad_cols(bf(moe_router[o_]), LANES)
            wg, wu, wd = bf(moe_w_gate[o_]), bf(moe_w_up[o_]), bf(moe_w_down[o_])
            xp, xp_bf = moe_top2_ln(xp_bf, xp, wr, wg, wu, wd, ln_g[l, 2], ln_b[l, 2], alpha)
            gs = router_gates(xs_bf, wr, n_experts)
            xs, xs_bf = moe_ln(xs_bf, xs, gs, wg, wu, wd, ln_g[l, 2], ln_b[l, 2], alpha)

    xh = mem_k_f32[0].shape[-1] // X_HEADS
    mem_shape = (bsz, mt, X_HEADS, xh)
    return (xp.reshape(bsz, t, d), xs.reshape(s, 1, d),
            jnp.stack(k_p), jnp.stack(v_p), jnp.stack(ki_p), jnp.stack(k_s), jnp.stack(v_s), jnp.stack(ki_s),
            jnp.stack(rgh_p), jnp.stack(rgc_p), jnp.stack(rgh_s), jnp.stack(rgc_s),
            jnp.stack(mlc_p), jnp.stack(mln_p), jnp.stack(mlm_p), jnp.stack(mlc_s), jnp.stack(mln_s), jnp.stack(mlm_s),
            jnp.stack([a.reshape(mem_shape) for a in mem_k_f32]), jnp.stack([a.reshape(mem_shape) for a in mem_v_f32]))
```

```python
import functools
import math

import jax
import jax.numpy as jnp
from jax import lax
from jax.experimental import pallas as pl
from jax.experimental.pallas import tpu as pltpu

F32 = jnp.float32
BF16 = jnp.bfloat16
I32 = jnp.int32

PAGE_SIZE = 128
A_HEADS = 8
A_HEAD_DIM = 64
A_WIDTH = A_HEADS * A_HEAD_DIM
IDX_HEADS = 4
IDX_DIM = 64
TOPK_MAX = 256
REL_BUCKETS = 32
REL_MAX_DIST = 128
RG_BLOCKS = 8
RG_CONV = 4
RG_C = 8.0
ML_HEADS = 4
ML_CHUNK = 128
X_HEADS = 4
TOP_K = 2
LN_EPS = 1e-5

LANES = 128
SUBLANES = 8
VMEM_LIMIT = 48 * 1024 * 1024
VMEM_LIMIT_RESIDENT = 56 * 1024 * 1024
NEG = -1e30
INT_MIN = -2 ** 31
TQ = 128

NT = (((1,), (1,)), ((), ()))
TN = (((0,), (0,)), ((), ()))
HI = lax.Precision.HIGHEST


def _cp(*sem):
    return pltpu.CompilerParams(dimension_semantics=sem, vmem_limit_bytes=VMEM_LIMIT)


def _full(shape):
    n = len(shape)
    return pl.BlockSpec(shape, lambda *_: (0,) * n)


def _row_tile(m, cap):
    t = min(m, cap)
    assert m % t == 0
    return t


def _as_mxu_operand(a):
    return a.astype(BF16).astype(F32)


def _mm(a, b, dims=None):
    prec = HI if a.dtype == F32 else None
    if dims is None:
        return jnp.dot(a, b, preferred_element_type=F32, precision=prec)
    return lax.dot_general(a, b, dims, preferred_element_type=F32, precision=prec)


def _proj_kernel(x_ref, *refs, n_w, out_counts, transposed):
    w_refs, o_refs = refs[:n_w], refs[n_w:]
    x = x_ref[...]
    k = 0
    for w_ref, cnt, tr in zip(w_refs, out_counts, transposed):
        if tr:
            y = _mm(w_ref[...], x, NT)
        else:
            y = _mm(x, w_ref[...])
        for _ in range(cnt):
            if tr:
                for j in range(y.shape[1] // LANES):
                    o_refs[k][j] = y[:, j * LANES:(j + 1) * LANES].astype(o_refs[k].dtype)
            else:
                o_refs[k][...] = y.astype(o_refs[k].dtype)
            k += 1


def multi_proj(x, ws, out_dtypes, transposed=None, tm_cap=512):
    m, kdim = x.shape
    tm = _row_tile(m, tm_cap)
    transposed = transposed or (False,) * len(ws)
    in_specs = [pl.BlockSpec((tm, kdim), lambda i: (i, 0))] + [_full(w.shape) for w in ws]
    out_shape, out_specs = [], []
    for w, dts, tr in zip(ws, out_dtypes, transposed):
        for dt in dts:
            if tr:
                assert tm % LANES == 0
                out_shape.append(jax.ShapeDtypeStruct((m // LANES, w.shape[0], LANES), dt))
                out_specs.append(pl.BlockSpec((tm // LANES, w.shape[0], LANES), lambda i: (i, 0, 0)))
            else:
                out_shape.append(jax.ShapeDtypeStruct((m, w.shape[1]), dt))
                out_specs.append(pl.BlockSpec((tm, w.shape[1]), lambda i: (i, 0)))
    kern = functools.partial(_proj_kernel, n_w=len(ws), out_counts=tuple(len(d) for d in out_dtypes),
                             transposed=tuple(transposed))
    return pl.pallas_call(kern, grid=(m // tm,), in_specs=in_specs, out_specs=out_specs,
                          out_shape=out_shape, compiler_params=_cp("parallel"), name="multi_proj")(x, *ws)


def _layernorm_rows(z, g, b):
    mu = jnp.mean(z, axis=-1, keepdims=True)
    zc = z - mu
    var = jnp.mean(zc * zc, axis=-1, keepdims=True)
    return zc * lax.rsqrt(var + LN_EPS) * g + b


def _mm_ln_kernel(*refs, n_a, alpha):
    a_refs, w_refs = refs[:n_a], refs[n_a:2 * n_a]
    r_ref, g_ref, b_ref, o_ref, obf_ref = refs[2 * n_a:]
    y = alpha * r_ref[...]
    for a_ref, w_ref in zip(a_refs, w_refs):
        y = y + _mm(a_ref[...], w_ref[...])
    out = _layernorm_rows(y, g_ref[...], b_ref[...])
    o_ref[...] = out
    obf_ref[...] = out.astype(BF16)


def mm_res_ln(a_list, w_list, resid, g, b, alpha, tm_cap=512):
    m, d = resid.shape
    tm = _row_tile(m, tm_cap)
    in_specs = ([pl.BlockSpec((tm, a.shape[1]), lambda i: (i, 0)) for a in a_list]
                + [_full(w.shape) for w in w_list]
                + [pl.BlockSpec((tm, d), lambda i: (i, 0)), _full((1, d)), _full((1, d))])
    out_specs = [pl.BlockSpec((tm, d), lambda i: (i, 0))] * 2
    out_shape = [jax.ShapeDtypeStruct((m, d), F32), jax.ShapeDtypeStruct((m, d), BF16)]
    kern = functools.partial(_mm_ln_kernel, n_a=len(a_list), alpha=alpha)
    return pl.pallas_call(kern, grid=(m // tm,), in_specs=in_specs, out_specs=out_specs, out_shape=out_shape,
                          compiler_params=_cp("parallel"), name="mm_res_ln")(
                              *a_list, *w_list, resid, g.reshape(1, d), b.reshape(1, d))


def _ffn_kernel(x_ref, r_ref, wg_ref, wu_ref, wd_ref, g_ref, b_ref, o_ref, obf_ref, acc_sc, *, alpha):
    f = pl.program_id(1)

    @pl.when(f == 0)
    def _():
        acc_sc[...] = alpha * r_ref[...]

    x = x_ref[...]
    hg = _mm(x, wg_ref[...])
    hu = _mm(x, wu_ref[...])
    h = (jax.nn.silu(hg) * hu).astype(x.dtype)
    acc_sc[...] += _mm(h, wd_ref[...])

    @pl.when(f == pl.num_programs(1) - 1)
    def _():
        out = _layernorm_rows(acc_sc[...], g_ref[...], b_ref[...])
        o_ref[...] = out
        obf_ref[...] = out.astype(BF16)


def ffn_ln(x_bf, resid, wg, wu, wd, g, b, alpha, tm_cap=512, tf_cap=1408):
    m, d = resid.shape
    dff = wg.shape[1]
    tm = _row_tile(m, tm_cap)
    tf = _ff_tile(dff, tf_cap)
    in_specs = [pl.BlockSpec((tm, d), lambda i, f: (i, 0)), pl.BlockSpec((tm, d), lambda i, f: (i, 0)),
                pl.BlockSpec((d, tf), lambda i, f: (0, f)), pl.BlockSpec((d, tf), lambda i, f: (0, f)),
                pl.BlockSpec((tf, d), lambda i, f: (f, 0)), _full((1, d)), _full((1, d))]
    out_specs = [pl.BlockSpec((tm, d), lambda i, f: (i, 0))] * 2
    out_shape = [jax.ShapeDtypeStruct((m, d), F32), jax.ShapeDtypeStruct((m, d), BF16)]
    return pl.pallas_call(functools.partial(_ffn_kernel, alpha=alpha), grid=(m // tm, dff // tf),
                          in_specs=in_specs, out_specs=out_specs, out_shape=out_shape,
                          scratch_shapes=[pltpu.VMEM((tm, d), F32)],
                          compiler_params=_cp("parallel", "arbitrary"), name="ffn_ln")(
                              x_bf, resid, wg, wu, wd, g.reshape(1, d), b.reshape(1, d))


def _top2(logits, n_experts):
    lane = lax.broadcasted_iota(I32, logits.shape, 1)
    valid = lane < n_experts
    logits = jnp.where(valid, logits, NEG)
    mx = jnp.max(logits, axis=-1, keepdims=True)
    e = jnp.where(valid, jnp.exp(logits - mx), 0.0)
    p = e / jnp.sum(e, axis=-1, keepdims=True)
    p = jnp.where(valid, p, -1.0)
    big = jnp.int32(LANES)
    m1 = jnp.max(p, axis=-1, keepdims=True)
    i1 = jnp.min(jnp.where(p == m1, lane, big), axis=-1, keepdims=True)
    p2 = jnp.where(lane == i1, -1.0, p)
    m2 = jnp.max(p2, axis=-1, keepdims=True)
    i2 = jnp.min(jnp.where(p2 == m2, lane, big), axis=-1, keepdims=True)
    tot = m1 + m2
    return lane, i1, i2, m1 / tot, m2 / tot


def _router_kernel(x_ref, w_ref, o_ref, *, n_experts):
    lane, i1, i2, g1, g2 = _top2(_mm(x_ref[...], w_ref[...]), n_experts)
    o_ref[...] = jnp.where(lane == i1, g1, jnp.where(lane == i2, g2, 0.0))


def router_gates(x, w_pad, n_experts, tm_cap=512):
    m, d = x.shape
    tm = _row_tile(m, tm_cap)
    return pl.pallas_call(functools.partial(_router_kernel, n_experts=n_experts), grid=(m // tm,),
                          in_specs=[pl.BlockSpec((tm, d), lambda i: (i, 0)), _full(w_pad.shape)],
                          out_specs=pl.BlockSpec((tm, LANES), lambda i: (i, 0)),
                          out_shape=jax.ShapeDtypeStruct((m, LANES), F32),
                          compiler_params=_cp("parallel"), name="router")(x, w_pad)


def _moe_kernel(x_ref, r_ref, gt_ref, wg_ref, wu_ref, wd_ref, g_ref, b_ref, o_ref, obf_ref, acc_sc, *, alpha):
    e = pl.program_id(1)
    f = pl.program_id(2)

    @pl.when((e == 0) & (f == 0))
    def _():
        acc_sc[...] = alpha * r_ref[...]

    gt = gt_ref[...]
    lane = lax.broadcasted_iota(I32, gt.shape, 1)
    ge = jnp.sum(jnp.where(lane == e, gt, 0.0), axis=-1, keepdims=True)
    x = x_ref[...]
    hg = jnp.dot(x, wg_ref[0], preferred_element_type=F32)
    hu = jnp.dot(x, wu_ref[0], preferred_element_type=F32)
    h = (jax.nn.silu(hg) * hu).astype(BF16)
    acc_sc[...] += ge * jnp.dot(h, wd_ref[0], preferred_element_type=F32)

    @pl.when((e == pl.num_programs(1) - 1) & (f == pl.num_programs(2) - 1))
    def _():
        out = _layernorm_rows(acc_sc[...], g_ref[...], b_ref[...])
        o_ref[...] = out
        obf_ref[...] = out.astype(BF16)


def moe_ln(x_bf, resid, gates, wg, wu, wd, g, b, alpha, tm_cap=512, tf=512):
    m, d = resid.shape
    ne, _, dff = wg.shape
    tm = _row_tile(m, tm_cap)
    assert dff % tf == 0
    in_specs = [pl.BlockSpec((tm, d), lambda i, e, f: (i, 0)), pl.BlockSpec((tm, d), lambda i, e, f: (i, 0)),
                pl.BlockSpec((tm, LANES), lambda i, e, f: (i, 0)),
                pl.BlockSpec((1, d, tf), lambda i, e, f: (e, 0, f)), pl.BlockSpec((1, d, tf), lambda i, e, f: (e, 0, f)),
                pl.BlockSpec((1, tf, d), lambda i, e, f: (e, f, 0)), _full((1, d)), _full((1, d))]
    out_specs = [pl.BlockSpec((tm, d), lambda i, e, f: (i, 0))] * 2
    out_shape = [jax.ShapeDtypeStruct((m, d), F32), jax.ShapeDtypeStruct((m, d), BF16)]
    return pl.pallas_call(functools.partial(_moe_kernel, alpha=alpha), grid=(m // tm, ne, dff // tf),
                          in_specs=in_specs, out_specs=out_specs, out_shape=out_shape,
                          scratch_shapes=[pltpu.VMEM((tm, d), F32)],
                          compiler_params=_cp("parallel", "arbitrary", "arbitrary"), name="moe_ln")(
                              x_bf, resid, gates, wg, wu, wd, g.reshape(1, d), b.reshape(1, d))


MOE_TILE = 512
COMBINE_SLACK = LANES
ROUTE_LANES = ("e1", "e2", "g1", "g2", "rank1", "rank2")


def _router_plan_kernel(x_ref, w_ref, ltri_ref, route_ref, before_ref, count_ref, carry_sc, *, n_experts):
    @pl.when(pl.program_id(0) == 0)
    def _():
        carry_sc[...] = jnp.zeros(carry_sc.shape, F32)

    lane, i1, i2, g1, g2 = _top2(_mm(x_ref[...], w_ref[...]), n_experts)
    chosen = jnp.where(lane == i1, 1.0, jnp.where(lane == i2, 1.0, 0.0))
    before = carry_sc[0:1, :]
    rank = before + jnp.dot(ltri_ref[...], chosen.astype(BF16), preferred_element_type=F32)
    r1 = jnp.sum(jnp.where(lane == i1, rank, 0.0), axis=-1, keepdims=True)
    r2 = jnp.sum(jnp.where(lane == i2, rank, 0.0), axis=-1, keepdims=True)
    vals = (i1.astype(F32), i2.astype(F32), g1, g2, r1, r2)
    route = jnp.zeros(lane.shape, F32)
    for k, v in enumerate(vals):
        route = jnp.where(lane == k, v, route)
    route_ref[...] = route
    before_ref[0] = jnp.broadcast_to(before, before_ref.shape[1:])
    after = before + jnp.sum(chosen, axis=0, keepdims=True)
    carry_sc[0:1, :] = after
    count_ref[...] = jnp.broadcast_to(after, count_ref.shape)


def router_plan(x_bf, w_pad, n_experts):
    m, d = x_bf.shape
    tb = MOE_TILE
    assert m % tb == 0
    ltri = (lax.broadcasted_iota(I32, (tb, tb), 1) < lax.broadcasted_iota(I32, (tb, tb), 0)).astype(BF16)
    return pl.pallas_call(
        functools.partial(_router_plan_kernel, n_experts=n_experts), grid=(m // tb,),
        in_specs=[pl.BlockSpec((tb, d), lambda i: (i, 0)), _full(w_pad.shape), _full((tb, tb))],
        out_specs=[pl.BlockSpec((tb, LANES), lambda i: (i, 0)), pl.BlockSpec((1, SUBLANES, LANES), lambda i: (i, 0, 0)),
                   _full((SUBLANES, LANES))],
        out_shape=[jax.ShapeDtypeStruct((m, LANES), F32), jax.ShapeDtypeStruct((m // tb, SUBLANES, LANES), F32),
                   jax.ShapeDtypeStruct((SUBLANES, LANES), F32)],
        scratch_shapes=[pltpu.VMEM((SUBLANES, LANES), F32)],
        compiler_params=_cp("arbitrary"), name="router_plan")(x_bf, w_pad, ltri)


def _moe_gather_kernel(blo_ref, bhi_ref, x_ref, pos_ref, g_ref, xs_ref, gs_ref, acc_sc, gacc_sc, *, tile):
    j = pl.program_id(0)
    acc_sc[...] = jnp.zeros(acc_sc.shape, F32)
    gacc_sc[...] = jnp.zeros(gacc_sc.shape, F32)
    rid = j * tile + lax.broadcasted_iota(I32, (tile, tile), 0)
    glane = lax.broadcasted_iota(I32, gacc_sc.shape, 1)

    def body(b, carry):
        off = pl.multiple_of(b * tile, tile)
        m1 = jnp.where(pos_ref[b, 0:1, :] == rid, 1.0, 0.0).astype(BF16)
        m2 = jnp.where(pos_ref[b, 1:2, :] == rid, 1.0, 0.0).astype(BF16)
        acc_sc[...] += jnp.dot(m1 + m2, x_ref[pl.ds(off, tile), :], preferred_element_type=F32)
        gb = g_ref[pl.ds(off, tile), :]
        gacc_sc[...] += jnp.where(glane < 3, jnp.dot(m1, gb, preferred_element_type=F32),
                                  jnp.dot(m2, gb, preferred_element_type=F32))
        return carry

    lax.fori_loop(blo_ref[j], bhi_ref[j] + 1, body, 0)
    xs_ref[...] = acc_sc[...].astype(xs_ref.dtype)
    gs_ref[...] = gacc_sc[...]


def moe_gather(x_bf, pos, gsplit, b_lo, b_hi, n_tiles):
    m, d = x_bf.shape
    tile = MOE_TILE
    resident = lambda shape: pl.BlockSpec(shape, lambda j, lo, hi: (0,) * len(shape), pipeline_mode=pl.Buffered(1))
    grid_spec = pltpu.PrefetchScalarGridSpec(
        num_scalar_prefetch=2, grid=(n_tiles,),
        in_specs=[resident((m, d)), resident(pos.shape), resident(gsplit.shape)],
        out_specs=[pl.BlockSpec((tile, d), lambda j, lo, hi: (j, 0)),
                   pl.BlockSpec((tile, LANES), lambda j, lo, hi: (j, 0))],
        scratch_shapes=[pltpu.VMEM((tile, d), F32), pltpu.VMEM((tile, LANES), F32)])
    return pl.pallas_call(
        functools.partial(_moe_gather_kernel, tile=tile), grid_spec=grid_spec,
        out_shape=[jax.ShapeDtypeStruct((n_tiles * tile, d), BF16),
                   jax.ShapeDtypeStruct((n_tiles * tile, LANES), F32)],
        compiler_params=pltpu.CompilerParams(dimension_semantics=("arbitrary",), vmem_limit_bytes=VMEM_LIMIT_RESIDENT),
        name="moe_gather")(b_lo, b_hi, x_bf, pos, gsplit)


def _moe_group_ffn_kernel(te_ref, nv_ref, x_ref, gs_ref, wg_ref, wu_ref, wd_ref, o_ref, acc_sc):
    j = pl.program_id(0)
    f = pl.program_id(1)

    @pl.when(f == 0)
    def _():
        acc_sc[...] = jnp.zeros(acc_sc.shape, F32)

    @pl.when(j < nv_ref[0])
    def _():
        x = x_ref[...]
        hg = jnp.dot(x, wg_ref[0], preferred_element_type=F32)
        hu = jnp.dot(x, wu_ref[0], preferred_element_type=F32)
        h = (jax.nn.silu(hg) * hu).astype(BF16)
        acc_sc[...] += jnp.dot(h, wd_ref[0], preferred_element_type=F32)

    @pl.when(f == pl.num_programs(1) - 1)
    def _():
        gs = gs_ref[...]
        lane = lax.broadcasted_iota(I32, gs.shape, 1)
        gate = jnp.sum(jnp.where(lane < 6, gs, 0.0), axis=-1, keepdims=True)
        o_ref[...] = acc_sc[...] * gate


def _ff_tile(dff, cap):
    return max(t for t in range(LANES, min(dff, cap) + 1, LANES) if dff % t == 0)


def moe_group_ffn(tile_expert, n_valid, x_sorted, gterms, wg, wu, wd, n_out_tiles, tf_cap=1792):
    tile = MOE_TILE
    n_tiles = x_sorted.shape[0] // tile
    d = x_sorted.shape[1]
    dff = wg.shape[2]
    tf = _ff_tile(dff, tf_cap)
    src = lambda j: jnp.minimum(j, n_tiles - 1)
    grid_spec = pltpu.PrefetchScalarGridSpec(
        num_scalar_prefetch=2, grid=(n_out_tiles, dff // tf),
        in_specs=[pl.BlockSpec((tile, d), lambda j, f, te, nv: (src(j), 0)),
                  pl.BlockSpec((tile, LANES), lambda j, f, te, nv: (src(j), 0)),
                  pl.BlockSpec((1, d, tf), lambda j, f, te, nv: (te[src(j)], 0, f)),
                  pl.BlockSpec((1, d, tf), lambda j, f, te, nv: (te[src(j)], 0, f)),
                  pl.BlockSpec((1, tf, d), lambda j, f, te, nv: (te[src(j)], f, 0))],
        out_specs=pl.BlockSpec((tile, d), lambda j, f, te, nv: (j, 0)),
        scratch_shapes=[pltpu.VMEM((tile, d), F32)])
    return pl.pallas_call(_moe_group_ffn_kernel, grid_spec=grid_spec,
                          out_shape=jax.ShapeDtypeStruct((n_out_tiles * tile, d), F32),
                          compiler_params=_cp("arbitrary", "arbitrary"), name="moe_group_ffn")(
                              tile_expert, n_valid, x_sorted, gterms, wg, wu, wd)


def _moe_combine_kernel(rlo_ref, short_ref, route_ref, r_ref, os_hbm, g_ref, b_ref, o_ref, obf_ref, buf, sem, acc_sc,
                        *, n_experts, windows, alpha):
    i = pl.program_id(0)
    acc_sc[...] = alpha * r_ref[...]
    route = route_ref[...]
    e1 = route[:, 0:1].astype(I32)
    e2 = route[:, 1:2].astype(I32)
    p1 = route[:, 2:3].astype(I32)
    p2 = route[:, 3:4].astype(I32)

    def accumulate(window):
        def window_copy(e, slot):
            lo = pl.multiple_of((rlo_ref[i * n_experts + e] // SUBLANES) * SUBLANES, SUBLANES)
            return lo, pltpu.make_async_copy(os_hbm.at[pl.ds(lo, window), :], buf.at[slot, pl.ds(0, window), :],
                                             sem.at[slot])

        window_copy(0, 0)[1].start()
        col = lax.broadcasted_iota(I32, (route.shape[0], window), 1)
        for e in range(n_experts):
            slot = e % 2
            if e + 1 < n_experts:
                window_copy(e + 1, 1 - slot)[1].start()
            lo, cp = window_copy(e, slot)
            cp.wait()
            c1 = jnp.where(e1 == e, p1 - lo, -1)
            c2 = jnp.where(e2 == e, p2 - lo, -1)
            pick = jnp.where(col == c1, 1.0, jnp.where(col == c2, 1.0, 0.0)).astype(BF16)
            rows = buf[slot, pl.ds(0, window), :]
            hi = rows.astype(BF16)
            lo_part = (rows - hi.astype(F32)).astype(BF16)
            acc_sc[...] += (jnp.dot(pick, hi, preferred_element_type=F32)
                            + jnp.dot(pick, lo_part, preferred_element_type=F32))

    short, full = windows

    @pl.when(short_ref[i] == 1)
    def _():
        accumulate(short)

    @pl.when(short_ref[i] != 1)
    def _():
        accumulate(full)

    out = _layernorm_rows(acc_sc[...], g_ref[...], b_ref[...])
    o_ref[...] = out
    obf_ref[...] = out.astype(BF16)


def moe_combine_ln(row_lo, short, route_pos, resid, out_sorted, g, b, alpha, n_experts, windows):
    m, d = resid.shape
    tile = MOE_TILE
    grid_spec = pltpu.PrefetchScalarGridSpec(
        num_scalar_prefetch=2, grid=(m // tile,),
        in_specs=[pl.BlockSpec((tile, LANES), lambda i, rlo, sh: (i, 0)),
                  pl.BlockSpec((tile, d), lambda i, rlo, sh: (i, 0)),
                  pl.BlockSpec(memory_space=pl.ANY),
                  pl.BlockSpec((1, d), lambda i, rlo, sh: (0, 0)), pl.BlockSpec((1, d), lambda i, rlo, sh: (0, 0))],
        out_specs=[pl.BlockSpec((tile, d), lambda i, rlo, sh: (i, 0))] * 2,
        scratch_shapes=[pltpu.VMEM((2, max(windows), d), F32), pltpu.SemaphoreType.DMA((2,)),
                        pltpu.VMEM((tile, d), F32)])
    return pl.pallas_call(
        functools.partial(_moe_combine_kernel, n_experts=n_experts, windows=windows, alpha=alpha),
        grid_spec=grid_spec,
        out_shape=[jax.ShapeDtypeStruct((m, d), F32), jax.ShapeDtypeStruct((m, d), BF16)],
        compiler_params=_cp("arbitrary"), name="moe_combine_ln")(
            row_lo, short, route_pos, resid, out_sorted, g.reshape(1, d), b.reshape(1, d))


def moe_top2_ln(x_bf, resid, w_router, wg, wu, wd, g, b, alpha):
    m, d = resid.shape
    ne = wg.shape[0]
    tile = MOE_TILE
    nb = m // tile
    n_tiles = (TOP_K * m) // tile + ne
    tail_tiles = -(-(tile + COMBINE_SLACK) // tile)
    route, before, count = router_plan(x_bf, w_router, ne)

    eid = jnp.arange(ne, dtype=I32)
    lookup = lambda table, idx: jnp.sum(jnp.where(idx[..., None] == eid, table, 0), axis=-1)
    cnt = count[0, :ne].astype(I32)
    tiles_e = (cnt + tile - 1) // tile
    first_tile = jnp.cumsum(tiles_e) - tiles_e
    n_valid = jnp.sum(tiles_e)
    row_start = first_tile * tile
    tid = jnp.arange(n_tiles, dtype=I32)
    tile_expert = jnp.minimum(jnp.sum((tid[:, None] >= (first_tile + tiles_e)[None, :]).astype(I32), axis=1), ne - 1)
    rank0 = (tid - lookup(first_tile, tile_expert)) * tile
    bef = before[:, 0, :ne].astype(I32)
    aft = jnp.concatenate([bef[1:], cnt[None, :]], axis=0)
    bef_t = lookup(bef[:, None, :], tile_expert[None, :])
    aft_t = lookup(aft[:, None, :], tile_expert[None, :])
    valid_t = tid < n_valid
    b_lo = jnp.where(valid_t, jnp.sum((aft_t <= rank0[None, :]).astype(I32), axis=0), 0)
    b_hi = jnp.where(valid_t, jnp.sum((bef_t < (rank0 + tile)[None, :]).astype(I32), axis=0) - 1, -1)
    e1, e2 = route[:, 0].astype(I32), route[:, 1].astype(I32)
    pos1 = lookup(row_start, e1) + route[:, 4].astype(I32)
    pos2 = lookup(row_start, e2) + route[:, 5].astype(I32)
    pos = jnp.stack([pos1.reshape(nb, tile), pos2.reshape(nb, tile)]
                    + [jnp.zeros((nb, tile), I32)] * (SUBLANES - 2), axis=1)

    def three_terms(v):
        hi = v.astype(BF16)
        mid = (v - hi.astype(F32)).astype(BF16)
        low = (v - hi.astype(F32) - mid.astype(F32)).astype(BF16)
        return [hi, mid, low]

    gsplit = jnp.stack(three_terms(route[:, 2]) + three_terms(route[:, 3]), axis=1)
    gsplit = jnp.pad(gsplit, ((0, 0), (0, LANES - gsplit.shape[1])))
    route_pos = jnp.stack([route[:, 0], route[:, 1], pos1.astype(F32), pos2.astype(F32)], axis=1)
    route_pos = jnp.pad(route_pos, ((0, 0), (0, LANES - route_pos.shape[1])))
    row_lo = row_start[None, :] + bef
    windows = (tile // 2, tile + COMBINE_SLACK)
    short = jnp.all(row_lo % SUBLANES + (aft - bef) <= windows[0], axis=1).astype(I32)

    x_sorted, gterms = moe_gather(x_bf, pos, gsplit, b_lo, b_hi, n_tiles)
    out_sorted = moe_group_ffn(tile_expert, n_valid.reshape(1), x_sorted, gterms, wg, wu, wd, n_tiles + tail_tiles)
    return moe_combine_ln(row_lo.reshape(-1), short, route_pos, resid, out_sorted, g, b, alpha, ne, windows)


NEG_INF_CODE = INT_MIN + 0x7FFFFF


def _code_to_f32(code):
    return pltpu.bitcast(jnp.where(code < 0, code ^ jnp.int32(0x7FFFFFFF), code), F32)


def _threshold_search(fold, topk, shape):
    kk = jnp.int32(topk)
    count = lambda pred: fold(lambda s, i: jnp.where(pred(s), 1, 0), "sum")
    c0 = count(lambda s: s >= 0.0)
    code = jnp.where(c0 >= kk, jnp.int32(0), jnp.int32(INT_MIN)) + jnp.zeros(shape, I32)

    def bit_body(j, code):
        cand = code | (jnp.int32(1) << (30 - j))
        cand_f = _code_to_f32(cand)
        return jnp.where(count(lambda s: s >= cand_f) >= kk, cand, code)

    code = lax.fori_loop(0, 31, bit_body, code)
    enough = code > NEG_INF_CODE
    floor_t = _code_to_f32(code)
    t = jnp.where(enough, fold(lambda s, i: jnp.where(s >= floor_t, s, jnp.inf), "min"), -jnp.inf)
    need = kk - count(lambda s: s > t)
    return t, jnp.where(enough, need, 0)


def _topk_threshold(fold, topk, n_idx_bits, rows):
    t, need = _threshold_search(fold, topk, (rows, 1))

    def tie_body(j, p):
        cand = p | (jnp.int32(1) << (n_idx_bits - 1 - j))
        got = fold(lambda s, i: jnp.where(s == t, jnp.where(i < cand, 1, 0), 0), "sum")
        return jnp.where(got <= need, cand, p)

    p = lax.fori_loop(0, n_idx_bits, tie_body, jnp.zeros((rows, 1), I32))
    return t, jnp.where(need > 0, p, 0)


CHUNK_GROUP = 4


def _dsa_prompt_kernel(qt_ref, k_ref, vt_ref, qit_ref, ki_ref, wit_ref, bias_ref, ltri_ref, o_ref,
                       score_sc, m_sc, l_sc, acc_sc, *, topk):
    i = pl.program_id(1)
    n_groups = (i + CHUNK_GROUP) // CHUNK_GROUP
    krow = lax.broadcasted_iota(I32, (TQ, TQ), 0)
    qcol = lax.broadcasted_iota(I32, (TQ, TQ), 1)
    qit = qit_ref[0, 0] * jnp.asarray(IDX_DIM ** -0.5, BF16)
    qit_heads = jnp.concatenate([qit[h * IDX_DIM:(h + 1) * IDX_DIM, :] for h in range(IDX_HEADS)], axis=1)
    wt = wit_ref[0, 0] * (IDX_HEADS ** -0.5)

    def score_body(g, carry):
        base = pl.multiple_of(g * (CHUNK_GROUP * TQ), CHUNK_GROUP * TQ)
        for j in range(CHUNK_GROUP):
            off = base + j * TQ
            s_heads = jnp.dot(ki_ref[0, pl.ds(off, TQ), :], qit_heads, preferred_element_type=F32)
            sc = jnp.zeros((TQ, TQ), F32)
            for h in range(IDX_HEADS):
                sc = sc + wt[h:h + 1, :] * jnp.maximum(s_heads[:, h * TQ:(h + 1) * TQ], 0.0)
            score_sc[pl.ds(off, TQ), :] = jnp.where(off + krow <= i * TQ + qcol, sc, -jnp.inf)
        return carry

    lax.fori_loop(0, n_groups, score_body, 0)

    def fold(fn, kind):
        init, join, finish = {"sum": (0, jnp.add, jnp.sum), "min": (jnp.inf, jnp.minimum, jnp.min)}[kind]

        def body(g, acc):
            base = pl.multiple_of(g * (CHUNK_GROUP * TQ), CHUNK_GROUP * TQ)
            for j in range(CHUNK_GROUP):
                acc = join(acc, fn(score_sc[pl.ds(base + j * TQ, TQ), :], None))
            return acc

        acc = lax.fori_loop(0, n_groups, body, jnp.full((TQ, TQ), init, I32 if kind == "sum" else F32))
        return finish(acc, axis=0, keepdims=True)

    t, need = _threshold_search(fold, topk, (1, TQ))
    need = need.astype(F32)

    m_sc[...] = jnp.full(m_sc.shape, NEG, F32)
    l_sc[...] = jnp.zeros(l_sc.shape, F32)
    acc_sc[...] = jnp.zeros(acc_sc.shape, F32)
    qt = qt_ref[0, 0] * jnp.asarray(A_HEAD_DIM ** -0.5, BF16)

    def att_body(c, ties_before):
        off = pl.multiple_of(c * TQ, TQ)
        key = score_sc[pl.ds(off, TQ), :]
        tie = key == t
        rank = ties_before + jnp.dot(ltri_ref[...], jnp.where(tie, 1.0, 0.0).astype(BF16),
                                     preferred_element_type=F32)
        sel = jnp.where(key > t, 1, jnp.where(tie, jnp.where(rank <= need, 1, 0), 0)) > 0
        kc = k_ref[0, pl.ds(off, TQ), :]
        bsel = jnp.minimum(i - c, 2)
        for h in range(A_HEADS):
            sl = slice(h * A_HEAD_DIM, (h + 1) * A_HEAD_DIM)
            s = jnp.dot(kc[:, sl], qt[sl, :], preferred_element_type=F32)
            s = jnp.where(sel, s + bias_ref[bsel, h], NEG)
            m_prev = m_sc[h, 0:1, :]
            m_new = jnp.maximum(m_prev, jnp.max(s, axis=0, keepdims=True))
            alpha = jnp.exp(m_prev - m_new)
            pe = jnp.exp(s - m_new)
            l_sc[h, 0:1, :] = alpha * l_sc[h, 0:1, :] + jnp.sum(pe, axis=0, keepdims=True)
            acc_sc[sl, :] = alpha * acc_sc[sl, :] + jnp.dot(vt_ref[0, c, sl, :], pe.astype(BF16),
                                                            preferred_element_type=F32)
            m_sc[h, 0:1, :] = m_new
        return rank[TQ - 1:TQ, :]

    lax.fori_loop(0, i + 1, att_body, jnp.zeros((1, TQ), F32))
    for h in range(A_HEADS):
        sl = slice(h * A_HEAD_DIM, (h + 1) * A_HEAD_DIM)
        acc_sc[sl, :] = acc_sc[sl, :] / l_sc[h, 0:1, :]
    o_ref[0] = acc_sc[...].T.astype(o_ref.dtype)


def dsa_prompt(qt, k, vt, qit, ki, wit, bias_tab):
    bsz, t, _ = k.shape
    assert TQ == LANES and t % (CHUNK_GROUP * TQ) == 0
    topk = min(TOPK_MAX, t // 4)
    nq = t // TQ
    ltri = (lax.broadcasted_iota(I32, (TQ, TQ), 1) <= lax.broadcasted_iota(I32, (TQ, TQ), 0)).astype(BF16)
    per_q = lambda rows: pl.BlockSpec((1, 1, rows, TQ), lambda b, i: (b, i, 0, 0))
    per_b = lambda w: pl.BlockSpec((1, t, w), lambda b, i: (b, 0, 0))
    stat = pltpu.VMEM((A_HEADS, SUBLANES, TQ), F32)
    return pl.pallas_call(
        functools.partial(_dsa_prompt_kernel, topk=topk), grid=(bsz, nq),
        in_specs=[per_q(A_WIDTH), per_b(A_WIDTH), pl.BlockSpec((1, nq, A_WIDTH, TQ), lambda b, i: (b, 0, 0, 0)),
                  per_q(IDX_HEADS * IDX_DIM), per_b(IDX_DIM), per_q(SUBLANES), _full(bias_tab.shape),
                  _full((TQ, TQ))],
        out_specs=pl.BlockSpec((1, TQ, A_WIDTH), lambda b, i: (b, i, 0)),
        out_shape=jax.ShapeDtypeStruct((bsz, t, A_WIDTH), BF16),
        scratch_shapes=[pltpu.VMEM((t, TQ), F32), stat, stat, pltpu.VMEM((A_WIDTH, TQ), F32)],
        compiler_params=_cp("parallel", "arbitrary"), name="dsa_prompt")(qt, k, vt, qit, ki, wit, bias_tab, ltri)


def _row_to_col(row):
    n = row.shape[1]
    eye = lax.broadcasted_iota(I32, (n, n), 0) == lax.broadcasted_iota(I32, (n, n), 1)
    return jnp.sum(jnp.where(eye, jnp.broadcast_to(row, (n, n)), 0.0), axis=1, keepdims=True)


def _col_to_row(col):
    n = col.shape[0]
    eye = lax.broadcasted_iota(I32, (n, n), 0) == lax.broadcasted_iota(I32, (n, n), 1)
    return jnp.sum(jnp.where(eye, jnp.broadcast_to(col, (n, n)), 0.0), axis=0, keepdims=True)


def _dsa_s_score_kernel(pt_ref, qi_ref, w_ref, *refs, n_pages):
    page_refs, o_ref = refs[:n_pages], refs[n_pages]
    w = _as_mxu_operand(w_ref[0] * (IDX_HEADS ** -0.5))
    qcols = [_row_to_col(_as_mxu_operand(qi_ref[0, h:h + 1, :] * (IDX_DIM ** -0.5))) for h in range(IDX_HEADS)]
    for p in range(n_pages):
        page = _as_mxu_operand(page_refs[p][0])
        sc = jnp.zeros((1, PAGE_SIZE), F32)
        for h in range(IDX_HEADS):
            s = jnp.sum(page * qcols[h], axis=0, keepdims=True)
            sc = sc + w[:, h:h + 1] * _as_mxu_operand(jnp.maximum(s, 0.0))
        o_ref[0, :, p * PAGE_SIZE:(p + 1) * PAGE_SIZE] = sc


def dsa_sample_scores(page_table, qi8, wi, kidx_pool_t):
    s, n_pages = page_table.shape
    pt = page_table.reshape(-1)
    page_spec = lambda p: pl.BlockSpec((1, IDX_DIM, PAGE_SIZE), lambda i, pt_ref: (pt_ref[i * n_pages + p], 0, 0))
    grid_spec = pltpu.PrefetchScalarGridSpec(
        num_scalar_prefetch=1, grid=(s,),
        in_specs=[pl.BlockSpec((1, SUBLANES, IDX_DIM), lambda i, pt_ref: (i, 0, 0)),
                  pl.BlockSpec((1, 1, LANES), lambda i, pt_ref: (i, 0, 0))] + [page_spec(p) for p in range(n_pages)],
        out_specs=pl.BlockSpec((1, 1, n_pages * PAGE_SIZE), lambda i, pt_ref: (i, 0, 0)))
    return pl.pallas_call(functools.partial(_dsa_s_score_kernel, n_pages=n_pages), grid_spec=grid_spec,
                          out_shape=jax.ShapeDtypeStruct((s, 1, n_pages * PAGE_SIZE), F32),
                          compiler_params=_cp("arbitrary"), name="dsa_sample_scores")(
                              pt, qi8, wi, *([kidx_pool_t] * n_pages))


def _dsa_s_select_kernel(sc_ref, qi_ref, kn_ref, wi_ref, mask_ref, mnew_ref, *, topk, n_idx_bits):
    s, lp = mask_ref.shape
    w = _as_mxu_operand(wi_ref[...] * (IDX_HEADS ** -0.5))
    sc_new = jnp.zeros((s, 1), F32)
    kn = _as_mxu_operand(kn_ref[...])
    for h in range(IDX_HEADS):
        s_new = jnp.sum(_as_mxu_operand(qi_ref[:, h * IDX_DIM:(h + 1) * IDX_DIM]) * kn, axis=-1, keepdims=True)
        sc_new = sc_new + w[:, h:h + 1] * _as_mxu_operand(jnp.maximum(s_new * (IDX_DIM ** -0.5), 0.0))
    col = lax.broadcasted_iota(I32, (s, lp), 1)

    def fold(fn, kind):
        join, finish = {"sum": (jnp.add, jnp.sum), "min": (jnp.minimum, jnp.min)}[kind]
        main = finish(fn(sc_ref[...], col), axis=1, keepdims=True)
        return join(main, fn(sc_new, jnp.int32(lp)))

    t, p = _topk_threshold(fold, topk, n_idx_bits, s)
    sc = sc_ref[...]
    mask_ref[...] = jnp.where(sc > t, 1.0, jnp.where(sc == t, jnp.where(col < p, 1.0, 0.0), 0.0))
    sel_new = jnp.where(sc_new > t, 1.0, jnp.where(sc_new == t, jnp.where(lp < p, 1.0, 0.0), 0.0))
    mnew_ref[...] = jnp.broadcast_to(sel_new, mnew_ref.shape)


def dsa_sample_select(raw, qi, ki_new, wi):
    s, lp = raw.shape
    length = lp + 1
    topk = min(TOPK_MAX, length // 4)
    kern = functools.partial(_dsa_s_select_kernel, topk=topk, n_idx_bits=length.bit_length())
    return pl.pallas_call(kern, out_shape=[jax.ShapeDtypeStruct((s, lp), F32), jax.ShapeDtypeStruct((s, LANES), F32)],
                          compiler_params=pltpu.CompilerParams(vmem_limit_bytes=VMEM_LIMIT),
                          name="dsa_sample_select")(raw, qi, ki_new, wi)


def _dsa_s_attn_kernel(pt_ref, q_ref, kn_ref, vn_ref, mask_ref, mnew_ref, bias_ref, bnew_ref, *refs, n_pages):
    k_refs, v_refs, o_ref = refs[:n_pages], refs[n_pages:2 * n_pages], refs[2 * n_pages]
    hd = A_HEAD_DIM
    q = _as_mxu_operand(q_ref[0] * (hd ** -0.5))
    qcols = [_row_to_col(q[h:h + 1, :]) for h in range(A_HEADS)]
    hrow = lax.broadcasted_iota(I32, (A_HEADS, PAGE_SIZE), 0)
    lg_new = jnp.sum(q * _as_mxu_operand(kn_ref[0]), axis=1, keepdims=True) + bnew_ref[:, 0:1]
    lg_new = jnp.where(mnew_ref[0][:, 0:1] > 0.0, lg_new, NEG)
    logits = []
    mx = lg_new
    for p in range(n_pages):
        lg = jnp.zeros((A_HEADS, PAGE_SIZE), F32)
        for h in range(A_HEADS):
            kp = _as_mxu_operand(k_refs[p][0, h * hd:(h + 1) * hd, :])
            s = jnp.sum(kp * qcols[h], axis=0, keepdims=True)
            lg = jnp.where(hrow == h, jnp.broadcast_to(s, lg.shape), lg)
        lg = jnp.where(mask_ref[0, p:p + 1, :] > 0.0, lg + bias_ref[p], NEG)
        logits.append(lg)
        mx = jnp.maximum(mx, jnp.max(lg, axis=1, keepdims=True))
    pe_new = jnp.exp(lg_new - mx)
    pes = [jnp.exp(lg - mx) for lg in logits]
    den = pe_new
    for pe in pes:
        den = den + jnp.sum(pe, axis=1, keepdims=True)
    probs = [_as_mxu_operand(pe / den) for pe in pes]
    p_new = _as_mxu_operand(pe_new / den)
    for h in range(A_HEADS):
        acc = jnp.zeros((hd, PAGE_SIZE), F32)
        for p in range(n_pages):
            acc = acc + probs[p][h:h + 1, :] * _as_mxu_operand(v_refs[p][0, h * hd:(h + 1) * hd, :])
        o_ref[0, h:h + 1, :] = (_col_to_row(jnp.sum(acc, axis=1, keepdims=True))
                                + p_new[h:h + 1, :] * _as_mxu_operand(vn_ref[0, h:h + 1, :]))


def dsa_sample_attn(page_table, q, k_new, v_new, mask, mask_new, bias_pages, bias_new, k_pool_t, v_pool_t):
    s, n_pages = page_table.shape
    pt = page_table.reshape(-1)
    per_s = lambda r, w: pl.BlockSpec((1, r, w), lambda i, pt_ref: (i, 0, 0))
    const = lambda shape: pl.BlockSpec(shape, lambda i, pt_ref: (0,) * len(shape))
    page_spec = lambda p: pl.BlockSpec((1, A_WIDTH, PAGE_SIZE), lambda i, pt_ref: (pt_ref[i * n_pages + p], 0, 0))
    hd = (A_HEADS, A_HEAD_DIM)
    grid_spec = pltpu.PrefetchScalarGridSpec(
        num_scalar_prefetch=1, grid=(s,),
        in_specs=[per_s(*hd), per_s(*hd), per_s(*hd), per_s(n_pages, PAGE_SIZE), per_s(1, LANES),
                  const(bias_pages.shape), const(bias_new.shape)]
                 + [page_spec(p) for p in range(n_pages)] * 2,
        out_specs=per_s(*hd))
    return pl.pallas_call(functools.partial(_dsa_s_attn_kernel, n_pages=n_pages), grid_spec=grid_spec,
                          out_shape=jax.ShapeDtypeStruct((s,) + hd, F32),
                          compiler_params=_cp("arbitrary"), name="dsa_sample_attn")(
                              pt, q, k_new, v_new, mask, mask_new, bias_pages, bias_new,
                              *([k_pool_t] * n_pages), *([v_pool_t] * n_pages))


def _rglru_gates(xc, wa, ba, wi, bi, lam):
    xb = xc.astype(wa.dtype)
    r = jax.nn.sigmoid(_mm(xb, wa) + ba)
    ig = jax.nn.sigmoid(_mm(xb, wi) + bi)
    log_a = -RG_C * r * jax.nn.softplus(-lam)
    a = jnp.exp(log_a)
    u = jnp.sqrt(1.0 - jnp.exp(2.0 * log_a)) * (ig * xc)
    return a, u


def _rglru_prompt_kernel(xr_ref, gr_ref, cw_ref, cb_ref, wa_ref, ba_ref, wi_ref, bi_ref, lam_ref,
                         y_ref, hl_ref, xs_sc, h_sc, *, tb):
    t = pl.program_id(1)
    pad = SUBLANES

    @pl.when(t == 0)
    def _():
        xs_sc[0:pad, :] = jnp.zeros((pad, xs_sc.shape[1]), F32)
        h_sc[...] = jnp.zeros(h_sc.shape, F32)

    xs_sc[pad:pad + tb, :] = xr_ref[0]
    xc = cb_ref[...] + cw_ref[RG_CONV - 1:RG_CONV, :] * xs_sc[pad:pad + tb, :]
    for j in range(1, RG_CONV):
        xc = xc + cw_ref[RG_CONV - 1 - j:RG_CONV - j, :] * xs_sc[pad - j:pad - j + tb, :]
    a, u = _rglru_gates(xc, wa_ref[...], ba_ref[...], wi_ref[...], bi_ref[...], lam_ref[...])
    rowi = lax.broadcasted_iota(I32, a.shape, 0)
    s = 1
    while s < tb:
        keep = rowi >= s
        a_sh = jnp.where(keep, pltpu.roll(a, s, 0), 1.0)
        u_sh = jnp.where(keep, pltpu.roll(u, s, 0), 0.0)
        u = a * u_sh + u
        a = a * a_sh
        s *= 2
    h = a * h_sc[0:1, :] + u
    y_ref[0] = (h * jax.nn.gelu(gr_ref[0])).astype(y_ref.dtype)
    h_sc[0:1, :] = h[tb - 1:tb, :]
    xs_sc[0:pad, :] = xs_sc[tb:tb + pad, :]

    @pl.when(t == pl.num_programs(1) - 1)
    def _():
        hl_ref[0] = h[tb - 1:tb, :]


def rglru_prompt(xr, gr, cw, cb, wa, ba, wi, bi, lam, tb_cap=256):
    bsz, t, w = xr.shape
    tb = _row_tile(t, tb_cap)
    blk = pl.BlockSpec((1, tb, w), lambda b, i: (b, i, 0))
    vec = _full((1, w))
    return pl.pallas_call(
        functools.partial(_rglru_prompt_kernel, tb=tb), grid=(bsz, t // tb),
        in_specs=[blk, blk, _full((RG_CONV, w)), vec, _full((w, w)), vec, _full((w, w)), vec, vec],
        out_specs=[blk, pl.BlockSpec((1, 1, w), lambda b, i: (b, 0, 0))],
        out_shape=[jax.ShapeDtypeStruct((bsz, t, w), BF16), jax.ShapeDtypeStruct((bsz, 1, w), F32)],
        scratch_shapes=[pltpu.VMEM((tb + SUBLANES, w), F32), pltpu.VMEM((SUBLANES, w), F32)],
        compiler_params=_cp("parallel", "arbitrary"), name="rglru_prompt")(
            xr, gr, cw, cb.reshape(1, w), wa, ba.reshape(1, w), wi, bi.reshape(1, w), lam.reshape(1, w))


def _rglru_step_kernel(xr_ref, gr_ref, b0_ref, b1_ref, b2_ref, h0_ref, cw_ref, cb_ref, wa_ref, ba_ref, wi_ref,
                       bi_ref, lam_ref, y_ref, h_ref):
    xc = (cb_ref[...] + cw_ref[0:1, :] * b0_ref[...] + cw_ref[1:2, :] * b1_ref[...]
          + cw_ref[2:3, :] * b2_ref[...] + cw_ref[3:4, :] * xr_ref[...])
    a, u = _rglru_gates(xc, wa_ref[...], ba_ref[...], wi_ref[...], bi_ref[...], lam_ref[...])
    h = a * h0_ref[...] + u
    h_ref[...] = h
    y_ref[...] = (h * jax.nn.gelu(gr_ref[...])).astype(y_ref.dtype)


def rglru_step(xr, gr, buf, h0, cw, cb, wa, ba, wi, bi, lam):
    s, w = xr.shape
    return pl.pallas_call(
        _rglru_step_kernel,
        out_shape=[jax.ShapeDtypeStruct((s, w), F32), jax.ShapeDtypeStruct((s, w), F32)],
        compiler_params=pltpu.CompilerParams(vmem_limit_bytes=VMEM_LIMIT), name="rglru_step")(
            xr, gr, buf[:, 0], buf[:, 1], buf[:, 2], h0, cw, cb.reshape(1, w), wa, ba.reshape(1, w),
            wi, bi.reshape(1, w), lam.reshape(1, w))


def _headnorm_gate(h, g, og):
    mu = jnp.mean(h, axis=-1, keepdims=True)
    hc = h - mu
    var = jnp.mean(hc * hc, axis=-1, keepdims=True)
    return hc * lax.rsqrt(var + LN_EPS) * g * jax.nn.sigmoid(og)


def _mlstm_prompt_kernel(q_ref, k_ref, v_ref, og_ref, gt_ref, bg_ref, ng_ref, tri_ref,
                         y_ref, c_ref, n_ref, m_ref, c_sc, n_sc, m_sc, *, cl, dh):
    ci = pl.program_id(1)

    @pl.when(ci == 0)
    def _():
        c_sc[...] = jnp.zeros(c_sc.shape, F32)
        n_sc[...] = jnp.zeros(n_sc.shape, F32)
        m_sc[...] = jnp.zeros(m_sc.shape, F32)

    pre = gt_ref[0] + bg_ref[...]
    lane = lax.broadcasted_iota(I32, pre.shape, 1)
    gv = jnp.where(lane < ML_HEADS, pre, jax.nn.log_sigmoid(pre))
    csum = jnp.dot(tri_ref[...], gv, preferred_element_type=F32, precision=HI)
    gv_t = gv.T
    csum_t = csum.T
    row = lax.broadcasted_iota(I32, (cl, cl), 0)
    col = lax.broadcasted_iota(I32, (cl, cl), 1)
    causal = col <= row
    kscale = dh ** -0.5
    for h in range(ML_HEADS):
        sl = slice(h * dh, (h + 1) * dh)
        qh = q_ref[0, :, sl]
        kh = k_ref[0, :, sl] * jnp.asarray(kscale, BF16)
        vh = v_ref[0, :, sl]
        b_col = csum[:, ML_HEADS + h:ML_HEADS + h + 1]
        i_col = gv[:, h:h + 1]
        b_row = csum_t[ML_HEADS + h:ML_HEADS + h + 1, :]
        i_row = gv_t[h:h + 1, :]
        m_prev = m_sc[h:h + 1, 0:1]
        c_prev = c_sc[h]
        n_prev = n_sc[h:h + 1, :]
        dmat = jnp.where(causal, b_col - b_row + i_row, NEG)
        inter = b_col + m_prev
        m_t = jnp.maximum(inter, jnp.max(dmat, axis=1, keepdims=True))
        g_inter = jnp.exp(inter - m_t)
        qk = lax.dot_general(qh, kh, NT, preferred_element_type=F32)
        ws = jnp.exp(dmat - m_t) * qk
        num = (g_inter * lax.dot_general(qh, c_prev.astype(BF16), NT, preferred_element_type=F32)
               + jnp.dot(ws.astype(BF16), vh, preferred_element_type=F32))
        qf = qh.astype(F32)
        den = g_inter * jnp.sum(qf * n_prev, axis=1, keepdims=True) + jnp.sum(ws, axis=1, keepdims=True)
        hout = num / jnp.maximum(jnp.abs(den), jnp.exp(-m_t))
        m_new = m_t[cl - 1:cl, :]
        b_last = b_col[cl - 1:cl, :]
        g_c = jnp.exp(b_last + m_prev - m_new)
        w_end = jnp.exp(b_last - b_col + i_col - m_new)
        kf = kh.astype(F32)
        vw = (vh.astype(F32) * w_end).astype(BF16)
        c_sc[h] = g_c * c_prev + lax.dot_general(vw, kh, TN, preferred_element_type=F32)
        n_sc[h:h + 1, :] = g_c * n_prev + jnp.sum(kf * w_end, axis=0, keepdims=True)
        m_sc[h:h + 1, :] = jnp.broadcast_to(m_new, (1, m_sc.shape[1]))
        y_ref[0, :, sl] = _headnorm_gate(hout, ng_ref[:, sl], og_ref[0, :, sl]).astype(y_ref.dtype)

    @pl.when(ci == pl.num_programs(1) - 1)
    def _():
        c_ref[0] = c_sc[...]
        n_ref[0] = n_sc[...]
        m_ref[0] = m_sc[...]


def mlstm_prompt(q, k, v, og, gates, bgate, norm_g):
    bsz, t, width = q.shape
    dh = width // ML_HEADS
    cl = math.gcd(t, ML_CHUNK)
    tri = (lax.broadcasted_iota(I32, (cl, cl), 1) <= lax.broadcasted_iota(I32, (cl, cl), 0)).astype(F32)
    blk = lambda w: pl.BlockSpec((1, cl, w), lambda b, i: (b, i, 0))
    return pl.pallas_call(
        functools.partial(_mlstm_prompt_kernel, cl=cl, dh=dh), grid=(bsz, t // cl),
        in_specs=[blk(width), blk(width), blk(width), blk(width), blk(LANES), _full((1, LANES)), _full((1, width)),
                  _full((cl, cl))],
        out_specs=[blk(width), pl.BlockSpec((1, ML_HEADS, dh, dh), lambda b, i: (b, 0, 0, 0)),
                   pl.BlockSpec((1, SUBLANES, dh), lambda b, i: (b, 0, 0)),
                   pl.BlockSpec((1, SUBLANES, LANES), lambda b, i: (b, 0, 0))],
        out_shape=[jax.ShapeDtypeStruct((bsz, t, width), BF16), jax.ShapeDtypeStruct((bsz, ML_HEADS, dh, dh), F32),
                   jax.ShapeDtypeStruct((bsz, SUBLANES, dh), F32), jax.ShapeDtypeStruct((bsz, SUBLANES, LANES), F32)],
        scratch_shapes=[pltpu.VMEM((ML_HEADS, dh, dh), F32), pltpu.VMEM((SUBLANES, dh), F32),
                        pltpu.VMEM((SUBLANES, LANES), F32)],
        compiler_params=_cp("parallel", "arbitrary"), name="mlstm_prompt")(
            q, k, v, og, gates, bgate, norm_g.reshape(1, width), tri)


def _mlstm_step_kernel(q_ref, k_ref, v_ref, og_ref, gt_ref, bg_ref, ng_ref, c0_ref, n0_ref, m0_ref,
                       y_ref, c_ref, n_ref, m_ref, *, sb, dh):
    kscale = dh ** -0.5
    m_ref[...] = jnp.zeros(m_ref.shape, F32)

    def seq_body(s, carry):
        rs = pl.ds(s, 1)
        pre = gt_ref[rs, :] + bg_ref[...]
        for h in range(ML_HEADS):
            sl = slice(h * dh, (h + 1) * dh)
            qh = q_ref[rs, sl]
            kh = k_ref[rs, sl] * kscale
            vh = v_ref[rs, sl]
            i_pre = pre[:, h:h + 1]
            logf = jax.nn.log_sigmoid(pre[:, ML_HEADS + h:ML_HEADS + h + 1])
            m_prev = m0_ref[rs, h:h + 1]
            c_prev = c0_ref[s, h]
            n_prev = n0_ref[s, h:h + 1, :]
            inter = logf + m_prev
            m_t = jnp.maximum(inter, i_pre)
            g_inter = jnp.exp(inter - m_t)
            w_in = jnp.exp(i_pre - m_t)
            q16 = _as_mxu_operand(qh)
            qk = jnp.sum(q16 * _as_mxu_operand(kh), axis=1, keepdims=True)
            q8 = jnp.broadcast_to(qh, (SUBLANES, dh)).astype(BF16)
            cq = lax.dot_general(q8, c_prev.astype(BF16), NT, preferred_element_type=F32)[0:1]
            num = g_inter * cq + (w_in * qk) * vh
            den = g_inter * jnp.sum(_as_mxu_operand(n_prev) * q16, axis=1, keepdims=True) + w_in * qk
            hout = num / jnp.maximum(jnp.abs(den), jnp.exp(-m_t))
            rsel = lax.broadcasted_iota(I32, (SUBLANES, dh), 0) == 0
            v8 = jnp.where(rsel, jnp.broadcast_to(w_in * vh, (SUBLANES, dh)), 0.0)
            k8 = jnp.broadcast_to(kh, (SUBLANES, dh))
            c_ref[s, h] = g_inter * c_prev + lax.dot_general(v8, k8, TN, preferred_element_type=F32, precision=HI)
            n_ref[s, h:h + 1, :] = g_inter * n_prev + w_in * kh
            m_ref[rs, h:h + 1] = m_t
            y_ref[rs, sl] = _headnorm_gate(hout, ng_ref[:, sl], og_ref[rs, sl]).astype(y_ref.dtype)
        return carry

    lax.fori_loop(0, sb, seq_body, 0)


def mlstm_step(q, k, v, og, gates, bgate, norm_g, c0, n0, m0, sb=SUBLANES):
    s, width = q.shape
    dh = width // ML_HEADS
    assert s % sb == 0
    row = lambda w: pl.BlockSpec((sb, w), lambda i: (i, 0))
    m0p = jnp.pad(m0, ((0, 0), (0, LANES - ML_HEADS)))
    outs = pl.pallas_call(
        functools.partial(_mlstm_step_kernel, sb=sb, dh=dh), grid=(s // sb,),
        in_specs=[row(width), row(width), row(width), row(width), row(LANES), _full((1, LANES)), _full((1, width)),
                  pl.BlockSpec((sb, ML_HEADS, dh, dh), lambda i: (i, 0, 0, 0)),
                  pl.BlockSpec((sb, ML_HEADS, dh), lambda i: (i, 0, 0)), row(LANES)],
        out_specs=[row(width), pl.BlockSpec((sb, ML_HEADS, dh, dh), lambda i: (i, 0, 0, 0)),
                   pl.BlockSpec((sb, ML_HEADS, dh), lambda i: (i, 0, 0)), row(LANES)],
        out_shape=[jax.ShapeDtypeStruct((s, width), F32), jax.ShapeDtypeStruct((s, ML_HEADS, dh, dh), F32),
                   jax.ShapeDtypeStruct((s, ML_HEADS, dh), F32), jax.ShapeDtypeStruct((s, LANES), F32)],
        compiler_params=_cp("parallel"), name="mlstm_step")(
            q, k, v, og, gates, bgate, norm_g.reshape(1, width), c0, n0, m0p)
    y, c, n, m = outs
    return y, c, n, m[:, :ML_HEADS]


def _xattn_prompt_kernel(q_ref, k_ref, v_ref, o_ref, *, dh):
    scale = dh ** -0.5
    for h in range(X_HEADS):
        sl = slice(h * dh, (h + 1) * dh)
        lg = lax.dot_general(q_ref[0, :, sl], k_ref[0, :, sl], NT, preferred_element_type=F32) * scale
        mx = jnp.max(lg, axis=-1, keepdims=True)
        e = jnp.exp(lg - mx)
        pr = e / jnp.sum(e, axis=-1, keepdims=True)
        o_ref[0, :, sl] = jnp.dot(pr.astype(BF16), v_ref[0, :, sl], preferred_element_type=F32).astype(o_ref.dtype)


def xattn_prompt(xq, mk, mv, tq_cap=512):
    bsz, t, d = xq.shape
    mt = mk.shape[1]
    tq = _row_tile(t, tq_cap)
    return pl.pallas_call(
        functools.partial(_xattn_prompt_kernel, dh=d // X_HEADS), grid=(bsz, t // tq),
        in_specs=[pl.BlockSpec((1, tq, d), lambda b, i: (b, i, 0)), pl.BlockSpec((1, mt, d), lambda b, i: (b, 0, 0)),
                  pl.BlockSpec((1, mt, d), lambda b, i: (b, 0, 0))],
        out_specs=pl.BlockSpec((1, tq, d), lambda b, i: (b, i, 0)),
        out_shape=jax.ShapeDtypeStruct((bsz, t, d), BF16),
        compiler_params=_cp("parallel", "parallel"), name="xattn_prompt")(xq, mk, mv)


def _xattn_step_kernel(q_ref, k_ref, v_ref, o_ref, *, sb, mt, dh):
    scale = dh ** -0.5
    parts = dh // LANES
    rows = X_HEADS * parts

    def seq_body(s, carry):
        rs = pl.ds(s, 1)
        qrow = _as_mxu_operand(q_ref[rs, :])
        pieces = []
        for h in range(X_HEADS):
            lg = jnp.zeros((mt, 1), F32)
            for c in range(parts):
                qp = qrow[:, h * dh + c * LANES:h * dh + (c + 1) * LANES]
                kp = _as_mxu_operand(k_ref[0, s, pl.ds(c * X_HEADS + h, mt, stride=rows), :])
                lg = lg + jnp.sum(kp * qp, axis=1, keepdims=True)
            lg = lg * scale
            e = jnp.exp(lg - jnp.max(lg, axis=0, keepdims=True))
            pr = _as_mxu_operand(e / jnp.sum(e, axis=0, keepdims=True))
            for c in range(parts):
                vp = _as_mxu_operand(v_ref[0, s, pl.ds(c * X_HEADS + h, mt, stride=rows), :])
                pieces.append(jnp.sum(pr * vp, axis=0, keepdims=True))
        o_ref[rs, :] = jnp.concatenate(pieces, axis=1)
        return carry

    lax.fori_loop(0, sb, seq_body, 0)


def _lane_chunk_view(mem):
    nl, s, mt, nh, dh = mem.shape
    v = mem.reshape(nl, s, mt, nh, dh // LANES, LANES)
    return jnp.transpose(v, (0, 1, 2, 4, 3, 5)).reshape(nl, s, mt * (dh // LANES) * nh, LANES)


def xattn_step(xq, mk_view, mv_view, layer, mt, sb=SUBLANES):
    s, d = xq.shape
    dh = d // X_HEADS
    assert s % sb == 0 and dh % LANES == 0
    nrows = mk_view.shape[2]
    mem_spec = pl.BlockSpec((1, sb, nrows, LANES), lambda i: (layer, i, 0, 0))
    return pl.pallas_call(
        functools.partial(_xattn_step_kernel, sb=sb, mt=mt, dh=dh), grid=(s // sb,),
        in_specs=[pl.BlockSpec((sb, d), lambda i: (i, 0)), mem_spec, mem_spec],
        out_specs=pl.BlockSpec((sb, d), lambda i: (i, 0)),
        out_shape=jax.ShapeDtypeStruct((s, d), F32),
        compiler_params=_cp("parallel"), name="xattn_step")(xq, mk_view, mv_view)


def _t5_bucket(dist):
    max_exact = REL_BUCKETS // 2
    d = jnp.maximum(dist.astype(F32), 1.0)
    large = max_exact + (jnp.log(d / max_exact) / math.log(REL_MAX_DIST / max_exact)
                         * (REL_BUCKETS - max_exact)).astype(I32)
    large = jnp.minimum(large, REL_BUCKETS - 1)
    return jnp.where(dist < max_exact, dist, large)


def _pad_cols(w, n):
    return jnp.pad(w, ((0, 0), (0, n - w.shape[1])))


def _block_diag(w):
    nb, di, do = w.shape
    eye = jnp.eye(nb, dtype=w.dtype)
    return jnp.einsum('nde,nm->ndme', w, eye).reshape(nb * di, nb * do)


def kernel(x_prompt, x_sample, cache_k, cache_v, cache_kidx, state_rglru_h, state_rglru_conv, state_mlstm_c,
           state_mlstm_n, state_mlstm_m, cache_mem_k, cache_mem_v, page_table, mem_prompt, ln_g, ln_b, rel_bias,
           w_in_even, w_out_even, rg_conv_w, rg_conv_b, rg_w_a, rg_b_a, rg_w_i, rg_b_i, rg_lambda, w_in_odd,
           b_gate_odd, ml_norm_g, w_out_odd, w_xq, w_xk, w_xv, w_xo, ffn_w_gate, ffn_w_up, ffn_w_down,
           moe_router, moe_w_gate, moe_w_up, moe_w_down):
    bsz, t, d = x_prompt.shape
    s = x_sample.shape[0]
    depth = ln_g.shape[0]
    n_pages = page_table.shape[1]
    past_len = n_pages * PAGE_SIZE
    mt = mem_prompt.shape[1]
    rg_w = rg_lambda.shape[1]
    ml_w = ml_norm_g.shape[1]
    n_experts = moe_router.shape[2]
    alpha = (2.0 * depth) ** 0.25
    bf = lambda a: a.astype(BF16)

    assert REL_MAX_DIST <= TQ
    rel32 = rel_bias.astype(F32)
    by_dist = rel32[_t5_bucket(jnp.arange(2 * TQ + 1, dtype=I32))]

    def spread(dist):
        out = jnp.broadcast_to(by_dist[2 * TQ], dist.shape + (A_HEADS,))
        for dd in range(2 * TQ):
            out = jnp.where((dist == dd)[..., None], by_dist[dd], out)
        return out

    kk_ = lax.broadcasted_iota(I32, (TQ, TQ), 0)
    qq_ = lax.broadcasted_iota(I32, (TQ, TQ), 1)
    bias_tab = jnp.stack([spread(jnp.maximum(qq_ - kk_, 0)), spread(TQ + qq_ - kk_),
                          spread(jnp.full((TQ, TQ), 2 * TQ, I32))])
    bias_tab = jnp.transpose(bias_tab, (0, 3, 1, 2))
    dist_s = (past_len - jnp.arange(past_len, dtype=I32)).reshape(n_pages, PAGE_SIZE)
    bias_s = jnp.transpose(rel32[_t5_bucket(dist_s)], (0, 2, 1))
    bias_s_new = jnp.broadcast_to(rel32[_t5_bucket(jnp.zeros((1,), I32))].reshape(A_HEADS, 1), (A_HEADS, LANES))
    mem_k_view, mem_v_view = _lane_chunk_view(cache_mem_k), _lane_chunk_view(cache_mem_v)

    mem_bf = bf(mem_prompt.reshape(bsz * mt, d))
    kv_w = [bf(w_xk[l]) for l in range(depth)] + [bf(w_xv[l]) for l in range(depth)]
    kv = multi_proj(mem_bf, kv_w, [(F32, BF16)] * (2 * depth))
    mem_k_f32 = [kv[2 * l] for l in range(depth)]
    mem_k_bf = [kv[2 * l + 1].reshape(bsz, mt, d) for l in range(depth)]
    mem_v_f32 = [kv[2 * depth + 2 * l] for l in range(depth)]
    mem_v_bf = [kv[2 * depth + 2 * l + 1].reshape(bsz, mt, d) for l in range(depth)]

    xp = x_prompt.reshape(bsz * t, d)
    xs = x_sample.reshape(s, d)
    xp_bf, xs_bf = bf(xp), bf(xs)
    outs = {}
    k_p, v_p, ki_p, k_s, v_s, ki_s = [], [], [], [], [], []
    rgh_p, rgc_p, rgh_s, rgc_s = [], [], [], []
    mlc_p, mln_p, mlm_p, mlc_s, mln_s, mlm_s = [], [], [], [], [], []

    for l in range(depth):
        if l % 2 == 0:
            e = l // 2
            w_in = w_in_even[e]
            o = 0
            cols = {}
            for name, width in (("q", A_WIDTH), ("k", A_WIDTH), ("v", A_WIDTH), ("qi", IDX_HEADS * IDX_DIM),
                                ("ki", IDX_DIM), ("wi", IDX_HEADS), ("xr", rg_w), ("gr", rg_w)):
                cols[name] = bf(w_in[:, o:o + width])
                o += width
            cols["wi"] = _pad_cols(cols["wi"], LANES)
            names = ("q", "k", "v", "qi", "ki", "wi", "xr", "gr")
            wq_t, wv_t, wqi_t = cols["q"].T, cols["v"].T, cols["qi"].T
            wi_t = jnp.pad(cols["wi"][:, :IDX_HEADS].T, ((0, SUBLANES - IDX_HEADS), (0, 0)))
            wa = bf(_block_diag(rg_w_a[e]))
            wi_g = bf(_block_diag(rg_w_i[e]))
            w_out = bf(w_out_even[e])

            nq = t // TQ
            pnames = ("k", "v", "ki", "xr", "gr")
            kf, kb, vf, kif, kib, xr, gr, qt, vt, qit, wit = multi_proj(
                xp_bf, [cols[n] for n in pnames] + [wq_t, wv_t, wqi_t, wi_t],
                [(F32, BF16), (F32,), (F32, BF16), (F32,), (F32,), (BF16,), (BF16,), (BF16,), (F32,)],
                transposed=(False,) * len(pnames) + (True,) * 4)
            r3 = lambda a: a.reshape(bsz, t, a.shape[-1])
            r4 = lambda a: a.reshape(bsz, nq, a.shape[-2], TQ)
            att = dsa_prompt(r4(qt), r3(kb), r4(vt), r4(qit), r3(kib), r4(wit), bias_tab)
            rg_y, h_last = rglru_prompt(r3(xr), r3(gr), rg_conv_w[e], rg_conv_b[e], wa, rg_b_a[e], wi_g, rg_b_i[e],
                                        rg_lambda[e])
            xp, xp_bf = mm_res_ln([att.reshape(bsz * t, A_WIDTH), rg_y.reshape(bsz * t, rg_w)],
                                  [w_out[:A_WIDTH], w_out[A_WIDTH:]], xp, ln_g[l, 0], ln_b[l, 0], alpha)
            k_p.append(kf.reshape(bsz, t, A_HEADS, A_HEAD_DIM))
            v_p.append(vf.reshape(bsz, t, A_HEADS, A_HEAD_DIM))
            ki_p.append(kif.reshape(bsz, t, IDX_DIM))
            rgh_p.append(h_last.reshape(bsz, rg_w))
            rgc_p.append(r3(xr)[:, t - (RG_CONV - 1):, :])

            q, kf, vf, qi, kif, wi, xr, gr = multi_proj(xs_bf, [cols[n] for n in names], [(F32,)] * len(names))
            qi8 = jnp.pad(qi.reshape(s, IDX_HEADS, IDX_DIM), ((0, 0), (0, SUBLANES - IDX_HEADS), (0, 0)))
            pool = cache_k.shape[1]
            kidx_t = jnp.transpose(cache_kidx[e], (0, 2, 1))
            k_t = jnp.transpose(cache_k[e], (0, 2, 3, 1)).reshape(pool, A_WIDTH, PAGE_SIZE)
            v_t = jnp.transpose(cache_v[e], (0, 2, 3, 1)).reshape(pool, A_WIDTH, PAGE_SIZE)
            raw = dsa_sample_scores(page_table, qi8, wi.reshape(s, 1, LANES), kidx_t)
            mask, mask_new = dsa_sample_select(raw.reshape(s, past_len), qi, kif, wi)
            hd3 = lambda a: a.reshape(s, A_HEADS, A_HEAD_DIM)
            att = dsa_sample_attn(page_table, hd3(q), hd3(kf), hd3(vf), mask.reshape(s, n_pages, PAGE_SIZE),
                                  mask_new.reshape(s, 1, LANES), bias_s, bias_s_new, k_t, v_t)
            rg_y, h_new = rglru_step(xr, gr, state_rglru_conv[e], state_rglru_h[e], rg_conv_w[e], rg_conv_b[e],
                                     wa, rg_b_a[e], wi_g, rg_b_i[e], rg_lambda[e])
            xs, xs_bf = mm_res_ln([bf(att.reshape(s, A_WIDTH)), bf(rg_y)], [w_out[:A_WIDTH], w_out[A_WIDTH:]], xs,
                                  ln_g[l, 0], ln_b[l, 0], alpha)
            k_s.append(kf.reshape(s, 1, A_HEADS, A_HEAD_DIM))
            v_s.append(vf.reshape(s, 1, A_HEADS, A_HEAD_DIM))
            ki_s.append(kif.reshape(s, 1, IDX_DIM))
            rgh_s.append(h_new)
            rgc_s.append(jnp.concatenate([state_rglru_conv[e][:, 1:], xr[:, None, :]], axis=1))
        else:
            o_ = l // 2
            w_in = w_in_odd[o_]
            ws = [bf(w_in[:, i * ml_w:(i + 1) * ml_w]) for i in range(4)]
            ws.append(_pad_cols(bf(w_in[:, 4 * ml_w:]), LANES))
            bgate = jnp.pad(b_gate_odd[o_].astype(F32), (0, LANES - 2 * ML_HEADS)).reshape(1, LANES)
            w_out = bf(w_out_odd[o_])

            q, k, v, og, gates = multi_proj(xp_bf, ws, [(BF16,), (BF16,), (BF16,), (F32,), (F32,)])
            r3 = lambda a: a.reshape(bsz, t, a.shape[-1])
            hn, c, n, m = mlstm_prompt(r3(q), r3(k), r3(v), r3(og), r3(gates), bgate, ml_norm_g[o_])
            xp, xp_bf = mm_res_ln([hn.reshape(bsz * t, ml_w)], [w_out], xp, ln_g[l, 0], ln_b[l, 0], alpha)
            mlc_p.append(c)
            mln_p.append(n[:, :ML_HEADS])
            mlm_p.append(m[:, :ML_HEADS, 0])

            q, k, v, og, gates = multi_proj(xs_bf, ws, [(F32,)] * len(ws))
            hn, c, n, m = mlstm_step(q, k, v, og, gates, bgate, ml_norm_g[o_], state_mlstm_c[o_], state_mlstm_n[o_],
                                     state_mlstm_m[o_])
            xs, xs_bf = mm_res_ln([bf(hn)], [w_out], xs, ln_g[l, 0], ln_b[l, 0], alpha)
            mlc_s.append(c)
            mln_s.append(n)
            mlm_s.append(m)

        wq, wo = bf(w_xq[l]), bf(w_xo[l])
        (xq,) = multi_proj(xp_bf, [wq], [(BF16,)])
        xo = xattn_prompt(xq.reshape(bsz, t, d), mem_k_bf[l], mem_v_bf[l])
        xp, xp_bf = mm_res_ln([xo.reshape(bsz * t, d)], [wo], xp, ln_g[l, 1], ln_b[l, 1], alpha)
        (xq,) = multi_proj(xs_bf, [wq], [(F32,)])
        xo = xattn_step(xq, mem_k_view, mem_v_view, l, mt)
        xs, xs_bf = mm_res_ln([bf(xo)], [wo], xs, ln_g[l, 1], ln_b[l, 1], alpha)

        if l % 2 == 0:
            e = l // 2
            wg, wu, wd = bf(ffn_w_gate[e]), bf(ffn_w_up[e]), bf(ffn_w_down[e])
            xp, xp_bf = ffn_ln(xp_bf, xp, wg, wu, wd, ln_g[l, 2], ln_b[l, 2], alpha)
            xs, xs_bf = ffn_ln(xs_bf, xs, wg, wu, wd, ln_g[l, 2], ln_b[l, 2], alpha)
        else:
            o_ = l // 2
            wr = _pad_cols(bf(moe_router[o_]), LANES)
            wg, wu, wd = bf(moe_w_gate[o_]), bf(moe_w_up[o_]), bf(moe_w_down[o_])
            xp, xp_bf = moe_top2_ln(xp_bf, xp, wr, wg, wu, wd, ln_g[l, 2], ln_b[l, 2], alpha)
            gs = router_gates(xs_bf, wr, n_experts)
            xs, xs_bf = moe_ln(xs_bf, xs, gs, wg, wu, wd, ln_g[l, 2], ln_b[l, 2], alpha)

    xh = mem_k_f32[0].shape[-1] // X_HEADS
    mem_shape = (bsz, mt, X_HEADS, xh)
    return (xp.reshape(bsz, t, d), xs.reshape(s, 1, d),
            jnp.stack(k_p), jnp.stack(v_p), jnp.stack(ki_p), jnp.stack(k_s), jnp.stack(v_s), jnp.stack(ki_s),
            jnp.stack(rgh_p), jnp.stack(rgc_p), jnp.stack(rgh_s), jnp.stack(rgc_s),
            jnp.stack(mlc_p), jnp.stack(mln_p), jnp.stack(mlm_p), jnp.stack(mlc_s), jnp.stack(mln_s), jnp.stack(mlm_s),
            jnp.stack([a.reshape(mem_shape) for a in mem_k_f32]), jnp.stack([a.reshape(mem_shape) for a in mem_v_f32]))
```

```python
import functools
import math

import jax
import jax.numpy as jnp
from jax import lax
from jax.experimental import pallas as pl
from jax.experimental.pallas import tpu as pltpu

F32 = jnp.float32
BF16 = jnp.bfloat16
I32 = jnp.int32

PAGE_SIZE = 128
A_HEADS = 8
A_HEAD_DIM = 64
A_WIDTH = A_HEADS * A_HEAD_DIM
IDX_HEADS = 4
IDX_DIM = 64
TOPK_MAX = 256
REL_BUCKETS = 32
REL_MAX_DIST = 128
RG_BLOCKS = 8
RG_CONV = 4
RG_C = 8.0
ML_HEADS = 4
ML_CHUNK = 128
X_HEADS = 4
TOP_K = 2
LN_EPS = 1e-5

LANES = 128
SUBLANES = 8
VMEM_LIMIT = 48 * 1024 * 1024
VMEM_LIMIT_RESIDENT = 56 * 1024 * 1024
NEG = -1e30
INT_MIN = -2 ** 31
TQ = 128

NT = (((1,), (1,)), ((), ()))
TN = (((0,), (0,)), ((), ()))
HI = lax.Precision.HIGHEST


def _cp(*sem):
    return pltpu.CompilerParams(dimension_semantics=sem, vmem_limit_bytes=VMEM_LIMIT)


def _full(shape):
    n = len(shape)
    return pl.BlockSpec(shape, lambda *_: (0,) * n)


def _row_tile(m, cap):
    t = min(m, cap)
    assert m % t == 0
    return t


def _as_mxu_operand(a):
    return a.astype(BF16).astype(F32)


def _mm(a, b, dims=None):
    prec = HI if a.dtype == F32 else None
    if dims is None:
        return jnp.dot(a, b, preferred_element_type=F32, precision=prec)
    return lax.dot_general(a, b, dims, preferred_element_type=F32, precision=prec)


def _proj_kernel(x_ref, *refs, n_w, out_counts, transposed):
    w_refs, o_refs = refs[:n_w], refs[n_w:]
    x = x_ref[...]
    k = 0
    for w_ref, cnt, tr in zip(w_refs, out_counts, transposed):
        if tr:
            y = _mm(w_ref[...], x, NT)
        else:
            y = _mm(x, w_ref[...])
        for _ in range(cnt):
            if tr:
                for j in range(y.shape[1] // LANES):
                    o_refs[k][j] = y[:, j * LANES:(j + 1) * LANES].astype(o_refs[k].dtype)
            else:
                o_refs[k][...] = y.astype(o_refs[k].dtype)
            k += 1


def multi_proj(x, ws, out_dtypes, transposed=None, tm_cap=512):
    m, kdim = x.shape
    tm = _row_tile(m, tm_cap)
    transposed = transposed or (False,) * len(ws)
    in_specs = [pl.BlockSpec((tm, kdim), lambda i: (i, 0))] + [_full(w.shape) for w in ws]
    out_shape, out_specs = [], []
    for w, dts, tr in zip(ws, out_dtypes, transposed):
        for dt in dts:
            if tr:
                assert tm % LANES == 0
                out_shape.append(jax.ShapeDtypeStruct((m // LANES, w.shape[0], LANES), dt))
                out_specs.append(pl.BlockSpec((tm // LANES, w.shape[0], LANES), lambda i: (i, 0, 0)))
            else:
                out_shape.append(jax.ShapeDtypeStruct((m, w.shape[1]), dt))
                out_specs.append(pl.BlockSpec((tm, w.shape[1]), lambda i: (i, 0)))
    kern = functools.partial(_proj_kernel, n_w=len(ws), out_counts=tuple(len(d) for d in out_dtypes),
                             transposed=tuple(transposed))
    return pl.pallas_call(kern, grid=(m // tm,), in_specs=in_specs, out_specs=out_specs,
                          out_shape=out_shape, compiler_params=_cp("parallel"), name="multi_proj")(x, *ws)


def _layernorm_rows(z, g, b):
    mu = jnp.mean(z, axis=-1, keepdims=True)
    zc = z - mu
    var = jnp.mean(zc * zc, axis=-1, keepdims=True)
    return zc * lax.rsqrt(var + LN_EPS) * g + b


def _mm_ln_kernel(*refs, n_a, alpha):
    a_refs, w_refs = refs[:n_a], refs[n_a:2 * n_a]
    r_ref, g_ref, b_ref, o_ref, obf_ref = refs[2 * n_a:]
    y = alpha * r_ref[...]
    for a_ref, w_ref in zip(a_refs, w_refs):
        y = y + _mm(a_ref[...], w_ref[...])
    out = _layernorm_rows(y, g_ref[...], b_ref[...])
    o_ref[...] = out
    obf_ref[...] = out.astype(BF16)


def mm_res_ln(a_list, w_list, resid, g, b, alpha, tm_cap=512):
    m, d = resid.shape
    tm = _row_tile(m, tm_cap)
    in_specs = ([pl.BlockSpec((tm, a.shape[1]), lambda i: (i, 0)) for a in a_list]
                + [_full(w.shape) for w in w_list]
                + [pl.BlockSpec((tm, d), lambda i: (i, 0)), _full((1, d)), _full((1, d))])
    out_specs = [pl.BlockSpec((tm, d), lambda i: (i, 0))] * 2
    out_shape = [jax.ShapeDtypeStruct((m, d), F32), jax.ShapeDtypeStruct((m, d), BF16)]
    kern = functools.partial(_mm_ln_kernel, n_a=len(a_list), alpha=alpha)
    return pl.pallas_call(kern, grid=(m // tm,), in_specs=in_specs, out_specs=out_specs, out_shape=out_shape,
                          compiler_params=_cp("parallel"), name="mm_res_ln")(
                              *a_list, *w_list, resid, g.reshape(1, d), b.reshape(1, d))


def _ffn_kernel(x_ref, r_ref, wg_ref, wu_ref, wd_ref, g_ref, b_ref, o_ref, obf_ref, acc_sc, *, alpha):
    f = pl.program_id(1)

    @pl.when(f == 0)
    def _():
        acc_sc[...] = alpha * r_ref[...]

    x = x_ref[...]
    hg = _mm(x, wg_ref[...])
    hu = _mm(x, wu_ref[...])
    h = (jax.nn.silu(hg) * hu).astype(x.dtype)
    acc_sc[...] += _mm(h, wd_ref[...])

    @pl.when(f == pl.num_programs(1) - 1)
    def _():
        out = _layernorm_rows(acc_sc[...], g_ref[...], b_ref[...])
        o_ref[...] = out
        obf_ref[...] = out.astype(BF16)


def ffn_ln(x_bf, resid, wg, wu, wd, g, b, alpha, tm_cap=512, tf_cap=1408):
    m, d = resid.shape
    dff = wg.shape[1]
    tm = _row_tile(m, tm_cap)
    tf = _ff_tile(dff, tf_cap)
    in_specs = [pl.BlockSpec((tm, d), lambda i, f: (i, 0)), pl.BlockSpec((tm, d), lambda i, f: (i, 0)),
                pl.BlockSpec((d, tf), lambda i, f: (0, f)), pl.BlockSpec((d, tf), lambda i, f: (0, f)),
                pl.BlockSpec((tf, d), lambda i, f: (f, 0)), _full((1, d)), _full((1, d))]
    out_specs = [pl.BlockSpec((tm, d), lambda i, f: (i, 0))] * 2
    out_shape = [jax.ShapeDtypeStruct((m, d), F32), jax.ShapeDtypeStruct((m, d), BF16)]
    return pl.pallas_call(functools.partial(_ffn_kernel, alpha=alpha), grid=(m // tm, dff // tf),
                          in_specs=in_specs, out_specs=out_specs, out_shape=out_shape,
                          scratch_shapes=[pltpu.VMEM((tm, d), F32)],
                          compiler_params=_cp("parallel", "arbitrary"), name="ffn_ln")(
                              x_bf, resid, wg, wu, wd, g.reshape(1, d), b.reshape(1, d))


def _top2(logits, n_experts):
    lane = lax.broadcasted_iota(I32, logits.shape, 1)
    valid = lane < n_experts
    logits = jnp.where(valid, logits, NEG)
    mx = jnp.max(logits, axis=-1, keepdims=True)
    e = jnp.where(valid, jnp.exp(logits - mx), 0.0)
    p = e / jnp.sum(e, axis=-1, keepdims=True)
    p = jnp.where(valid, p, -1.0)
    big = jnp.int32(LANES)
    m1 = jnp.max(p, axis=-1, keepdims=True)
    i1 = jnp.min(jnp.where(p == m1, lane, big), axis=-1, keepdims=True)
    p2 = jnp.where(lane == i1, -1.0, p)
    m2 = jnp.max(p2, axis=-1, keepdims=True)
    i2 = jnp.min(jnp.where(p2 == m2, lane, big), axis=-1, keepdims=True)
    tot = m1 + m2
    return lane, i1, i2, m1 / tot, m2 / tot


def _router_kernel(x_ref, w_ref, o_ref, *, n_experts):
    lane, i1, i2, g1, g2 = _top2(_mm(x_ref[...], w_ref[...]), n_experts)
    o_ref[...] = jnp.where(lane == i1, g1, jnp.where(lane == i2, g2, 0.0))


def router_gates(x, w_pad, n_experts, tm_cap=512):
    m, d = x.shape
    tm = _row_tile(m, tm_cap)
    return pl.pallas_call(functools.partial(_router_kernel, n_experts=n_experts), grid=(m // tm,),
                          in_specs=[pl.BlockSpec((tm, d), lambda i: (i, 0)), _full(w_pad.shape)],
                          out_specs=pl.BlockSpec((tm, LANES), lambda i: (i, 0)),
                          out_shape=jax.ShapeDtypeStruct((m, LANES), F32),
                          compiler_params=_cp("parallel"), name="router")(x, w_pad)


def _moe_kernel(x_ref, r_ref, gt_ref, wg_ref, wu_ref, wd_ref, g_ref, b_ref, o_ref, obf_ref, acc_sc, *, alpha):
    e = pl.program_id(1)
    f = pl.program_id(2)

    @pl.when((e == 0) & (f == 0))
    def _():
        acc_sc[...] = alpha * r_ref[...]

    gt = gt_ref[...]
    lane = lax.broadcasted_iota(I32, gt.shape, 1)
    ge = jnp.sum(jnp.where(lane == e, gt, 0.0), axis=-1, keepdims=True)
    x = x_ref[...]
    hg = jnp.dot(x, wg_ref[0], preferred_element_type=F32)
    hu = jnp.dot(x, wu_ref[0], preferred_element_type=F32)
    h = (jax.nn.silu(hg) * hu).astype(BF16)
    acc_sc[...] += ge * jnp.dot(h, wd_ref[0], preferred_element_type=F32)

    @pl.when((e == pl.num_programs(1) - 1) & (f == pl.num_programs(2) - 1))
    def _():
        out = _layernorm_rows(acc_sc[...], g_ref[...], b_ref[...])
        o_ref[...] = out
        obf_ref[...] = out.astype(BF16)


def moe_ln(x_bf, resid, gates, wg, wu, wd, g, b, alpha, tm_cap=512, tf=512):
    m, d = resid.shape
    ne, _, dff = wg.shape
    tm = _row_tile(m, tm_cap)
    assert dff % tf == 0
    in_specs = [pl.BlockSpec((tm, d), lambda i, e, f: (i, 0)), pl.BlockSpec((tm, d), lambda i, e, f: (i, 0)),
                pl.BlockSpec((tm, LANES), lambda i, e, f: (i, 0)),
                pl.BlockSpec((1, d, tf), lambda i, e, f: (e, 0, f)), pl.BlockSpec((1, d, tf), lambda i, e, f: (e, 0, f)),
                pl.BlockSpec((1, tf, d), lambda i, e, f: (e, f, 0)), _full((1, d)), _full((1, d))]
    out_specs = [pl.BlockSpec((tm, d), lambda i, e, f: (i, 0))] * 2
    out_shape = [jax.ShapeDtypeStruct((m, d), F32), jax.ShapeDtypeStruct((m, d), BF16)]
    return pl.pallas_call(functools.partial(_moe_kernel, alpha=alpha), grid=(m // tm, ne, dff // tf),
                          in_specs=in_specs, out_specs=out_specs, out_shape=out_shape,
                          scratch_shapes=[pltpu.VMEM((tm, d), F32)],
                          compiler_params=_cp("parallel", "arbitrary", "arbitrary"), name="moe_ln")(
                              x_bf, resid, gates, wg, wu, wd, g.reshape(1, d), b.reshape(1, d))


MOE_TILE = 512
COMBINE_SLACK = LANES
ROUTE_LANES = ("e1", "e2", "g1", "g2", "rank1", "rank2")


def _router_plan_kernel(x_ref, w_ref, ltri_ref, route_ref, before_ref, count_ref, carry_sc, *, n_experts):
    @pl.when(pl.program_id(0) == 0)
    def _():
        carry_sc[...] = jnp.zeros(carry_sc.shape, F32)

    lane, i1, i2, g1, g2 = _top2(_mm(x_ref[...], w_ref[...]), n_experts)
    chosen = jnp.where(lane == i1, 1.0, jnp.where(lane == i2, 1.0, 0.0))
    before = carry_sc[0:1, :]
    rank = before + jnp.dot(ltri_ref[...], chosen.astype(BF16), preferred_element_type=F32)
    r1 = jnp.sum(jnp.where(lane == i1, rank, 0.0), axis=-1, keepdims=True)
    r2 = jnp.sum(jnp.where(lane == i2, rank, 0.0), axis=-1, keepdims=True)
    vals = (i1.astype(F32), i2.astype(F32), g1, g2, r1, r2)
    route = jnp.zeros(lane.shape, F32)
    for k, v in enumerate(vals):
        route = jnp.where(lane == k, v, route)
    route_ref[...] = route
    before_ref[0] = jnp.broadcast_to(before, before_ref.shape[1:])
    after = before + jnp.sum(chosen, axis=0, keepdims=True)
    carry_sc[0:1, :] = after
    count_ref[...] = jnp.broadcast_to(after, count_ref.shape)


def router_plan(x_bf, w_pad, n_experts):
    m, d = x_bf.shape
    tb = MOE_TILE
    assert m % tb == 0
    ltri = (lax.broadcasted_iota(I32, (tb, tb), 1) < lax.broadcasted_iota(I32, (tb, tb), 0)).astype(BF16)
    return pl.pallas_call(
        functools.partial(_router_plan_kernel, n_experts=n_experts), grid=(m // tb,),
        in_specs=[pl.BlockSpec((tb, d), lambda i: (i, 0)), _full(w_pad.shape), _full((tb, tb))],
        out_specs=[pl.BlockSpec((tb, LANES), lambda i: (i, 0)), pl.BlockSpec((1, SUBLANES, LANES), lambda i: (i, 0, 0)),
                   _full((SUBLANES, LANES))],
        out_shape=[jax.ShapeDtypeStruct((m, LANES), F32), jax.ShapeDtypeStruct((m // tb, SUBLANES, LANES), F32),
                   jax.ShapeDtypeStruct((SUBLANES, LANES), F32)],
        scratch_shapes=[pltpu.VMEM((SUBLANES, LANES), F32)],
        compiler_params=_cp("arbitrary"), name="router_plan")(x_bf, w_pad, ltri)


def _moe_gather_kernel(blo_ref, bhi_ref, x_ref, pos_ref, g_ref, xs_ref, gs_ref, acc_sc, gacc_sc, *, tile):
    j = pl.program_id(0)
    acc_sc[...] = jnp.zeros(acc_sc.shape, F32)
    gacc_sc[...] = jnp.zeros(gacc_sc.shape, F32)
    rid = j * tile + lax.broadcasted_iota(I32, (tile, tile), 0)
    glane = lax.broadcasted_iota(I32, gacc_sc.shape, 1)

    def body(b, carry):
        off = pl.multiple_of(b * tile, tile)
        m1 = jnp.where(pos_ref[b, 0:1, :] == rid, 1.0, 0.0).astype(BF16)
        m2 = jnp.where(pos_ref[b, 1:2, :] == rid, 1.0, 0.0).astype(BF16)
        acc_sc[...] += jnp.dot(m1 + m2, x_ref[pl.ds(off, tile), :], preferred_element_type=F32)
        gb = g_ref[pl.ds(off, tile), :]
        gacc_sc[...] += jnp.where(glane < 3, jnp.dot(m1, gb, preferred_element_type=F32),
                                  jnp.dot(m2, gb, preferred_element_type=F32))
        return carry

    lax.fori_loop(blo_ref[j], bhi_ref[j] + 1, body, 0)
    xs_ref[...] = acc_sc[...].astype(xs_ref.dtype)
    gs_ref[...] = gacc_sc[...]


def moe_gather(x_bf, pos, gsplit, b_lo, b_hi, n_tiles):
    m, d = x_bf.shape
    tile = MOE_TILE
    resident = lambda shape: pl.BlockSpec(shape, lambda j, lo, hi: (0,) * len(shape), pipeline_mode=pl.Buffered(1))
    grid_spec = pltpu.PrefetchScalarGridSpec(
        num_scalar_prefetch=2, grid=(n_tiles,),
        in_specs=[resident((m, d)), resident(pos.shape), resident(gsplit.shape)],
        out_specs=[pl.BlockSpec((tile, d), lambda j, lo, hi: (j, 0)),
                   pl.BlockSpec((tile, LANES), lambda j, lo, hi: (j, 0))],
        scratch_shapes=[pltpu.VMEM((tile, d), F32), pltpu.VMEM((tile, LANES), F32)])
    return pl.pallas_call(
        functools.partial(_moe_gather_kernel, tile=tile), grid_spec=grid_spec,
        out_shape=[jax.ShapeDtypeStruct((n_tiles * tile, d), BF16),
                   jax.ShapeDtypeStruct((n_tiles * tile, LANES), F32)],
        compiler_params=pltpu.CompilerParams(dimension_semantics=("arbitrary",), vmem_limit_bytes=VMEM_LIMIT_RESIDENT),
        name="moe_gather")(b_lo, b_hi, x_bf, pos, gsplit)


def _moe_group_ffn_kernel(te_ref, nv_ref, x_ref, gs_ref, wg_ref, wu_ref, wd_ref, o_ref, acc_sc):
    j = pl.program_id(0)
    f = pl.program_id(1)

    @pl.when(f == 0)
    def _():
        acc_sc[...] = jnp.zeros(acc_sc.shape, F32)

    @pl.when(j < nv_ref[0])
    def _():
        x = x_ref[...]
        hg = jnp.dot(x, wg_ref[0], preferred_element_type=F32)
        hu = jnp.dot(x, wu_ref[0], preferred_element_type=F32)
        h = (jax.nn.silu(hg) * hu).astype(BF16)
        acc_sc[...] += jnp.dot(h, wd_ref[0], preferred_element_type=F32)

    @pl.when(f == pl.num_programs(1) - 1)
    def _():
        gs = gs_ref[...]
        lane = lax.broadcasted_iota(I32, gs.shape, 1)
        gate = jnp.sum(jnp.where(lane < 6, gs, 0.0), axis=-1, keepdims=True)
        o_ref[...] = acc_sc[...] * gate


def _ff_tile(dff, cap):
    return max(t for t in range(LANES, min(dff, cap) + 1, LANES) if dff % t == 0)


def moe_group_ffn(tile_expert, n_valid, x_sorted, gterms, wg, wu, wd, n_out_tiles, tf_cap=1792):
    tile = MOE_TILE
    n_tiles = x_sorted.shape[0] // tile
    d = x_sorted.shape[1]
    dff = wg.shape[2]
    tf = _ff_tile(dff, tf_cap)
    src = lambda j: jnp.minimum(j, n_tiles - 1)
    grid_spec = pltpu.PrefetchScalarGridSpec(
        num_scalar_prefetch=2, grid=(n_out_tiles, dff // tf),
        in_specs=[pl.BlockSpec((tile, d), lambda j, f, te, nv: (src(j), 0)),
                  pl.BlockSpec((tile, LANES), lambda j, f, te, nv: (src(j), 0)),
                  pl.BlockSpec((1, d, tf), lambda j, f, te, nv: (te[src(j)], 0, f)),
                  pl.BlockSpec((1, d, tf), lambda j, f, te, nv: (te[src(j)], 0, f)),
                  pl.BlockSpec((1, tf, d), lambda j, f, te, nv: (te[src(j)], f, 0))],
        out_specs=pl.BlockSpec((tile, d), lambda j, f, te, nv: (j, 0)),
        scratch_shapes=[pltpu.VMEM((tile, d), F32)])
    return pl.pallas_call(_moe_group_ffn_kernel, grid_spec=grid_spec,
                          out_shape=jax.ShapeDtypeStruct((n_out_tiles * tile, d), F32),
                          compiler_params=_cp("arbitrary", "arbitrary"), name="moe_group_ffn")(
                              tile_expert, n_valid, x_sorted, gterms, wg, wu, wd)


def _moe_combine_kernel(rlo_ref, short_ref, route_ref, r_ref, os_hbm, g_ref, b_ref, o_ref, obf_ref, buf, sem, acc_sc,
                        *, n_experts, windows, alpha):
    i = pl.program_id(0)
    acc_sc[...] = alpha * r_ref[...]
    route = route_ref[...]
    e1 = route[:, 0:1].astype(I32)
    e2 = route[:, 1:2].astype(I32)
    p1 = route[:, 2:3].astype(I32)
    p2 = route[:, 3:4].astype(I32)

    def accumulate(window):
        def window_copy(e, slot):
            lo = pl.multiple_of((rlo_ref[i * n_experts + e] // SUBLANES) * SUBLANES, SUBLANES)
            return lo, pltpu.make_async_copy(os_hbm.at[pl.ds(lo, window), :], buf.at[slot, pl.ds(0, window), :],
                                             sem.at[slot])

        window_copy(0, 0)[1].start()
        col = lax.broadcasted_iota(I32, (route.shape[0], window), 1)
        for e in range(n_experts):
            slot = e % 2
            if e + 1 < n_experts:
                window_copy(e + 1, 1 - slot)[1].start()
            lo, cp = window_copy(e, slot)
            cp.wait()
            c1 = jnp.where(e1 == e, p1 - lo, -1)
            c2 = jnp.where(e2 == e, p2 - lo, -1)
            pick = jnp.where(col == c1, 1.0, jnp.where(col == c2, 1.0, 0.0)).astype(BF16)
            rows = buf[slot, pl.ds(0, window), :]
            hi = rows.astype(BF16)
            lo_part = (rows - hi.astype(F32)).astype(BF16)
            acc_sc[...] += (jnp.dot(pick, hi, preferred_element_type=F32)
                            + jnp.dot(pick, lo_part, preferred_element_type=F32))

    short, full = windows

    @pl.when(short_ref[i] == 1)
    def _():
        accumulate(short)

    @pl.when(short_ref[i] != 1)
    def _():
        accumulate(full)

    out = _layernorm_rows(acc_sc[...], g_ref[...], b_ref[...])
    o_ref[...] = out
    obf_ref[...] = out.astype(BF16)


def moe_combine_ln(row_lo, short, route_pos, resid, out_sorted, g, b, alpha, n_experts, windows):
    m, d = resid.shape
    tile = MOE_TILE
    grid_spec = pltpu.PrefetchScalarGridSpec(
        num_scalar_prefetch=2, grid=(m // tile,),
        in_specs=[pl.BlockSpec((tile, LANES), lambda i, rlo, sh: (i, 0)),
                  pl.BlockSpec((tile, d), lambda i, rlo, sh: (i, 0)),
                  pl.BlockSpec(memory_space=pl.ANY),
                  pl.BlockSpec((1, d), lambda i, rlo, sh: (0, 0)), pl.BlockSpec((1, d), lambda i, rlo, sh: (0, 0))],
        out_specs=[pl.BlockSpec((tile, d), lambda i, rlo, sh: (i, 0))] * 2,
        scratch_shapes=[pltpu.VMEM((2, max(windows), d), F32), pltpu.SemaphoreType.DMA((2,)),
                        pltpu.VMEM((tile, d), F32)])
    return pl.pallas_call(
        functools.partial(_moe_combine_kernel, n_experts=n_experts, windows=windows, alpha=alpha),
        grid_spec=grid_spec,
        out_shape=[jax.ShapeDtypeStruct((m, d), F32), jax.ShapeDtypeStruct((m, d), BF16)],
        compiler_params=_cp("arbitrary"), name="moe_combine_ln")(
            row_lo, short, route_pos, resid, out_sorted, g.reshape(1, d), b.reshape(1, d))


def moe_top2_ln(x_bf, resid, w_router, wg, wu, wd, g, b, alpha):
    m, d = resid.shape
    ne = wg.shape[0]
    tile = MOE_TILE
    nb = m // tile
    n_tiles = (TOP_K * m) // tile + ne
    tail_tiles = -(-(tile + COMBINE_SLACK) // tile)
    route, before, count = router_plan(x_bf, w_router, ne)

    eid = jnp.arange(ne, dtype=I32)
    lookup = lambda table, idx: jnp.sum(jnp.where(idx[..., None] == eid, table, 0), axis=-1)
    cnt = count[0, :ne].astype(I32)
    tiles_e = (cnt + tile - 1) // tile
    first_tile = jnp.cumsum(tiles_e) - tiles_e
    n_valid = jnp.sum(tiles_e)
    row_start = first_tile * tile
    tid = jnp.arange(n_tiles, dtype=I32)
    tile_expert = jnp.minimum(jnp.sum((tid[:, None] >= (first_tile + tiles_e)[None, :]).astype(I32), axis=1), ne - 1)
    rank0 = (tid - lookup(first_tile, tile_expert)) * tile
    bef = before[:, 0, :ne].astype(I32)
    aft = jnp.concatenate([bef[1:], cnt[None, :]], axis=0)
    bef_t = lookup(bef[:, None, :], tile_expert[None, :])
    aft_t = lookup(aft[:, None, :], tile_expert[None, :])
    valid_t = tid < n_valid
    b_lo = jnp.where(valid_t, jnp.sum((aft_t <= rank0[None, :]).astype(I32), axis=0), 0)
    b_hi = jnp.where(valid_t, jnp.sum((bef_t < (rank0 + tile)[None, :]).astype(I32), axis=0) - 1, -1)
    e1, e2 = route[:, 0].astype(I32), route[:, 1].astype(I32)
    pos1 = lookup(row_start, e1) + route[:, 4].astype(I32)
    pos2 = lookup(row_start, e2) + route[:, 5].astype(I32)
    pos = jnp.stack([pos1.reshape(nb, tile), pos2.reshape(nb, tile)]
                    + [jnp.zeros((nb, tile), I32)] * (SUBLANES - 2), axis=1)

    def three_terms(v):
        hi = v.astype(BF16)
        mid = (v - hi.astype(F32)).astype(BF16)
        low = (v - hi.astype(F32) - mid.astype(F32)).astype(BF16)
        return [hi, mid, low]

    gsplit = jnp.stack(three_terms(route[:, 2]) + three_terms(route[:, 3]), axis=1)
    gsplit = jnp.pad(gsplit, ((0, 0), (0, LANES - gsplit.shape[1])))
    route_pos = jnp.stack([route[:, 0], route[:, 1], pos1.astype(F32), pos2.astype(F32)], axis=1)
    route_pos = jnp.pad(route_pos, ((0, 0), (0, LANES - route_pos.shape[1])))
    row_lo = row_start[None, :] + bef
    windows = (tile // 2, tile + COMBINE_SLACK)
    short = jnp.all(row_lo % SUBLANES + (aft - bef) <= windows[0], axis=1).astype(I32)

    x_sorted, gterms = moe_gather(x_bf, pos, gsplit, b_lo, b_hi, n_tiles)
    out_sorted = moe_group_ffn(tile_expert, n_valid.reshape(1), x_sorted, gterms, wg, wu, wd, n_tiles + tail_tiles)
    return moe_combine_ln(row_lo.reshape(-1), short, route_pos, resid, out_sorted, g, b, alpha, ne, windows)


NEG_INF_CODE = INT_MIN + 0x7FFFFF


def _code_to_f32(code):
    return pltpu.bitcast(jnp.where(code < 0, code ^ jnp.int32(0x7FFFFFFF), code), F32)


def _threshold_search(fold, topk, shape):
    kk = jnp.int32(topk)
    count = lambda pred: fold(lambda s, i: jnp.where(pred(s), 1, 0), "sum")
    c0 = count(lambda s: s >= 0.0)
    code = jnp.where(c0 >= kk, jnp.int32(0), jnp.int32(INT_MIN)) + jnp.zeros(shape, I32)

    def bit_body(j, code):
        cand = code | (jnp.int32(1) << (30 - j))
        cand_f = _code_to_f32(cand)
        return jnp.where(count(lambda s: s >= cand_f) >= kk, cand, code)

    code = lax.fori_loop(0, 31, bit_body, code)
    enough = code > NEG_INF_CODE
    floor_t = _code_to_f32(code)
    t = jnp.where(enough, fold(lambda s, i: jnp.where(s >= floor_t, s, jnp.inf), "min"), -jnp.inf)
    need = kk - count(lambda s: s > t)
    return t, jnp.where(enough, need, 0)


def _topk_threshold(fold, topk, n_idx_bits, rows):
    t, need = _threshold_search(fold, topk, (rows, 1))

    def tie_body(j, p):
        cand = p | (jnp.int32(1) << (n_idx_bits - 1 - j))
        got = fold(lambda s, i: jnp.where(s == t, jnp.where(i < cand, 1, 0), 0), "sum")
        return jnp.where(got <= need, cand, p)

    p = lax.fori_loop(0, n_idx_bits, tie_body, jnp.zeros((rows, 1), I32))
    return t, jnp.where(need > 0, p, 0)


CHUNK_GROUP = 4
ATT_CHUNKS = 4


def _dsa_prompt_kernel(qt_ref, k_ref, vt_ref, qit_ref, ki_ref, wit_ref, bias_ref, ltri_ref, o_ref,
                       score_sc, m_sc, l_sc, acc_sc, *, topk):
    i = pl.program_id(1)
    n_groups = (i + CHUNK_GROUP) // CHUNK_GROUP
    krow = lax.broadcasted_iota(I32, (TQ, TQ), 0)
    qcol = lax.broadcasted_iota(I32, (TQ, TQ), 1)
    qit = qit_ref[0, 0] * jnp.asarray(IDX_DIM ** -0.5, BF16)
    qit_heads = jnp.concatenate([qit[h * IDX_DIM:(h + 1) * IDX_DIM, :] for h in range(IDX_HEADS)], axis=1)
    wt = wit_ref[0, 0] * (IDX_HEADS ** -0.5)

    def score_body(g, carry):
        base = pl.multiple_of(g * (CHUNK_GROUP * TQ), CHUNK_GROUP * TQ)
        for j in range(CHUNK_GROUP):
            off = base + j * TQ
            s_heads = jnp.dot(ki_ref[0, pl.ds(off, TQ), :], qit_heads, preferred_element_type=F32)
            sc = jnp.zeros((TQ, TQ), F32)
            for h in range(IDX_HEADS):
                sc = sc + wt[h:h + 1, :] * jnp.maximum(s_heads[:, h * TQ:(h + 1) * TQ], 0.0)
            score_sc[pl.ds(off, TQ), :] = jnp.where(off + krow <= i * TQ + qcol, sc, -jnp.inf)
        return carry

    lax.fori_loop(0, n_groups, score_body, 0)

    def fold(fn, kind):
        init, join, finish = {"sum": (0, jnp.add, jnp.sum), "min": (jnp.inf, jnp.minimum, jnp.min)}[kind]

        def body(g, acc):
            base = pl.multiple_of(g * (CHUNK_GROUP * TQ), CHUNK_GROUP * TQ)
            for j in range(CHUNK_GROUP):
                acc = join(acc, fn(score_sc[pl.ds(base + j * TQ, TQ), :], None))
            return acc

        acc = lax.fori_loop(0, n_groups, body, jnp.full((TQ, TQ), init, I32 if kind == "sum" else F32))
        return finish(acc, axis=0, keepdims=True)

    t, need = _threshold_search(fold, topk, (1, TQ))
    need = need.astype(F32)

    m_sc[...] = jnp.full(m_sc.shape, NEG, F32)
    l_sc[...] = jnp.zeros(l_sc.shape, F32)
    acc_sc[...] = jnp.zeros(acc_sc.shape, F32)
    qt = qt_ref[0, 0] * jnp.asarray(A_HEAD_DIM ** -0.5, BF16)

    def att_body(pair, ties_before):
        chunks, sels, kcs = [], [], []
        for j in range(ATT_CHUNKS):
            c = pair * ATT_CHUNKS + j
            off = pl.multiple_of(c * TQ, TQ)
            key = score_sc[pl.ds(off, TQ), :]
            tie = key == t
            rank = ties_before + jnp.dot(ltri_ref[...], jnp.where(tie, 1.0, 0.0).astype(BF16),
                                         preferred_element_type=F32)
            ties_before = rank[TQ - 1:TQ, :]
            sels.append(jnp.where(key > t, 1, jnp.where(tie, jnp.where(rank <= need, 1, 0), 0)) > 0)
            kcs.append(k_ref[0, pl.ds(off, TQ), :])
            chunks.append(c)
        for h in range(A_HEADS):
            sl = slice(h * A_HEAD_DIM, (h + 1) * A_HEAD_DIM)
            ss = []
            for j in range(ATT_CHUNKS):
                s = jnp.dot(kcs[j][:, sl], qt[sl, :], preferred_element_type=F32)
                bias = bias_ref[jnp.clip(i - chunks[j], 0, 2), h]
                ss.append(jnp.where(sels[j], s + bias, NEG))
            m_prev = m_sc[h, 0:1, :]
            m_new = jnp.maximum(m_prev, jnp.max(functools.reduce(jnp.maximum, ss), axis=0, keepdims=True))
            alpha = jnp.exp(m_prev - m_new)
            pes = [jnp.exp(s - m_new) for s in ss]
            l_sc[h, 0:1, :] = alpha * l_sc[h, 0:1, :] + jnp.sum(functools.reduce(jnp.add, pes), axis=0, keepdims=True)
            pv = functools.reduce(jnp.add, [jnp.dot(vt_ref[0, chunks[j], sl, :], pes[j].astype(BF16),
                                                    preferred_element_type=F32) for j in range(ATT_CHUNKS)])
            acc_sc[sl, :] = alpha * acc_sc[sl, :] + pv
            m_sc[h, 0:1, :] = m_new
        return ties_before

    lax.fori_loop(0, (i + ATT_CHUNKS) // ATT_CHUNKS, att_body, jnp.zeros((1, TQ), F32))
    for h in range(A_HEADS):
        sl = slice(h * A_HEAD_DIM, (h + 1) * A_HEAD_DIM)
        acc_sc[sl, :] = acc_sc[sl, :] / l_sc[h, 0:1, :]
    o_ref[0] = acc_sc[...].T.astype(o_ref.dtype)


def dsa_prompt(qt, k, vt, qit, ki, wit, bias_tab):
    bsz, t, _ = k.shape
    assert TQ == LANES and t % (CHUNK_GROUP * TQ) == 0
    topk = min(TOPK_MAX, t // 4)
    nq = t // TQ
    ltri = (lax.broadcasted_iota(I32, (TQ, TQ), 1) <= lax.broadcasted_iota(I32, (TQ, TQ), 0)).astype(BF16)
    per_q = lambda rows: pl.BlockSpec((1, 1, rows, TQ), lambda b, i: (b, i, 0, 0))
    per_b = lambda w: pl.BlockSpec((1, t, w), lambda b, i: (b, 0, 0))
    stat = pltpu.VMEM((A_HEADS, SUBLANES, TQ), F32)
    return pl.pallas_call(
        functools.partial(_dsa_prompt_kernel, topk=topk), grid=(bsz, nq),
        in_specs=[per_q(A_WIDTH), per_b(A_WIDTH), pl.BlockSpec((1, nq, A_WIDTH, TQ), lambda b, i: (b, 0, 0, 0)),
                  per_q(IDX_HEADS * IDX_DIM), per_b(IDX_DIM), per_q(SUBLANES), _full(bias_tab.shape),
                  _full((TQ, TQ))],
        out_specs=pl.BlockSpec((1, TQ, A_WIDTH), lambda b, i: (b, i, 0)),
        out_shape=jax.ShapeDtypeStruct((bsz, t, A_WIDTH), BF16),
        scratch_shapes=[pltpu.VMEM((t, TQ), F32), stat, stat, pltpu.VMEM((A_WIDTH, TQ), F32)],
        compiler_params=_cp("parallel", "arbitrary"), name="dsa_prompt")(qt, k, vt, qit, ki, wit, bias_tab, ltri)


def _row_to_col(row):
    n = row.shape[1]
    eye = lax.broadcasted_iota(I32, (n, n), 0) == lax.broadcasted_iota(I32, (n, n), 1)
    return jnp.sum(jnp.where(eye, jnp.broadcast_to(row, (n, n)), 0.0), axis=1, keepdims=True)


def _col_to_row(col):
    n = col.shape[0]
    eye = lax.broadcasted_iota(I32, (n, n), 0) == lax.broadcasted_iota(I32, (n, n), 1)
    return jnp.sum(jnp.where(eye, jnp.broadcast_to(col, (n, n)), 0.0), axis=0, keepdims=True)


def _dsa_s_score_kernel(pt_ref, qi_ref, w_ref, *refs, n_pages):
    page_refs, o_ref = refs[:n_pages], refs[n_pages]
    w = _as_mxu_operand(w_ref[0] * (IDX_HEADS ** -0.5))
    qcols = [_row_to_col(_as_mxu_operand(qi_ref[0, h:h + 1, :] * (IDX_DIM ** -0.5))) for h in range(IDX_HEADS)]
    for p in range(n_pages):
        page = _as_mxu_operand(page_refs[p][0])
        sc = jnp.zeros((1, PAGE_SIZE), F32)
        for h in range(IDX_HEADS):
            s = jnp.sum(page * qcols[h], axis=0, keepdims=True)
            sc = sc + w[:, h:h + 1] * _as_mxu_operand(jnp.maximum(s, 0.0))
        o_ref[0, :, p * PAGE_SIZE:(p + 1) * PAGE_SIZE] = sc


def dsa_sample_scores(page_table, qi8, wi, kidx_pool_t):
    s, n_pages = page_table.shape
    pt = page_table.reshape(-1)
    page_spec = lambda p: pl.BlockSpec((1, IDX_DIM, PAGE_SIZE), lambda i, pt_ref: (pt_ref[i * n_pages + p], 0, 0))
    grid_spec = pltpu.PrefetchScalarGridSpec(
        num_scalar_prefetch=1, grid=(s,),
        in_specs=[pl.BlockSpec((1, SUBLANES, IDX_DIM), lambda i, pt_ref: (i, 0, 0)),
                  pl.BlockSpec((1, 1, LANES), lambda i, pt_ref: (i, 0, 0))] + [page_spec(p) for p in range(n_pages)],
        out_specs=pl.BlockSpec((1, 1, n_pages * PAGE_SIZE), lambda i, pt_ref: (i, 0, 0)))
    return pl.pallas_call(functools.partial(_dsa_s_score_kernel, n_pages=n_pages), grid_spec=grid_spec,
                          out_shape=jax.ShapeDtypeStruct((s, 1, n_pages * PAGE_SIZE), F32),
                          compiler_params=_cp("arbitrary"), name="dsa_sample_scores")(
                              pt, qi8, wi, *([kidx_pool_t] * n_pages))


def _dsa_s_select_kernel(sc_ref, qi_ref, kn_ref, wi_ref, mask_ref, mnew_ref, *, topk, n_idx_bits):
    s, lp = mask_ref.shape
    w = _as_mxu_operand(wi_ref[...] * (IDX_HEADS ** -0.5))
    sc_new = jnp.zeros((s, 1), F32)
    kn = _as_mxu_operand(kn_ref[...])
    for h in range(IDX_HEADS):
        s_new = jnp.sum(_as_mxu_operand(qi_ref[:, h * IDX_DIM:(h + 1) * IDX_DIM]) * kn, axis=-1, keepdims=True)
        sc_new = sc_new + w[:, h:h + 1] * _as_mxu_operand(jnp.maximum(s_new * (IDX_DIM ** -0.5), 0.0))
    col = lax.broadcasted_iota(I32, (s, lp), 1)

    def fold(fn, kind):
        join, finish = {"sum": (jnp.add, jnp.sum), "min": (jnp.minimum, jnp.min)}[kind]
        main = finish(fn(sc_ref[...], col), axis=1, keepdims=True)
        return join(main, fn(sc_new, jnp.int32(lp)))

    t, p = _topk_threshold(fold, topk, n_idx_bits, s)
    sc = sc_ref[...]
    mask_ref[...] = jnp.where(sc > t, 1.0, jnp.where(sc == t, jnp.where(col < p, 1.0, 0.0), 0.0))
    sel_new = jnp.where(sc_new > t, 1.0, jnp.where(sc_new == t, jnp.where(lp < p, 1.0, 0.0), 0.0))
    mnew_ref[...] = jnp.broadcast_to(sel_new, mnew_ref.shape)


def dsa_sample_select(raw, qi, ki_new, wi):
    s, lp = raw.shape
    length = lp + 1
    topk = min(TOPK_MAX, length // 4)
    kern = functools.partial(_dsa_s_select_kernel, topk=topk, n_idx_bits=length.bit_length())
    return pl.pallas_call(kern, out_shape=[jax.ShapeDtypeStruct((s, lp), F32), jax.ShapeDtypeStruct((s, LANES), F32)],
                          compiler_params=pltpu.CompilerParams(vmem_limit_bytes=VMEM_LIMIT),
                          name="dsa_sample_select")(raw, qi, ki_new, wi)


def _dsa_s_attn_kernel(pt_ref, q_ref, kn_ref, vn_ref, mask_ref, mnew_ref, bias_ref, bnew_ref, *refs, n_pages):
    k_refs, v_refs, o_ref = refs[:n_pages], refs[n_pages:2 * n_pages], refs[2 * n_pages]
    hd = A_HEAD_DIM
    q = _as_mxu_operand(q_ref[0] * (hd ** -0.5))
    qcols = [_row_to_col(q[h:h + 1, :]) for h in range(A_HEADS)]
    hrow = lax.broadcasted_iota(I32, (A_HEADS, PAGE_SIZE), 0)
    lg_new = jnp.sum(q * _as_mxu_operand(kn_ref[0]), axis=1, keepdims=True) + bnew_ref[:, 0:1]
    lg_new = jnp.where(mnew_ref[0][:, 0:1] > 0.0, lg_new, NEG)
    logits = []
    mx = lg_new
    for p in range(n_pages):
        lg = jnp.zeros((A_HEADS, PAGE_SIZE), F32)
        for h in range(A_HEADS):
            kp = _as_mxu_operand(k_refs[p][0, h * hd:(h + 1) * hd, :])
            s = jnp.sum(kp * qcols[h], axis=0, keepdims=True)
            lg = jnp.where(hrow == h, jnp.broadcast_to(s, lg.shape), lg)
        lg = jnp.where(mask_ref[0, p:p + 1, :] > 0.0, lg + bias_ref[p], NEG)
        logits.append(lg)
        mx = jnp.maximum(mx, jnp.max(lg, axis=1, keepdims=True))
    pe_new = jnp.exp(lg_new - mx)
    pes = [jnp.exp(lg - mx) for lg in logits]
    den = pe_new
    for pe in pes:
        den = den + jnp.sum(pe, axis=1, keepdims=True)
    probs = [_as_mxu_operand(pe / den) for pe in pes]
    p_new = _as_mxu_operand(pe_new / den)
    for h in range(A_HEADS):
        acc = jnp.zeros((hd, PAGE_SIZE), F32)
        for p in range(n_pages):
            acc = acc + probs[p][h:h + 1, :] * _as_mxu_operand(v_refs[p][0, h * hd:(h + 1) * hd, :])
        o_ref[0, h:h + 1, :] = (_col_to_row(jnp.sum(acc, axis=1, keepdims=True))
                                + p_new[h:h + 1, :] * _as_mxu_operand(vn_ref[0, h:h + 1, :]))


def dsa_sample_attn(page_table, q, k_new, v_new, mask, mask_new, bias_pages, bias_new, k_pool_t, v_pool_t):
    s, n_pages = page_table.shape
    pt = page_table.reshape(-1)
    per_s = lambda r, w: pl.BlockSpec((1, r, w), lambda i, pt_ref: (i, 0, 0))
    const = lambda shape: pl.BlockSpec(shape, lambda i, pt_ref: (0,) * len(shape))
    page_spec = lambda p: pl.BlockSpec((1, A_WIDTH, PAGE_SIZE), lambda i, pt_ref: (pt_ref[i * n_pages + p], 0, 0))
    hd = (A_HEADS, A_HEAD_DIM)
    grid_spec = pltpu.PrefetchScalarGridSpec(
        num_scalar_prefetch=1, grid=(s,),
        in_specs=[per_s(*hd), per_s(*hd), per_s(*hd), per_s(n_pages, PAGE_SIZE), per_s(1, LANES),
                  const(bias_pages.shape), const(bias_new.shape)]
                 + [page_spec(p) for p in range(n_pages)] * 2,
        out_specs=per_s(*hd))
    return pl.pallas_call(functools.partial(_dsa_s_attn_kernel, n_pages=n_pages), grid_spec=grid_spec,
                          out_shape=jax.ShapeDtypeStruct((s,) + hd, F32),
                          compiler_params=_cp("arbitrary"), name="dsa_sample_attn")(
                              pt, q, k_new, v_new, mask, mask_new, bias_pages, bias_new,
                              *([k_pool_t] * n_pages), *([v_pool_t] * n_pages))


def _rglru_gates(xc, wa, ba, wi, bi, lam):
    xb = xc.astype(wa.dtype)
    r = jax.nn.sigmoid(_mm(xb, wa) + ba)
    ig = jax.nn.sigmoid(_mm(xb, wi) + bi)
    log_a = -RG_C * r * jax.nn.softplus(-lam)
    a = jnp.exp(log_a)
    u = jnp.sqrt(1.0 - jnp.exp(2.0 * log_a)) * (ig * xc)
    return a, u


def _rglru_prompt_kernel(xr_ref, gr_ref, cw_ref, cb_ref, wa_ref, ba_ref, wi_ref, bi_ref, lam_ref,
                         y_ref, hl_ref, xs_sc, h_sc, *, tb):
    t = pl.program_id(1)
    pad = SUBLANES

    @pl.when(t == 0)
    def _():
        xs_sc[0:pad, :] = jnp.zeros((pad, xs_sc.shape[1]), F32)
        h_sc[...] = jnp.zeros(h_sc.shape, F32)

    xs_sc[pad:pad + tb, :] = xr_ref[0]
    xc = cb_ref[...] + cw_ref[RG_CONV - 1:RG_CONV, :] * xs_sc[pad:pad + tb, :]
    for j in range(1, RG_CONV):
        xc = xc + cw_ref[RG_CONV - 1 - j:RG_CONV - j, :] * xs_sc[pad - j:pad - j + tb, :]
    a, u = _rglru_gates(xc, wa_ref[...], ba_ref[...], wi_ref[...], bi_ref[...], lam_ref[...])
    rowi = lax.broadcasted_iota(I32, a.shape, 0)
    s = 1
    while s < tb:
        keep = rowi >= s
        a_sh = jnp.where(keep, pltpu.roll(a, s, 0), 1.0)
        u_sh = jnp.where(keep, pltpu.roll(u, s, 0), 0.0)
        u = a * u_sh + u
        a = a * a_sh
        s *= 2
    h = a * h_sc[0:1, :] + u
    y_ref[0] = (h * jax.nn.gelu(gr_ref[0])).astype(y_ref.dtype)
    h_sc[0:1, :] = h[tb - 1:tb, :]
    xs_sc[0:pad, :] = xs_sc[tb:tb + pad, :]

    @pl.when(t == pl.num_programs(1) - 1)
    def _():
        hl_ref[0] = h[tb - 1:tb, :]


def rglru_prompt(xr, gr, cw, cb, wa, ba, wi, bi, lam, tb_cap=256):
    bsz, t, w = xr.shape
    tb = _row_tile(t, tb_cap)
    blk = pl.BlockSpec((1, tb, w), lambda b, i: (b, i, 0))
    vec = _full((1, w))
    return pl.pallas_call(
        functools.partial(_rglru_prompt_kernel, tb=tb), grid=(bsz, t // tb),
        in_specs=[blk, blk, _full((RG_CONV, w)), vec, _full((w, w)), vec, _full((w, w)), vec, vec],
        out_specs=[blk, pl.BlockSpec((1, 1, w), lambda b, i: (b, 0, 0))],
        out_shape=[jax.ShapeDtypeStruct((bsz, t, w), BF16), jax.ShapeDtypeStruct((bsz, 1, w), F32)],
        scratch_shapes=[pltpu.VMEM((tb + SUBLANES, w), F32), pltpu.VMEM((SUBLANES, w), F32)],
        compiler_params=_cp("parallel", "arbitrary"), name="rglru_prompt")(
            xr, gr, cw, cb.reshape(1, w), wa, ba.reshape(1, w), wi, bi.reshape(1, w), lam.reshape(1, w))


def _rglru_step_kernel(xr_ref, gr_ref, b0_ref, b1_ref, b2_ref, h0_ref, cw_ref, cb_ref, wa_ref, ba_ref, wi_ref,
                       bi_ref, lam_ref, y_ref, h_ref):
    xc = (cb_ref[...] + cw_ref[0:1, :] * b0_ref[...] + cw_ref[1:2, :] * b1_ref[...]
          + cw_ref[2:3, :] * b2_ref[...] + cw_ref[3:4, :] * xr_ref[...])
    a, u = _rglru_gates(xc, wa_ref[...], ba_ref[...], wi_ref[...], bi_ref[...], lam_ref[...])
    h = a * h0_ref[...] + u
    h_ref[...] = h
    y_ref[...] = (h * jax.nn.gelu(gr_ref[...])).astype(y_ref.dtype)


def rglru_step(xr, gr, buf, h0, cw, cb, wa, ba, wi, bi, lam):
    s, w = xr.shape
    return pl.pallas_call(
        _rglru_step_kernel,
        out_shape=[jax.ShapeDtypeStruct((s, w), F32), jax.ShapeDtypeStruct((s, w), F32)],
        compiler_params=pltpu.CompilerParams(vmem_limit_bytes=VMEM_LIMIT), name="rglru_step")(
            xr, gr, buf[:, 0], buf[:, 1], buf[:, 2], h0, cw, cb.reshape(1, w), wa, ba.reshape(1, w),
            wi, bi.reshape(1, w), lam.reshape(1, w))


def _headnorm_gate(h, g, og):
    mu = jnp.mean(h, axis=-1, keepdims=True)
    hc = h - mu
    var = jnp.mean(hc * hc, axis=-1, keepdims=True)
    return hc * lax.rsqrt(var + LN_EPS) * g * jax.nn.sigmoid(og)


def _mlstm_prompt_kernel(q_ref, k_ref, v_ref, og_ref, gt_ref, bg_ref, ng_ref, tri_ref,
                         y_ref, c_ref, n_ref, m_ref, c_sc, n_sc, m_sc, *, cl, dh):
    ci = pl.program_id(1)

    @pl.when(ci == 0)
    def _():
        c_sc[...] = jnp.zeros(c_sc.shape, F32)
        n_sc[...] = jnp.zeros(n_sc.shape, F32)
        m_sc[...] = jnp.zeros(m_sc.shape, F32)

    pre = gt_ref[0] + bg_ref[...]
    lane = lax.broadcasted_iota(I32, pre.shape, 1)
    gv = jnp.where(lane < ML_HEADS, pre, jax.nn.log_sigmoid(pre))
    csum = jnp.dot(tri_ref[...], gv, preferred_element_type=F32, precision=HI)
    gv_t = gv.T
    csum_t = csum.T
    row = lax.broadcasted_iota(I32, (cl, cl), 0)
    col = lax.broadcasted_iota(I32, (cl, cl), 1)
    causal = col <= row
    kscale = dh ** -0.5
    for h in range(ML_HEADS):
        sl = slice(h * dh, (h + 1) * dh)
        qh = q_ref[0, :, sl]
        kh = k_ref[0, :, sl] * jnp.asarray(kscale, BF16)
        vh = v_ref[0, :, sl]
        b_col = csum[:, ML_HEADS + h:ML_HEADS + h + 1]
        i_col = gv[:, h:h + 1]
        b_row = csum_t[ML_HEADS + h:ML_HEADS + h + 1, :]
        i_row = gv_t[h:h + 1, :]
        m_prev = m_sc[h:h + 1, 0:1]
        c_prev = c_sc[h]
        n_prev = n_sc[h:h + 1, :]
        dmat = jnp.where(causal, b_col - b_row + i_row, NEG)
        inter = b_col + m_prev
        m_t = jnp.maximum(inter, jnp.max(dmat, axis=1, keepdims=True))
        g_inter = jnp.exp(inter - m_t)
        qk = lax.dot_general(qh, kh, NT, preferred_element_type=F32)
        ws = jnp.exp(dmat - m_t) * qk
        num = (g_inter * lax.dot_general(qh, c_prev.astype(BF16), NT, preferred_element_type=F32)
               + jnp.dot(ws.astype(BF16), vh, preferred_element_type=F32))
        qf = qh.astype(F32)
        den = g_inter * jnp.sum(qf * n_prev, axis=1, keepdims=True) + jnp.sum(ws, axis=1, keepdims=True)
        hout = num / jnp.maximum(jnp.abs(den), jnp.exp(-m_t))
        m_new = m_t[cl - 1:cl, :]
        b_last = b_col[cl - 1:cl, :]
        g_c = jnp.exp(b_last + m_prev - m_new)
        w_end = jnp.exp(b_last - b_col + i_col - m_new)
        kf = kh.astype(F32)
        vw = (vh.astype(F32) * w_end).astype(BF16)
        c_sc[h] = g_c * c_prev + lax.dot_general(vw, kh, TN, preferred_element_type=F32)
        n_sc[h:h + 1, :] = g_c * n_prev + jnp.sum(kf * w_end, axis=0, keepdims=True)
        m_sc[h:h + 1, :] = jnp.broadcast_to(m_new, (1, m_sc.shape[1]))
        y_ref[0, :, sl] = _headnorm_gate(hout, ng_ref[:, sl], og_ref[0, :, sl]).astype(y_ref.dtype)

    @pl.when(ci == pl.num_programs(1) - 1)
    def _():
        c_ref[0] = c_sc[...]
        n_ref[0] = n_sc[...]
        m_ref[0] = m_sc[...]


def mlstm_prompt(q, k, v, og, gates, bgate, norm_g):
    bsz, t, width = q.shape
    dh = width // ML_HEADS
    cl = math.gcd(t, ML_CHUNK)
    tri = (lax.broadcasted_iota(I32, (cl, cl), 1) <= lax.broadcasted_iota(I32, (cl, cl), 0)).astype(F32)
    blk = lambda w: pl.BlockSpec((1, cl, w), lambda b, i: (b, i, 0))
    return pl.pallas_call(
        functools.partial(_mlstm_prompt_kernel, cl=cl, dh=dh), grid=(bsz, t // cl),
        in_specs=[blk(width), blk(width), blk(width), blk(width), blk(LANES), _full((1, LANES)), _full((1, width)),
                  _full((cl, cl))],
        out_specs=[blk(width), pl.BlockSpec((1, ML_HEADS, dh, dh), lambda b, i: (b, 0, 0, 0)),
                   pl.BlockSpec((1, SUBLANES, dh), lambda b, i: (b, 0, 0)),
                   pl.BlockSpec((1, SUBLANES, LANES), lambda b, i: (b, 0, 0))],
        out_shape=[jax.ShapeDtypeStruct((bsz, t, width), BF16), jax.ShapeDtypeStruct((bsz, ML_HEADS, dh, dh), F32),
                   jax.ShapeDtypeStruct((bsz, SUBLANES, dh), F32), jax.ShapeDtypeStruct((bsz, SUBLANES, LANES), F32)],
        scratch_shapes=[pltpu.VMEM((ML_HEADS, dh, dh), F32), pltpu.VMEM((SUBLANES, dh), F32),
                        pltpu.VMEM((SUBLANES, LANES), F32)],
        compiler_params=_cp("parallel", "arbitrary"), name="mlstm_prompt")(
            q, k, v, og, gates, bgate, norm_g.reshape(1, width), tri)


def _mlstm_step_kernel(q_ref, k_ref, v_ref, og_ref, gt_ref, bg_ref, ng_ref, c0_ref, n0_ref, m0_ref,
                       y_ref, c_ref, n_ref, m_ref, *, sb, dh):
    kscale = dh ** -0.5
    m_ref[...] = jnp.zeros(m_ref.shape, F32)

    def seq_body(s, carry):
        rs = pl.ds(s, 1)
        pre = gt_ref[rs, :] + bg_ref[...]
        for h in range(ML_HEADS):
            sl = slice(h * dh, (h + 1) * dh)
            qh = q_ref[rs, sl]
            kh = k_ref[rs, sl] * kscale
            vh = v_ref[rs, sl]
            i_pre = pre[:, h:h + 1]
            logf = jax.nn.log_sigmoid(pre[:, ML_HEADS + h:ML_HEADS + h + 1])
            m_prev = m0_ref[rs, h:h + 1]
            c_prev = c0_ref[s, h]
            n_prev = n0_ref[s, h:h + 1, :]
            inter = logf + m_prev
            m_t = jnp.maximum(inter, i_pre)
            g_inter = jnp.exp(inter - m_t)
            w_in = jnp.exp(i_pre - m_t)
            q16 = _as_mxu_operand(qh)
            qk = jnp.sum(q16 * _as_mxu_operand(kh), axis=1, keepdims=True)
            q8 = jnp.broadcast_to(qh, (SUBLANES, dh)).astype(BF16)
            cq = lax.dot_general(q8, c_prev.astype(BF16), NT, preferred_element_type=F32)[0:1]
            num = g_inter * cq + (w_in * qk) * vh
            den = g_inter * jnp.sum(_as_mxu_operand(n_prev) * q16, axis=1, keepdims=True) + w_in * qk
            hout = num / jnp.maximum(jnp.abs(den), jnp.exp(-m_t))
            rsel = lax.broadcasted_iota(I32, (SUBLANES, dh), 0) == 0
            v8 = jnp.where(rsel, jnp.broadcast_to(w_in * vh, (SUBLANES, dh)), 0.0)
            k8 = jnp.broadcast_to(kh, (SUBLANES, dh))
            c_ref[s, h] = g_inter * c_prev + lax.dot_general(v8, k8, TN, preferred_element_type=F32, precision=HI)
            n_ref[s, h:h + 1, :] = g_inter * n_prev + w_in * kh
            m_ref[rs, h:h + 1] = m_t
            y_ref[rs, sl] = _headnorm_gate(hout, ng_ref[:, sl], og_ref[rs, sl]).astype(y_ref.dtype)
        return carry

    lax.fori_loop(0, sb, seq_body, 0)


def mlstm_step(q, k, v, og, gates, bgate, norm_g, c0, n0, m0, sb=SUBLANES):
    s, width = q.shape
    dh = width // ML_HEADS
    assert s % sb == 0
    row = lambda w: pl.BlockSpec((sb, w), lambda i: (i, 0))
    m0p = jnp.pad(m0, ((0, 0), (0, LANES - ML_HEADS)))
    outs = pl.pallas_call(
        functools.partial(_mlstm_step_kernel, sb=sb, dh=dh), grid=(s // sb,),
        in_specs=[row(width), row(width), row(width), row(width), row(LANES), _full((1, LANES)), _full((1, width)),
                  pl.BlockSpec((sb, ML_HEADS, dh, dh), lambda i: (i, 0, 0, 0)),
                  pl.BlockSpec((sb, ML_HEADS, dh), lambda i: (i, 0, 0)), row(LANES)],
        out_specs=[row(width), pl.BlockSpec((sb, ML_HEADS, dh, dh), lambda i: (i, 0, 0, 0)),
                   pl.BlockSpec((sb, ML_HEADS, dh), lambda i: (i, 0, 0)), row(LANES)],
        out_shape=[jax.ShapeDtypeStruct((s, width), F32), jax.ShapeDtypeStruct((s, ML_HEADS, dh, dh), F32),
                   jax.ShapeDtypeStruct((s, ML_HEADS, dh), F32), jax.ShapeDtypeStruct((s, LANES), F32)],
        compiler_params=_cp("parallel"), name="mlstm_step")(
            q, k, v, og, gates, bgate, norm_g.reshape(1, width), c0, n0, m0p)
    y, c, n, m = outs
    return y, c, n, m[:, :ML_HEADS]


def _xattn_prompt_kernel(q_ref, k_ref, v_ref, o_ref, *, dh):
    scale = dh ** -0.5
    for h in range(X_HEADS):
        sl = slice(h * dh, (h + 1) * dh)
        lg = lax.dot_general(q_ref[0, :, sl], k_ref[0, :, sl], NT, preferred_element_type=F32) * scale
        mx = jnp.max(lg, axis=-1, keepdims=True)
        e = jnp.exp(lg - mx)
        pr = e / jnp.sum(e, axis=-1, keepdims=True)
        o_ref[0, :, sl] = jnp.dot(pr.astype(BF16), v_ref[0, :, sl], preferred_element_type=F32).astype(o_ref.dtype)


def xattn_prompt(xq, mk, mv, tq_cap=512):
    bsz, t, d = xq.shape
    mt = mk.shape[1]
    tq = _row_tile(t, tq_cap)
    return pl.pallas_call(
        functools.partial(_xattn_prompt_kernel, dh=d // X_HEADS), grid=(bsz, t // tq),
        in_specs=[pl.BlockSpec((1, tq, d), lambda b, i: (b, i, 0)), pl.BlockSpec((1, mt, d), lambda b, i: (b, 0, 0)),
                  pl.BlockSpec((1, mt, d), lambda b, i: (b, 0, 0))],
        out_specs=pl.BlockSpec((1, tq, d), lambda b, i: (b, i, 0)),
        out_shape=jax.ShapeDtypeStruct((bsz, t, d), BF16),
        compiler_params=_cp("parallel", "parallel"), name="xattn_prompt")(xq, mk, mv)


def _xattn_step_kernel(q_ref, k_ref, v_ref, o_ref, *, sb, mt, dh):
    scale = dh ** -0.5
    parts = dh // LANES
    rows = X_HEADS * parts

    def seq_body(s, carry):
        rs = pl.ds(s, 1)
        qrow = _as_mxu_operand(q_ref[rs, :])
        pieces = []
        for h in range(X_HEADS):
            lg = jnp.zeros((mt, 1), F32)
            for c in range(parts):
                qp = qrow[:, h * dh + c * LANES:h * dh + (c + 1) * LANES]
                kp = _as_mxu_operand(k_ref[0, s, pl.ds(c * X_HEADS + h, mt, stride=rows), :])
                lg = lg + jnp.sum(kp * qp, axis=1, keepdims=True)
            lg = lg * scale
            e = jnp.exp(lg - jnp.max(lg, axis=0, keepdims=True))
            pr = _as_mxu_operand(e / jnp.sum(e, axis=0, keepdims=True))
            for c in range(parts):
                vp = _as_mxu_operand(v_ref[0, s, pl.ds(c * X_HEADS + h, mt, stride=rows), :])
                pieces.append(jnp.sum(pr * vp, axis=0, keepdims=True))
        o_ref[rs, :] = jnp.concatenate(pieces, axis=1)
        return carry

    lax.fori_loop(0, sb, seq_body, 0)


def _lane_chunk_view(mem):
    nl, s, mt, nh, dh = mem.shape
    v = mem.reshape(nl, s, mt, nh, dh // LANES, LANES)
    return jnp.transpose(v, (0, 1, 2, 4, 3, 5)).reshape(nl, s, mt * (dh // LANES) * nh, LANES)


def xattn_step(xq, mk_view, mv_view, layer, mt, sb=SUBLANES):
    s, d = xq.shape
    dh = d // X_HEADS
    assert s % sb == 0 and dh % LANES == 0
    nrows = mk_view.shape[2]
    mem_spec = pl.BlockSpec((1, sb, nrows, LANES), lambda i: (layer, i, 0, 0))
    return pl.pallas_call(
        functools.partial(_xattn_step_kernel, sb=sb, mt=mt, dh=dh), grid=(s // sb,),
        in_specs=[pl.BlockSpec((sb, d), lambda i: (i, 0)), mem_spec, mem_spec],
        out_specs=pl.BlockSpec((sb, d), lambda i: (i, 0)),
        out_shape=jax.ShapeDtypeStruct((s, d), F32),
        compiler_params=_cp("parallel"), name="xattn_step")(xq, mk_view, mv_view)


def _t5_bucket(dist):
    max_exact = REL_BUCKETS // 2
    d = jnp.maximum(dist.astype(F32), 1.0)
    large = max_exact + (jnp.log(d / max_exact) / math.log(REL_MAX_DIST / max_exact)
                         * (REL_BUCKETS - max_exact)).astype(I32)
    large = jnp.minimum(large, REL_BUCKETS - 1)
    return jnp.where(dist < max_exact, dist, large)


def _pad_cols(w, n):
    return jnp.pad(w, ((0, 0), (0, n - w.shape[1])))


def _block_diag(w):
    nb, di, do = w.shape
    eye = jnp.eye(nb, dtype=w.dtype)
    return jnp.einsum('nde,nm->ndme', w, eye).reshape(nb * di, nb * do)


def kernel(x_prompt, x_sample, cache_k, cache_v, cache_kidx, state_rglru_h, state_rglru_conv, state_mlstm_c,
           state_mlstm_n, state_mlstm_m, cache_mem_k, cache_mem_v, page_table, mem_prompt, ln_g, ln_b, rel_bias,
           w_in_even, w_out_even, rg_conv_w, rg_conv_b, rg_w_a, rg_b_a, rg_w_i, rg_b_i, rg_lambda, w_in_odd,
           b_gate_odd, ml_norm_g, w_out_odd, w_xq, w_xk, w_xv, w_xo, ffn_w_gate, ffn_w_up, ffn_w_down,
           moe_router, moe_w_gate, moe_w_up, moe_w_down):
    bsz, t, d = x_prompt.shape
    s = x_sample.shape[0]
    depth = ln_g.shape[0]
    n_pages = page_table.shape[1]
    past_len = n_pages * PAGE_SIZE
    mt = mem_prompt.shape[1]
    rg_w = rg_lambda.shape[1]
    ml_w = ml_norm_g.shape[1]
    n_experts = moe_router.shape[2]
    alpha = (2.0 * depth) ** 0.25
    bf = lambda a: a.astype(BF16)

    assert REL_MAX_DIST <= TQ
    rel32 = rel_bias.astype(F32)
    by_dist = rel32[_t5_bucket(jnp.arange(2 * TQ + 1, dtype=I32))]

    def spread(dist):
        out = jnp.broadcast_to(by_dist[2 * TQ], dist.shape + (A_HEADS,))
        for dd in range(2 * TQ):
            out = jnp.where((dist == dd)[..., None], by_dist[dd], out)
        return out

    kk_ = lax.broadcasted_iota(I32, (TQ, TQ), 0)
    qq_ = lax.broadcasted_iota(I32, (TQ, TQ), 1)
    bias_tab = jnp.stack([spread(jnp.maximum(qq_ - kk_, 0)), spread(TQ + qq_ - kk_),
                          spread(jnp.full((TQ, TQ), 2 * TQ, I32))])
    bias_tab = jnp.transpose(bias_tab, (0, 3, 1, 2))
    dist_s = (past_len - jnp.arange(past_len, dtype=I32)).reshape(n_pages, PAGE_SIZE)
    bias_s = jnp.transpose(rel32[_t5_bucket(dist_s)], (0, 2, 1))
    bias_s_new = jnp.broadcast_to(rel32[_t5_bucket(jnp.zeros((1,), I32))].reshape(A_HEADS, 1), (A_HEADS, LANES))
    mem_k_view, mem_v_view = _lane_chunk_view(cache_mem_k), _lane_chunk_view(cache_mem_v)

    mem_bf = bf(mem_prompt.reshape(bsz * mt, d))
    kv_w = [bf(w_xk[l]) for l in range(depth)] + [bf(w_xv[l]) for l in range(depth)]
    kv = multi_proj(mem_bf, kv_w, [(F32, BF16)] * (2 * depth))
    mem_k_f32 = [kv[2 * l] for l in range(depth)]
    mem_k_bf = [kv[2 * l + 1].reshape(bsz, mt, d) for l in range(depth)]
    mem_v_f32 = [kv[2 * depth + 2 * l] for l in range(depth)]
    mem_v_bf = [kv[2 * depth + 2 * l + 1].reshape(bsz, mt, d) for l in range(depth)]

    xp = x_prompt.reshape(bsz * t, d)
    xs = x_sample.reshape(s, d)
    xp_bf, xs_bf = bf(xp), bf(xs)
    outs = {}
    k_p, v_p, ki_p, k_s, v_s, ki_s = [], [], [], [], [], []
    rgh_p, rgc_p, rgh_s, rgc_s = [], [], [], []
    mlc_p, mln_p, mlm_p, mlc_s, mln_s, mlm_s = [], [], [], [], [], []

    for l in range(depth):
        if l % 2 == 0:
            e = l // 2
            w_in = w_in_even[e]
            o = 0
            cols = {}
            for name, width in (("q", A_WIDTH), ("k", A_WIDTH), ("v", A_WIDTH), ("qi", IDX_HEADS * IDX_DIM),
                                ("ki", IDX_DIM), ("wi", IDX_HEADS), ("xr", rg_w), ("gr", rg_w)):
                cols[name] = bf(w_in[:, o:o + width])
                o += width
            cols["wi"] = _pad_cols(cols["wi"], LANES)
            names = ("q", "k", "v", "qi", "ki", "wi", "xr", "gr")
            wq_t, wv_t, wqi_t = cols["q"].T, cols["v"].T, cols["qi"].T
            wi_t = jnp.pad(cols["wi"][:, :IDX_HEADS].T, ((0, SUBLANES - IDX_HEADS), (0, 0)))
            wa = bf(_block_diag(rg_w_a[e]))
            wi_g = bf(_block_diag(rg_w_i[e]))
            w_out = bf(w_out_even[e])

            nq = t // TQ
            pnames = ("k", "v", "ki", "xr", "gr")
            kf, kb, vf, kif, kib, xr, gr, qt, vt, qit, wit = multi_proj(
                xp_bf, [cols[n] for n in pnames] + [wq_t, wv_t, wqi_t, wi_t],
                [(F32, BF16), (F32,), (F32, BF16), (F32,), (F32,), (BF16,), (BF16,), (BF16,), (F32,)],
                transposed=(False,) * len(pnames) + (True,) * 4)
            r3 = lambda a: a.reshape(bsz, t, a.shape[-1])
            r4 = lambda a: a.reshape(bsz, nq, a.shape[-2], TQ)
            att = dsa_prompt(r4(qt), r3(kb), r4(vt), r4(qit), r3(kib), r4(wit), bias_tab)
            rg_y, h_last = rglru_prompt(r3(xr), r3(gr), rg_conv_w[e], rg_conv_b[e], wa, rg_b_a[e], wi_g, rg_b_i[e],
                                        rg_lambda[e])
            xp, xp_bf = mm_res_ln([att.reshape(bsz * t, A_WIDTH), rg_y.reshape(bsz * t, rg_w)],
                                  [w_out[:A_WIDTH], w_out[A_WIDTH:]], xp, ln_g[l, 0], ln_b[l, 0], alpha)
            k_p.append(kf.reshape(bsz, t, A_HEADS, A_HEAD_DIM))
            v_p.append(vf.reshape(bsz, t, A_HEADS, A_HEAD_DIM))
            ki_p.append(kif.reshape(bsz, t, IDX_DIM))
            rgh_p.append(h_last.reshape(bsz, rg_w))
            rgc_p.append(r3(xr)[:, t - (RG_CONV - 1):, :])

            q, kf, vf, qi, kif, wi, xr, gr = multi_proj(xs_bf, [cols[n] for n in names], [(F32,)] * len(names))
            qi8 = jnp.pad(qi.reshape(s, IDX_HEADS, IDX_DIM), ((0, 0), (0, SUBLANES - IDX_HEADS), (0, 0)))
            pool = cache_k.shape[1]
            kidx_t = jnp.transpose(cache_kidx[e], (0, 2, 1))
            k_t = jnp.transpose(cache_k[e], (0, 2, 3, 1)).reshape(pool, A_WIDTH, PAGE_SIZE)
            v_t = jnp.transpose(cache_v[e], (0, 2, 3, 1)).reshape(pool, A_WIDTH, PAGE_SIZE)
            raw = dsa_sample_scores(page_table, qi8, wi.reshape(s, 1, LANES), kidx_t)
            mask, mask_new = dsa_sample_select(raw.reshape(s, past_len), qi, kif, wi)
            hd3 = lambda a: a.reshape(s, A_HEADS, A_HEAD_DIM)
            att = dsa_sample_attn(page_table, hd3(q), hd3(kf), hd3(vf), mask.reshape(s, n_pages, PAGE_SIZE),
                                  mask_new.reshape(s, 1, LANES), bias_s, bias_s_new, k_t, v_t)
            rg_y, h_new = rglru_step(xr, gr, state_rglru_conv[e], state_rglru_h[e], rg_conv_w[e], rg_conv_b[e],
                                     wa, rg_b_a[e], wi_g, rg_b_i[e], rg_lambda[e])
            xs, xs_bf = mm_res_ln([bf(att.reshape(s, A_WIDTH)), bf(rg_y)], [w_out[:A_WIDTH], w_out[A_WIDTH:]], xs,
                                  ln_g[l, 0], ln_b[l, 0], alpha)
            k_s.append(kf.reshape(s, 1, A_HEADS, A_HEAD_DIM))
            v_s.append(vf.reshape(s, 1, A_HEADS, A_HEAD_DIM))
            ki_s.append(kif.reshape(s, 1, IDX_DIM))
            rgh_s.append(h_new)
            rgc_s.append(jnp.concatenate([state_rglru_conv[e][:, 1:], xr[:, None, :]], axis=1))
        else:
            o_ = l // 2
            w_in = w_in_odd[o_]
            ws = [bf(w_in[:, i * ml_w:(i + 1) * ml_w]) for i in range(4)]
            ws.append(_pad_cols(bf(w_in[:, 4 * ml_w:]), LANES))
            bgate = jnp.pad(b_gate_odd[o_].astype(F32), (0, LANES - 2 * ML_HEADS)).reshape(1, LANES)
            w_out = bf(w_out_odd[o_])

            q, k, v, og, gates = multi_proj(xp_bf, ws, [(BF16,), (BF16,), (BF16,), (F32,), (F32,)])
            r3 = lambda a: a.reshape(bsz, t, a.shape[-1])
            hn, c, n, m = mlstm_prompt(r3(q), r3(k), r3(v), r3(og), r3(gates), bgate, ml_norm_g[o_])
            xp, xp_bf = mm_res_ln([hn.reshape(bsz * t, ml_w)], [w_out], xp, ln_g[l, 0], ln_b[l, 0], alpha)
            mlc_p.append(c)
            mln_p.append(n[:, :ML_HEADS])
            mlm_p.append(m[:, :ML_HEADS, 0])

            q, k, v, og, gates = multi_proj(xs_bf, ws, [(F32,)] * len(ws))
            hn, c, n, m = mlstm_step(q, k, v, og, gates, bgate, ml_norm_g[o_], state_mlstm_c[o_], state_mlstm_n[o_],
                                     state_mlstm_m[o_])
            xs, xs_bf = mm_res_ln([bf(hn)], [w_out], xs, ln_g[l, 0], ln_b[l, 0], alpha)
            mlc_s.append(c)
            mln_s.append(n)
            mlm_s.append(m)

        wq, wo = bf(w_xq[l]), bf(w_xo[l])
        (xq,) = multi_proj(xp_bf, [wq], [(BF16,)])
        xo = xattn_prompt(xq.reshape(bsz, t, d), mem_k_bf[l], mem_v_bf[l])
        xp, xp_bf = mm_res_ln([xo.reshape(bsz * t, d)], [wo], xp, ln_g[l, 1], ln_b[l, 1], alpha)
        (xq,) = multi_proj(xs_bf, [wq], [(F32,)])
        xo = xattn_step(xq, mem_k_view, mem_v_view, l, mt)
        xs, xs_bf = mm_res_ln([bf(xo)], [wo], xs, ln_g[l, 1], ln_b[l, 1], alpha)

        if l % 2 == 0:
            e = l // 2
            wg, wu, wd = bf(ffn_w_gate[e]), bf(ffn_w_up[e]), bf(ffn_w_down[e])
            xp, xp_bf = ffn_ln(xp_bf, xp, wg, wu, wd, ln_g[l, 2], ln_b[l, 2], alpha)
            xs, xs_bf = ffn_ln(xs_bf, xs, wg, wu, wd, ln_g[l, 2], ln_b[l, 2], alpha)
        else:
            o_ = l // 2
            wr = _pad_cols(bf(moe_router[o_]), LANES)
            wg, wu, wd = bf(moe_w_gate[o_]), bf(moe_w_up[o_]), bf(moe_w_down[o_])
            xp, xp_bf = moe_top2_ln(xp_bf, xp, wr, wg, wu, wd, ln_g[l, 2], ln_b[l, 2], alpha)
            gs = router_gates(xs_bf, wr, n_experts)
            xs, xs_bf = moe_ln(xs_bf, xs, gs, wg, wu, wd, ln_g[l, 2], ln_b[l, 2], alpha)

    xh = mem_k_f32[0].shape[-1] // X_HEADS
    mem_shape = (bsz, mt, X_HEADS, xh)
    return (xp.reshape(bsz, t, d), xs.reshape(s, 1, d),
            jnp.stack(k_p), jnp.stack(v_p), jnp.stack(ki_p), jnp.stack(k_s), jnp.stack(v_s), jnp.stack(ki_s),
            jnp.stack(rgh_p), jnp.stack(rgc_p), jnp.stack(rgh_s), jnp.stack(rgc_s),
            jnp.stack(mlc_p), jnp.stack(mln_p), jnp.stack(mlm_p), jnp.stack(mlc_s), jnp.stack(mln_s), jnp.stack(mlm_s),
            jnp.stack([a.reshape(mem_shape) for a in mem_k_f32]), jnp.stack([a.reshape(mem_shape) for a in mem_v_f32]))
```

```python
import functools
import math

import jax
import jax.numpy as jnp
from jax import lax
from jax.experimental import pallas as pl
from jax.experimental.pallas import tpu as pltpu

F32 = jnp.float32
BF16 = jnp.bfloat16
I32 = jnp.int32

PAGE_SIZE = 128
A_HEADS = 8
A_HEAD_DIM = 64
A_WIDTH = A_HEADS * A_HEAD_DIM
IDX_HEADS = 4
IDX_DIM = 64
TOPK_MAX = 256
REL_BUCKETS = 32
REL_MAX_DIST = 128
RG_BLOCKS = 8
RG_CONV = 4
RG_C = 8.0
ML_HEADS = 4
ML_CHUNK = 128
X_HEADS = 4
TOP_K = 2
LN_EPS = 1e-5

LANES = 128
SUBLANES = 8
VMEM_LIMIT = 48 * 1024 * 1024
VMEM_LIMIT_RESIDENT = 56 * 1024 * 1024
NEG = -1e30
INT_MIN = -2 ** 31
TQ = 128

NT = (((1,), (1,)), ((), ()))
TN = (((0,), (0,)), ((), ()))
HI = lax.Precision.HIGHEST


def _cp(*sem):
    return pltpu.CompilerParams(dimension_semantics=sem, vmem_limit_bytes=VMEM_LIMIT)


def _full(shape):
    n = len(shape)
    return pl.BlockSpec(shape, lambda *_: (0,) * n)


def _row_tile(m, cap):
    t = min(m, cap)
    assert m % t == 0
    return t


def _as_mxu_operand(a):
    return a.astype(BF16).astype(F32)


def _mm(a, b, dims=None):
    prec = HI if a.dtype == F32 else None
    if dims is None:
        return jnp.dot(a, b, preferred_element_type=F32, precision=prec)
    return lax.dot_general(a, b, dims, preferred_element_type=F32, precision=prec)


def _proj_kernel(x_ref, *refs, n_w, out_counts, transposed):
    w_refs, o_refs = refs[:n_w], refs[n_w:]
    x = x_ref[...]
    k = 0
    for w_ref, cnt, tr in zip(w_refs, out_counts, transposed):
        if tr:
            y = _mm(w_ref[...], x, NT)
        else:
            y = _mm(x, w_ref[...])
        for _ in range(cnt):
            if tr:
                for j in range(y.shape[1] // LANES):
                    o_refs[k][j] = y[:, j * LANES:(j + 1) * LANES].astype(o_refs[k].dtype)
            else:
                o_refs[k][...] = y.astype(o_refs[k].dtype)
            k += 1


def multi_proj(x, ws, out_dtypes, transposed=None, tm_cap=512):
    m, kdim = x.shape
    tm = _row_tile(m, tm_cap)
    transposed = transposed or (False,) * len(ws)
    in_specs = [pl.BlockSpec((tm, kdim), lambda i: (i, 0))] + [_full(w.shape) for w in ws]
    out_shape, out_specs = [], []
    for w, dts, tr in zip(ws, out_dtypes, transposed):
        for dt in dts:
            if tr:
                assert tm % LANES == 0
                out_shape.append(jax.ShapeDtypeStruct((m // LANES, w.shape[0], LANES), dt))
                out_specs.append(pl.BlockSpec((tm // LANES, w.shape[0], LANES), lambda i: (i, 0, 0)))
            else:
                out_shape.append(jax.ShapeDtypeStruct((m, w.shape[1]), dt))
                out_specs.append(pl.BlockSpec((tm, w.shape[1]), lambda i: (i, 0)))
    kern = functools.partial(_proj_kernel, n_w=len(ws), out_counts=tuple(len(d) for d in out_dtypes),
                             transposed=tuple(transposed))
    return pl.pallas_call(kern, grid=(m // tm,), in_specs=in_specs, out_specs=out_specs,
                          out_shape=out_shape, compiler_params=_cp("parallel"), name="multi_proj")(x, *ws)


def _layernorm_rows(z, g, b):
    mu = jnp.mean(z, axis=-1, keepdims=True)
    zc = z - mu
    var = jnp.mean(zc * zc, axis=-1, keepdims=True)
    return zc * lax.rsqrt(var + LN_EPS) * g + b


def _mm_ln_kernel(*refs, n_a, n_next, alpha):
    a_refs, w_refs = refs[:n_a], refs[n_a:2 * n_a]
    r_ref, g_ref, b_ref = refs[2 * n_a:2 * n_a + 3]
    next_w_refs = refs[2 * n_a + 3:2 * n_a + 3 + n_next]
    o_ref, obf_ref = refs[2 * n_a + 3 + n_next:2 * n_a + 5 + n_next]
    next_o_refs = refs[2 * n_a + 5 + n_next:]
    y = alpha * r_ref[...]
    for a_ref, w_ref in zip(a_refs, w_refs):
        y = y + _mm(a_ref[...], w_ref[...])
    out = _layernorm_rows(y, g_ref[...], b_ref[...])
    o_ref[...] = out
    out_bf = out.astype(BF16)
    obf_ref[...] = out_bf
    for w_ref, n_ref in zip(next_w_refs, next_o_refs):
        n_ref[...] = _mm(out_bf, w_ref[...]).astype(n_ref.dtype)


def mm_res_ln(a_list, w_list, resid, g, b, alpha, next_ws=(), tm_cap=512):
    m, d = resid.shape
    tm = _row_tile(m, tm_cap)
    rows = lambda w: pl.BlockSpec((tm, w), lambda i: (i, 0))
    in_specs = ([rows(a.shape[1]) for a in a_list] + [_full(w.shape) for w in w_list]
                + [rows(d), _full((1, d)), _full((1, d))] + [_full(w.shape) for w in next_ws])
    out_specs = [rows(d), rows(d)] + [rows(w.shape[1]) for w in next_ws]
    out_shape = ([jax.ShapeDtypeStruct((m, d), F32), jax.ShapeDtypeStruct((m, d), BF16)]
                 + [jax.ShapeDtypeStruct((m, w.shape[1]), BF16) for w in next_ws])
    kern = functools.partial(_mm_ln_kernel, n_a=len(a_list), n_next=len(next_ws), alpha=alpha)
    return pl.pallas_call(kern, grid=(m // tm,), in_specs=in_specs, out_specs=out_specs, out_shape=out_shape,
                          compiler_params=_cp("parallel"), name="mm_res_ln")(
                              *a_list, *w_list, resid, g.reshape(1, d), b.reshape(1, d), *next_ws)


def _ffn_kernel(x_ref, r_ref, wg_ref, wu_ref, wd_ref, g_ref, b_ref, o_ref, obf_ref, acc_sc, *, alpha):
    f = pl.program_id(1)

    @pl.when(f == 0)
    def _():
        acc_sc[...] = alpha * r_ref[...]

    x = x_ref[...]
    hg = _mm(x, wg_ref[...])
    hu = _mm(x, wu_ref[...])
    h = (jax.nn.silu(hg) * hu).astype(x.dtype)
    acc_sc[...] += _mm(h, wd_ref[...])

    @pl.when(f == pl.num_programs(1) - 1)
    def _():
        out = _layernorm_rows(acc_sc[...], g_ref[...], b_ref[...])
        o_ref[...] = out
        obf_ref[...] = out.astype(BF16)


def ffn_ln(x_bf, resid, wg, wu, wd, g, b, alpha, tm_cap=512, tf_cap=1408):
    m, d = resid.shape
    dff = wg.shape[1]
    tm = _row_tile(m, tm_cap)
    tf = _ff_tile(dff, tf_cap)
    in_specs = [pl.BlockSpec((tm, d), lambda i, f: (i, 0)), pl.BlockSpec((tm, d), lambda i, f: (i, 0)),
                pl.BlockSpec((d, tf), lambda i, f: (0, f)), pl.BlockSpec((d, tf), lambda i, f: (0, f)),
                pl.BlockSpec((tf, d), lambda i, f: (f, 0)), _full((1, d)), _full((1, d))]
    out_specs = [pl.BlockSpec((tm, d), lambda i, f: (i, 0))] * 2
    out_shape = [jax.ShapeDtypeStruct((m, d), F32), jax.ShapeDtypeStruct((m, d), BF16)]
    return pl.pallas_call(functools.partial(_ffn_kernel, alpha=alpha), grid=(m // tm, dff // tf),
                          in_specs=in_specs, out_specs=out_specs, out_shape=out_shape,
                          scratch_shapes=[pltpu.VMEM((tm, d), F32)],
                          compiler_params=_cp("parallel", "arbitrary"), name="ffn_ln")(
                              x_bf, resid, wg, wu, wd, g.reshape(1, d), b.reshape(1, d))


def _top2(logits, n_experts):
    lane = lax.broadcasted_iota(I32, logits.shape, 1)
    valid = lane < n_experts
    logits = jnp.where(valid, logits, NEG)
    mx = jnp.max(logits, axis=-1, keepdims=True)
    e = jnp.where(valid, jnp.exp(logits - mx), 0.0)
    p = e / jnp.sum(e, axis=-1, keepdims=True)
    p = jnp.where(valid, p, -1.0)
    big = jnp.int32(LANES)
    m1 = jnp.max(p, axis=-1, keepdims=True)
    i1 = jnp.min(jnp.where(p == m1, lane, big), axis=-1, keepdims=True)
    p2 = jnp.where(lane == i1, -1.0, p)
    m2 = jnp.max(p2, axis=-1, keepdims=True)
    i2 = jnp.min(jnp.where(p2 == m2, lane, big), axis=-1, keepdims=True)
    tot = m1 + m2
    return lane, i1, i2, m1 / tot, m2 / tot


def _router_kernel(x_ref, w_ref, o_ref, *, n_experts):
    lane, i1, i2, g1, g2 = _top2(_mm(x_ref[...], w_ref[...]), n_experts)
    o_ref[...] = jnp.where(lane == i1, g1, jnp.where(lane == i2, g2, 0.0))


def router_gates(x, w_pad, n_experts, tm_cap=512):
    m, d = x.shape
    tm = _row_tile(m, tm_cap)
    return pl.pallas_call(functools.partial(_router_kernel, n_experts=n_experts), grid=(m // tm,),
                          in_specs=[pl.BlockSpec((tm, d), lambda i: (i, 0)), _full(w_pad.shape)],
                          out_specs=pl.BlockSpec((tm, LANES), lambda i: (i, 0)),
                          out_shape=jax.ShapeDtypeStruct((m, LANES), F32),
                          compiler_params=_cp("parallel"), name="router")(x, w_pad)


def _moe_kernel(x_ref, r_ref, gt_ref, wg_ref, wu_ref, wd_ref, g_ref, b_ref, o_ref, obf_ref, acc_sc, *, alpha):
    e = pl.program_id(1)
    f = pl.program_id(2)

    @pl.when((e == 0) & (f == 0))
    def _():
        acc_sc[...] = alpha * r_ref[...]

    gt = gt_ref[...]
    lane = lax.broadcasted_iota(I32, gt.shape, 1)
    ge = jnp.sum(jnp.where(lane == e, gt, 0.0), axis=-1, keepdims=True)
    x = x_ref[...]
    hg = jnp.dot(x, wg_ref[0], preferred_element_type=F32)
    hu = jnp.dot(x, wu_ref[0], preferred_element_type=F32)
    h = (jax.nn.silu(hg) * hu).astype(BF16)
    acc_sc[...] += ge * jnp.dot(h, wd_ref[0], preferred_element_type=F32)

    @pl.when((e == pl.num_programs(1) - 1) & (f == pl.num_programs(2) - 1))
    def _():
        out = _layernorm_rows(acc_sc[...], g_ref[...], b_ref[...])
        o_ref[...] = out
        obf_ref[...] = out.astype(BF16)


def moe_ln(x_bf, resid, gates, wg, wu, wd, g, b, alpha, tm_cap=512, tf=512):
    m, d = resid.shape
    ne, _, dff = wg.shape
    tm = _row_tile(m, tm_cap)
    assert dff % tf == 0
    in_specs = [pl.BlockSpec((tm, d), lambda i, e, f: (i, 0)), pl.BlockSpec((tm, d), lambda i, e, f: (i, 0)),
                pl.BlockSpec((tm, LANES), lambda i, e, f: (i, 0)),
                pl.BlockSpec((1, d, tf), lambda i, e, f: (e, 0, f)), pl.BlockSpec((1, d, tf), lambda i, e, f: (e, 0, f)),
                pl.BlockSpec((1, tf, d), lambda i, e, f: (e, f, 0)), _full((1, d)), _full((1, d))]
    out_specs = [pl.BlockSpec((tm, d), lambda i, e, f: (i, 0))] * 2
    out_shape = [jax.ShapeDtypeStruct((m, d), F32), jax.ShapeDtypeStruct((m, d), BF16)]
    return pl.pallas_call(functools.partial(_moe_kernel, alpha=alpha), grid=(m // tm, ne, dff // tf),
                          in_specs=in_specs, out_specs=out_specs, out_shape=out_shape,
                          scratch_shapes=[pltpu.VMEM((tm, d), F32)],
                          compiler_params=_cp("parallel", "arbitrary", "arbitrary"), name="moe_ln")(
                              x_bf, resid, gates, wg, wu, wd, g.reshape(1, d), b.reshape(1, d))


MOE_TILE = 512
COMBINE_SLACK = LANES
ROUTE_LANES = ("e1", "e2", "g1", "g2", "rank1", "rank2")


def _router_plan_kernel(x_ref, w_ref, ltri_ref, route_ref, before_ref, count_ref, carry_sc, *, n_experts):
    @pl.when(pl.program_id(0) == 0)
    def _():
        carry_sc[...] = jnp.zeros(carry_sc.shape, F32)

    lane, i1, i2, g1, g2 = _top2(_mm(x_ref[...], w_ref[...]), n_experts)
    chosen = jnp.where(lane == i1, 1.0, jnp.where(lane == i2, 1.0, 0.0))
    before = carry_sc[0:1, :]
    rank = before + jnp.dot(ltri_ref[...], chosen.astype(BF16), preferred_element_type=F32)
    r1 = jnp.sum(jnp.where(lane == i1, rank, 0.0), axis=-1, keepdims=True)
    r2 = jnp.sum(jnp.where(lane == i2, rank, 0.0), axis=-1, keepdims=True)
    vals = (i1.astype(F32), i2.astype(F32), g1, g2, r1, r2)
    route = jnp.zeros(lane.shape, F32)
    for k, v in enumerate(vals):
        route = jnp.where(lane == k, v, route)
    route_ref[...] = route
    before_ref[0] = jnp.broadcast_to(before, before_ref.shape[1:])
    after = before + jnp.sum(chosen, axis=0, keepdims=True)
    carry_sc[0:1, :] = after
    count_ref[...] = jnp.broadcast_to(after, count_ref.shape)


def router_plan(x_bf, w_pad, n_experts):
    m, d = x_bf.shape
    tb = MOE_TILE
    assert m % tb == 0
    ltri = (lax.broadcasted_iota(I32, (tb, tb), 1) < lax.broadcasted_iota(I32, (tb, tb), 0)).astype(BF16)
    return pl.pallas_call(
        functools.partial(_router_plan_kernel, n_experts=n_experts), grid=(m // tb,),
        in_specs=[pl.BlockSpec((tb, d), lambda i: (i, 0)), _full(w_pad.shape), _full((tb, tb))],
        out_specs=[pl.BlockSpec((tb, LANES), lambda i: (i, 0)), pl.BlockSpec((1, SUBLANES, LANES), lambda i: (i, 0, 0)),
                   _full((SUBLANES, LANES))],
        out_shape=[jax.ShapeDtypeStruct((m, LANES), F32), jax.ShapeDtypeStruct((m // tb, SUBLANES, LANES), F32),
                   jax.ShapeDtypeStruct((SUBLANES, LANES), F32)],
        scratch_shapes=[pltpu.VMEM((SUBLANES, LANES), F32)],
        compiler_params=_cp("arbitrary"), name="router_plan")(x_bf, w_pad, ltri)


def _moe_gather_kernel(blo_ref, bhi_ref, x_ref, pos_ref, g_ref, xs_ref, gs_ref, acc_sc, gacc_sc, *, tile):
    j = pl.program_id(0)
    acc_sc[...] = jnp.zeros(acc_sc.shape, F32)
    gacc_sc[...] = jnp.zeros(gacc_sc.shape, F32)
    rid = j * tile + lax.broadcasted_iota(I32, (tile, tile), 0)
    glane = lax.broadcasted_iota(I32, gacc_sc.shape, 1)

    def body(b, carry):
        off = pl.multiple_of(b * tile, tile)
        m1 = jnp.where(pos_ref[b, 0:1, :] == rid, 1.0, 0.0).astype(BF16)
        m2 = jnp.where(pos_ref[b, 1:2, :] == rid, 1.0, 0.0).astype(BF16)
        acc_sc[...] += jnp.dot(m1 + m2, x_ref[pl.ds(off, tile), :], preferred_element_type=F32)
        gb = g_ref[pl.ds(off, tile), :]
        gacc_sc[...] += jnp.where(glane < 3, jnp.dot(m1, gb, preferred_element_type=F32),
                                  jnp.dot(m2, gb, preferred_element_type=F32))
        return carry

    lax.fori_loop(blo_ref[j], bhi_ref[j] + 1, body, 0)
    xs_ref[...] = acc_sc[...].astype(xs_ref.dtype)
    gs_ref[...] = gacc_sc[...]


def moe_gather(x_bf, pos, gsplit, b_lo, b_hi, n_tiles):
    m, d = x_bf.shape
    tile = MOE_TILE
    resident = lambda shape: pl.BlockSpec(shape, lambda j, lo, hi: (0,) * len(shape), pipeline_mode=pl.Buffered(1))
    grid_spec = pltpu.PrefetchScalarGridSpec(
        num_scalar_prefetch=2, grid=(n_tiles,),
        in_specs=[resident((m, d)), resident(pos.shape), resident(gsplit.shape)],
        out_specs=[pl.BlockSpec((tile, d), lambda j, lo, hi: (j, 0)),
                   pl.BlockSpec((tile, LANES), lambda j, lo, hi: (j, 0))],
        scratch_shapes=[pltpu.VMEM((tile, d), F32), pltpu.VMEM((tile, LANES), F32)])
    return pl.pallas_call(
        functools.partial(_moe_gather_kernel, tile=tile), grid_spec=grid_spec,
        out_shape=[jax.ShapeDtypeStruct((n_tiles * tile, d), BF16),
                   jax.ShapeDtypeStruct((n_tiles * tile, LANES), F32)],
        compiler_params=pltpu.CompilerParams(dimension_semantics=("arbitrary",), vmem_limit_bytes=VMEM_LIMIT_RESIDENT),
        name="moe_gather")(b_lo, b_hi, x_bf, pos, gsplit)


def _moe_group_ffn_kernel(te_ref, nv_ref, x_ref, gs_ref, wg_ref, wu_ref, wd_ref, o_ref, acc_sc):
    j = pl.program_id(0)
    f = pl.program_id(1)

    @pl.when(f == 0)
    def _():
        acc_sc[...] = jnp.zeros(acc_sc.shape, F32)

    @pl.when(j < nv_ref[0])
    def _():
        x = x_ref[...]
        hg = jnp.dot(x, wg_ref[0], preferred_element_type=F32)
        hu = jnp.dot(x, wu_ref[0], preferred_element_type=F32)
        h = (jax.nn.silu(hg) * hu).astype(BF16)
        acc_sc[...] += jnp.dot(h, wd_ref[0], preferred_element_type=F32)

    @pl.when(f == pl.num_programs(1) - 1)
    def _():
        gs = gs_ref[...]
        lane = lax.broadcasted_iota(I32, gs.shape, 1)
        gate = jnp.sum(jnp.where(lane < 6, gs, 0.0), axis=-1, keepdims=True)
        o_ref[...] = acc_sc[...] * gate


def _ff_tile(dff, cap):
    return max(t for t in range(LANES, min(dff, cap) + 1, LANES) if dff % t == 0)


def moe_group_ffn(tile_expert, n_valid, x_sorted, gterms, wg, wu, wd, n_out_tiles, tf_cap=1792):
    tile = MOE_TILE
    n_tiles = x_sorted.shape[0] // tile
    d = x_sorted.shape[1]
    dff = wg.shape[2]
    tf = _ff_tile(dff, tf_cap)
    src = lambda j: jnp.minimum(j, n_tiles - 1)
    grid_spec = pltpu.PrefetchScalarGridSpec(
        num_scalar_prefetch=2, grid=(n_out_tiles, dff // tf),
        in_specs=[pl.BlockSpec((tile, d), lambda j, f, te, nv: (src(j), 0)),
                  pl.BlockSpec((tile, LANES), lambda j, f, te, nv: (src(j), 0)),
                  pl.BlockSpec((1, d, tf), lambda j, f, te, nv: (te[src(j)], 0, f)),
                  pl.BlockSpec((1, d, tf), lambda j, f, te, nv: (te[src(j)], 0, f)),
                  pl.BlockSpec((1, tf, d), lambda j, f, te, nv: (te[src(j)], f, 0))],
        out_specs=pl.BlockSpec((tile, d), lambda j, f, te, nv: (j, 0)),
        scratch_shapes=[pltpu.VMEM((tile, d), F32)])
    return pl.pallas_call(_moe_group_ffn_kernel, grid_spec=grid_spec,
                          out_shape=jax.ShapeDtypeStruct((n_out_tiles * tile, d), F32),
                          compiler_params=_cp("arbitrary", "arbitrary"), name="moe_group_ffn")(
                              tile_expert, n_valid, x_sorted, gterms, wg, wu, wd)


def _moe_combine_kernel(rlo_ref, short_ref, route_ref, r_ref, os_hbm, g_ref, b_ref, o_ref, obf_ref, buf, sem, acc_sc,
                        *, n_experts, windows, alpha):
    i = pl.program_id(0)
    acc_sc[...] = alpha * r_ref[...]
    route = route_ref[...]
    e1 = route[:, 0:1].astype(I32)
    e2 = route[:, 1:2].astype(I32)
    p1 = route[:, 2:3].astype(I32)
    p2 = route[:, 3:4].astype(I32)

    def accumulate(window):
        def window_copy(e, slot):
            lo = pl.multiple_of((rlo_ref[i * n_experts + e] // SUBLANES) * SUBLANES, SUBLANES)
            return lo, pltpu.make_async_copy(os_hbm.at[pl.ds(lo, window), :], buf.at[slot, pl.ds(0, window), :],
                                             sem.at[slot])

        window_copy(0, 0)[1].start()
        col = lax.broadcasted_iota(I32, (route.shape[0], window), 1)
        for e in range(n_experts):
            slot = e % 2
            if e + 1 < n_experts:
                window_copy(e + 1, 1 - slot)[1].start()
            lo, cp = window_copy(e, slot)
            cp.wait()
            c1 = jnp.where(e1 == e, p1 - lo, -1)
            c2 = jnp.where(e2 == e, p2 - lo, -1)
            pick = jnp.where(col == c1, 1.0, jnp.where(col == c2, 1.0, 0.0)).astype(BF16)
            rows = buf[slot, pl.ds(0, window), :]
            hi = rows.astype(BF16)
            lo_part = (rows - hi.astype(F32)).astype(BF16)
            acc_sc[...] += (jnp.dot(pick, hi, preferred_element_type=F32)
                            + jnp.dot(pick, lo_part, preferred_element_type=F32))

    short, full = windows

    @pl.when(short_ref[i] == 1)
    def _():
        accumulate(short)

    @pl.when(short_ref[i] != 1)
    def _():
        accumulate(full)

    out = _layernorm_rows(acc_sc[...], g_ref[...], b_ref[...])
    o_ref[...] = out
    obf_ref[...] = out.astype(BF16)


def moe_combine_ln(row_lo, short, route_pos, resid, out_sorted, g, b, alpha, n_experts, windows):
    m, d = resid.shape
    tile = MOE_TILE
    grid_spec = pltpu.PrefetchScalarGridSpec(
        num_scalar_prefetch=2, grid=(m // tile,),
        in_specs=[pl.BlockSpec((tile, LANES), lambda i, rlo, sh: (i, 0)),
                  pl.BlockSpec((tile, d), lambda i, rlo, sh: (i, 0)),
                  pl.BlockSpec(memory_space=pl.ANY),
                  pl.BlockSpec((1, d), lambda i, rlo, sh: (0, 0)), pl.BlockSpec((1, d), lambda i, rlo, sh: (0, 0))],
        out_specs=[pl.BlockSpec((tile, d), lambda i, rlo, sh: (i, 0))] * 2,
        scratch_shapes=[pltpu.VMEM((2, max(windows), d), F32), pltpu.SemaphoreType.DMA((2,)),
                        pltpu.VMEM((tile, d), F32)])
    return pl.pallas_call(
        functools.partial(_moe_combine_kernel, n_experts=n_experts, windows=windows, alpha=alpha),
        grid_spec=grid_spec,
        out_shape=[jax.ShapeDtypeStruct((m, d), F32), jax.ShapeDtypeStruct((m, d), BF16)],
        compiler_params=_cp("arbitrary"), name="moe_combine_ln")(
            row_lo, short, route_pos, resid, out_sorted, g.reshape(1, d), b.reshape(1, d))


def moe_top2_ln(x_bf, resid, w_router, wg, wu, wd, g, b, alpha):
    m, d = resid.shape
    ne = wg.shape[0]
    tile = MOE_TILE
    nb = m // tile
    n_tiles = (TOP_K * m) // tile + ne
    tail_tiles = -(-(tile + COMBINE_SLACK) // tile)
    route, before, count = router_plan(x_bf, w_router, ne)

    eid = jnp.arange(ne, dtype=I32)
    lookup = lambda table, idx: jnp.sum(jnp.where(idx[..., None] == eid, table, 0), axis=-1)
    cnt = count[0, :ne].astype(I32)
    tiles_e = (cnt + tile - 1) // tile
    first_tile = jnp.cumsum(tiles_e) - tiles_e
    n_valid = jnp.sum(tiles_e)
    row_start = first_tile * tile
    tid = jnp.arange(n_tiles, dtype=I32)
    tile_expert = jnp.minimum(jnp.sum((tid[:, None] >= (first_tile + tiles_e)[None, :]).astype(I32), axis=1), ne - 1)
    rank0 = (tid - lookup(first_tile, tile_expert)) * tile
    bef = before[:, 0, :ne].astype(I32)
    aft = jnp.concatenate([bef[1:], cnt[None, :]], axis=0)
    bef_t = lookup(bef[:, None, :], tile_expert[None, :])
    aft_t = lookup(aft[:, None, :], tile_expert[None, :])
    valid_t = tid < n_valid
    b_lo = jnp.where(valid_t, jnp.sum((aft_t <= rank0[None, :]).astype(I32), axis=0), 0)
    b_hi = jnp.where(valid_t, jnp.sum((bef_t < (rank0 + tile)[None, :]).astype(I32), axis=0) - 1, -1)
    e1, e2 = route[:, 0].astype(I32), route[:, 1].astype(I32)
    pos1 = lookup(row_start, e1) + route[:, 4].astype(I32)
    pos2 = lookup(row_start, e2) + route[:, 5].astype(I32)
    pos = jnp.stack([pos1.reshape(nb, tile), pos2.reshape(nb, tile)]
                    + [jnp.zeros((nb, tile), I32)] * (SUBLANES - 2), axis=1)

    def three_terms(v):
        hi = v.astype(BF16)
        mid = (v - hi.astype(F32)).astype(BF16)
        low = (v - hi.astype(F32) - mid.astype(F32)).astype(BF16)
        return [hi, mid, low]

    gsplit = jnp.stack(three_terms(route[:, 2]) + three_terms(route[:, 3]), axis=1)
    gsplit = jnp.pad(gsplit, ((0, 0), (0, LANES - gsplit.shape[1])))
    route_pos = jnp.stack([route[:, 0], route[:, 1], pos1.astype(F32), pos2.astype(F32)], axis=1)
    route_pos = jnp.pad(route_pos, ((0, 0), (0, LANES - route_pos.shape[1])))
    row_lo = row_start[None, :] + bef
    windows = (tile // 2, tile + COMBINE_SLACK)
    short = jnp.all(row_lo % SUBLANES + (aft - bef) <= windows[0], axis=1).astype(I32)

    x_sorted, gterms = moe_gather(x_bf, pos, gsplit, b_lo, b_hi, n_tiles)
    out_sorted = moe_group_ffn(tile_expert, n_valid.reshape(1), x_sorted, gterms, wg, wu, wd, n_tiles + tail_tiles)
    return moe_combine_ln(row_lo.reshape(-1), short, route_pos, resid, out_sorted, g, b, alpha, ne, windows)


NEG_INF_CODE = INT_MIN + 0x7FFFFF


def _code_to_f32(code):
    return pltpu.bitcast(jnp.where(code < 0, code ^ jnp.int32(0x7FFFFFFF), code), F32)


def _threshold_search(fold, topk, shape):
    kk = jnp.int32(topk)
    count = lambda pred: fold(lambda s, i: jnp.where(pred(s), 1, 0), "sum")
    c0 = count(lambda s: s >= 0.0)
    code = jnp.where(c0 >= kk, jnp.int32(0), jnp.int32(INT_MIN)) + jnp.zeros(shape, I32)

    def bit_body(j, code):
        cand = code | (jnp.int32(1) << (30 - j))
        cand_f = _code_to_f32(cand)
        return jnp.where(count(lambda s: s >= cand_f) >= kk, cand, code)

    code = lax.fori_loop(0, 31, bit_body, code)
    enough = code > NEG_INF_CODE
    floor_t = _code_to_f32(code)
    t = jnp.where(enough, fold(lambda s, i: jnp.where(s >= floor_t, s, jnp.inf), "min"), -jnp.inf)
    need = kk - count(lambda s: s > t)
    return t, jnp.where(enough, need, 0)


def _topk_threshold(fold, topk, n_idx_bits, rows):
    t, need = _threshold_search(fold, topk, (rows, 1))

    def tie_body(j, p):
        cand = p | (jnp.int32(1) << (n_idx_bits - 1 - j))
        got = fold(lambda s, i: jnp.where(s == t, jnp.where(i < cand, 1, 0), 0), "sum")
        return jnp.where(got <= need, cand, p)

    p = lax.fori_loop(0, n_idx_bits, tie_body, jnp.zeros((rows, 1), I32))
    return t, jnp.where(need > 0, p, 0)


CHUNK_GROUP = 4
ATT_CHUNKS = 4


def _dsa_prompt_kernel(qt_ref, k_ref, vt_ref, qit_ref, ki_ref, wit_ref, bias_ref, ltri_ref, o_ref,
                       score_sc, m_sc, l_sc, acc_sc, *, topk):
    i = pl.program_id(1)
    n_groups = (i + CHUNK_GROUP) // CHUNK_GROUP
    krow = lax.broadcasted_iota(I32, (TQ, TQ), 0)
    qcol = lax.broadcasted_iota(I32, (TQ, TQ), 1)
    qit = qit_ref[0, 0] * jnp.asarray(IDX_DIM ** -0.5, BF16)
    qit_heads = jnp.concatenate([qit[h * IDX_DIM:(h + 1) * IDX_DIM, :] for h in range(IDX_HEADS)], axis=1)
    wt = wit_ref[0, 0] * (IDX_HEADS ** -0.5)

    def score_body(g, carry):
        base = pl.multiple_of(g * (CHUNK_GROUP * TQ), CHUNK_GROUP * TQ)
        for j in range(CHUNK_GROUP):
            off = base + j * TQ
            s_heads = jnp.dot(ki_ref[0, pl.ds(off, TQ), :], qit_heads, preferred_element_type=F32)
            sc = jnp.zeros((TQ, TQ), F32)
            for h in range(IDX_HEADS):
                sc = sc + wt[h:h + 1, :] * jnp.maximum(s_heads[:, h * TQ:(h + 1) * TQ], 0.0)
            score_sc[pl.ds(off, TQ), :] = jnp.where(off + krow <= i * TQ + qcol, sc, -jnp.inf)
        return carry

    lax.fori_loop(0, n_groups, score_body, 0)

    def fold(fn, kind):
        init, join, finish = {"sum": (0, jnp.add, jnp.sum), "min": (jnp.inf, jnp.minimum, jnp.min)}[kind]

        def body(g, acc):
            base = pl.multiple_of(g * (CHUNK_GROUP * TQ), CHUNK_GROUP * TQ)
            for j in range(CHUNK_GROUP):
                acc = join(acc, fn(score_sc[pl.ds(base + j * TQ, TQ), :], None))
            return acc

        acc = lax.fori_loop(0, n_groups, body, jnp.full((TQ, TQ), init, I32 if kind == "sum" else F32))
        return finish(acc, axis=0, keepdims=True)

    t, need = _threshold_search(fold, topk, (1, TQ))
    need = need.astype(F32)

    m_sc[...] = jnp.full(m_sc.shape, NEG, F32)
    l_sc[...] = jnp.zeros(l_sc.shape, F32)
    acc_sc[...] = jnp.zeros(acc_sc.shape, F32)
    qt = qt_ref[0, 0] * jnp.asarray(A_HEAD_DIM ** -0.5, BF16)

    def att_body(pair, ties_before):
        chunks, sels, kcs = [], [], []
        for j in range(ATT_CHUNKS):
            c = pair * ATT_CHUNKS + j
            off = pl.multiple_of(c * TQ, TQ)
            key = score_sc[pl.ds(off, TQ), :]
            tie = key == t
            rank = ties_before + jnp.dot(ltri_ref[...], jnp.where(tie, 1.0, 0.0).astype(BF16),
                                         preferred_element_type=F32)
            ties_before = rank[TQ - 1:TQ, :]
            sels.append(jnp.where(key > t, 1, jnp.where(tie, jnp.where(rank <= need, 1, 0), 0)) > 0)
            kcs.append(k_ref[0, pl.ds(off, TQ), :])
            chunks.append(c)
        for h in range(A_HEADS):
            sl = slice(h * A_HEAD_DIM, (h + 1) * A_HEAD_DIM)
            ss = []
            for j in range(ATT_CHUNKS):
                s = jnp.dot(kcs[j][:, sl], qt[sl, :], preferred_element_type=F32)
                bias = bias_ref[jnp.clip(i - chunks[j], 0, 2), h]
                ss.append(jnp.where(sels[j], s + bias, NEG))
            m_prev = m_sc[h, 0:1, :]
            m_new = jnp.maximum(m_prev, jnp.max(functools.reduce(jnp.maximum, ss), axis=0, keepdims=True))
            alpha = jnp.exp(m_prev - m_new)
            pes = [jnp.exp(s - m_new) for s in ss]
            l_sc[h, 0:1, :] = alpha * l_sc[h, 0:1, :] + jnp.sum(functools.reduce(jnp.add, pes), axis=0, keepdims=True)
            pv = functools.reduce(jnp.add, [jnp.dot(vt_ref[0, chunks[j], sl, :], pes[j].astype(BF16),
                                                    preferred_element_type=F32) for j in range(ATT_CHUNKS)])
            acc_sc[sl, :] = alpha * acc_sc[sl, :] + pv
            m_sc[h, 0:1, :] = m_new
        return ties_before

    lax.fori_loop(0, (i + ATT_CHUNKS) // ATT_CHUNKS, att_body, jnp.zeros((1, TQ), F32))
    for h in range(A_HEADS):
        sl = slice(h * A_HEAD_DIM, (h + 1) * A_HEAD_DIM)
        acc_sc[sl, :] = acc_sc[sl, :] / l_sc[h, 0:1, :]
    o_ref[0] = acc_sc[...].T.astype(o_ref.dtype)


def dsa_prompt(qt, k, vt, qit, ki, wit, bias_tab):
    bsz, t, _ = k.shape
    assert TQ == LANES and t % (CHUNK_GROUP * TQ) == 0
    topk = min(TOPK_MAX, t // 4)
    nq = t // TQ
    ltri = (lax.broadcasted_iota(I32, (TQ, TQ), 1) <= lax.broadcasted_iota(I32, (TQ, TQ), 0)).astype(BF16)
    per_q = lambda rows: pl.BlockSpec((1, 1, rows, TQ), lambda b, i: (b, i, 0, 0))
    per_b = lambda w: pl.BlockSpec((1, t, w), lambda b, i: (b, 0, 0))
    stat = pltpu.VMEM((A_HEADS, SUBLANES, TQ), F32)
    return pl.pallas_call(
        functools.partial(_dsa_prompt_kernel, topk=topk), grid=(bsz, nq),
        in_specs=[per_q(A_WIDTH), per_b(A_WIDTH), pl.BlockSpec((1, nq, A_WIDTH, TQ), lambda b, i: (b, 0, 0, 0)),
                  per_q(IDX_HEADS * IDX_DIM), per_b(IDX_DIM), per_q(SUBLANES), _full(bias_tab.shape),
                  _full((TQ, TQ))],
        out_specs=pl.BlockSpec((1, TQ, A_WIDTH), lambda b, i: (b, i, 0)),
        out_shape=jax.ShapeDtypeStruct((bsz, t, A_WIDTH), BF16),
        scratch_shapes=[pltpu.VMEM((t, TQ), F32), stat, stat, pltpu.VMEM((A_WIDTH, TQ), F32)],
        compiler_params=_cp("parallel", "arbitrary"), name="dsa_prompt")(qt, k, vt, qit, ki, wit, bias_tab, ltri)


def _row_to_col(row):
    n = row.shape[1]
    eye = lax.broadcasted_iota(I32, (n, n), 0) == lax.broadcasted_iota(I32, (n, n), 1)
    return jnp.sum(jnp.where(eye, jnp.broadcast_to(row, (n, n)), 0.0), axis=1, keepdims=True)


def _col_to_row(col):
    n = col.shape[0]
    eye = lax.broadcasted_iota(I32, (n, n), 0) == lax.broadcasted_iota(I32, (n, n), 1)
    return jnp.sum(jnp.where(eye, jnp.broadcast_to(col, (n, n)), 0.0), axis=0, keepdims=True)


def _dsa_s_score_kernel(pt_ref, qi_ref, w_ref, *refs, n_pages):
    page_refs, o_ref = refs[:n_pages], refs[n_pages]
    w = _as_mxu_operand(w_ref[0] * (IDX_HEADS ** -0.5))
    qcols = [_row_to_col(_as_mxu_operand(qi_ref[0, h:h + 1, :] * (IDX_DIM ** -0.5))) for h in range(IDX_HEADS)]
    for p in range(n_pages):
        page = _as_mxu_operand(page_refs[p][0])
        sc = jnp.zeros((1, PAGE_SIZE), F32)
        for h in range(IDX_HEADS):
            s = jnp.sum(page * qcols[h], axis=0, keepdims=True)
            sc = sc + w[:, h:h + 1] * _as_mxu_operand(jnp.maximum(s, 0.0))
        o_ref[0, :, p * PAGE_SIZE:(p + 1) * PAGE_SIZE] = sc


def dsa_sample_scores(page_table, qi8, wi, kidx_pool_t):
    s, n_pages = page_table.shape
    pt = page_table.reshape(-1)
    page_spec = lambda p: pl.BlockSpec((1, IDX_DIM, PAGE_SIZE), lambda i, pt_ref: (pt_ref[i * n_pages + p], 0, 0))
    grid_spec = pltpu.PrefetchScalarGridSpec(
        num_scalar_prefetch=1, grid=(s,),
        in_specs=[pl.BlockSpec((1, SUBLANES, IDX_DIM), lambda i, pt_ref: (i, 0, 0)),
                  pl.BlockSpec((1, 1, LANES), lambda i, pt_ref: (i, 0, 0))] + [page_spec(p) for p in range(n_pages)],
        out_specs=pl.BlockSpec((1, 1, n_pages * PAGE_SIZE), lambda i, pt_ref: (i, 0, 0)))
    return pl.pallas_call(functools.partial(_dsa_s_score_kernel, n_pages=n_pages), grid_spec=grid_spec,
                          out_shape=jax.ShapeDtypeStruct((s, 1, n_pages * PAGE_SIZE), F32),
                          compiler_params=_cp("arbitrary"), name="dsa_sample_scores")(
                              pt, qi8, wi, *([kidx_pool_t] * n_pages))


def _dsa_s_select_kernel(sc_ref, qi_ref, kn_ref, wi_ref, mask_ref, mnew_ref, *, topk, n_idx_bits):
    s, lp = mask_ref.shape
    w = _as_mxu_operand(wi_ref[...] * (IDX_HEADS ** -0.5))
    sc_new = jnp.zeros((s, 1), F32)
    kn = _as_mxu_operand(kn_ref[...])
    for h in range(IDX_HEADS):
        s_new = jnp.sum(_as_mxu_operand(qi_ref[:, h * IDX_DIM:(h + 1) * IDX_DIM]) * kn, axis=-1, keepdims=True)
        sc_new = sc_new + w[:, h:h + 1] * _as_mxu_operand(jnp.maximum(s_new * (IDX_DIM ** -0.5), 0.0))
    col = lax.broadcasted_iota(I32, (s, lp), 1)

    def fold(fn, kind):
        join, finish = {"sum": (jnp.add, jnp.sum), "min": (jnp.minimum, jnp.min)}[kind]
        main = finish(fn(sc_ref[...], col), axis=1, keepdims=True)
        return join(main, fn(sc_new, jnp.int32(lp)))

    t, p = _topk_threshold(fold, topk, n_idx_bits, s)
    sc = sc_ref[...]
    mask_ref[...] = jnp.where(sc > t, 1.0, jnp.where(sc == t, jnp.where(col < p, 1.0, 0.0), 0.0))
    sel_new = jnp.where(sc_new > t, 1.0, jnp.where(sc_new == t, jnp.where(lp < p, 1.0, 0.0), 0.0))
    mnew_ref[...] = jnp.broadcast_to(sel_new, mnew_ref.shape)


def dsa_sample_select(raw, qi, ki_new, wi):
    s, lp = raw.shape
    length = lp + 1
    topk = min(TOPK_MAX, length // 4)
    kern = functools.partial(_dsa_s_select_kernel, topk=topk, n_idx_bits=length.bit_length())
    return pl.pallas_call(kern, out_shape=[jax.ShapeDtypeStruct((s, lp), F32), jax.ShapeDtypeStruct((s, LANES), F32)],
                          compiler_params=pltpu.CompilerParams(vmem_limit_bytes=VMEM_LIMIT),
                          name="dsa_sample_select")(raw, qi, ki_new, wi)


def _dsa_s_attn_kernel(pt_ref, q_ref, kn_ref, vn_ref, mask_ref, mnew_ref, bias_ref, bnew_ref, *refs, n_pages):
    k_refs, v_refs, o_ref = refs[:n_pages], refs[n_pages:2 * n_pages], refs[2 * n_pages]
    hd = A_HEAD_DIM
    q = _as_mxu_operand(q_ref[0] * (hd ** -0.5))
    qcols = [_row_to_col(q[h:h + 1, :]) for h in range(A_HEADS)]
    hrow = lax.broadcasted_iota(I32, (A_HEADS, PAGE_SIZE), 0)
    lg_new = jnp.sum(q * _as_mxu_operand(kn_ref[0]), axis=1, keepdims=True) + bnew_ref[:, 0:1]
    lg_new = jnp.where(mnew_ref[0][:, 0:1] > 0.0, lg_new, NEG)
    logits = []
    mx = lg_new
    for p in range(n_pages):
        lg = jnp.zeros((A_HEADS, PAGE_SIZE), F32)
        for h in range(A_HEADS):
            kp = _as_mxu_operand(k_refs[p][0, h * hd:(h + 1) * hd, :])
            s = jnp.sum(kp * qcols[h], axis=0, keepdims=True)
            lg = jnp.where(hrow == h, jnp.broadcast_to(s, lg.shape), lg)
        lg = jnp.where(mask_ref[0, p:p + 1, :] > 0.0, lg + bias_ref[p], NEG)
        logits.append(lg)
        mx = jnp.maximum(mx, jnp.max(lg, axis=1, keepdims=True))
    pe_new = jnp.exp(lg_new - mx)
    pes = [jnp.exp(lg - mx) for lg in logits]
    den = pe_new
    for pe in pes:
        den = den + jnp.sum(pe, axis=1, keepdims=True)
    probs = [_as_mxu_operand(pe / den) for pe in pes]
    p_new = _as_mxu_operand(pe_new / den)
    for h in range(A_HEADS):
        acc = jnp.zeros((hd, PAGE_SIZE), F32)
        for p in range(n_pages):
            acc = acc + probs[p][h:h + 1, :] * _as_mxu_operand(v_refs[p][0, h * hd:(h + 1) * hd, :])
        o_ref[0, h:h + 1, :] = (_col_to_row(jnp.sum(acc, axis=1, keepdims=True))
                                + p_new[h:h + 1, :] * _as_mxu_operand(vn_ref[0, h:h + 1, :]))


def dsa_sample_attn(page_table, q, k_new, v_new, mask, mask_new, bias_pages, bias_new, k_pool_t, v_pool_t):
    s, n_pages = page_table.shape
    pt = page_table.reshape(-1)
    per_s = lambda r, w: pl.BlockSpec((1, r, w), lambda i, pt_ref: (i, 0, 0))
    const = lambda shape: pl.BlockSpec(shape, lambda i, pt_ref: (0,) * len(shape))
    page_spec = lambda p: pl.BlockSpec((1, A_WIDTH, PAGE_SIZE), lambda i, pt_ref: (pt_ref[i * n_pages + p], 0, 0))
    hd = (A_HEADS, A_HEAD_DIM)
    grid_spec = pltpu.PrefetchScalarGridSpec(
        num_scalar_prefetch=1, grid=(s,),
        in_specs=[per_s(*hd), per_s(*hd), per_s(*hd), per_s(n_pages, PAGE_SIZE), per_s(1, LANES),
                  const(bias_pages.shape), const(bias_new.shape)]
                 + [page_spec(p) for p in range(n_pages)] * 2,
        out_specs=per_s(*hd))
    return pl.pallas_call(functools.partial(_dsa_s_attn_kernel, n_pages=n_pages), grid_spec=grid_spec,
                          out_shape=jax.ShapeDtypeStruct((s,) + hd, F32),
                          compiler_params=_cp("arbitrary"), name="dsa_sample_attn")(
                              pt, q, k_new, v_new, mask, mask_new, bias_pages, bias_new,
                              *([k_pool_t] * n_pages), *([v_pool_t] * n_pages))


def _rglru_gates(xc, wa, ba, wi, bi, lam):
    xb = xc.astype(wa.dtype)
    r = jax.nn.sigmoid(_mm(xb, wa) + ba)
    ig = jax.nn.sigmoid(_mm(xb, wi) + bi)
    log_a = -RG_C * r * jax.nn.softplus(-lam)
    a = jnp.exp(log_a)
    u = jnp.sqrt(1.0 - jnp.exp(2.0 * log_a)) * (ig * xc)
    return a, u


def _rglru_prompt_kernel(xr_ref, gr_ref, cw_ref, cb_ref, wa_ref, ba_ref, wi_ref, bi_ref, lam_ref,
                         y_ref, hl_ref, xs_sc, h_sc, *, tb):
    t = pl.program_id(1)
    pad = SUBLANES

    @pl.when(t == 0)
    def _():
        xs_sc[0:pad, :] = jnp.zeros((pad, xs_sc.shape[1]), F32)
        h_sc[...] = jnp.zeros(h_sc.shape, F32)

    xs_sc[pad:pad + tb, :] = xr_ref[0]
    xc = cb_ref[...] + cw_ref[RG_CONV - 1:RG_CONV, :] * xs_sc[pad:pad + tb, :]
    for j in range(1, RG_CONV):
        xc = xc + cw_ref[RG_CONV - 1 - j:RG_CONV - j, :] * xs_sc[pad - j:pad - j + tb, :]
    a, u = _rglru_gates(xc, wa_ref[...], ba_ref[...], wi_ref[...], bi_ref[...], lam_ref[...])
    rowi = lax.broadcasted_iota(I32, a.shape, 0)
    s = 1
    while s < tb:
        keep = rowi >= s
        a_sh = jnp.where(keep, pltpu.roll(a, s, 0), 1.0)
        u_sh = jnp.where(keep, pltpu.roll(u, s, 0), 0.0)
        u = a * u_sh + u
        a = a * a_sh
        s *= 2
    h = a * h_sc[0:1, :] + u
    y_ref[0] = (h * jax.nn.gelu(gr_ref[0])).astype(y_ref.dtype)
    h_sc[0:1, :] = h[tb - 1:tb, :]
    xs_sc[0:pad, :] = xs_sc[tb:tb + pad, :]

    @pl.when(t == pl.num_programs(1) - 1)
    def _():
        hl_ref[0] = h[tb - 1:tb, :]


def rglru_prompt(xr, gr, cw, cb, wa, ba, wi, bi, lam, tb_cap=256):
    bsz, t, w = xr.shape
    tb = _row_tile(t, tb_cap)
    blk = pl.BlockSpec((1, tb, w), lambda b, i: (b, i, 0))
    vec = _full((1, w))
    return pl.pallas_call(
        functools.partial(_rglru_prompt_kernel, tb=tb), grid=(bsz, t // tb),
        in_specs=[blk, blk, _full((RG_CONV, w)), vec, _full((w, w)), vec, _full((w, w)), vec, vec],
        out_specs=[blk, pl.BlockSpec((1, 1, w), lambda b, i: (b, 0, 0))],
        out_shape=[jax.ShapeDtypeStruct((bsz, t, w), BF16), jax.ShapeDtypeStruct((bsz, 1, w), F32)],
        scratch_shapes=[pltpu.VMEM((tb + SUBLANES, w), F32), pltpu.VMEM((SUBLANES, w), F32)],
        compiler_params=_cp("parallel", "arbitrary"), name="rglru_prompt")(
            xr, gr, cw, cb.reshape(1, w), wa, ba.reshape(1, w), wi, bi.reshape(1, w), lam.reshape(1, w))


def _rglru_step_kernel(xr_ref, gr_ref, b0_ref, b1_ref, b2_ref, h0_ref, cw_ref, cb_ref, wa_ref, ba_ref, wi_ref,
                       bi_ref, lam_ref, y_ref, h_ref):
    xc = (cb_ref[...] + cw_ref[0:1, :] * b0_ref[...] + cw_ref[1:2, :] * b1_ref[...]
          + cw_ref[2:3, :] * b2_ref[...] + cw_ref[3:4, :] * xr_ref[...])
    a, u = _rglru_gates(xc, wa_ref[...], ba_ref[...], wi_ref[...], bi_ref[...], lam_ref[...])
    h = a * h0_ref[...] + u
    h_ref[...] = h
    y_ref[...] = (h * jax.nn.gelu(gr_ref[...])).astype(y_ref.dtype)


def rglru_step(xr, gr, buf, h0, cw, cb, wa, ba, wi, bi, lam):
    s, w = xr.shape
    return pl.pallas_call(
        _rglru_step_kernel,
        out_shape=[jax.ShapeDtypeStruct((s, w), F32), jax.ShapeDtypeStruct((s, w), F32)],
        compiler_params=pltpu.CompilerParams(vmem_limit_bytes=VMEM_LIMIT), name="rglru_step")(
            xr, gr, buf[:, 0], buf[:, 1], buf[:, 2], h0, cw, cb.reshape(1, w), wa, ba.reshape(1, w),
            wi, bi.reshape(1, w), lam.reshape(1, w))


def _headnorm_gate(h, g, og):
    mu = jnp.mean(h, axis=-1, keepdims=True)
    hc = h - mu
    var = jnp.mean(hc * hc, axis=-1, keepdims=True)
    return hc * lax.rsqrt(var + LN_EPS) * g * jax.nn.sigmoid(og)


def _mlstm_prompt_kernel(q_ref, k_ref, v_ref, og_ref, gt_ref, bg_ref, ng_ref, tri_ref,
                         y_ref, c_ref, n_ref, m_ref, c_sc, n_sc, m_sc, *, cl, dh):
    ci = pl.program_id(1)

    @pl.when(ci == 0)
    def _():
        c_sc[...] = jnp.zeros(c_sc.shape, F32)
        n_sc[...] = jnp.zeros(n_sc.shape, F32)
        m_sc[...] = jnp.zeros(m_sc.shape, F32)

    pre = gt_ref[0] + bg_ref[...]
    lane = lax.broadcasted_iota(I32, pre.shape, 1)
    gv = jnp.where(lane < ML_HEADS, pre, jax.nn.log_sigmoid(pre))
    csum = jnp.dot(tri_ref[...], gv, preferred_element_type=F32, precision=HI)
    gv_t = gv.T
    csum_t = csum.T
    row = lax.broadcasted_iota(I32, (cl, cl), 0)
    col = lax.broadcasted_iota(I32, (cl, cl), 1)
    causal = col <= row
    kscale = dh ** -0.5
    for h in range(ML_HEADS):
        sl = slice(h * dh, (h + 1) * dh)
        qh = q_ref[0, :, sl]
        kh = k_ref[0, :, sl] * jnp.asarray(kscale, BF16)
        vh = v_ref[0, :, sl]
        b_col = csum[:, ML_HEADS + h:ML_HEADS + h + 1]
        i_col = gv[:, h:h + 1]
        b_row = csum_t[ML_HEADS + h:ML_HEADS + h + 1, :]
        i_row = gv_t[h:h + 1, :]
        m_prev = m_sc[h, 0:1, 0:1]
        c_prev = c_sc[h]
        n_prev = n_sc[h, 0:1, :]
        dmat = jnp.where(causal, b_col - b_row + i_row, NEG)
        inter = b_col + m_prev
        m_t = jnp.maximum(inter, jnp.max(dmat, axis=1, keepdims=True))
        g_inter = jnp.exp(inter - m_t)
        qk = lax.dot_general(qh, kh, NT, preferred_element_type=F32)
        ws = jnp.exp(dmat - m_t) * qk
        num = (g_inter * lax.dot_general(qh, c_prev.astype(BF16), NT, preferred_element_type=F32)
               + jnp.dot(ws.astype(BF16), vh, preferred_element_type=F32))
        qf = qh.astype(F32)
        den = g_inter * jnp.sum(qf * n_prev, axis=1, keepdims=True) + jnp.sum(ws, axis=1, keepdims=True)
        hout = num / jnp.maximum(jnp.abs(den), jnp.exp(-m_t))
        m_new = m_t[cl - 1:cl, :]
        b_last = b_col[cl - 1:cl, :]
        g_c = jnp.exp(b_last + m_prev - m_new)
        w_end = jnp.exp(b_last - b_col + i_col - m_new)
        kf = kh.astype(F32)
        vw = (vh.astype(F32) * w_end).astype(BF16)
        c_sc[h] = g_c * c_prev + lax.dot_general(vw, kh, TN, preferred_element_type=F32)
        n_sc[h, 0:1, :] = g_c * n_prev + jnp.sum(kf * w_end, axis=0, keepdims=True)
        m_sc[h, 0:1, :] = jnp.broadcast_to(m_new, (1, m_sc.shape[2]))
        y_ref[0, :, sl] = _headnorm_gate(hout, ng_ref[:, sl], og_ref[0, :, sl]).astype(y_ref.dtype)

    @pl.when(ci == pl.num_programs(1) - 1)
    def _():
        c_ref[0] = c_sc[...]
        n_ref[0] = jnp.zeros(n_ref.shape[1:], F32)
        m_ref[0] = jnp.zeros(m_ref.shape[1:], F32)
        for h in range(ML_HEADS):
            n_ref[0, h:h + 1, :] = n_sc[h, 0:1, :]
            m_ref[0, h:h + 1, :] = m_sc[h, 0:1, :]


def mlstm_prompt(q, k, v, og, gates, bgate, norm_g):
    bsz, t, width = q.shape
    dh = width // ML_HEADS
    cl = math.gcd(t, ML_CHUNK)
    tri = (lax.broadcasted_iota(I32, (cl, cl), 1) <= lax.broadcasted_iota(I32, (cl, cl), 0)).astype(F32)
    blk = lambda w: pl.BlockSpec((1, cl, w), lambda b, i: (b, i, 0))
    return pl.pallas_call(
        functools.partial(_mlstm_prompt_kernel, cl=cl, dh=dh), grid=(bsz, t // cl),
        in_specs=[blk(width), blk(width), blk(width), blk(width), blk(LANES), _full((1, LANES)), _full((1, width)),
                  _full((cl, cl))],
        out_specs=[blk(width), pl.BlockSpec((1, ML_HEADS, dh, dh), lambda b, i: (b, 0, 0, 0)),
                   pl.BlockSpec((1, SUBLANES, dh), lambda b, i: (b, 0, 0)),
                   pl.BlockSpec((1, SUBLANES, LANES), lambda b, i: (b, 0, 0))],
        out_shape=[jax.ShapeDtypeStruct((bsz, t, width), BF16), jax.ShapeDtypeStruct((bsz, ML_HEADS, dh, dh), F32),
                   jax.ShapeDtypeStruct((bsz, SUBLANES, dh), F32), jax.ShapeDtypeStruct((bsz, SUBLANES, LANES), F32)],
        scratch_shapes=[pltpu.VMEM((ML_HEADS, dh, dh), F32), pltpu.VMEM((ML_HEADS, SUBLANES, dh), F32),
                        pltpu.VMEM((ML_HEADS, SUBLANES, LANES), F32)],
        compiler_params=_cp("parallel", "arbitrary"), name="mlstm_prompt")(
            q, k, v, og, gates, bgate, norm_g.reshape(1, width), tri)


def _mlstm_step_kernel(q_ref, k_ref, v_ref, og_ref, gt_ref, bg_ref, ng_ref, c0_ref, n0_ref, m0_ref,
                       y_ref, c_ref, n_ref, m_ref, *, sb, dh):
    kscale = dh ** -0.5
    m_ref[...] = jnp.zeros(m_ref.shape, F32)

    def seq_body(s, carry):
        rs = pl.ds(s, 1)
        pre = gt_ref[rs, :] + bg_ref[...]
        for h in range(ML_HEADS):
            sl = slice(h * dh, (h + 1) * dh)
            qh = q_ref[rs, sl]
            kh = k_ref[rs, sl] * kscale
            vh = v_ref[rs, sl]
            i_pre = pre[:, h:h + 1]
            logf = jax.nn.log_sigmoid(pre[:, ML_HEADS + h:ML_HEADS + h + 1])
            m_prev = m0_ref[rs, h:h + 1]
            c_prev = c0_ref[s, h]
            n_prev = n0_ref[s, h:h + 1, :]
            inter = logf + m_prev
            m_t = jnp.maximum(inter, i_pre)
            g_inter = jnp.exp(inter - m_t)
            w_in = jnp.exp(i_pre - m_t)
            q16 = _as_mxu_operand(qh)
            qk = jnp.sum(q16 * _as_mxu_operand(kh), axis=1, keepdims=True)
            q8 = jnp.broadcast_to(qh, (SUBLANES, dh)).astype(BF16)
            cq = lax.dot_general(q8, c_prev.astype(BF16), NT, preferred_element_type=F32)[0:1]
            num = g_inter * cq + (w_in * qk) * vh
            den = g_inter * jnp.sum(_as_mxu_operand(n_prev) * q16, axis=1, keepdims=True) + w_in * qk
            hout = num / jnp.maximum(jnp.abs(den), jnp.exp(-m_t))
            rsel = lax.broadcasted_iota(I32, (SUBLANES, dh), 0) == 0
            v8 = jnp.where(rsel, jnp.broadcast_to(w_in * vh, (SUBLANES, dh)), 0.0)
            k8 = jnp.broadcast_to(kh, (SUBLANES, dh))
            c_ref[s, h] = g_inter * c_prev + lax.dot_general(v8, k8, TN, preferred_element_type=F32, precision=HI)
            n_ref[s, h:h + 1, :] = g_inter * n_prev + w_in * kh
            m_ref[rs, h:h + 1] = m_t
            y_ref[rs, sl] = _headnorm_gate(hout, ng_ref[:, sl], og_ref[rs, sl]).astype(y_ref.dtype)
        return carry

    lax.fori_loop(0, sb, seq_body, 0)


def mlstm_step(q, k, v, og, gates, bgate, norm_g, c0, n0, m0, sb=SUBLANES):
    s, width = q.shape
    dh = width // ML_HEADS
    assert s % sb == 0
    row = lambda w: pl.BlockSpec((sb, w), lambda i: (i, 0))
    m0p = jnp.pad(m0, ((0, 0), (0, LANES - ML_HEADS)))
    outs = pl.pallas_call(
        functools.partial(_mlstm_step_kernel, sb=sb, dh=dh), grid=(s // sb,),
        in_specs=[row(width), row(width), row(width), row(width), row(LANES), _full((1, LANES)), _full((1, width)),
                  pl.BlockSpec((sb, ML_HEADS, dh, dh), lambda i: (i, 0, 0, 0)),
                  pl.BlockSpec((sb, ML_HEADS, dh), lambda i: (i, 0, 0)), row(LANES)],
        out_specs=[row(width), pl.BlockSpec((sb, ML_HEADS, dh, dh), lambda i: (i, 0, 0, 0)),
                   pl.BlockSpec((sb, ML_HEADS, dh), lambda i: (i, 0, 0)), row(LANES)],
        out_shape=[jax.ShapeDtypeStruct((s, width), F32), jax.ShapeDtypeStruct((s, ML_HEADS, dh, dh), F32),
                   jax.ShapeDtypeStruct((s, ML_HEADS, dh), F32), jax.ShapeDtypeStruct((s, LANES), F32)],
        compiler_params=_cp("parallel"), name="mlstm_step")(
            q, k, v, og, gates, bgate, norm_g.reshape(1, width), c0, n0, m0p)
    y, c, n, m = outs
    return y, c, n, m[:, :ML_HEADS]


def _xattn_prompt_kernel(q_ref, k_ref, v_ref, o_ref, *, dh):
    scale = dh ** -0.5
    for h in range(X_HEADS):
        sl = slice(h * dh, (h + 1) * dh)
        lg = lax.dot_general(q_ref[0, :, sl], k_ref[0, :, sl], NT, preferred_element_type=F32) * scale
        mx = jnp.max(lg, axis=-1, keepdims=True)
        e = jnp.exp(lg - mx)
        pr = e / jnp.sum(e, axis=-1, keepdims=True)
        o_ref[0, :, sl] = jnp.dot(pr.astype(BF16), v_ref[0, :, sl], preferred_element_type=F32).astype(o_ref.dtype)


def xattn_prompt(xq, mk, mv, tq_cap=512):
    bsz, t, d = xq.shape
    mt = mk.shape[1]
    tq = _row_tile(t, tq_cap)
    return pl.pallas_call(
        functools.partial(_xattn_prompt_kernel, dh=d // X_HEADS), grid=(bsz, t // tq),
        in_specs=[pl.BlockSpec((1, tq, d), lambda b, i: (b, i, 0)), pl.BlockSpec((1, mt, d), lambda b, i: (b, 0, 0)),
                  pl.BlockSpec((1, mt, d), lambda b, i: (b, 0, 0))],
        out_specs=pl.BlockSpec((1, tq, d), lambda b, i: (b, i, 0)),
        out_shape=jax.ShapeDtypeStruct((bsz, t, d), BF16),
        compiler_params=_cp("parallel", "parallel"), name="xattn_prompt")(xq, mk, mv)


def _xattn_step_kernel(q_ref, k_ref, v_ref, o_ref, *, sb, mt, dh):
    scale = dh ** -0.5
    parts = dh // LANES
    rows = X_HEADS * parts

    def seq_body(s, carry):
        rs = pl.ds(s, 1)
        qrow = _as_mxu_operand(q_ref[rs, :])
        pieces = []
        for h in range(X_HEADS):
            lg = jnp.zeros((mt, 1), F32)
            for c in range(parts):
                qp = qrow[:, h * dh + c * LANES:h * dh + (c + 1) * LANES]
                kp = _as_mxu_operand(k_ref[0, s, pl.ds(c * X_HEADS + h, mt, stride=rows), :])
                lg = lg + jnp.sum(kp * qp, axis=1, keepdims=True)
            lg = lg * scale
            e = jnp.exp(lg - jnp.max(lg, axis=0, keepdims=True))
            pr = _as_mxu_operand(e / jnp.sum(e, axis=0, keepdims=True))
            for c in range(parts):
                vp = _as_mxu_operand(v_ref[0, s, pl.ds(c * X_HEADS + h, mt, stride=rows), :])
                pieces.append(jnp.sum(pr * vp, axis=0, keepdims=True))
        o_ref[rs, :] = jnp.concatenate(pieces, axis=1)
        return carry

    lax.fori_loop(0, sb, seq_body, 0)


def _lane_chunk_view(mem):
    nl, s, mt, nh, dh = mem.shape
    v = mem.reshape(nl, s, mt, nh, dh // LANES, LANES)
    return jnp.transpose(v, (0, 1, 2, 4, 3, 5)).reshape(nl, s, mt * (dh // LANES) * nh, LANES)


def xattn_step(xq, mk_view, mv_view, layer, mt, sb=SUBLANES):
    s, d = xq.shape
    dh = d // X_HEADS
    assert s % sb == 0 and dh % LANES == 0
    nrows = mk_view.shape[2]
    mem_spec = pl.BlockSpec((1, sb, nrows, LANES), lambda i: (layer, i, 0, 0))
    return pl.pallas_call(
        functools.partial(_xattn_step_kernel, sb=sb, mt=mt, dh=dh), grid=(s // sb,),
        in_specs=[pl.BlockSpec((sb, d), lambda i: (i, 0)), mem_spec, mem_spec],
        out_specs=pl.BlockSpec((sb, d), lambda i: (i, 0)),
        out_shape=jax.ShapeDtypeStruct((s, d), F32),
        compiler_params=_cp("parallel"), name="xattn_step")(xq, mk_view, mv_view)


def _t5_bucket(dist):
    max_exact = REL_BUCKETS // 2
    d = jnp.maximum(dist.astype(F32), 1.0)
    large = max_exact + (jnp.log(d / max_exact) / math.log(REL_MAX_DIST / max_exact)
                         * (REL_BUCKETS - max_exact)).astype(I32)
    large = jnp.minimum(large, REL_BUCKETS - 1)
    return jnp.where(dist < max_exact, dist, large)


def _pad_cols(w, n):
    return jnp.pad(w, ((0, 0), (0, n - w.shape[1])))


def _block_diag(w):
    nb, di, do = w.shape
    eye = jnp.eye(nb, dtype=w.dtype)
    return jnp.einsum('nde,nm->ndme', w, eye).reshape(nb * di, nb * do)


def kernel(x_prompt, x_sample, cache_k, cache_v, cache_kidx, state_rglru_h, state_rglru_conv, state_mlstm_c,
           state_mlstm_n, state_mlstm_m, cache_mem_k, cache_mem_v, page_table, mem_prompt, ln_g, ln_b, rel_bias,
           w_in_even, w_out_even, rg_conv_w, rg_conv_b, rg_w_a, rg_b_a, rg_w_i, rg_b_i, rg_lambda, w_in_odd,
           b_gate_odd, ml_norm_g, w_out_odd, w_xq, w_xk, w_xv, w_xo, ffn_w_gate, ffn_w_up, ffn_w_down,
           moe_router, moe_w_gate, moe_w_up, moe_w_down):
    bsz, t, d = x_prompt.shape
    s = x_sample.shape[0]
    depth = ln_g.shape[0]
    n_pages = page_table.shape[1]
    past_len = n_pages * PAGE_SIZE
    mt = mem_prompt.shape[1]
    rg_w = rg_lambda.shape[1]
    ml_w = ml_norm_g.shape[1]
    n_experts = moe_router.shape[2]
    alpha = (2.0 * depth) ** 0.25
    bf = lambda a: a.astype(BF16)

    assert REL_MAX_DIST <= TQ
    rel32 = rel_bias.astype(F32)
    by_dist = rel32[_t5_bucket(jnp.arange(2 * TQ + 1, dtype=I32))]

    def spread(dist):
        out = jnp.broadcast_to(by_dist[2 * TQ], dist.shape + (A_HEADS,))
        for dd in range(2 * TQ):
            out = jnp.where((dist == dd)[..., None], by_dist[dd], out)
        return out

    kk_ = lax.broadcasted_iota(I32, (TQ, TQ), 0)
    qq_ = lax.broadcasted_iota(I32, (TQ, TQ), 1)
    bias_tab = jnp.stack([spread(jnp.maximum(qq_ - kk_, 0)), spread(TQ + qq_ - kk_),
                          spread(jnp.full((TQ, TQ), 2 * TQ, I32))])
    bias_tab = jnp.transpose(bias_tab, (0, 3, 1, 2))
    dist_s = (past_len - jnp.arange(past_len, dtype=I32)).reshape(n_pages, PAGE_SIZE)
    bias_s = jnp.transpose(rel32[_t5_bucket(dist_s)], (0, 2, 1))
    bias_s_new = jnp.broadcast_to(rel32[_t5_bucket(jnp.zeros((1,), I32))].reshape(A_HEADS, 1), (A_HEADS, LANES))
    mem_k_view, mem_v_view = _lane_chunk_view(cache_mem_k), _lane_chunk_view(cache_mem_v)

    mem_bf = bf(mem_prompt.reshape(bsz * mt, d))
    kv_w = [bf(w_xk[l]) for l in range(depth)] + [bf(w_xv[l]) for l in range(depth)]
    kv = multi_proj(mem_bf, kv_w, [(F32, BF16)] * (2 * depth))
    mem_k_f32 = [kv[2 * l] for l in range(depth)]
    mem_k_bf = [kv[2 * l + 1].reshape(bsz, mt, d) for l in range(depth)]
    mem_v_f32 = [kv[2 * depth + 2 * l] for l in range(depth)]
    mem_v_bf = [kv[2 * depth + 2 * l + 1].reshape(bsz, mt, d) for l in range(depth)]

    xp = x_prompt.reshape(bsz * t, d)
    xs = x_sample.reshape(s, d)
    xp_bf, xs_bf = bf(xp), bf(xs)
    outs = {}
    k_p, v_p, ki_p, k_s, v_s, ki_s = [], [], [], [], [], []
    rgh_p, rgc_p, rgh_s, rgc_s = [], [], [], []
    mlc_p, mln_p, mlm_p, mlc_s, mln_s, mlm_s = [], [], [], [], [], []

    for l in range(depth):
        wq, wo = bf(w_xq[l]), bf(w_xo[l])
        if l % 2 == 0:
            e = l // 2
            w_in = w_in_even[e]
            o = 0
            cols = {}
            for name, width in (("q", A_WIDTH), ("k", A_WIDTH), ("v", A_WIDTH), ("qi", IDX_HEADS * IDX_DIM),
                                ("ki", IDX_DIM), ("wi", IDX_HEADS), ("xr", rg_w), ("gr", rg_w)):
                cols[name] = bf(w_in[:, o:o + width])
                o += width
            cols["wi"] = _pad_cols(cols["wi"], LANES)
            names = ("q", "k", "v", "qi", "ki", "wi", "xr", "gr")
            wq_t, wv_t, wqi_t = cols["q"].T, cols["v"].T, cols["qi"].T
            wi_t = jnp.pad(cols["wi"][:, :IDX_HEADS].T, ((0, SUBLANES - IDX_HEADS), (0, 0)))
            wa = bf(_block_diag(rg_w_a[e]))
            wi_g = bf(_block_diag(rg_w_i[e]))
            w_out = bf(w_out_even[e])

            nq = t // TQ
            pnames = ("k", "v", "ki", "xr", "gr")
            kf, kb, vf, kif, kib, xr, gr, qt, vt, qit, wit = multi_proj(
                xp_bf, [cols[n] for n in pnames] + [wq_t, wv_t, wqi_t, wi_t],
                [(F32, BF16), (F32,), (F32, BF16), (F32,), (F32,), (BF16,), (BF16,), (BF16,), (F32,)],
                transposed=(False,) * len(pnames) + (True,) * 4)
            r3 = lambda a: a.reshape(bsz, t, a.shape[-1])
            r4 = lambda a: a.reshape(bsz, nq, a.shape[-2], TQ)
            att = dsa_prompt(r4(qt), r3(kb), r4(vt), r4(qit), r3(kib), r4(wit), bias_tab)
            rg_y, h_last = rglru_prompt(r3(xr), r3(gr), rg_conv_w[e], rg_conv_b[e], wa, rg_b_a[e], wi_g, rg_b_i[e],
                                        rg_lambda[e])
            xp, xp_bf, xq_p = mm_res_ln([att.reshape(bsz * t, A_WIDTH), rg_y.reshape(bsz * t, rg_w)],
                                        [w_out[:A_WIDTH], w_out[A_WIDTH:]], xp, ln_g[l, 0], ln_b[l, 0], alpha,
                                        next_ws=[wq])
            k_p.append(kf.reshape(bsz, t, A_HEADS, A_HEAD_DIM))
            v_p.append(vf.reshape(bsz, t, A_HEADS, A_HEAD_DIM))
            ki_p.append(kif.reshape(bsz, t, IDX_DIM))
            rgh_p.append(h_last.reshape(bsz, rg_w))
            rgc_p.append(r3(xr)[:, t - (RG_CONV - 1):, :])

            q, kf, vf, qi, kif, wi, xr, gr = multi_proj(xs_bf, [cols[n] for n in names], [(F32,)] * len(names))
            qi8 = jnp.pad(qi.reshape(s, IDX_HEADS, IDX_DIM), ((0, 0), (0, SUBLANES - IDX_HEADS), (0, 0)))
            pool = cache_k.shape[1]
            kidx_t = jnp.transpose(cache_kidx[e], (0, 2, 1))
            k_t = jnp.transpose(cache_k[e], (0, 2, 3, 1)).reshape(pool, A_WIDTH, PAGE_SIZE)
            v_t = jnp.transpose(cache_v[e], (0, 2, 3, 1)).reshape(pool, A_WIDTH, PAGE_SIZE)
            raw = dsa_sample_scores(page_table, qi8, wi.reshape(s, 1, LANES), kidx_t)
            mask, mask_new = dsa_sample_select(raw.reshape(s, past_len), qi, kif, wi)
            hd3 = lambda a: a.reshape(s, A_HEADS, A_HEAD_DIM)
            att = dsa_sample_attn(page_table, hd3(q), hd3(kf), hd3(vf), mask.reshape(s, n_pages, PAGE_SIZE),
                                  mask_new.reshape(s, 1, LANES), bias_s, bias_s_new, k_t, v_t)
            rg_y, h_new = rglru_step(xr, gr, state_rglru_conv[e], state_rglru_h[e], rg_conv_w[e], rg_conv_b[e],
                                     wa, rg_b_a[e], wi_g, rg_b_i[e], rg_lambda[e])
            xs, xs_bf = mm_res_ln([bf(att.reshape(s, A_WIDTH)), bf(rg_y)], [w_out[:A_WIDTH], w_out[A_WIDTH:]], xs,
                                  ln_g[l, 0], ln_b[l, 0], alpha)
            k_s.append(kf.reshape(s, 1, A_HEADS, A_HEAD_DIM))
            v_s.append(vf.reshape(s, 1, A_HEADS, A_HEAD_DIM))
            ki_s.append(kif.reshape(s, 1, IDX_DIM))
            rgh_s.append(h_new)
            rgc_s.append(jnp.concatenate([state_rglru_conv[e][:, 1:], xr[:, None, :]], axis=1))
        else:
            o_ = l // 2
            w_in = w_in_odd[o_]
            ws = [bf(w_in[:, i * ml_w:(i + 1) * ml_w]) for i in range(4)]
            ws.append(_pad_cols(bf(w_in[:, 4 * ml_w:]), LANES))
            bgate = jnp.pad(b_gate_odd[o_].astype(F32), (0, LANES - 2 * ML_HEADS)).reshape(1, LANES)
            w_out = bf(w_out_odd[o_])

            q, k, v, og, gates = multi_proj(xp_bf, ws, [(BF16,), (BF16,), (BF16,), (F32,), (F32,)])
            r3 = lambda a: a.reshape(bsz, t, a.shape[-1])
            hn, c, n, m = mlstm_prompt(r3(q), r3(k), r3(v), r3(og), r3(gates), bgate, ml_norm_g[o_])
            xp, xp_bf, xq_p = mm_res_ln([hn.reshape(bsz * t, ml_w)], [w_out], xp, ln_g[l, 0], ln_b[l, 0], alpha,
                                        next_ws=[wq])
            mlc_p.append(c)
            mln_p.append(n[:, :ML_HEADS])
            mlm_p.append(m[:, :ML_HEADS, 0])

            q, k, v, og, gates = multi_proj(xs_bf, ws, [(F32,)] * len(ws))
            hn, c, n, m = mlstm_step(q, k, v, og, gates, bgate, ml_norm_g[o_], state_mlstm_c[o_], state_mlstm_n[o_],
                                     state_mlstm_m[o_])
            xs, xs_bf = mm_res_ln([bf(hn)], [w_out], xs, ln_g[l, 0], ln_b[l, 0], alpha)
            mlc_s.append(c)
            mln_s.append(n)
            mlm_s.append(m)

        xo = xattn_prompt(xq_p.reshape(bsz, t, d), mem_k_bf[l], mem_v_bf[l])
        xp, xp_bf = mm_res_ln([xo.reshape(bsz * t, d)], [wo], xp, ln_g[l, 1], ln_b[l, 1], alpha)
        (xq,) = multi_proj(xs_bf, [wq], [(F32,)])
        xo = xattn_step(xq, mem_k_view, mem_v_view, l, mt)
        xs, xs_bf = mm_res_ln([bf(xo)], [wo], xs, ln_g[l, 1], ln_b[l, 1], alpha)

        if l % 2 == 0:
            e = l // 2
            wg, wu, wd = bf(ffn_w_gate[e]), bf(ffn_w_up[e]), bf(ffn_w_down[e])
            xp, xp_bf = ffn_ln(xp_bf, xp, wg, wu, wd, ln_g[l, 2], ln_b[l, 2], alpha)
            xs, xs_bf = ffn_ln(xs_bf, xs, wg, wu, wd, ln_g[l, 2], ln_b[l, 2], alpha)
        else:
            o_ = l // 2
            wr = _pad_cols(bf(moe_router[o_]), LANES)
            wg, wu, wd = bf(moe_w_gate[o_]), bf(moe_w_up[o_]), bf(moe_w_down[o_])
            xp, xp_bf = moe_top2_ln(xp_bf, xp, wr, wg, wu, wd, ln_g[l, 2], ln_b[l, 2], alpha)
            gs = router_gates(xs_bf, wr, n_experts)
            xs, xs_bf = moe_ln(xs_bf, xs, gs, wg, wu, wd, ln_g[l, 2], ln_b[l, 2], alpha)

    xh = mem_k_f32[0].shape[-1] // X_HEADS
    mem_shape = (bsz, mt, X_HEADS, xh)
    return (xp.reshape(bsz, t, d), xs.reshape(s, 1, d),
            jnp.stack(k_p), jnp.stack(v_p), jnp.stack(ki_p), jnp.stack(k_s), jnp.stack(v_s), jnp.stack(ki_s),
            jnp.stack(rgh_p), jnp.stack(rgc_p), jnp.stack(rgh_s), jnp.stack(rgc_s),
            jnp.stack(mlc_p), jnp.stack(mln_p), jnp.stack(mlm_p), jnp.stack(mlc_s), jnp.stack(mln_s), jnp.stack(mlm_s),
            jnp.stack([a.reshape(mem_shape) for a in mem_k_f32]), jnp.stack([a.reshape(mem_shape) for a in mem_v_f32]))
```

```python
import functools
import math

import jax
import jax.numpy as jnp
from jax import lax
from jax.experimental import pallas as pl
from jax.experimental.pallas import tpu as pltpu

F32 = jnp.float32
BF16 = jnp.bfloat16
I32 = jnp.int32

PAGE_SIZE = 128
A_HEADS = 8
A_HEAD_DIM = 64
A_WIDTH = A_HEADS * A_HEAD_DIM
IDX_HEADS = 4
IDX_DIM = 64
TOPK_MAX = 256
REL_BUCKETS = 32
REL_MAX_DIST = 128
RG_BLOCKS = 8
RG_CONV = 4
RG_C = 8.0
ML_HEADS = 4
ML_CHUNK = 128
X_HEADS = 4
TOP_K = 2
LN_EPS = 1e-5

LANES = 128
SUBLANES = 8
VMEM_LIMIT = 48 * 1024 * 1024
VMEM_LIMIT_RESIDENT = 56 * 1024 * 1024
NEG = -1e30
INT_MIN = -2 ** 31
TQ = 128

NT = (((1,), (1,)), ((), ()))
TN = (((0,), (0,)), ((), ()))
HI = lax.Precision.HIGHEST


def _cp(*sem):
    return pltpu.CompilerParams(dimension_semantics=sem, vmem_limit_bytes=VMEM_LIMIT)


def _full(shape):
    n = len(shape)
    return pl.BlockSpec(shape, lambda *_: (0,) * n)


def _row_tile(m, cap):
    t = min(m, cap)
    assert m % t == 0
    return t


def _as_mxu_operand(a):
    return a.astype(BF16).astype(F32)


def _mm(a, b, dims=None):
    prec = HI if a.dtype == F32 else None
    if dims is None:
        return jnp.dot(a, b, preferred_element_type=F32, precision=prec)
    return lax.dot_general(a, b, dims, preferred_element_type=F32, precision=prec)


def _proj_kernel(x_ref, *refs, n_w, out_counts, transposed):
    w_refs, o_refs = refs[:n_w], refs[n_w:]
    x = x_ref[...]
    k = 0
    for w_ref, cnt, tr in zip(w_refs, out_counts, transposed):
        if tr:
            y = _mm(w_ref[...], x, NT)
        else:
            y = _mm(x, w_ref[...])
        for _ in range(cnt):
            if tr:
                for j in range(y.shape[1] // LANES):
                    o_refs[k][j] = y[:, j * LANES:(j + 1) * LANES].astype(o_refs[k].dtype)
            else:
                o_refs[k][...] = y.astype(o_refs[k].dtype)
            k += 1


def multi_proj(x, ws, out_dtypes, transposed=None, tm_cap=512):
    m, kdim = x.shape
    tm = _row_tile(m, tm_cap)
    transposed = transposed or (False,) * len(ws)
    in_specs = [pl.BlockSpec((tm, kdim), lambda i: (i, 0))] + [_full(w.shape) for w in ws]
    out_shape, out_specs = [], []
    for w, dts, tr in zip(ws, out_dtypes, transposed):
        for dt in dts:
            if tr:
                assert tm % LANES == 0
                out_shape.append(jax.ShapeDtypeStruct((m // LANES, w.shape[0], LANES), dt))
                out_specs.append(pl.BlockSpec((tm // LANES, w.shape[0], LANES), lambda i: (i, 0, 0)))
            else:
                out_shape.append(jax.ShapeDtypeStruct((m, w.shape[1]), dt))
                out_specs.append(pl.BlockSpec((tm, w.shape[1]), lambda i: (i, 0)))
    kern = functools.partial(_proj_kernel, n_w=len(ws), out_counts=tuple(len(d) for d in out_dtypes),
                             transposed=tuple(transposed))
    return pl.pallas_call(kern, grid=(m // tm,), in_specs=in_specs, out_specs=out_specs,
                          out_shape=out_shape, compiler_params=_cp("parallel"), name="multi_proj")(x, *ws)


def _layernorm_rows(z, g, b):
    mu = jnp.mean(z, axis=-1, keepdims=True)
    zc = z - mu
    var = jnp.mean(zc * zc, axis=-1, keepdims=True)
    return zc * lax.rsqrt(var + LN_EPS) * g + b


def _mm_ln_kernel(*refs, n_a, n_next, alpha):
    a_refs, w_refs = refs[:n_a], refs[n_a:2 * n_a]
    r_ref, g_ref, b_ref = refs[2 * n_a:2 * n_a + 3]
    next_w_refs = refs[2 * n_a + 3:2 * n_a + 3 + n_next]
    o_ref, obf_ref = refs[2 * n_a + 3 + n_next:2 * n_a + 5 + n_next]
    next_o_refs = refs[2 * n_a + 5 + n_next:]
    y = alpha * r_ref[...]
    for a_ref, w_ref in zip(a_refs, w_refs):
        y = y + _mm(a_ref[...], w_ref[...])
    out = _layernorm_rows(y, g_ref[...], b_ref[...])
    o_ref[...] = out
    out_bf = out.astype(BF16)
    obf_ref[...] = out_bf
    for w_ref, n_ref in zip(next_w_refs, next_o_refs):
        n_ref[...] = _mm(out_bf, w_ref[...]).astype(n_ref.dtype)


def mm_res_ln(a_list, w_list, resid, g, b, alpha, next_ws=(), tm_cap=512):
    m, d = resid.shape
    tm = _row_tile(m, tm_cap)
    rows = lambda w: pl.BlockSpec((tm, w), lambda i: (i, 0))
    in_specs = ([rows(a.shape[1]) for a in a_list] + [_full(w.shape) for w in w_list]
                + [rows(d), _full((1, d)), _full((1, d))] + [_full(w.shape) for w in next_ws])
    out_specs = [rows(d), rows(d)] + [rows(w.shape[1]) for w in next_ws]
    out_shape = ([jax.ShapeDtypeStruct((m, d), F32), jax.ShapeDtypeStruct((m, d), BF16)]
                 + [jax.ShapeDtypeStruct((m, w.shape[1]), BF16) for w in next_ws])
    kern = functools.partial(_mm_ln_kernel, n_a=len(a_list), n_next=len(next_ws), alpha=alpha)
    return pl.pallas_call(kern, grid=(m // tm,), in_specs=in_specs, out_specs=out_specs, out_shape=out_shape,
                          compiler_params=_cp("parallel"), name="mm_res_ln")(
                              *a_list, *w_list, resid, g.reshape(1, d), b.reshape(1, d), *next_ws)


def _ffn_kernel(x_ref, r_ref, wg_ref, wu_ref, wd_ref, g_ref, b_ref, o_ref, obf_ref, acc_sc, *, alpha):
    f = pl.program_id(1)

    @pl.when(f == 0)
    def _():
        acc_sc[...] = alpha * r_ref[...]

    x = x_ref[...]
    hg = _mm(x, wg_ref[...])
    hu = _mm(x, wu_ref[...])
    h = (jax.nn.silu(hg) * hu).astype(x.dtype)
    acc_sc[...] += _mm(h, wd_ref[...])

    @pl.when(f == pl.num_programs(1) - 1)
    def _():
        out = _layernorm_rows(acc_sc[...], g_ref[...], b_ref[...])
        o_ref[...] = out
        obf_ref[...] = out.astype(BF16)


def ffn_ln(x_bf, resid, wg, wu, wd, g, b, alpha, tm_cap=512, tf_cap=1408):
    m, d = resid.shape
    dff = wg.shape[1]
    tm = _row_tile(m, tm_cap)
    tf = _ff_tile(dff, tf_cap)
    in_specs = [pl.BlockSpec((tm, d), lambda i, f: (i, 0)), pl.BlockSpec((tm, d), lambda i, f: (i, 0)),
                pl.BlockSpec((d, tf), lambda i, f: (0, f)), pl.BlockSpec((d, tf), lambda i, f: (0, f)),
                pl.BlockSpec((tf, d), lambda i, f: (f, 0)), _full((1, d)), _full((1, d))]
    out_specs = [pl.BlockSpec((tm, d), lambda i, f: (i, 0))] * 2
    out_shape = [jax.ShapeDtypeStruct((m, d), F32), jax.ShapeDtypeStruct((m, d), BF16)]
    return pl.pallas_call(functools.partial(_ffn_kernel, alpha=alpha), grid=(m // tm, dff // tf),
                          in_specs=in_specs, out_specs=out_specs, out_shape=out_shape,
                          scratch_shapes=[pltpu.VMEM((tm, d), F32)],
                          compiler_params=_cp("parallel", "arbitrary"), name="ffn_ln")(
                              x_bf, resid, wg, wu, wd, g.reshape(1, d), b.reshape(1, d))


def _top2(logits, n_experts):
    lane = lax.broadcasted_iota(I32, logits.shape, 1)
    valid = lane < n_experts
    logits = jnp.where(valid, logits, NEG)
    mx = jnp.max(logits, axis=-1, keepdims=True)
    e = jnp.where(valid, jnp.exp(logits - mx), 0.0)
    p = e / jnp.sum(e, axis=-1, keepdims=True)
    p = jnp.where(valid, p, -1.0)
    big = jnp.int32(LANES)
    m1 = jnp.max(p, axis=-1, keepdims=True)
    i1 = jnp.min(jnp.where(p == m1, lane, big), axis=-1, keepdims=True)
    p2 = jnp.where(lane == i1, -1.0, p)
    m2 = jnp.max(p2, axis=-1, keepdims=True)
    i2 = jnp.min(jnp.where(p2 == m2, lane, big), axis=-1, keepdims=True)
    tot = m1 + m2
    return lane, i1, i2, m1 / tot, m2 / tot


def _router_kernel(x_ref, w_ref, o_ref, *, n_experts):
    lane, i1, i2, g1, g2 = _top2(_mm(x_ref[...], w_ref[...]), n_experts)
    o_ref[...] = jnp.where(lane == i1, g1, jnp.where(lane == i2, g2, 0.0))


def router_gates(x, w_pad, n_experts, tm_cap=512):
    m, d = x.shape
    tm = _row_tile(m, tm_cap)
    return pl.pallas_call(functools.partial(_router_kernel, n_experts=n_experts), grid=(m // tm,),
                          in_specs=[pl.BlockSpec((tm, d), lambda i: (i, 0)), _full(w_pad.shape)],
                          out_specs=pl.BlockSpec((tm, LANES), lambda i: (i, 0)),
                          out_shape=jax.ShapeDtypeStruct((m, LANES), F32),
                          compiler_params=_cp("parallel"), name="router")(x, w_pad)


def _moe_kernel(x_ref, r_ref, gt_ref, wg_ref, wu_ref, wd_ref, g_ref, b_ref, o_ref, obf_ref, acc_sc, *, alpha):
    e = pl.program_id(1)
    f = pl.program_id(2)

    @pl.when((e == 0) & (f == 0))
    def _():
        acc_sc[...] = alpha * r_ref[...]

    gt = gt_ref[...]
    lane = lax.broadcasted_iota(I32, gt.shape, 1)
    ge = jnp.sum(jnp.where(lane == e, gt, 0.0), axis=-1, keepdims=True)
    x = x_ref[...]
    hg = jnp.dot(x, wg_ref[0], preferred_element_type=F32)
    hu = jnp.dot(x, wu_ref[0], preferred_element_type=F32)
    h = (jax.nn.silu(hg) * hu).astype(BF16)
    acc_sc[...] += ge * jnp.dot(h, wd_ref[0], preferred_element_type=F32)

    @pl.when((e == pl.num_programs(1) - 1) & (f == pl.num_programs(2) - 1))
    def _():
        out = _layernorm_rows(acc_sc[...], g_ref[...], b_ref[...])
        o_ref[...] = out
        obf_ref[...] = out.astype(BF16)


def moe_ln(x_bf, resid, gates, wg, wu, wd, g, b, alpha, tm_cap=512, tf=512):
    m, d = resid.shape
    ne, _, dff = wg.shape
    tm = _row_tile(m, tm_cap)
    assert dff % tf == 0
    in_specs = [pl.BlockSpec((tm, d), lambda i, e, f: (i, 0)), pl.BlockSpec((tm, d), lambda i, e, f: (i, 0)),
                pl.BlockSpec((tm, LANES), lambda i, e, f: (i, 0)),
                pl.BlockSpec((1, d, tf), lambda i, e, f: (e, 0, f)), pl.BlockSpec((1, d, tf), lambda i, e, f: (e, 0, f)),
                pl.BlockSpec((1, tf, d), lambda i, e, f: (e, f, 0)), _full((1, d)), _full((1, d))]
    out_specs = [pl.BlockSpec((tm, d), lambda i, e, f: (i, 0))] * 2
    out_shape = [jax.ShapeDtypeStruct((m, d), F32), jax.ShapeDtypeStruct((m, d), BF16)]
    return pl.pallas_call(functools.partial(_moe_kernel, alpha=alpha), grid=(m // tm, ne, dff // tf),
                          in_specs=in_specs, out_specs=out_specs, out_shape=out_shape,
                          scratch_shapes=[pltpu.VMEM((tm, d), F32)],
                          compiler_params=_cp("parallel", "arbitrary", "arbitrary"), name="moe_ln")(
                              x_bf, resid, gates, wg, wu, wd, g.reshape(1, d), b.reshape(1, d))


MOE_TILE = 512
COMBINE_SLACK = LANES
ROUTE_LANES = ("e1", "e2", "g1", "g2", "rank1", "rank2")


def _router_plan_kernel(x_ref, w_ref, ltri_ref, route_ref, before_ref, count_ref, carry_sc, *, n_experts):
    @pl.when(pl.program_id(0) == 0)
    def _():
        carry_sc[...] = jnp.zeros(carry_sc.shape, F32)

    lane, i1, i2, g1, g2 = _top2(_mm(x_ref[...], w_ref[...]), n_experts)
    chosen = jnp.where(lane == i1, 1.0, jnp.where(lane == i2, 1.0, 0.0))
    before = carry_sc[0:1, :]
    rank = before + jnp.dot(ltri_ref[...], chosen.astype(BF16), preferred_element_type=F32)
    r1 = jnp.sum(jnp.where(lane == i1, rank, 0.0), axis=-1, keepdims=True)
    r2 = jnp.sum(jnp.where(lane == i2, rank, 0.0), axis=-1, keepdims=True)
    vals = (i1.astype(F32), i2.astype(F32), g1, g2, r1, r2)
    route = jnp.zeros(lane.shape, F32)
    for k, v in enumerate(vals):
        route = jnp.where(lane == k, v, route)
    route_ref[...] = route
    before_ref[0] = jnp.broadcast_to(before, before_ref.shape[1:])
    after = before + jnp.sum(chosen, axis=0, keepdims=True)
    carry_sc[0:1, :] = after
    count_ref[...] = jnp.broadcast_to(after, count_ref.shape)


def router_plan(x_bf, w_pad, n_experts):
    m, d = x_bf.shape
    tb = MOE_TILE
    assert m % tb == 0
    ltri = (lax.broadcasted_iota(I32, (tb, tb), 1) < lax.broadcasted_iota(I32, (tb, tb), 0)).astype(BF16)
    return pl.pallas_call(
        functools.partial(_router_plan_kernel, n_experts=n_experts), grid=(m // tb,),
        in_specs=[pl.BlockSpec((tb, d), lambda i: (i, 0)), _full(w_pad.shape), _full((tb, tb))],
        out_specs=[pl.BlockSpec((tb, LANES), lambda i: (i, 0)), pl.BlockSpec((1, SUBLANES, LANES), lambda i: (i, 0, 0)),
                   _full((SUBLANES, LANES))],
        out_shape=[jax.ShapeDtypeStruct((m, LANES), F32), jax.ShapeDtypeStruct((m // tb, SUBLANES, LANES), F32),
                   jax.ShapeDtypeStruct((SUBLANES, LANES), F32)],
        scratch_shapes=[pltpu.VMEM((SUBLANES, LANES), F32)],
        compiler_params=_cp("arbitrary"), name="router_plan")(x_bf, w_pad, ltri)


def _moe_gather_kernel(blo_ref, bhi_ref, x_ref, pos_ref, g_ref, xs_ref, gs_ref, acc_sc, gacc_sc, *, tile):
    j = pl.program_id(0)
    acc_sc[...] = jnp.zeros(acc_sc.shape, F32)
    gacc_sc[...] = jnp.zeros(gacc_sc.shape, F32)
    rid = j * tile + lax.broadcasted_iota(I32, (tile, tile), 0)
    glane = lax.broadcasted_iota(I32, gacc_sc.shape, 1)

    def body(b, carry):
        off = pl.multiple_of(b * tile, tile)
        m1 = jnp.where(pos_ref[b, 0:1, :] == rid, 1.0, 0.0).astype(BF16)
        m2 = jnp.where(pos_ref[b, 1:2, :] == rid, 1.0, 0.0).astype(BF16)
        acc_sc[...] += jnp.dot(m1 + m2, x_ref[pl.ds(off, tile), :], preferred_element_type=F32)
        gb = g_ref[pl.ds(off, tile), :]
        gacc_sc[...] += jnp.where(glane < 3, jnp.dot(m1, gb, preferred_element_type=F32),
                                  jnp.dot(m2, gb, preferred_element_type=F32))
        return carry

    lax.fori_loop(blo_ref[j], bhi_ref[j] + 1, body, 0)
    xs_ref[...] = acc_sc[...].astype(xs_ref.dtype)
    gs_ref[...] = gacc_sc[...]


def moe_gather(x_bf, pos, gsplit, b_lo, b_hi, n_tiles):
    m, d = x_bf.shape
    tile = MOE_TILE
    resident = lambda shape: pl.BlockSpec(shape, lambda j, lo, hi: (0,) * len(shape), pipeline_mode=pl.Buffered(1))
    grid_spec = pltpu.PrefetchScalarGridSpec(
        num_scalar_prefetch=2, grid=(n_tiles,),
        in_specs=[resident((m, d)), resident(pos.shape), resident(gsplit.shape)],
        out_specs=[pl.BlockSpec((tile, d), lambda j, lo, hi: (j, 0)),
                   pl.BlockSpec((tile, LANES), lambda j, lo, hi: (j, 0))],
        scratch_shapes=[pltpu.VMEM((tile, d), F32), pltpu.VMEM((tile, LANES), F32)])
    return pl.pallas_call(
        functools.partial(_moe_gather_kernel, tile=tile), grid_spec=grid_spec,
        out_shape=[jax.ShapeDtypeStruct((n_tiles * tile, d), BF16),
                   jax.ShapeDtypeStruct((n_tiles * tile, LANES), F32)],
        compiler_params=pltpu.CompilerParams(dimension_semantics=("arbitrary",), vmem_limit_bytes=VMEM_LIMIT_RESIDENT),
        name="moe_gather")(b_lo, b_hi, x_bf, pos, gsplit)


def _moe_group_ffn_kernel(te_ref, nv_ref, x_ref, gs_ref, wg_ref, wu_ref, wd_ref, o_ref, acc_sc):
    j = pl.program_id(0)
    f = pl.program_id(1)

    @pl.when(f == 0)
    def _():
        acc_sc[...] = jnp.zeros(acc_sc.shape, F32)

    @pl.when(j < nv_ref[0])
    def _():
        x = x_ref[...]
        hg = jnp.dot(x, wg_ref[0], preferred_element_type=F32)
        hu = jnp.dot(x, wu_ref[0], preferred_element_type=F32)
        h = (jax.nn.silu(hg) * hu).astype(BF16)
        acc_sc[...] += jnp.dot(h, wd_ref[0], preferred_element_type=F32)

    @pl.when(f == pl.num_programs(1) - 1)
    def _():
        gs = gs_ref[...]
        lane = lax.broadcasted_iota(I32, gs.shape, 1)
        gate = jnp.sum(jnp.where(lane < 6, gs, 0.0), axis=-1, keepdims=True)
        o_ref[...] = acc_sc[...] * gate


def _ff_tile(dff, cap):
    return max(t for t in range(LANES, min(dff, cap) + 1, LANES) if dff % t == 0)


def moe_group_ffn(tile_expert, n_valid, x_sorted, gterms, wg, wu, wd, n_out_tiles, tf_cap=1792):
    tile = MOE_TILE
    n_tiles = x_sorted.shape[0] // tile
    d = x_sorted.shape[1]
    dff = wg.shape[2]
    tf = _ff_tile(dff, tf_cap)
    src = lambda j: jnp.minimum(j, n_tiles - 1)
    grid_spec = pltpu.PrefetchScalarGridSpec(
        num_scalar_prefetch=2, grid=(n_out_tiles, dff // tf),
        in_specs=[pl.BlockSpec((tile, d), lambda j, f, te, nv: (src(j), 0)),
                  pl.BlockSpec((tile, LANES), lambda j, f, te, nv: (src(j), 0)),
                  pl.BlockSpec((1, d, tf), lambda j, f, te, nv: (te[src(j)], 0, f)),
                  pl.BlockSpec((1, d, tf), lambda j, f, te, nv: (te[src(j)], 0, f)),
                  pl.BlockSpec((1, tf, d), lambda j, f, te, nv: (te[src(j)], f, 0))],
        out_specs=pl.BlockSpec((tile, d), lambda j, f, te, nv: (j, 0)),
        scratch_shapes=[pltpu.VMEM((tile, d), F32)])
    return pl.pallas_call(_moe_group_ffn_kernel, grid_spec=grid_spec,
                          out_shape=jax.ShapeDtypeStruct((n_out_tiles * tile, d), F32),
                          compiler_params=_cp("arbitrary", "arbitrary"), name="moe_group_ffn")(
                              tile_expert, n_valid, x_sorted, gterms, wg, wu, wd)


def _moe_combine_kernel(rlo_ref, short_ref, route_ref, r_ref, os_hbm, g_ref, b_ref, o_ref, obf_ref, buf, sem, acc_sc,
                        *, n_experts, windows, alpha):
    i = pl.program_id(0)
    acc_sc[...] = alpha * r_ref[...]
    route = route_ref[...]
    e1 = route[:, 0:1].astype(I32)
    e2 = route[:, 1:2].astype(I32)
    p1 = route[:, 2:3].astype(I32)
    p2 = route[:, 3:4].astype(I32)

    def accumulate(window):
        def window_copy(e, slot):
            lo = pl.multiple_of((rlo_ref[i * n_experts + e] // SUBLANES) * SUBLANES, SUBLANES)
            return lo, pltpu.make_async_copy(os_hbm.at[pl.ds(lo, window), :], buf.at[slot, pl.ds(0, window), :],
                                             sem.at[slot])

        window_copy(0, 0)[1].start()
        col = lax.broadcasted_iota(I32, (route.shape[0], window), 1)
        for e in range(n_experts):
            slot = e % 2
            if e + 1 < n_experts:
                window_copy(e + 1, 1 - slot)[1].start()
            lo, cp = window_copy(e, slot)
            cp.wait()
            c1 = jnp.where(e1 == e, p1 - lo, -1)
            c2 = jnp.where(e2 == e, p2 - lo, -1)
            pick = jnp.where(col == c1, 1.0, jnp.where(col == c2, 1.0, 0.0)).astype(BF16)
            rows = buf[slot, pl.ds(0, window), :]
            hi = rows.astype(BF16)
            lo_part = (rows - hi.astype(F32)).astype(BF16)
            acc_sc[...] += (jnp.dot(pick, hi, preferred_element_type=F32)
                            + jnp.dot(pick, lo_part, preferred_element_type=F32))

    short, full = windows

    @pl.when(short_ref[i] == 1)
    def _():
        accumulate(short)

    @pl.when(short_ref[i] != 1)
    def _():
        accumulate(full)

    out = _layernorm_rows(acc_sc[...], g_ref[...], b_ref[...])
    o_ref[...] = out
    obf_ref[...] = out.astype(BF16)


def moe_combine_ln(row_lo, short, route_pos, resid, out_sorted, g, b, alpha, n_experts, windows):
    m, d = resid.shape
    tile = MOE_TILE
    grid_spec = pltpu.PrefetchScalarGridSpec(
        num_scalar_prefetch=2, grid=(m // tile,),
        in_specs=[pl.BlockSpec((tile, LANES), lambda i, rlo, sh: (i, 0)),
                  pl.BlockSpec((tile, d), lambda i, rlo, sh: (i, 0)),
                  pl.BlockSpec(memory_space=pl.ANY),
                  pl.BlockSpec((1, d), lambda i, rlo, sh: (0, 0)), pl.BlockSpec((1, d), lambda i, rlo, sh: (0, 0))],
        out_specs=[pl.BlockSpec((tile, d), lambda i, rlo, sh: (i, 0))] * 2,
        scratch_shapes=[pltpu.VMEM((2, max(windows), d), F32), pltpu.SemaphoreType.DMA((2,)),
                        pltpu.VMEM((tile, d), F32)])
    return pl.pallas_call(
        functools.partial(_moe_combine_kernel, n_experts=n_experts, windows=windows, alpha=alpha),
        grid_spec=grid_spec,
        out_shape=[jax.ShapeDtypeStruct((m, d), F32), jax.ShapeDtypeStruct((m, d), BF16)],
        compiler_params=_cp("arbitrary"), name="moe_combine_ln")(
            row_lo, short, route_pos, resid, out_sorted, g.reshape(1, d), b.reshape(1, d))


def moe_top2_ln(x_bf, resid, w_router, wg, wu, wd, g, b, alpha):
    m, d = resid.shape
    ne = wg.shape[0]
    tile = MOE_TILE
    nb = m // tile
    n_tiles = (TOP_K * m) // tile + ne
    tail_tiles = -(-(tile + COMBINE_SLACK) // tile)
    route, before, count = router_plan(x_bf, w_router, ne)

    eid = jnp.arange(ne, dtype=I32)
    lookup = lambda table, idx: jnp.sum(jnp.where(idx[..., None] == eid, table, 0), axis=-1)
    cnt = count[0, :ne].astype(I32)
    tiles_e = (cnt + tile - 1) // tile
    first_tile = jnp.cumsum(tiles_e) - tiles_e
    n_valid = jnp.sum(tiles_e)
    row_start = first_tile * tile
    tid = jnp.arange(n_tiles, dtype=I32)
    tile_expert = jnp.minimum(jnp.sum((tid[:, None] >= (first_tile + tiles_e)[None, :]).astype(I32), axis=1), ne - 1)
    rank0 = (tid - lookup(first_tile, tile_expert)) * tile
    bef = before[:, 0, :ne].astype(I32)
    aft = jnp.concatenate([bef[1:], cnt[None, :]], axis=0)
    bef_t = lookup(bef[:, None, :], tile_expert[None, :])
    aft_t = lookup(aft[:, None, :], tile_expert[None, :])
    valid_t = tid < n_valid
    b_lo = jnp.where(valid_t, jnp.sum((aft_t <= rank0[None, :]).astype(I32), axis=0), 0)
    b_hi = jnp.where(valid_t, jnp.sum((bef_t < (rank0 + tile)[None, :]).astype(I32), axis=0) - 1, -1)
    e1, e2 = route[:, 0].astype(I32), route[:, 1].astype(I32)
    pos1 = lookup(row_start, e1) + route[:, 4].astype(I32)
    pos2 = lookup(row_start, e2) + route[:, 5].astype(I32)
    pos = jnp.stack([pos1.reshape(nb, tile), pos2.reshape(nb, tile)]
                    + [jnp.zeros((nb, tile), I32)] * (SUBLANES - 2), axis=1)

    def three_terms(v):
        hi = v.astype(BF16)
        mid = (v - hi.astype(F32)).astype(BF16)
        low = (v - hi.astype(F32) - mid.astype(F32)).astype(BF16)
        return [hi, mid, low]

    gsplit = jnp.stack(three_terms(route[:, 2]) + three_terms(route[:, 3]), axis=1)
    gsplit = jnp.pad(gsplit, ((0, 0), (0, LANES - gsplit.shape[1])))
    route_pos = jnp.stack([route[:, 0], route[:, 1], pos1.astype(F32), pos2.astype(F32)], axis=1)
    route_pos = jnp.pad(route_pos, ((0, 0), (0, LANES - route_pos.shape[1])))
    row_lo = row_start[None, :] + bef
    windows = (tile // 2, tile + COMBINE_SLACK)
    short = jnp.all(row_lo % SUBLANES + (aft - bef) <= windows[0], axis=1).astype(I32)

    x_sorted, gterms = moe_gather(x_bf, pos, gsplit, b_lo, b_hi, n_tiles)
    out_sorted = moe_group_ffn(tile_expert, n_valid.reshape(1), x_sorted, gterms, wg, wu, wd, n_tiles + tail_tiles)
    return moe_combine_ln(row_lo.reshape(-1), short, route_pos, resid, out_sorted, g, b, alpha, ne, windows)


NEG_INF_CODE = INT_MIN + 0x7FFFFF


def _code_to_f32(code):
    return pltpu.bitcast(jnp.where(code < 0, code ^ jnp.int32(0x7FFFFFFF), code), F32)


def _threshold_search(fold, topk, shape):
    kk = jnp.float32(topk)
    count = lambda pred: fold(lambda s, i: jnp.where(pred(s), 1.0, 0.0), "sum")
    c0 = count(lambda s: s >= 0.0)
    code = jnp.where(c0 >= kk, jnp.int32(0), jnp.int32(INT_MIN)) + jnp.zeros(shape, I32)

    def bit_body(j, code):
        cand = code | (jnp.int32(1) << (30 - j))
        cand_f = _code_to_f32(cand)
        return jnp.where(count(lambda s: s >= cand_f) >= kk, cand, code)

    code = lax.fori_loop(0, 31, bit_body, code)
    enough = code > NEG_INF_CODE
    floor_t = _code_to_f32(code)
    t = jnp.where(enough, fold(lambda s, i: jnp.where(s >= floor_t, s, jnp.inf), "min"), -jnp.inf)
    need = kk - count(lambda s: s > t)
    return t, jnp.where(enough, need, 0)


def _topk_threshold(fold, topk, n_idx_bits, rows):
    t, need = _threshold_search(fold, topk, (rows, 1))

    def tie_body(j, p):
        cand = p | (jnp.int32(1) << (n_idx_bits - 1 - j))
        got = fold(lambda s, i: jnp.where(s == t, jnp.where(i < cand, 1.0, 0.0), 0.0), "sum")
        return jnp.where(got <= need, cand, p)

    p = lax.fori_loop(0, n_idx_bits, tie_body, jnp.zeros((rows, 1), I32))
    return t, jnp.where(need > 0, p, 0)


CHUNK_GROUP = 4
ATT_CHUNKS = 4


def _dsa_prompt_kernel(qt_ref, k_ref, vt_ref, qit_ref, ki_ref, wit_ref, bias_ref, ltri_ref, o_ref,
                       score_sc, m_sc, l_sc, acc_sc, *, topk):
    i = pl.program_id(1)
    n_groups = (i + CHUNK_GROUP) // CHUNK_GROUP
    krow = lax.broadcasted_iota(I32, (TQ, TQ), 0)
    qcol = lax.broadcasted_iota(I32, (TQ, TQ), 1)
    qit = qit_ref[0, 0] * jnp.asarray(IDX_DIM ** -0.5, BF16)
    qit_heads = jnp.concatenate([qit[h * IDX_DIM:(h + 1) * IDX_DIM, :] for h in range(IDX_HEADS)], axis=1)
    wt = wit_ref[0, 0] * (IDX_HEADS ** -0.5)

    def score_body(g, carry):
        base = pl.multiple_of(g * (CHUNK_GROUP * TQ), CHUNK_GROUP * TQ)
        for j in range(CHUNK_GROUP):
            off = base + j * TQ
            s_heads = jnp.dot(ki_ref[0, pl.ds(off, TQ), :], qit_heads, preferred_element_type=F32)
            sc = jnp.zeros((TQ, TQ), F32)
            for h in range(IDX_HEADS):
                sc = sc + wt[h:h + 1, :] * jnp.maximum(s_heads[:, h * TQ:(h + 1) * TQ], 0.0)
            score_sc[pl.ds(off, TQ), :] = jnp.where(off + krow <= i * TQ + qcol, sc, -jnp.inf)
        return carry

    lax.fori_loop(0, n_groups, score_body, 0)

    def fold(fn, kind):
        init, join, finish = {"sum": (0.0, jnp.add, jnp.sum), "min": (jnp.inf, jnp.minimum, jnp.min)}[kind]

        def body(g, acc):
            base = pl.multiple_of(g * (CHUNK_GROUP * TQ), CHUNK_GROUP * TQ)
            for j in range(CHUNK_GROUP):
                acc = join(acc, fn(score_sc[pl.ds(base + j * TQ, TQ), :], None))
            return acc

        acc = lax.fori_loop(0, n_groups, body, jnp.full((TQ, TQ), init, F32))
        return finish(acc, axis=0, keepdims=True)

    t, need = _threshold_search(fold, topk, (1, TQ))

    m_sc[...] = jnp.full(m_sc.shape, NEG, F32)
    l_sc[...] = jnp.zeros(l_sc.shape, F32)
    acc_sc[...] = jnp.zeros(acc_sc.shape, F32)
    qt = qt_ref[0, 0] * jnp.asarray(A_HEAD_DIM ** -0.5, BF16)

    def att_body(pair, ties_before):
        chunks, sels, kcs = [], [], []
        for j in range(ATT_CHUNKS):
            c = pair * ATT_CHUNKS + j
            off = pl.multiple_of(c * TQ, TQ)
            key = score_sc[pl.ds(off, TQ), :]
            tie = key == t
            rank = ties_before + jnp.dot(ltri_ref[...], jnp.where(tie, 1.0, 0.0).astype(BF16),
                                         preferred_element_type=F32)
            ties_before = rank[TQ - 1:TQ, :]
            sels.append(jnp.where(key > t, 1, jnp.where(tie, jnp.where(rank <= need, 1, 0), 0)) > 0)
            kcs.append(k_ref[0, pl.ds(off, TQ), :])
            chunks.append(c)
        for h in range(A_HEADS):
            sl = slice(h * A_HEAD_DIM, (h + 1) * A_HEAD_DIM)
            ss = []
            for j in range(ATT_CHUNKS):
                s = jnp.dot(kcs[j][:, sl], qt[sl, :], preferred_element_type=F32)
                bias = bias_ref[jnp.clip(i - chunks[j], 0, 2), h]
                ss.append(jnp.where(sels[j], s + bias, NEG))
            m_prev = m_sc[h, 0:1, :]
            m_new = jnp.maximum(m_prev, jnp.max(functools.reduce(jnp.maximum, ss), axis=0, keepdims=True))
            alpha = jnp.exp(m_prev - m_new)
            pes = [jnp.exp(s - m_new) for s in ss]
            l_sc[h, 0:1, :] = alpha * l_sc[h, 0:1, :] + jnp.sum(functools.reduce(jnp.add, pes), axis=0, keepdims=True)
            pv = functools.reduce(jnp.add, [jnp.dot(vt_ref[0, chunks[j], sl, :], pes[j].astype(BF16),
                                                    preferred_element_type=F32) for j in range(ATT_CHUNKS)])
            acc_sc[sl, :] = alpha * acc_sc[sl, :] + pv
            m_sc[h, 0:1, :] = m_new
        return ties_before

    lax.fori_loop(0, (i + ATT_CHUNKS) // ATT_CHUNKS, att_body, jnp.zeros((1, TQ), F32))
    for h in range(A_HEADS):
        sl = slice(h * A_HEAD_DIM, (h + 1) * A_HEAD_DIM)
        acc_sc[sl, :] = acc_sc[sl, :] / l_sc[h, 0:1, :]
    o_ref[0] = acc_sc[...].T.astype(o_ref.dtype)


def dsa_prompt(qt, k, vt, qit, ki, wit, bias_tab):
    bsz, t, _ = k.shape
    assert TQ == LANES and t % (CHUNK_GROUP * TQ) == 0
    topk = min(TOPK_MAX, t // 4)
    nq = t // TQ
    ltri = (lax.broadcasted_iota(I32, (TQ, TQ), 1) <= lax.broadcasted_iota(I32, (TQ, TQ), 0)).astype(BF16)
    per_q = lambda rows: pl.BlockSpec((1, 1, rows, TQ), lambda b, i: (b, i, 0, 0))
    per_b = lambda w: pl.BlockSpec((1, t, w), lambda b, i: (b, 0, 0))
    stat = pltpu.VMEM((A_HEADS, SUBLANES, TQ), F32)
    return pl.pallas_call(
        functools.partial(_dsa_prompt_kernel, topk=topk), grid=(bsz, nq),
        in_specs=[per_q(A_WIDTH), per_b(A_WIDTH), pl.BlockSpec((1, nq, A_WIDTH, TQ), lambda b, i: (b, 0, 0, 0)),
                  per_q(IDX_HEADS * IDX_DIM), per_b(IDX_DIM), per_q(SUBLANES), _full(bias_tab.shape),
                  _full((TQ, TQ))],
        out_specs=pl.BlockSpec((1, TQ, A_WIDTH), lambda b, i: (b, i, 0)),
        out_shape=jax.ShapeDtypeStruct((bsz, t, A_WIDTH), BF16),
        scratch_shapes=[pltpu.VMEM((t, TQ), F32), stat, stat, pltpu.VMEM((A_WIDTH, TQ), F32)],
        compiler_params=_cp("parallel", "arbitrary"), name="dsa_prompt")(qt, k, vt, qit, ki, wit, bias_tab, ltri)


def _row_to_col(row):
    n = row.shape[1]
    eye = lax.broadcasted_iota(I32, (n, n), 0) == lax.broadcasted_iota(I32, (n, n), 1)
    return jnp.sum(jnp.where(eye, jnp.broadcast_to(row, (n, n)), 0.0), axis=1, keepdims=True)


def _col_to_row(col):
    n = col.shape[0]
    eye = lax.broadcasted_iota(I32, (n, n), 0) == lax.broadcasted_iota(I32, (n, n), 1)
    return jnp.sum(jnp.where(eye, jnp.broadcast_to(col, (n, n)), 0.0), axis=0, keepdims=True)


def _dsa_s_score_kernel(pt_ref, qi_ref, w_ref, *refs, n_pages):
    page_refs, o_ref = refs[:n_pages], refs[n_pages]
    w = _as_mxu_operand(w_ref[0] * (IDX_HEADS ** -0.5))
    qcols = [_row_to_col(_as_mxu_operand(qi_ref[0, h:h + 1, :] * (IDX_DIM ** -0.5))) for h in range(IDX_HEADS)]
    for p in range(n_pages):
        page = _as_mxu_operand(page_refs[p][0])
        sc = jnp.zeros((1, PAGE_SIZE), F32)
        for h in range(IDX_HEADS):
            s = jnp.sum(page * qcols[h], axis=0, keepdims=True)
            sc = sc + w[:, h:h + 1] * _as_mxu_operand(jnp.maximum(s, 0.0))
        o_ref[0, :, p * PAGE_SIZE:(p + 1) * PAGE_SIZE] = sc


def dsa_sample_scores(page_table, qi8, wi, kidx_pool_t):
    s, n_pages = page_table.shape
    pt = page_table.reshape(-1)
    page_spec = lambda p: pl.BlockSpec((1, IDX_DIM, PAGE_SIZE), lambda i, pt_ref: (pt_ref[i * n_pages + p], 0, 0))
    grid_spec = pltpu.PrefetchScalarGridSpec(
        num_scalar_prefetch=1, grid=(s,),
        in_specs=[pl.BlockSpec((1, SUBLANES, IDX_DIM), lambda i, pt_ref: (i, 0, 0)),
                  pl.BlockSpec((1, 1, LANES), lambda i, pt_ref: (i, 0, 0))] + [page_spec(p) for p in range(n_pages)],
        out_specs=pl.BlockSpec((1, 1, n_pages * PAGE_SIZE), lambda i, pt_ref: (i, 0, 0)))
    return pl.pallas_call(functools.partial(_dsa_s_score_kernel, n_pages=n_pages), grid_spec=grid_spec,
                          out_shape=jax.ShapeDtypeStruct((s, 1, n_pages * PAGE_SIZE), F32),
                          compiler_params=_cp("arbitrary"), name="dsa_sample_scores")(
                              pt, qi8, wi, *([kidx_pool_t] * n_pages))


def _dsa_s_select_kernel(sc_ref, qi_ref, kn_ref, wi_ref, mask_ref, mnew_ref, *, topk, n_idx_bits):
    s, lp = mask_ref.shape
    w = _as_mxu_operand(wi_ref[...] * (IDX_HEADS ** -0.5))
    sc_new = jnp.zeros((s, 1), F32)
    kn = _as_mxu_operand(kn_ref[...])
    for h in range(IDX_HEADS):
        s_new = jnp.sum(_as_mxu_operand(qi_ref[:, h * IDX_DIM:(h + 1) * IDX_DIM]) * kn, axis=-1, keepdims=True)
        sc_new = sc_new + w[:, h:h + 1] * _as_mxu_operand(jnp.maximum(s_new * (IDX_DIM ** -0.5), 0.0))
    col = lax.broadcasted_iota(I32, (s, lp), 1)

    def fold(fn, kind):
        join, finish = {"sum": (jnp.add, jnp.sum), "min": (jnp.minimum, jnp.min)}[kind]
        main = finish(fn(sc_ref[...], col), axis=1, keepdims=True)
        return join(main, fn(sc_new, jnp.int32(lp)))

    t, p = _topk_threshold(fold, topk, n_idx_bits, s)
    sc = sc_ref[...]
    mask_ref[...] = jnp.where(sc > t, 1.0, jnp.where(sc == t, jnp.where(col < p, 1.0, 0.0), 0.0))
    sel_new = jnp.where(sc_new > t, 1.0, jnp.where(sc_new == t, jnp.where(lp < p, 1.0, 0.0), 0.0))
    mnew_ref[...] = jnp.broadcast_to(sel_new, mnew_ref.shape)


def dsa_sample_select(raw, qi, ki_new, wi):
    s, lp = raw.shape
    length = lp + 1
    topk = min(TOPK_MAX, length // 4)
    kern = functools.partial(_dsa_s_select_kernel, topk=topk, n_idx_bits=length.bit_length())
    return pl.pallas_call(kern, out_shape=[jax.ShapeDtypeStruct((s, lp), F32), jax.ShapeDtypeStruct((s, LANES), F32)],
                          compiler_params=pltpu.CompilerParams(vmem_limit_bytes=VMEM_LIMIT),
                          name="dsa_sample_select")(raw, qi, ki_new, wi)


def _dsa_s_attn_kernel(pt_ref, q_ref, kn_ref, vn_ref, mask_ref, mnew_ref, bias_ref, bnew_ref, *refs, n_pages):
    k_refs, v_refs, o_ref = refs[:n_pages], refs[n_pages:2 * n_pages], refs[2 * n_pages]
    hd = A_HEAD_DIM
    q = _as_mxu_operand(q_ref[0] * (hd ** -0.5))
    qcols = [_row_to_col(q[h:h + 1, :]) for h in range(A_HEADS)]
    hrow = lax.broadcasted_iota(I32, (A_HEADS, PAGE_SIZE), 0)
    lg_new = jnp.sum(q * _as_mxu_operand(kn_ref[0]), axis=1, keepdims=True) + bnew_ref[:, 0:1]
    lg_new = jnp.where(mnew_ref[0][:, 0:1] > 0.0, lg_new, NEG)
    logits = []
    mx = lg_new
    for p in range(n_pages):
        lg = jnp.zeros((A_HEADS, PAGE_SIZE), F32)
        for h in range(A_HEADS):
            kp = _as_mxu_operand(k_refs[p][0, h * hd:(h + 1) * hd, :])
            s = jnp.sum(kp * qcols[h], axis=0, keepdims=True)
            lg = jnp.where(hrow == h, jnp.broadcast_to(s, lg.shape), lg)
        lg = jnp.where(mask_ref[0, p:p + 1, :] > 0.0, lg + bias_ref[p], NEG)
        logits.append(lg)
        mx = jnp.maximum(mx, jnp.max(lg, axis=1, keepdims=True))
    pe_new = jnp.exp(lg_new - mx)
    pes = [jnp.exp(lg - mx) for lg in logits]
    den = pe_new
    for pe in pes:
        den = den + jnp.sum(pe, axis=1, keepdims=True)
    probs = [_as_mxu_operand(pe / den) for pe in pes]
    p_new = _as_mxu_operand(pe_new / den)
    for h in range(A_HEADS):
        acc = jnp.zeros((hd, PAGE_SIZE), F32)
        for p in range(n_pages):
            acc = acc + probs[p][h:h + 1, :] * _as_mxu_operand(v_refs[p][0, h * hd:(h + 1) * hd, :])
        o_ref[0, h:h + 1, :] = (_col_to_row(jnp.sum(acc, axis=1, keepdims=True))
                                + p_new[h:h + 1, :] * _as_mxu_operand(vn_ref[0, h:h + 1, :]))


def dsa_sample_attn(page_table, q, k_new, v_new, mask, mask_new, bias_pages, bias_new, k_pool_t, v_pool_t):
    s, n_pages = page_table.shape
    pt = page_table.reshape(-1)
    per_s = lambda r, w: pl.BlockSpec((1, r, w), lambda i, pt_ref: (i, 0, 0))
    const = lambda shape: pl.BlockSpec(shape, lambda i, pt_ref: (0,) * len(shape))
    page_spec = lambda p: pl.BlockSpec((1, A_WIDTH, PAGE_SIZE), lambda i, pt_ref: (pt_ref[i * n_pages + p], 0, 0))
    hd = (A_HEADS, A_HEAD_DIM)
    grid_spec = pltpu.PrefetchScalarGridSpec(
        num_scalar_prefetch=1, grid=(s,),
        in_specs=[per_s(*hd), per_s(*hd), per_s(*hd), per_s(n_pages, PAGE_SIZE), per_s(1, LANES),
                  const(bias_pages.shape), const(bias_new.shape)]
                 + [page_spec(p) for p in range(n_pages)] * 2,
        out_specs=per_s(*hd))
    return pl.pallas_call(functools.partial(_dsa_s_attn_kernel, n_pages=n_pages), grid_spec=grid_spec,
                          out_shape=jax.ShapeDtypeStruct((s,) + hd, F32),
                          compiler_params=_cp("arbitrary"), name="dsa_sample_attn")(
                              pt, q, k_new, v_new, mask, mask_new, bias_pages, bias_new,
                              *([k_pool_t] * n_pages), *([v_pool_t] * n_pages))


def _rglru_gates(xc, wa, ba, wi, bi, lam):
    xb = xc.astype(wa.dtype)
    r = jax.nn.sigmoid(_mm(xb, wa) + ba)
    ig = jax.nn.sigmoid(_mm(xb, wi) + bi)
    log_a = -RG_C * r * jax.nn.softplus(-lam)
    a = jnp.exp(log_a)
    u = jnp.sqrt(1.0 - jnp.exp(2.0 * log_a)) * (ig * xc)
    return a, u


def _rglru_prompt_kernel(xr_ref, gr_ref, cw_ref, cb_ref, wa_ref, ba_ref, wi_ref, bi_ref, lam_ref,
                         y_ref, hl_ref, xs_sc, h_sc, *, tb):
    t = pl.program_id(1)
    pad = SUBLANES

    @pl.when(t == 0)
    def _():
        xs_sc[0:pad, :] = jnp.zeros((pad, xs_sc.shape[1]), F32)
        h_sc[...] = jnp.zeros(h_sc.shape, F32)

    xs_sc[pad:pad + tb, :] = xr_ref[0]
    xc = cb_ref[...] + cw_ref[RG_CONV - 1:RG_CONV, :] * xs_sc[pad:pad + tb, :]
    for j in range(1, RG_CONV):
        xc = xc + cw_ref[RG_CONV - 1 - j:RG_CONV - j, :] * xs_sc[pad - j:pad - j + tb, :]
    a, u = _rglru_gates(xc, wa_ref[...], ba_ref[...], wi_ref[...], bi_ref[...], lam_ref[...])
    rowi = lax.broadcasted_iota(I32, a.shape, 0)
    s = 1
    while s < tb:
        keep = rowi >= s
        a_sh = jnp.where(keep, pltpu.roll(a, s, 0), 1.0)
        u_sh = jnp.where(keep, pltpu.roll(u, s, 0), 0.0)
        u = a * u_sh + u
        a = a * a_sh
        s *= 2
    h = a * h_sc[0:1, :] + u
    y_ref[0] = (h * jax.nn.gelu(gr_ref[0])).astype(y_ref.dtype)
    h_sc[0:1, :] = h[tb - 1:tb, :]
    xs_sc[0:pad, :] = xs_sc[tb:tb + pad, :]

    @pl.when(t == pl.num_programs(1) - 1)
    def _():
        hl_ref[0] = h[tb - 1:tb, :]


def rglru_prompt(xr, gr, cw, cb, wa, ba, wi, bi, lam, tb_cap=256):
    bsz, t, w = xr.shape
    tb = _row_tile(t, tb_cap)
    blk = pl.BlockSpec((1, tb, w), lambda b, i: (b, i, 0))
    vec = _full((1, w))
    return pl.pallas_call(
        functools.partial(_rglru_prompt_kernel, tb=tb), grid=(bsz, t // tb),
        in_specs=[blk, blk, _full((RG_CONV, w)), vec, _full((w, w)), vec, _full((w, w)), vec, vec],
        out_specs=[blk, pl.BlockSpec((1, 1, w), lambda b, i: (b, 0, 0))],
        out_shape=[jax.ShapeDtypeStruct((bsz, t, w), BF16), jax.ShapeDtypeStruct((bsz, 1, w), F32)],
        scratch_shapes=[pltpu.VMEM((tb + SUBLANES, w), F32), pltpu.VMEM((SUBLANES, w), F32)],
        compiler_params=_cp("parallel", "arbitrary"), name="rglru_prompt")(
            xr, gr, cw, cb.reshape(1, w), wa, ba.reshape(1, w), wi, bi.reshape(1, w), lam.reshape(1, w))


def _rglru_step_kernel(xr_ref, gr_ref, b0_ref, b1_ref, b2_ref, h0_ref, cw_ref, cb_ref, wa_ref, ba_ref, wi_ref,
                       bi_ref, lam_ref, y_ref, h_ref):
    xc = (cb_ref[...] + cw_ref[0:1, :] * b0_ref[...] + cw_ref[1:2, :] * b1_ref[...]
          + cw_ref[2:3, :] * b2_ref[...] + cw_ref[3:4, :] * xr_ref[...])
    a, u = _rglru_gates(xc, wa_ref[...], ba_ref[...], wi_ref[...], bi_ref[...], lam_ref[...])
    h = a * h0_ref[...] + u
    h_ref[...] = h
    y_ref[...] = (h * jax.nn.gelu(gr_ref[...])).astype(y_ref.dtype)


def rglru_step(xr, gr, buf, h0, cw, cb, wa, ba, wi, bi, lam):
    s, w = xr.shape
    return pl.pallas_call(
        _rglru_step_kernel,
        out_shape=[jax.ShapeDtypeStruct((s, w), F32), jax.ShapeDtypeStruct((s, w), F32)],
        compiler_params=pltpu.CompilerParams(vmem_limit_bytes=VMEM_LIMIT), name="rglru_step")(
            xr, gr, buf[:, 0], buf[:, 1], buf[:, 2], h0, cw, cb.reshape(1, w), wa, ba.reshape(1, w),
            wi, bi.reshape(1, w), lam.reshape(1, w))


def _headnorm_gate(h, g, og):
    mu = jnp.mean(h, axis=-1, keepdims=True)
    hc = h - mu
    var = jnp.mean(hc * hc, axis=-1, keepdims=True)
    return hc * lax.rsqrt(var + LN_EPS) * g * jax.nn.sigmoid(og)


def _mlstm_prompt_kernel(q_ref, k_ref, v_ref, og_ref, gt_ref, bg_ref, ng_ref, tri_ref,
                         y_ref, c_ref, n_ref, m_ref, c_sc, n_sc, m_sc, *, cl, dh):
    ci = pl.program_id(1)

    @pl.when(ci == 0)
    def _():
        c_sc[...] = jnp.zeros(c_sc.shape, F32)
        n_sc[...] = jnp.zeros(n_sc.shape, F32)
        m_sc[...] = jnp.zeros(m_sc.shape, F32)

    pre = gt_ref[0] + bg_ref[...]
    lane = lax.broadcasted_iota(I32, pre.shape, 1)
    gv = jnp.where(lane < ML_HEADS, pre, jax.nn.log_sigmoid(pre))
    csum = jnp.dot(tri_ref[...], gv, preferred_element_type=F32, precision=HI)
    gv_t = gv.T
    csum_t = csum.T
    row = lax.broadcasted_iota(I32, (cl, cl), 0)
    col = lax.broadcasted_iota(I32, (cl, cl), 1)
    causal = col <= row
    kscale = dh ** -0.5
    for h in range(ML_HEADS):
        sl = slice(h * dh, (h + 1) * dh)
        qh = q_ref[0, :, sl]
        kh = k_ref[0, :, sl] * jnp.asarray(kscale, BF16)
        vh = v_ref[0, :, sl]
        b_col = csum[:, ML_HEADS + h:ML_HEADS + h + 1]
        i_col = gv[:, h:h + 1]
        b_row = csum_t[ML_HEADS + h:ML_HEADS + h + 1, :]
        i_row = gv_t[h:h + 1, :]
        m_prev = m_sc[h, 0:1, 0:1]
        c_prev = c_sc[h]
        n_prev = n_sc[h, 0:1, :]
        dmat = jnp.where(causal, b_col - b_row + i_row, NEG)
        inter = b_col + m_prev
        m_t = jnp.maximum(inter, jnp.max(dmat, axis=1, keepdims=True))
        g_inter = jnp.exp(inter - m_t)
        qk = lax.dot_general(qh, kh, NT, preferred_element_type=F32)
        ws = jnp.exp(dmat - m_t) * qk
        num = (g_inter * lax.dot_general(qh, c_prev.astype(BF16), NT, preferred_element_type=F32)
               + jnp.dot(ws.astype(BF16), vh, preferred_element_type=F32))
        qf = qh.astype(F32)
        den = g_inter * jnp.sum(qf * n_prev, axis=1, keepdims=True) + jnp.sum(ws, axis=1, keepdims=True)
        hout = num / jnp.maximum(jnp.abs(den), jnp.exp(-m_t))
        m_new = m_t[cl - 1:cl, :]
        b_last = b_col[cl - 1:cl, :]
        g_c = jnp.exp(b_last + m_prev - m_new)
        w_end = jnp.exp(b_last - b_col + i_col - m_new)
        kf = kh.astype(F32)
        vw = (vh.astype(F32) * w_end).astype(BF16)
        c_sc[h] = g_c * c_prev + lax.dot_general(vw, kh, TN, preferred_element_type=F32)
        n_sc[h, 0:1, :] = g_c * n_prev + jnp.sum(kf * w_end, axis=0, keepdims=True)
        m_sc[h, 0:1, :] = jnp.broadcast_to(m_new, (1, m_sc.shape[2]))
        y_ref[0, :, sl] = _headnorm_gate(hout, ng_ref[:, sl], og_ref[0, :, sl]).astype(y_ref.dtype)

    @pl.when(ci == pl.num_programs(1) - 1)
    def _():
        c_ref[0] = c_sc[...]
        n_ref[0] = jnp.zeros(n_ref.shape[1:], F32)
        m_ref[0] = jnp.zeros(m_ref.shape[1:], F32)
        for h in range(ML_HEADS):
            n_ref[0, h:h + 1, :] = n_sc[h, 0:1, :]
            m_ref[0, h:h + 1, :] = m_sc[h, 0:1, :]


def mlstm_prompt(q, k, v, og, gates, bgate, norm_g):
    bsz, t, width = q.shape
    dh = width // ML_HEADS
    cl = math.gcd(t, ML_CHUNK)
    tri = (lax.broadcasted_iota(I32, (cl, cl), 1) <= lax.broadcasted_iota(I32, (cl, cl), 0)).astype(F32)
    blk = lambda w: pl.BlockSpec((1, cl, w), lambda b, i: (b, i, 0))
    return pl.pallas_call(
        functools.partial(_mlstm_prompt_kernel, cl=cl, dh=dh), grid=(bsz, t // cl),
        in_specs=[blk(width), blk(width), blk(width), blk(width), blk(LANES), _full((1, LANES)), _full((1, width)),
                  _full((cl, cl))],
        out_specs=[blk(width), pl.BlockSpec((1, ML_HEADS, dh, dh), lambda b, i: (b, 0, 0, 0)),
                   pl.BlockSpec((1, SUBLANES, dh), lambda b, i: (b, 0, 0)),
                   pl.BlockSpec((1, SUBLANES, LANES), lambda b, i: (b, 0, 0))],
        out_shape=[jax.ShapeDtypeStruct((bsz, t, width), BF16), jax.ShapeDtypeStruct((bsz, ML_HEADS, dh, dh), F32),
                   jax.ShapeDtypeStruct((bsz, SUBLANES, dh), F32), jax.ShapeDtypeStruct((bsz, SUBLANES, LANES), F32)],
        scratch_shapes=[pltpu.VMEM((ML_HEADS, dh, dh), F32), pltpu.VMEM((ML_HEADS, SUBLANES, dh), F32),
                        pltpu.VMEM((ML_HEADS, SUBLANES, LANES), F32)],
        compiler_params=_cp("parallel", "arbitrary"), name="mlstm_prompt")(
            q, k, v, og, gates, bgate, norm_g.reshape(1, width), tri)


def _mlstm_step_kernel(q_ref, k_ref, v_ref, og_ref, gt_ref, bg_ref, ng_ref, c0_ref, n0_ref, m0_ref,
                       y_ref, c_ref, n_ref, m_ref, *, sb, dh):
    kscale = dh ** -0.5
    m_ref[...] = jnp.zeros(m_ref.shape, F32)

    def seq_body(s, carry):
        rs = pl.ds(s, 1)
        pre = gt_ref[rs, :] + bg_ref[...]
        for h in range(ML_HEADS):
            sl = slice(h * dh, (h + 1) * dh)
            qh = q_ref[rs, sl]
            kh = k_ref[rs, sl] * kscale
            vh = v_ref[rs, sl]
            i_pre = pre[:, h:h + 1]
            logf = jax.nn.log_sigmoid(pre[:, ML_HEADS + h:ML_HEADS + h + 1])
            m_prev = m0_ref[rs, h:h + 1]
            c_prev = c0_ref[s, h]
            n_prev = n0_ref[s, h:h + 1, :]
            inter = logf + m_prev
            m_t = jnp.maximum(inter, i_pre)
            g_inter = jnp.exp(inter - m_t)
            w_in = jnp.exp(i_pre - m_t)
            q16 = _as_mxu_operand(qh)
            qk = jnp.sum(q16 * _as_mxu_operand(kh), axis=1, keepdims=True)
            q8 = jnp.broadcast_to(qh, (SUBLANES, dh)).astype(BF16)
            cq = lax.dot_general(q8, c_prev.astype(BF16), NT, preferred_element_type=F32)[0:1]
            num = g_inter * cq + (w_in * qk) * vh
            den = g_inter * jnp.sum(_as_mxu_operand(n_prev) * q16, axis=1, keepdims=True) + w_in * qk
            hout = num / jnp.maximum(jnp.abs(den), jnp.exp(-m_t))
            c_ref[s, h] = g_inter * c_prev + _row_to_col(w_in * vh) * kh
            n_ref[s, h:h + 1, :] = g_inter * n_prev + w_in * kh
            m_ref[rs, h:h + 1] = m_t
            y_ref[rs, sl] = _headnorm_gate(hout, ng_ref[:, sl], og_ref[rs, sl]).astype(y_ref.dtype)
        return carry

    lax.fori_loop(0, sb, seq_body, 0)


def mlstm_step(q, k, v, og, gates, bgate, norm_g, c0, n0, m0, sb=SUBLANES):
    s, width = q.shape
    dh = width // ML_HEADS
    assert s % sb == 0
    row = lambda w: pl.BlockSpec((sb, w), lambda i: (i, 0))
    m0p = jnp.pad(m0, ((0, 0), (0, LANES - ML_HEADS)))
    outs = pl.pallas_call(
        functools.partial(_mlstm_step_kernel, sb=sb, dh=dh), grid=(s // sb,),
        in_specs=[row(width), row(width), row(width), row(width), row(LANES), _full((1, LANES)), _full((1, width)),
                  pl.BlockSpec((sb, ML_HEADS, dh, dh), lambda i: (i, 0, 0, 0)),
                  pl.BlockSpec((sb, ML_HEADS, dh), lambda i: (i, 0, 0)), row(LANES)],
        out_specs=[row(width), pl.BlockSpec((sb, ML_HEADS, dh, dh), lambda i: (i, 0, 0, 0)),
                   pl.BlockSpec((sb, ML_HEADS, dh), lambda i: (i, 0, 0)), row(LANES)],
        out_shape=[jax.ShapeDtypeStruct((s, width), F32), jax.ShapeDtypeStruct((s, ML_HEADS, dh, dh), F32),
                   jax.ShapeDtypeStruct((s, ML_HEADS, dh), F32), jax.ShapeDtypeStruct((s, LANES), F32)],
        compiler_params=_cp("parallel"), name="mlstm_step")(
            q, k, v, og, gates, bgate, norm_g.reshape(1, width), c0, n0, m0p)
    y, c, n, m = outs
    return y, c, n, m[:, :ML_HEADS]


def _xattn_prompt_kernel(q_ref, k_ref, v_ref, o_ref, *, dh):
    scale = dh ** -0.5
    for h in range(X_HEADS):
        sl = slice(h * dh, (h + 1) * dh)
        lg = lax.dot_general(q_ref[0, :, sl], k_ref[0, :, sl], NT, preferred_element_type=F32) * scale
        mx = jnp.max(lg, axis=-1, keepdims=True)
        e = jnp.exp(lg - mx)
        pr = e / jnp.sum(e, axis=-1, keepdims=True)
        o_ref[0, :, sl] = jnp.dot(pr.astype(BF16), v_ref[0, :, sl], preferred_element_type=F32).astype(o_ref.dtype)


def xattn_prompt(xq, mk, mv, tq_cap=512):
    bsz, t, d = xq.shape
    mt = mk.shape[1]
    tq = _row_tile(t, tq_cap)
    return pl.pallas_call(
        functools.partial(_xattn_prompt_kernel, dh=d // X_HEADS), grid=(bsz, t // tq),
        in_specs=[pl.BlockSpec((1, tq, d), lambda b, i: (b, i, 0)), pl.BlockSpec((1, mt, d), lambda b, i: (b, 0, 0)),
                  pl.BlockSpec((1, mt, d), lambda b, i: (b, 0, 0))],
        out_specs=pl.BlockSpec((1, tq, d), lambda b, i: (b, i, 0)),
        out_shape=jax.ShapeDtypeStruct((bsz, t, d), BF16),
        compiler_params=_cp("parallel", "parallel"), name="xattn_prompt")(xq, mk, mv)


def _xattn_step_kernel(q_ref, k_ref, v_ref, o_ref, *, sb, mt, dh):
    scale = dh ** -0.5
    parts = dh // LANES
    rows = X_HEADS * parts

    def seq_body(s, carry):
        rs = pl.ds(s, 1)
        qrow = _as_mxu_operand(q_ref[rs, :])
        pieces = []
        for h in range(X_HEADS):
            lg = jnp.zeros((mt, 1), F32)
            for c in range(parts):
                qp = qrow[:, h * dh + c * LANES:h * dh + (c + 1) * LANES]
                kp = _as_mxu_operand(k_ref[0, s, pl.ds(c * X_HEADS + h, mt, stride=rows), :])
                lg = lg + jnp.sum(kp * qp, axis=1, keepdims=True)
            lg = lg * scale
            e = jnp.exp(lg - jnp.max(lg, axis=0, keepdims=True))
            pr = _as_mxu_operand(e / jnp.sum(e, axis=0, keepdims=True))
            for c in range(parts):
                vp = _as_mxu_operand(v_ref[0, s, pl.ds(c * X_HEADS + h, mt, stride=rows), :])
                pieces.append(jnp.sum(pr * vp, axis=0, keepdims=True))
        o_ref[rs, :] = jnp.concatenate(pieces, axis=1)
        return carry

    lax.fori_loop(0, sb, seq_body, 0)


def _lane_chunk_view(mem):
    nl, s, mt, nh, dh = mem.shape
    v = mem.reshape(nl, s, mt, nh, dh // LANES, LANES)
    return jnp.transpose(v, (0, 1, 2, 4, 3, 5)).reshape(nl, s, mt * (dh // LANES) * nh, LANES)


def xattn_step(xq, mk_view, mv_view, layer, mt, sb=SUBLANES):
    s, d = xq.shape
    dh = d // X_HEADS
    assert s % sb == 0 and dh % LANES == 0
    nrows = mk_view.shape[2]
    mem_spec = pl.BlockSpec((1, sb, nrows, LANES), lambda i: (layer, i, 0, 0))
    return pl.pallas_call(
        functools.partial(_xattn_step_kernel, sb=sb, mt=mt, dh=dh), grid=(s // sb,),
        in_specs=[pl.BlockSpec((sb, d), lambda i: (i, 0)), mem_spec, mem_spec],
        out_specs=pl.BlockSpec((sb, d), lambda i: (i, 0)),
        out_shape=jax.ShapeDtypeStruct((s, d), F32),
        compiler_params=_cp("parallel"), name="xattn_step")(xq, mk_view, mv_view)


def _t5_bucket(dist):
    max_exact = REL_BUCKETS // 2
    d = jnp.maximum(dist.astype(F32), 1.0)
    large = max_exact + (jnp.log(d / max_exact) / math.log(REL_MAX_DIST / max_exact)
                         * (REL_BUCKETS - max_exact)).astype(I32)
    large = jnp.minimum(large, REL_BUCKETS - 1)
    return jnp.where(dist < max_exact, dist, large)


def _pad_cols(w, n):
    return jnp.pad(w, ((0, 0), (0, n - w.shape[1])))


def _block_diag(w):
    nb, di, do = w.shape
    eye = jnp.eye(nb, dtype=w.dtype)
    return jnp.einsum('nde,nm->ndme', w, eye).reshape(nb * di, nb * do)


def kernel(x_prompt, x_sample, cache_k, cache_v, cache_kidx, state_rglru_h, state_rglru_conv, state_mlstm_c,
           state_mlstm_n, state_mlstm_m, cache_mem_k, cache_mem_v, page_table, mem_prompt, ln_g, ln_b, rel_bias,
           w_in_even, w_out_even, rg_conv_w, rg_conv_b, rg_w_a, rg_b_a, rg_w_i, rg_b_i, rg_lambda, w_in_odd,
           b_gate_odd, ml_norm_g, w_out_odd, w_xq, w_xk, w_xv, w_xo, ffn_w_gate, ffn_w_up, ffn_w_down,
           moe_router, moe_w_gate, moe_w_up, moe_w_down):
    bsz, t, d = x_prompt.shape
    s = x_sample.shape[0]
    depth = ln_g.shape[0]
    n_pages = page_table.shape[1]
    past_len = n_pages * PAGE_SIZE
    mt = mem_prompt.shape[1]
    rg_w = rg_lambda.shape[1]
    ml_w = ml_norm_g.shape[1]
    n_experts = moe_router.shape[2]
    alpha = (2.0 * depth) ** 0.25
    bf = lambda a: a.astype(BF16)

    assert REL_MAX_DIST <= TQ
    rel32 = rel_bias.astype(F32)
    by_dist = rel32[_t5_bucket(jnp.arange(2 * TQ + 1, dtype=I32))]

    def spread(dist):
        out = jnp.broadcast_to(by_dist[2 * TQ], dist.shape + (A_HEADS,))
        for dd in range(2 * TQ):
            out = jnp.where((dist == dd)[..., None], by_dist[dd], out)
        return out

    kk_ = lax.broadcasted_iota(I32, (TQ, TQ), 0)
    qq_ = lax.broadcasted_iota(I32, (TQ, TQ), 1)
    bias_tab = jnp.stack([spread(jnp.maximum(qq_ - kk_, 0)), spread(TQ + qq_ - kk_),
                          spread(jnp.full((TQ, TQ), 2 * TQ, I32))])
    bias_tab = jnp.transpose(bias_tab, (0, 3, 1, 2))
    dist_s = (past_len - jnp.arange(past_len, dtype=I32)).reshape(n_pages, PAGE_SIZE)
    bias_s = jnp.transpose(rel32[_t5_bucket(dist_s)], (0, 2, 1))
    bias_s_new = jnp.broadcast_to(rel32[_t5_bucket(jnp.zeros((1,), I32))].reshape(A_HEADS, 1), (A_HEADS, LANES))
    mem_k_view, mem_v_view = _lane_chunk_view(cache_mem_k), _lane_chunk_view(cache_mem_v)

    mem_bf = bf(mem_prompt.reshape(bsz * mt, d))
    kv_w = [bf(w_xk[l]) for l in range(depth)] + [bf(w_xv[l]) for l in range(depth)]
    kv = multi_proj(mem_bf, kv_w, [(F32, BF16)] * (2 * depth))
    mem_k_f32 = [kv[2 * l] for l in range(depth)]
    mem_k_bf = [kv[2 * l + 1].reshape(bsz, mt, d) for l in range(depth)]
    mem_v_f32 = [kv[2 * depth + 2 * l] for l in range(depth)]
    mem_v_bf = [kv[2 * depth + 2 * l + 1].reshape(bsz, mt, d) for l in range(depth)]

    xp = x_prompt.reshape(bsz * t, d)
    xs = x_sample.reshape(s, d)
    xp_bf, xs_bf = bf(xp), bf(xs)
    outs = {}
    k_p, v_p, ki_p, k_s, v_s, ki_s = [], [], [], [], [], []
    rgh_p, rgc_p, rgh_s, rgc_s = [], [], [], []
    mlc_p, mln_p, mlm_p, mlc_s, mln_s, mlm_s = [], [], [], [], [], []

    for l in range(depth):
        wq, wo = bf(w_xq[l]), bf(w_xo[l])
        if l % 2 == 0:
            e = l // 2
            w_in = w_in_even[e]
            o = 0
            cols = {}
            for name, width in (("q", A_WIDTH), ("k", A_WIDTH), ("v", A_WIDTH), ("qi", IDX_HEADS * IDX_DIM),
                                ("ki", IDX_DIM), ("wi", IDX_HEADS), ("xr", rg_w), ("gr", rg_w)):
                cols[name] = bf(w_in[:, o:o + width])
                o += width
            cols["wi"] = _pad_cols(cols["wi"], LANES)
            names = ("q", "k", "v", "qi", "ki", "wi", "xr", "gr")
            wq_t, wv_t, wqi_t = cols["q"].T, cols["v"].T, cols["qi"].T
            wi_t = jnp.pad(cols["wi"][:, :IDX_HEADS].T, ((0, SUBLANES - IDX_HEADS), (0, 0)))
            wa = bf(_block_diag(rg_w_a[e]))
            wi_g = bf(_block_diag(rg_w_i[e]))
            w_out = bf(w_out_even[e])

            nq = t // TQ
            pnames = ("k", "v", "ki", "xr", "gr")
            kf, kb, vf, kif, kib, xr, gr, qt, vt, qit, wit = multi_proj(
                xp_bf, [cols[n] for n in pnames] + [wq_t, wv_t, wqi_t, wi_t],
                [(F32, BF16), (F32,), (F32, BF16), (F32,), (F32,), (BF16,), (BF16,), (BF16,), (F32,)],
                transposed=(False,) * len(pnames) + (True,) * 4)
            r3 = lambda a: a.reshape(bsz, t, a.shape[-1])
            r4 = lambda a: a.reshape(bsz, nq, a.shape[-2], TQ)
            att = dsa_prompt(r4(qt), r3(kb), r4(vt), r4(qit), r3(kib), r4(wit), bias_tab)
            rg_y, h_last = rglru_prompt(r3(xr), r3(gr), rg_conv_w[e], rg_conv_b[e], wa, rg_b_a[e], wi_g, rg_b_i[e],
                                        rg_lambda[e])
            xp, xp_bf, xq_p = mm_res_ln([att.reshape(bsz * t, A_WIDTH), rg_y.reshape(bsz * t, rg_w)],
                                        [w_out[:A_WIDTH], w_out[A_WIDTH:]], xp, ln_g[l, 0], ln_b[l, 0], alpha,
                                        next_ws=[wq])
            k_p.append(kf.reshape(bsz, t, A_HEADS, A_HEAD_DIM))
            v_p.append(vf.reshape(bsz, t, A_HEADS, A_HEAD_DIM))
            ki_p.append(kif.reshape(bsz, t, IDX_DIM))
            rgh_p.append(h_last.reshape(bsz, rg_w))
            rgc_p.append(r3(xr)[:, t - (RG_CONV - 1):, :])

            q, kf, vf, qi, kif, wi, xr, gr = multi_proj(xs_bf, [cols[n] for n in names], [(F32,)] * len(names))
            qi8 = jnp.pad(qi.reshape(s, IDX_HEADS, IDX_DIM), ((0, 0), (0, SUBLANES - IDX_HEADS), (0, 0)))
            pool = cache_k.shape[1]
            kidx_t = jnp.transpose(cache_kidx[e], (0, 2, 1))
            k_t = jnp.transpose(cache_k[e], (0, 2, 3, 1)).reshape(pool, A_WIDTH, PAGE_SIZE)
            v_t = jnp.transpose(cache_v[e], (0, 2, 3, 1)).reshape(pool, A_WIDTH, PAGE_SIZE)
            raw = dsa_sample_scores(page_table, qi8, wi.reshape(s, 1, LANES), kidx_t)
            mask, mask_new = dsa_sample_select(raw.reshape(s, past_len), qi, kif, wi)
            hd3 = lambda a: a.reshape(s, A_HEADS, A_HEAD_DIM)
            att = dsa_sample_attn(page_table, hd3(q), hd3(kf), hd3(vf), mask.reshape(s, n_pages, PAGE_SIZE),
                                  mask_new.reshape(s, 1, LANES), bias_s, bias_s_new, k_t, v_t)
            rg_y, h_new = rglru_step(xr, gr, state_rglru_conv[e], state_rglru_h[e], rg_conv_w[e], rg_conv_b[e],
                                     wa, rg_b_a[e], wi_g, rg_b_i[e], rg_lambda[e])
            xs, xs_bf = mm_res_ln([bf(att.reshape(s, A_WIDTH)), bf(rg_y)], [w_out[:A_WIDTH], w_out[A_WIDTH:]], xs,
                                  ln_g[l, 0], ln_b[l, 0], alpha)
            k_s.append(kf.reshape(s, 1, A_HEADS, A_HEAD_DIM))
            v_s.append(vf.reshape(s, 1, A_HEADS, A_HEAD_DIM))
            ki_s.append(kif.reshape(s, 1, IDX_DIM))
            rgh_s.append(h_new)
            rgc_s.append(jnp.concatenate([state_rglru_conv[e][:, 1:], xr[:, None, :]], axis=1))
        else:
            o_ = l // 2
            w_in = w_in_odd[o_]
            ws = [bf(w_in[:, i * ml_w:(i + 1) * ml_w]) for i in range(4)]
            ws.append(_pad_cols(bf(w_in[:, 4 * ml_w:]), LANES))
            bgate = jnp.pad(b_gate_odd[o_].astype(F32), (0, LANES - 2 * ML_HEADS)).reshape(1, LANES)
            w_out = bf(w_out_odd[o_])

            q, k, v, og, gates = multi_proj(xp_bf, ws, [(BF16,), (BF16,), (BF16,), (F32,), (F32,)])
            r3 = lambda a: a.reshape(bsz, t, a.shape[-1])
            hn, c, n, m = mlstm_prompt(r3(q), r3(k), r3(v), r3(og), r3(gates), bgate, ml_norm_g[o_])
            xp, xp_bf, xq_p = mm_res_ln([hn.reshape(bsz * t, ml_w)], [w_out], xp, ln_g[l, 0], ln_b[l, 0], alpha,
                                        next_ws=[wq])
            mlc_p.append(c)
            mln_p.append(n[:, :ML_HEADS])
            mlm_p.append(m[:, :ML_HEADS, 0])

            q, k, v, og, gates = multi_proj(xs_bf, ws, [(F32,)] * len(ws))
            hn, c, n, m = mlstm_step(q, k, v, og, gates, bgate, ml_norm_g[o_], state_mlstm_c[o_], state_mlstm_n[o_],
                                     state_mlstm_m[o_])
            xs, xs_bf = mm_res_ln([bf(hn)], [w_out], xs, ln_g[l, 0], ln_b[l, 0], alpha)
            mlc_s.append(c)
            mln_s.append(n)
            mlm_s.append(m)

        xo = xattn_prompt(xq_p.reshape(bsz, t, d), mem_k_bf[l], mem_v_bf[l])
        xp, xp_bf = mm_res_ln([xo.reshape(bsz * t, d)], [wo], xp, ln_g[l, 1], ln_b[l, 1], alpha)
        (xq,) = multi_proj(xs_bf, [wq], [(F32,)])
        xo = xattn_step(xq, mem_k_view, mem_v_view, l, mt)
        xs, xs_bf = mm_res_ln([bf(xo)], [wo], xs, ln_g[l, 1], ln_b[l, 1], alpha)

        if l % 2 == 0:
            e = l // 2
            wg, wu, wd = bf(ffn_w_gate[e]), bf(ffn_w_up[e]), bf(ffn_w_down[e])
            xp, xp_bf = ffn_ln(xp_bf, xp, wg, wu, wd, ln_g[l, 2], ln_b[l, 2], alpha)
            xs, xs_bf = ffn_ln(xs_bf, xs, wg, wu, wd, ln_g[l, 2], ln_b[l, 2], alpha)
        else:
            o_ = l // 2
            wr = _pad_cols(bf(moe_router[o_]), LANES)
            wg, wu, wd = bf(moe_w_gate[o_]), bf(moe_w_up[o_]), bf(moe_w_down[o_])
            xp, xp_bf = moe_top2_ln(xp_bf, xp, wr, wg, wu, wd, ln_g[l, 2], ln_b[l, 2], alpha)
            gs = router_gates(xs_bf, wr, n_experts)
            xs, xs_bf = moe_ln(xs_bf, xs, gs, wg, wu, wd, ln_g[l, 2], ln_b[l, 2], alpha)

    xh = mem_k_f32[0].shape[-1] // X_HEADS
    mem_shape = (bsz, mt, X_HEADS, xh)
    return (xp.reshape(bsz, t, d), xs.reshape(s, 1, d),
            jnp.stack(k_p), jnp.stack(v_p), jnp.stack(ki_p), jnp.stack(k_s), jnp.stack(v_s), jnp.stack(ki_s),
            jnp.stack(rgh_p), jnp.stack(rgc_p), jnp.stack(rgh_s), jnp.stack(rgc_s),
            jnp.stack(mlc_p), jnp.stack(mln_p), jnp.stack(mlm_p), jnp.stack(mlc_s), jnp.stack(mln_s), jnp.stack(mlm_s),
            jnp.stack([a.reshape(mem_shape) for a in mem_k_f32]), jnp.stack([a.reshape(mem_shape) for a in mem_v_f32]))
```

```python
import functools
import math

import jax
import jax.numpy as jnp
from jax import lax
from jax.experimental import pallas as pl
from jax.experimental.pallas import tpu as pltpu

F32 = jnp.float32
BF16 = jnp.bfloat16
I32 = jnp.int32

PAGE_SIZE = 128
A_HEADS = 8
A_HEAD_DIM = 64
A_WIDTH = A_HEADS * A_HEAD_DIM
IDX_HEADS = 4
IDX_DIM = 64
TOPK_MAX = 256
REL_BUCKETS = 32
REL_MAX_DIST = 128
RG_BLOCKS = 8
RG_CONV = 4
RG_C = 8.0
ML_HEADS = 4
ML_CHUNK = 128
X_HEADS = 4
TOP_K = 2
LN_EPS = 1e-5

LANES = 128
SUBLANES = 8
VMEM_LIMIT = 48 * 1024 * 1024
VMEM_LIMIT_RESIDENT = 56 * 1024 * 1024
NEG = -1e30
INT_MIN = -2 ** 31
TQ = 128

NT = (((1,), (1,)), ((), ()))
TN = (((0,), (0,)), ((), ()))
HI = lax.Precision.HIGHEST


def _cp(*sem):
    return pltpu.CompilerParams(dimension_semantics=sem, vmem_limit_bytes=VMEM_LIMIT)


def _full(shape):
    n = len(shape)
    return pl.BlockSpec(shape, lambda *_: (0,) * n)


def _row_tile(m, cap):
    t = min(m, cap)
    assert m % t == 0
    return t


def _as_mxu_operand(a):
    return a.astype(BF16).astype(F32)


def _mm(a, b, dims=None):
    prec = HI if a.dtype == F32 else None
    if dims is None:
        return jnp.dot(a, b, preferred_element_type=F32, precision=prec)
    return lax.dot_general(a, b, dims, preferred_element_type=F32, precision=prec)


def _proj_kernel(x_ref, *refs, n_w, out_counts, transposed):
    w_refs, o_refs = refs[:n_w], refs[n_w:]
    x = x_ref[...]
    k = 0
    for w_ref, cnt, tr in zip(w_refs, out_counts, transposed):
        if tr:
            y = _mm(w_ref[...], x, NT)
        else:
            y = _mm(x, w_ref[...])
        for _ in range(cnt):
            if tr:
                for j in range(y.shape[1] // LANES):
                    o_refs[k][j] = y[:, j * LANES:(j + 1) * LANES].astype(o_refs[k].dtype)
            else:
                o_refs[k][...] = y.astype(o_refs[k].dtype)
            k += 1


def multi_proj(x, ws, out_dtypes, transposed=None, tm_cap=512):
    m, kdim = x.shape
    tm = _row_tile(m, tm_cap)
    transposed = transposed or (False,) * len(ws)
    in_specs = [pl.BlockSpec((tm, kdim), lambda i: (i, 0))] + [_full(w.shape) for w in ws]
    out_shape, out_specs = [], []
    for w, dts, tr in zip(ws, out_dtypes, transposed):
        for dt in dts:
            if tr:
                assert tm % LANES == 0
                out_shape.append(jax.ShapeDtypeStruct((m // LANES, w.shape[0], LANES), dt))
                out_specs.append(pl.BlockSpec((tm // LANES, w.shape[0], LANES), lambda i: (i, 0, 0)))
            else:
                out_shape.append(jax.ShapeDtypeStruct((m, w.shape[1]), dt))
                out_specs.append(pl.BlockSpec((tm, w.shape[1]), lambda i: (i, 0)))
    kern = functools.partial(_proj_kernel, n_w=len(ws), out_counts=tuple(len(d) for d in out_dtypes),
                             transposed=tuple(transposed))
    return pl.pallas_call(kern, grid=(m // tm,), in_specs=in_specs, out_specs=out_specs,
                          out_shape=out_shape, compiler_params=_cp("parallel"), name="multi_proj")(x, *ws)


def _layernorm_rows(z, g, b):
    mu = jnp.mean(z, axis=-1, keepdims=True)
    zc = z - mu
    var = jnp.mean(zc * zc, axis=-1, keepdims=True)
    return zc * lax.rsqrt(var + LN_EPS) * g + b


def _mm_ln_kernel(*refs, n_a, n_next, alpha):
    a_refs, w_refs = refs[:n_a], refs[n_a:2 * n_a]
    r_ref, g_ref, b_ref = refs[2 * n_a:2 * n_a + 3]
    next_w_refs = refs[2 * n_a + 3:2 * n_a + 3 + n_next]
    o_ref, obf_ref = refs[2 * n_a + 3 + n_next:2 * n_a + 5 + n_next]
    next_o_refs = refs[2 * n_a + 5 + n_next:]
    y = alpha * r_ref[...]
    for a_ref, w_ref in zip(a_refs, w_refs):
        y = y + _mm(a_ref[...], w_ref[...])
    out = _layernorm_rows(y, g_ref[...], b_ref[...])
    o_ref[...] = out
    out_bf = out.astype(BF16)
    obf_ref[...] = out_bf
    for w_ref, n_ref in zip(next_w_refs, next_o_refs):
        n_ref[...] = _mm(out_bf, w_ref[...]).astype(n_ref.dtype)


def mm_res_ln(a_list, w_list, resid, g, b, alpha, next_ws=(), tm_cap=512):
    m, d = resid.shape
    tm = _row_tile(m, tm_cap)
    rows = lambda w: pl.BlockSpec((tm, w), lambda i: (i, 0))
    in_specs = ([rows(a.shape[1]) for a in a_list] + [_full(w.shape) for w in w_list]
                + [rows(d), _full((1, d)), _full((1, d))] + [_full(w.shape) for w in next_ws])
    out_specs = [rows(d), rows(d)] + [rows(w.shape[1]) for w in next_ws]
    out_shape = ([jax.ShapeDtypeStruct((m, d), F32), jax.ShapeDtypeStruct((m, d), BF16)]
                 + [jax.ShapeDtypeStruct((m, w.shape[1]), BF16) for w in next_ws])
    kern = functools.partial(_mm_ln_kernel, n_a=len(a_list), n_next=len(next_ws), alpha=alpha)
    return pl.pallas_call(kern, grid=(m // tm,), in_specs=in_specs, out_specs=out_specs, out_shape=out_shape,
                          compiler_params=_cp("parallel"), name="mm_res_ln")(
                              *a_list, *w_list, resid, g.reshape(1, d), b.reshape(1, d), *next_ws)


def _ffn_kernel(x_ref, r_ref, wg_ref, wu_ref, wd_ref, g_ref, b_ref, o_ref, obf_ref, acc_sc, *, alpha):
    f = pl.program_id(1)

    @pl.when(f == 0)
    def _():
        acc_sc[...] = alpha * r_ref[...]

    x = x_ref[...]
    hg = _mm(x, wg_ref[...])
    hu = _mm(x, wu_ref[...])
    h = (jax.nn.silu(hg) * hu).astype(x.dtype)
    acc_sc[...] += _mm(h, wd_ref[...])

    @pl.when(f == pl.num_programs(1) - 1)
    def _():
        out = _layernorm_rows(acc_sc[...], g_ref[...], b_ref[...])
        o_ref[...] = out
        obf_ref[...] = out.astype(BF16)


def ffn_ln(x_bf, resid, wg, wu, wd, g, b, alpha, tm_cap=512, tf_cap=1408):
    m, d = resid.shape
    dff = wg.shape[1]
    tm = _row_tile(m, tm_cap)
    tf = _ff_tile(dff, tf_cap)
    in_specs = [pl.BlockSpec((tm, d), lambda i, f: (i, 0)), pl.BlockSpec((tm, d), lambda i, f: (i, 0)),
                pl.BlockSpec((d, tf), lambda i, f: (0, f)), pl.BlockSpec((d, tf), lambda i, f: (0, f)),
                pl.BlockSpec((tf, d), lambda i, f: (f, 0)), _full((1, d)), _full((1, d))]
    out_specs = [pl.BlockSpec((tm, d), lambda i, f: (i, 0))] * 2
    out_shape = [jax.ShapeDtypeStruct((m, d), F32), jax.ShapeDtypeStruct((m, d), BF16)]
    return pl.pallas_call(functools.partial(_ffn_kernel, alpha=alpha), grid=(m // tm, dff // tf),
                          in_specs=in_specs, out_specs=out_specs, out_shape=out_shape,
                          scratch_shapes=[pltpu.VMEM((tm, d), F32)],
                          compiler_params=_cp("parallel", "arbitrary"), name="ffn_ln")(
                              x_bf, resid, wg, wu, wd, g.reshape(1, d), b.reshape(1, d))


def _top2(logits, n_experts):
    lane = lax.broadcasted_iota(I32, logits.shape, 1)
    valid = lane < n_experts
    logits = jnp.where(valid, logits, NEG)
    mx = jnp.max(logits, axis=-1, keepdims=True)
    e = jnp.where(valid, jnp.exp(logits - mx), 0.0)
    p = e / jnp.sum(e, axis=-1, keepdims=True)
    p = jnp.where(valid, p, -1.0)
    big = jnp.int32(LANES)
    m1 = jnp.max(p, axis=-1, keepdims=True)
    i1 = jnp.min(jnp.where(p == m1, lane, big), axis=-1, keepdims=True)
    p2 = jnp.where(lane == i1, -1.0, p)
    m2 = jnp.max(p2, axis=-1, keepdims=True)
    i2 = jnp.min(jnp.where(p2 == m2, lane, big), axis=-1, keepdims=True)
    tot = m1 + m2
    return lane, i1, i2, m1 / tot, m2 / tot


def _router_kernel(x_ref, w_ref, o_ref, *, n_experts):
    lane, i1, i2, g1, g2 = _top2(_mm(x_ref[...], w_ref[...]), n_experts)
    o_ref[...] = jnp.where(lane == i1, g1, jnp.where(lane == i2, g2, 0.0))


def router_gates(x, w_pad, n_experts, tm_cap=512):
    m, d = x.shape
    tm = _row_tile(m, tm_cap)
    return pl.pallas_call(functools.partial(_router_kernel, n_experts=n_experts), grid=(m // tm,),
                          in_specs=[pl.BlockSpec((tm, d), lambda i: (i, 0)), _full(w_pad.shape)],
                          out_specs=pl.BlockSpec((tm, LANES), lambda i: (i, 0)),
                          out_shape=jax.ShapeDtypeStruct((m, LANES), F32),
                          compiler_params=_cp("parallel"), name="router")(x, w_pad)


def _moe_kernel(x_ref, r_ref, gt_ref, wg_ref, wu_ref, wd_ref, g_ref, b_ref, o_ref, obf_ref, acc_sc, *, alpha):
    e = pl.program_id(1)
    f = pl.program_id(2)

    @pl.when((e == 0) & (f == 0))
    def _():
        acc_sc[...] = alpha * r_ref[...]

    gt = gt_ref[...]
    lane = lax.broadcasted_iota(I32, gt.shape, 1)
    ge = jnp.sum(jnp.where(lane == e, gt, 0.0), axis=-1, keepdims=True)
    x = x_ref[...]
    hg = jnp.dot(x, wg_ref[0], preferred_element_type=F32)
    hu = jnp.dot(x, wu_ref[0], preferred_element_type=F32)
    h = (jax.nn.silu(hg) * hu).astype(BF16)
    acc_sc[...] += ge * jnp.dot(h, wd_ref[0], preferred_element_type=F32)

    @pl.when((e == pl.num_programs(1) - 1) & (f == pl.num_programs(2) - 1))
    def _():
        out = _layernorm_rows(acc_sc[...], g_ref[...], b_ref[...])
        o_ref[...] = out
        obf_ref[...] = out.astype(BF16)


def moe_ln(x_bf, resid, gates, wg, wu, wd, g, b, alpha, tm_cap=512, tf=512):
    m, d = resid.shape
    ne, _, dff = wg.shape
    tm = _row_tile(m, tm_cap)
    assert dff % tf == 0
    in_specs = [pl.BlockSpec((tm, d), lambda i, e, f: (i, 0)), pl.BlockSpec((tm, d), lambda i, e, f: (i, 0)),
                pl.BlockSpec((tm, LANES), lambda i, e, f: (i, 0)),
                pl.BlockSpec((1, d, tf), lambda i, e, f: (e, 0, f)), pl.BlockSpec((1, d, tf), lambda i, e, f: (e, 0, f)),
                pl.BlockSpec((1, tf, d), lambda i, e, f: (e, f, 0)), _full((1, d)), _full((1, d))]
    out_specs = [pl.BlockSpec((tm, d), lambda i, e, f: (i, 0))] * 2
    out_shape = [jax.ShapeDtypeStruct((m, d), F32), jax.ShapeDtypeStruct((m, d), BF16)]
    return pl.pallas_call(functools.partial(_moe_kernel, alpha=alpha), grid=(m // tm, ne, dff // tf),
                          in_specs=in_specs, out_specs=out_specs, out_shape=out_shape,
                          scratch_shapes=[pltpu.VMEM((tm, d), F32)],
                          compiler_params=_cp("parallel", "arbitrary", "arbitrary"), name="moe_ln")(
                              x_bf, resid, gates, wg, wu, wd, g.reshape(1, d), b.reshape(1, d))


MOE_TILE = 512
COMBINE_SLACK = LANES
ROUTE_LANES = ("e1", "e2", "g1", "g2", "rank1", "rank2")


def _router_plan_kernel(x_ref, w_ref, ltri_ref, route_ref, before_ref, count_ref, carry_sc, *, n_experts):
    @pl.when(pl.program_id(0) == 0)
    def _():
        carry_sc[...] = jnp.zeros(carry_sc.shape, F32)

    lane, i1, i2, g1, g2 = _top2(_mm(x_ref[...], w_ref[...]), n_experts)
    chosen = jnp.where(lane == i1, 1.0, jnp.where(lane == i2, 1.0, 0.0))
    before = carry_sc[0:1, :]
    rank = before + jnp.dot(ltri_ref[...], chosen.astype(BF16), preferred_element_type=F32)
    r1 = jnp.sum(jnp.where(lane == i1, rank, 0.0), axis=-1, keepdims=True)
    r2 = jnp.sum(jnp.where(lane == i2, rank, 0.0), axis=-1, keepdims=True)
    vals = (i1.astype(F32), i2.astype(F32), g1, g2, r1, r2)
    route = jnp.zeros(lane.shape, F32)
    for k, v in enumerate(vals):
        route = jnp.where(lane == k, v, route)
    route_ref[...] = route
    before_ref[0] = jnp.broadcast_to(before, before_ref.shape[1:])
    after = before + jnp.sum(chosen, axis=0, keepdims=True)
    carry_sc[0:1, :] = after
    count_ref[...] = jnp.broadcast_to(after, count_ref.shape)


def router_plan(x_bf, w_pad, n_experts):
    m, d = x_bf.shape
    tb = MOE_TILE
    assert m % tb == 0
    ltri = (lax.broadcasted_iota(I32, (tb, tb), 1) < lax.broadcasted_iota(I32, (tb, tb), 0)).astype(BF16)
    return pl.pallas_call(
        functools.partial(_router_plan_kernel, n_experts=n_experts), grid=(m // tb,),
        in_specs=[pl.BlockSpec((tb, d), lambda i: (i, 0)), _full(w_pad.shape), _full((tb, tb))],
        out_specs=[pl.BlockSpec((tb, LANES), lambda i: (i, 0)), pl.BlockSpec((1, SUBLANES, LANES), lambda i: (i, 0, 0)),
                   _full((SUBLANES, LANES))],
        out_shape=[jax.ShapeDtypeStruct((m, LANES), F32), jax.ShapeDtypeStruct((m // tb, SUBLANES, LANES), F32),
                   jax.ShapeDtypeStruct((SUBLANES, LANES), F32)],
        scratch_shapes=[pltpu.VMEM((SUBLANES, LANES), F32)],
        compiler_params=_cp("arbitrary"), name="router_plan")(x_bf, w_pad, ltri)


def _moe_gather_kernel(blo_ref, bhi_ref, x_ref, pos_ref, g_ref, xs_ref, gs_ref, acc_sc, gacc_sc, *, tile):
    j = pl.program_id(0)
    acc_sc[...] = jnp.zeros(acc_sc.shape, F32)
    gacc_sc[...] = jnp.zeros(gacc_sc.shape, F32)
    rid = j * tile + lax.broadcasted_iota(I32, (tile, tile), 0)
    glane = lax.broadcasted_iota(I32, gacc_sc.shape, 1)

    def body(b, carry):
        off = pl.multiple_of(b * tile, tile)
        m1 = jnp.where(pos_ref[b, 0:1, :] == rid, 1.0, 0.0).astype(BF16)
        m2 = jnp.where(pos_ref[b, 1:2, :] == rid, 1.0, 0.0).astype(BF16)
        acc_sc[...] += jnp.dot(m1 + m2, x_ref[pl.ds(off, tile), :], preferred_element_type=F32)
        gb = g_ref[pl.ds(off, tile), :]
        gacc_sc[...] += jnp.where(glane < 3, jnp.dot(m1, gb, preferred_element_type=F32),
                                  jnp.dot(m2, gb, preferred_element_type=F32))
        return carry

    lax.fori_loop(blo_ref[j], bhi_ref[j] + 1, body, 0)
    xs_ref[...] = acc_sc[...].astype(xs_ref.dtype)
    gs_ref[...] = gacc_sc[...]


def moe_gather(x_bf, pos, gsplit, b_lo, b_hi, n_tiles):
    m, d = x_bf.shape
    tile = MOE_TILE
    resident = lambda shape: pl.BlockSpec(shape, lambda j, lo, hi: (0,) * len(shape), pipeline_mode=pl.Buffered(1))
    grid_spec = pltpu.PrefetchScalarGridSpec(
        num_scalar_prefetch=2, grid=(n_tiles,),
        in_specs=[resident((m, d)), resident(pos.shape), resident(gsplit.shape)],
        out_specs=[pl.BlockSpec((tile, d), lambda j, lo, hi: (j, 0)),
                   pl.BlockSpec((tile, LANES), lambda j, lo, hi: (j, 0))],
        scratch_shapes=[pltpu.VMEM((tile, d), F32), pltpu.VMEM((tile, LANES), F32)])
    return pl.pallas_call(
        functools.partial(_moe_gather_kernel, tile=tile), grid_spec=grid_spec,
        out_shape=[jax.ShapeDtypeStruct((n_tiles * tile, d), BF16),
                   jax.ShapeDtypeStruct((n_tiles * tile, LANES), F32)],
        compiler_params=pltpu.CompilerParams(dimension_semantics=("arbitrary",), vmem_limit_bytes=VMEM_LIMIT_RESIDENT),
        name="moe_gather")(b_lo, b_hi, x_bf, pos, gsplit)


def _moe_group_ffn_kernel(te_ref, nv_ref, x_ref, gs_ref, wg_ref, wu_ref, wd_ref, o_ref, acc_sc):
    j = pl.program_id(0)
    f = pl.program_id(1)

    @pl.when(f == 0)
    def _():
        acc_sc[...] = jnp.zeros(acc_sc.shape, F32)

    @pl.when(j < nv_ref[0])
    def _():
        x = x_ref[...]
        hg = jnp.dot(x, wg_ref[0], preferred_element_type=F32)
        hu = jnp.dot(x, wu_ref[0], preferred_element_type=F32)
        h = (jax.nn.silu(hg) * hu).astype(BF16)
        acc_sc[...] += jnp.dot(h, wd_ref[0], preferred_element_type=F32)

    @pl.when(f == pl.num_programs(1) - 1)
    def _():
        gs = gs_ref[...]
        lane = lax.broadcasted_iota(I32, gs.shape, 1)
        gate = jnp.sum(jnp.where(lane < 6, gs, 0.0), axis=-1, keepdims=True)
        o_ref[...] = acc_sc[...] * gate


def _ff_tile(dff, cap):
    return max(t for t in range(LANES, min(dff, cap) + 1, LANES) if dff % t == 0)


def moe_group_ffn(tile_expert, n_valid, x_sorted, gterms, wg, wu, wd, n_out_tiles, tf_cap=1792):
    tile = MOE_TILE
    n_tiles = x_sorted.shape[0] // tile
    d = x_sorted.shape[1]
    dff = wg.shape[2]
    tf = _ff_tile(dff, tf_cap)
    src = lambda j: jnp.minimum(j, n_tiles - 1)
    grid_spec = pltpu.PrefetchScalarGridSpec(
        num_scalar_prefetch=2, grid=(n_out_tiles, dff // tf),
        in_specs=[pl.BlockSpec((tile, d), lambda j, f, te, nv: (src(j), 0)),
                  pl.BlockSpec((tile, LANES), lambda j, f, te, nv: (src(j), 0)),
                  pl.BlockSpec((1, d, tf), lambda j, f, te, nv: (te[src(j)], 0, f)),
                  pl.BlockSpec((1, d, tf), lambda j, f, te, nv: (te[src(j)], 0, f)),
                  pl.BlockSpec((1, tf, d), lambda j, f, te, nv: (te[src(j)], f, 0))],
        out_specs=pl.BlockSpec((tile, d), lambda j, f, te, nv: (j, 0)),
        scratch_shapes=[pltpu.VMEM((tile, d), F32)])
    return pl.pallas_call(_moe_group_ffn_kernel, grid_spec=grid_spec,
                          out_shape=jax.ShapeDtypeStruct((n_out_tiles * tile, d), F32),
                          compiler_params=_cp("arbitrary", "arbitrary"), name="moe_group_ffn")(
                              tile_expert, n_valid, x_sorted, gterms, wg, wu, wd)


def _moe_combine_kernel(rlo_ref, short_ref, route_ref, r_ref, os_hbm, g_ref, b_ref, o_ref, obf_ref, buf, sem, acc_sc,
                        *, n_experts, windows, alpha):
    i = pl.program_id(0)
    acc_sc[...] = alpha * r_ref[...]
    route = route_ref[...]
    e1 = route[:, 0:1].astype(I32)
    e2 = route[:, 1:2].astype(I32)
    p1 = route[:, 2:3].astype(I32)
    p2 = route[:, 3:4].astype(I32)

    def accumulate(window):
        def window_copy(e, slot):
            lo = pl.multiple_of((rlo_ref[i * n_experts + e] // SUBLANES) * SUBLANES, SUBLANES)
            return lo, pltpu.make_async_copy(os_hbm.at[pl.ds(lo, window), :], buf.at[slot, pl.ds(0, window), :],
                                             sem.at[slot])

        window_copy(0, 0)[1].start()
        col = lax.broadcasted_iota(I32, (route.shape[0], window), 1)
        for e in range(n_experts):
            slot = e % 2
            if e + 1 < n_experts:
                window_copy(e + 1, 1 - slot)[1].start()
            lo, cp = window_copy(e, slot)
            cp.wait()
            c1 = jnp.where(e1 == e, p1 - lo, -1)
            c2 = jnp.where(e2 == e, p2 - lo, -1)
            pick = jnp.where(col == c1, 1.0, jnp.where(col == c2, 1.0, 0.0)).astype(BF16)
            rows = buf[slot, pl.ds(0, window), :]
            hi = rows.astype(BF16)
            lo_part = (rows - hi.astype(F32)).astype(BF16)
            acc_sc[...] += (jnp.dot(pick, hi, preferred_element_type=F32)
                            + jnp.dot(pick, lo_part, preferred_element_type=F32))

    short, full = windows

    @pl.when(short_ref[i] == 1)
    def _():
        accumulate(short)

    @pl.when(short_ref[i] != 1)
    def _():
        accumulate(full)

    out = _layernorm_rows(acc_sc[...], g_ref[...], b_ref[...])
    o_ref[...] = out
    obf_ref[...] = out.astype(BF16)


def moe_combine_ln(row_lo, short, route_pos, resid, out_sorted, g, b, alpha, n_experts, windows):
    m, d = resid.shape
    tile = MOE_TILE
    grid_spec = pltpu.PrefetchScalarGridSpec(
        num_scalar_prefetch=2, grid=(m // tile,),
        in_specs=[pl.BlockSpec((tile, LANES), lambda i, rlo, sh: (i, 0)),
                  pl.BlockSpec((tile, d), lambda i, rlo, sh: (i, 0)),
                  pl.BlockSpec(memory_space=pl.ANY),
                  pl.BlockSpec((1, d), lambda i, rlo, sh: (0, 0)), pl.BlockSpec((1, d), lambda i, rlo, sh: (0, 0))],
        out_specs=[pl.BlockSpec((tile, d), lambda i, rlo, sh: (i, 0))] * 2,
        scratch_shapes=[pltpu.VMEM((2, max(windows), d), F32), pltpu.SemaphoreType.DMA((2,)),
                        pltpu.VMEM((tile, d), F32)])
    return pl.pallas_call(
        functools.partial(_moe_combine_kernel, n_experts=n_experts, windows=windows, alpha=alpha),
        grid_spec=grid_spec,
        out_shape=[jax.ShapeDtypeStruct((m, d), F32), jax.ShapeDtypeStruct((m, d), BF16)],
        compiler_params=_cp("arbitrary"), name="moe_combine_ln")(
            row_lo, short, route_pos, resid, out_sorted, g.reshape(1, d), b.reshape(1, d))


def moe_top2_ln(x_bf, resid, w_router, wg, wu, wd, g, b, alpha):
    m, d = resid.shape
    ne = wg.shape[0]
    tile = MOE_TILE
    nb = m // tile
    n_tiles = (TOP_K * m) // tile + ne
    tail_tiles = -(-(tile + COMBINE_SLACK) // tile)
    route, before, count = router_plan(x_bf, w_router, ne)

    eid = jnp.arange(ne, dtype=I32)
    lookup = lambda table, idx: jnp.sum(jnp.where(idx[..., None] == eid, table, 0), axis=-1)
    cnt = count[0, :ne].astype(I32)
    tiles_e = (cnt + tile - 1) // tile
    first_tile = jnp.cumsum(tiles_e) - tiles_e
    n_valid = jnp.sum(tiles_e)
    row_start = first_tile * tile
    tid = jnp.arange(n_tiles, dtype=I32)
    tile_expert = jnp.minimum(jnp.sum((tid[:, None] >= (first_tile + tiles_e)[None, :]).astype(I32), axis=1), ne - 1)
    rank0 = (tid - lookup(first_tile, tile_expert)) * tile
    bef = before[:, 0, :ne].astype(I32)
    aft = jnp.concatenate([bef[1:], cnt[None, :]], axis=0)
    bef_t = lookup(bef[:, None, :], tile_expert[None, :])
    aft_t = lookup(aft[:, None, :], tile_expert[None, :])
    valid_t = tid < n_valid
    b_lo = jnp.where(valid_t, jnp.sum((aft_t <= rank0[None, :]).astype(I32), axis=0), 0)
    b_hi = jnp.where(valid_t, jnp.sum((bef_t < (rank0 + tile)[None, :]).astype(I32), axis=0) - 1, -1)
    e1, e2 = route[:, 0].astype(I32), route[:, 1].astype(I32)
    pos1 = lookup(row_start, e1) + route[:, 4].astype(I32)
    pos2 = lookup(row_start, e2) + route[:, 5].astype(I32)
    pos = jnp.stack([pos1.reshape(nb, tile), pos2.reshape(nb, tile)]
                    + [jnp.zeros((nb, tile), I32)] * (SUBLANES - 2), axis=1)

    def three_terms(v):
        hi = v.astype(BF16)
        mid = (v - hi.astype(F32)).astype(BF16)
        low = (v - hi.astype(F32) - mid.astype(F32)).astype(BF16)
        return [hi, mid, low]

    gsplit = jnp.stack(three_terms(route[:, 2]) + three_terms(route[:, 3]), axis=1)
    gsplit = jnp.pad(gsplit, ((0, 0), (0, LANES - gsplit.shape[1])))
    route_pos = jnp.stack([route[:, 0], route[:, 1], pos1.astype(F32), pos2.astype(F32)], axis=1)
    route_pos = jnp.pad(route_pos, ((0, 0), (0, LANES - route_pos.shape[1])))
    row_lo = row_start[None, :] + bef
    windows = (tile // 2, tile + COMBINE_SLACK)
    short = jnp.all(row_lo % SUBLANES + (aft - bef) <= windows[0], axis=1).astype(I32)

    x_sorted, gterms = moe_gather(x_bf, pos, gsplit, b_lo, b_hi, n_tiles)
    out_sorted = moe_group_ffn(tile_expert, n_valid.reshape(1), x_sorted, gterms, wg, wu, wd, n_tiles + tail_tiles)
    return moe_combine_ln(row_lo.reshape(-1), short, route_pos, resid, out_sorted, g, b, alpha, ne, windows)


NEG_INF_CODE = INT_MIN + 0x7FFFFF


def _code_to_f32(code):
    return pltpu.bitcast(jnp.where(code < 0, code ^ jnp.int32(0x7FFFFFFF), code), F32)


def _threshold_search(fold, topk, shape):
    kk = jnp.int32(topk)
    count = lambda pred: fold(lambda s, i: jnp.where(pred(s), 1, 0), "sum")
    c0 = count(lambda s: s >= 0.0)
    code = jnp.where(c0 >= kk, jnp.int32(0), jnp.int32(INT_MIN)) + jnp.zeros(shape, I32)

    def bit_body(j, code):
        cand = code | (jnp.int32(1) << (30 - j))
        cand_f = _code_to_f32(cand)
        return jnp.where(count(lambda s: s >= cand_f) >= kk, cand, code)

    code = lax.fori_loop(0, 31, bit_body, code)
    enough = code > NEG_INF_CODE
    floor_t = _code_to_f32(code)
    t = jnp.where(enough, fold(lambda s, i: jnp.where(s >= floor_t, s, jnp.inf), "min"), -jnp.inf)
    need = kk - count(lambda s: s > t)
    return t, jnp.where(enough, need, 0)


def _topk_threshold(fold, topk, n_idx_bits, rows):
    t, need = _threshold_search(fold, topk, (rows, 1))

    def tie_body(j, p):
        cand = p | (jnp.int32(1) << (n_idx_bits - 1 - j))
        got = fold(lambda s, i: jnp.where(s == t, jnp.where(i < cand, 1, 0), 0), "sum")
        return jnp.where(got <= need, cand, p)

    p = lax.fori_loop(0, n_idx_bits, tie_body, jnp.zeros((rows, 1), I32))
    return t, jnp.where(need > 0, p, 0)


CHUNK_GROUP = 4
ATT_CHUNKS = 4


def _dsa_prompt_kernel(qt_ref, k_ref, vt_ref, qit_ref, ki_ref, wit_ref, bias_ref, ltri_ref, o_ref,
                       score_sc, m_sc, l_sc, acc_sc, *, topk):
    i = pl.program_id(1)
    n_groups = (i + CHUNK_GROUP) // CHUNK_GROUP
    krow = lax.broadcasted_iota(I32, (TQ, TQ), 0)
    qcol = lax.broadcasted_iota(I32, (TQ, TQ), 1)
    qit = qit_ref[0, 0] * jnp.asarray(IDX_DIM ** -0.5, BF16)
    qit_heads = jnp.concatenate([qit[h * IDX_DIM:(h + 1) * IDX_DIM, :] for h in range(IDX_HEADS)], axis=1)
    wt = wit_ref[0, 0] * (IDX_HEADS ** -0.5)

    def score_body(g, carry):
        base = pl.multiple_of(g * (CHUNK_GROUP * TQ), CHUNK_GROUP * TQ)
        for j in range(CHUNK_GROUP):
            off = base + j * TQ
            s_heads = jnp.dot(ki_ref[0, pl.ds(off, TQ), :], qit_heads, preferred_element_type=F32)
            sc = jnp.zeros((TQ, TQ), F32)
            for h in range(IDX_HEADS):
                sc = sc + wt[h:h + 1, :] * jnp.maximum(s_heads[:, h * TQ:(h + 1) * TQ], 0.0)
            score_sc[pl.ds(off, TQ), :] = jnp.where(off + krow <= i * TQ + qcol, sc, -jnp.inf)
        return carry

    lax.fori_loop(0, n_groups, score_body, 0)

    def fold(fn, kind):
        init, join, finish = {"sum": (0, jnp.add, jnp.sum), "min": (jnp.inf, jnp.minimum, jnp.min)}[kind]

        def body(g, acc):
            base = pl.multiple_of(g * (CHUNK_GROUP * TQ), CHUNK_GROUP * TQ)
            for j in range(CHUNK_GROUP):
                acc = join(acc, fn(score_sc[pl.ds(base + j * TQ, TQ), :], None))
            return acc

        acc = lax.fori_loop(0, n_groups, body, jnp.full((TQ, TQ), init, I32 if kind == "sum" else F32))
        return finish(acc, axis=0, keepdims=True)

    t, need = _threshold_search(fold, topk, (1, TQ))
    need = need.astype(F32)

    m_sc[...] = jnp.full(m_sc.shape, NEG, F32)
    l_sc[...] = jnp.zeros(l_sc.shape, F32)
    acc_sc[...] = jnp.zeros(acc_sc.shape, F32)
    qt = qt_ref[0, 0] * jnp.asarray(A_HEAD_DIM ** -0.5, BF16)

    def att_body(pair, ties_before):
        chunks, sels, kcs = [], [], []
        for j in range(ATT_CHUNKS):
            c = pair * ATT_CHUNKS + j
            off = pl.multiple_of(c * TQ, TQ)
            key = score_sc[pl.ds(off, TQ), :]
            tie = key == t
            rank = ties_before + jnp.dot(ltri_ref[...], jnp.where(tie, 1.0, 0.0).astype(BF16),
                                         preferred_element_type=F32)
            ties_before = rank[TQ - 1:TQ, :]
            sels.append(jnp.where(key > t, 1, jnp.where(tie, jnp.where(rank <= need, 1, 0), 0)) > 0)
            kcs.append(k_ref[0, pl.ds(off, TQ), :])
            chunks.append(c)
        for h in range(A_HEADS):
            sl = slice(h * A_HEAD_DIM, (h + 1) * A_HEAD_DIM)
            ss = []
            for j in range(ATT_CHUNKS):
                s = jnp.dot(kcs[j][:, sl], qt[sl, :], preferred_element_type=F32)
                bias = bias_ref[jnp.clip(i - chunks[j], 0, 2), h]
                ss.append(jnp.where(sels[j], s + bias, NEG))
            m_prev = m_sc[h, 0:1, :]
            m_new = jnp.maximum(m_prev, jnp.max(functools.reduce(jnp.maximum, ss), axis=0, keepdims=True))
            alpha = jnp.exp(m_prev - m_new)
            pes = [jnp.exp(s - m_new) for s in ss]
            l_sc[h, 0:1, :] = alpha * l_sc[h, 0:1, :] + jnp.sum(functools.reduce(jnp.add, pes), axis=0, keepdims=True)
            pv = functools.reduce(jnp.add, [jnp.dot(vt_ref[0, chunks[j], sl, :], pes[j].astype(BF16),
                                                    preferred_element_type=F32) for j in range(ATT_CHUNKS)])
            acc_sc[sl, :] = alpha * acc_sc[sl, :] + pv
            m_sc[h, 0:1, :] = m_new
        return ties_before

    lax.fori_loop(0, (i + ATT_CHUNKS) // ATT_CHUNKS, att_body, jnp.zeros((1, TQ), F32))
    for h in range(A_HEADS):
        sl = slice(h * A_HEAD_DIM, (h + 1) * A_HEAD_DIM)
        acc_sc[sl, :] = acc_sc[sl, :] / l_sc[h, 0:1, :]
    o_ref[0] = acc_sc[...].T.astype(o_ref.dtype)


def dsa_prompt(qt, k, vt, qit, ki, wit, bias_tab):
    bsz, t, _ = k.shape
    assert TQ == LANES and t % (CHUNK_GROUP * TQ) == 0
    topk = min(TOPK_MAX, t // 4)
    nq = t // TQ
    ltri = (lax.broadcasted_iota(I32, (TQ, TQ), 1) <= lax.broadcasted_iota(I32, (TQ, TQ), 0)).astype(BF16)
    per_q = lambda rows: pl.BlockSpec((1, 1, rows, TQ), lambda b, i: (b, i, 0, 0))
    per_b = lambda w: pl.BlockSpec((1, t, w), lambda b, i: (b, 0, 0))
    stat = pltpu.VMEM((A_HEADS, SUBLANES, TQ), F32)
    return pl.pallas_call(
        functools.partial(_dsa_prompt_kernel, topk=topk), grid=(bsz, nq),
        in_specs=[per_q(A_WIDTH), per_b(A_WIDTH), pl.BlockSpec((1, nq, A_WIDTH, TQ), lambda b, i: (b, 0, 0, 0)),
                  per_q(IDX_HEADS * IDX_DIM), per_b(IDX_DIM), per_q(SUBLANES), _full(bias_tab.shape),
                  _full((TQ, TQ))],
        out_specs=pl.BlockSpec((1, TQ, A_WIDTH), lambda b, i: (b, i, 0)),
        out_shape=jax.ShapeDtypeStruct((bsz, t, A_WIDTH), BF16),
        scratch_shapes=[pltpu.VMEM((t, TQ), F32), stat, stat, pltpu.VMEM((A_WIDTH, TQ), F32)],
        compiler_params=_cp("parallel", "arbitrary"), name="dsa_prompt")(qt, k, vt, qit, ki, wit, bias_tab, ltri)


def _row_to_col(row):
    n = row.shape[1]
    eye = lax.broadcasted_iota(I32, (n, n), 0) == lax.broadcasted_iota(I32, (n, n), 1)
    return jnp.sum(jnp.where(eye, jnp.broadcast_to(row, (n, n)), 0.0), axis=1, keepdims=True)


def _col_to_row(col):
    n = col.shape[0]
    eye = lax.broadcasted_iota(I32, (n, n), 0) == lax.broadcasted_iota(I32, (n, n), 1)
    return jnp.sum(jnp.where(eye, jnp.broadcast_to(col, (n, n)), 0.0), axis=0, keepdims=True)


def _dsa_s_score_kernel(pt_ref, qi_ref, w_ref, *refs, n_pages):
    page_refs, o_ref = refs[:n_pages], refs[n_pages]
    w = _as_mxu_operand(w_ref[0] * (IDX_HEADS ** -0.5))
    qcols = [_row_to_col(_as_mxu_operand(qi_ref[0, h:h + 1, :] * (IDX_DIM ** -0.5))) for h in range(IDX_HEADS)]
    for p in range(n_pages):
        page = _as_mxu_operand(page_refs[p][0])
        sc = jnp.zeros((1, PAGE_SIZE), F32)
        for h in range(IDX_HEADS):
            s = jnp.sum(page * qcols[h], axis=0, keepdims=True)
            sc = sc + w[:, h:h + 1] * _as_mxu_operand(jnp.maximum(s, 0.0))
        o_ref[0, :, p * PAGE_SIZE:(p + 1) * PAGE_SIZE] = sc


def dsa_sample_scores(page_table, qi8, wi, kidx_pool_t):
    s, n_pages = page_table.shape
    pt = page_table.reshape(-1)
    page_spec = lambda p: pl.BlockSpec((1, IDX_DIM, PAGE_SIZE), lambda i, pt_ref: (pt_ref[i * n_pages + p], 0, 0))
    grid_spec = pltpu.PrefetchScalarGridSpec(
        num_scalar_prefetch=1, grid=(s,),
        in_specs=[pl.BlockSpec((1, SUBLANES, IDX_DIM), lambda i, pt_ref: (i, 0, 0)),
                  pl.BlockSpec((1, 1, LANES), lambda i, pt_ref: (i, 0, 0))] + [page_spec(p) for p in range(n_pages)],
        out_specs=pl.BlockSpec((1, 1, n_pages * PAGE_SIZE), lambda i, pt_ref: (i, 0, 0)))
    return pl.pallas_call(functools.partial(_dsa_s_score_kernel, n_pages=n_pages), grid_spec=grid_spec,
                          out_shape=jax.ShapeDtypeStruct((s, 1, n_pages * PAGE_SIZE), F32),
                          compiler_params=_cp("arbitrary"), name="dsa_sample_scores")(
                              pt, qi8, wi, *([kidx_pool_t] * n_pages))


def _dsa_s_select_kernel(sc_ref, qi_ref, kn_ref, wi_ref, mask_ref, mnew_ref, *, topk, n_idx_bits):
    s, lp = mask_ref.shape
    w = _as_mxu_operand(wi_ref[...] * (IDX_HEADS ** -0.5))
    sc_new = jnp.zeros((s, 1), F32)
    kn = _as_mxu_operand(kn_ref[...])
    for h in range(IDX_HEADS):
        s_new = jnp.sum(_as_mxu_operand(qi_ref[:, h * IDX_DIM:(h + 1) * IDX_DIM]) * kn, axis=-1, keepdims=True)
        sc_new = sc_new + w[:, h:h + 1] * _as_mxu_operand(jnp.maximum(s_new * (IDX_DIM ** -0.5), 0.0))
    col = lax.broadcasted_iota(I32, (s, lp), 1)

    def fold(fn, kind):
        join, finish = {"sum": (jnp.add, jnp.sum), "min": (jnp.minimum, jnp.min)}[kind]
        main = finish(fn(sc_ref[...], col), axis=1, keepdims=True)
        return join(main, fn(sc_new, jnp.int32(lp)))

    t, p = _topk_threshold(fold, topk, n_idx_bits, s)
    sc = sc_ref[...]
    mask_ref[...] = jnp.where(sc > t, 1.0, jnp.where(sc == t, jnp.where(col < p, 1.0, 0.0), 0.0))
    sel_new = jnp.where(sc_new > t, 1.0, jnp.where(sc_new == t, jnp.where(lp < p, 1.0, 0.0), 0.0))
    mnew_ref[...] = jnp.broadcast_to(sel_new, mnew_ref.shape)


def dsa_sample_select(raw, qi, ki_new, wi):
    s, lp = raw.shape
    length = lp + 1
    topk = min(TOPK_MAX, length // 4)
    kern = functools.partial(_dsa_s_select_kernel, topk=topk, n_idx_bits=length.bit_length())
    return pl.pallas_call(kern, out_shape=[jax.ShapeDtypeStruct((s, lp), F32), jax.ShapeDtypeStruct((s, LANES), F32)],
                          compiler_params=pltpu.CompilerParams(vmem_limit_bytes=VMEM_LIMIT),
                          name="dsa_sample_select")(raw, qi, ki_new, wi)


def _dsa_s_attn_kernel(pt_ref, q_ref, kn_ref, vn_ref, mask_ref, mnew_ref, bias_ref, bnew_ref, *refs, n_pages):
    k_refs, v_refs, o_ref = refs[:n_pages], refs[n_pages:2 * n_pages], refs[2 * n_pages]
    hd = A_HEAD_DIM
    q = _as_mxu_operand(q_ref[0] * (hd ** -0.5))
    qcols = [_row_to_col(q[h:h + 1, :]) for h in range(A_HEADS)]
    hrow = lax.broadcasted_iota(I32, (A_HEADS, PAGE_SIZE), 0)
    lg_new = jnp.sum(q * _as_mxu_operand(kn_ref[0]), axis=1, keepdims=True) + bnew_ref[:, 0:1]
    lg_new = jnp.where(mnew_ref[0][:, 0:1] > 0.0, lg_new, NEG)
    logits = []
    mx = lg_new
    for p in range(n_pages):
        lg = jnp.zeros((A_HEADS, PAGE_SIZE), F32)
        for h in range(A_HEADS):
            kp = _as_mxu_operand(k_refs[p][0, h * hd:(h + 1) * hd, :])
            s = jnp.sum(kp * qcols[h], axis=0, keepdims=True)
            lg = jnp.where(hrow == h, jnp.broadcast_to(s, lg.shape), lg)
        lg = jnp.where(mask_ref[0, p:p + 1, :] > 0.0, lg + bias_ref[p], NEG)
        logits.append(lg)
        mx = jnp.maximum(mx, jnp.max(lg, axis=1, keepdims=True))
    pe_new = jnp.exp(lg_new - mx)
    pes = [jnp.exp(lg - mx) for lg in logits]
    den = pe_new
    for pe in pes:
        den = den + jnp.sum(pe, axis=1, keepdims=True)
    probs = [_as_mxu_operand(pe / den) for pe in pes]
    p_new = _as_mxu_operand(pe_new / den)
    for h in range(A_HEADS):
        acc = jnp.zeros((hd, PAGE_SIZE), F32)
        for p in range(n_pages):
            acc = acc + probs[p][h:h + 1, :] * _as_mxu_operand(v_refs[p][0, h * hd:(h + 1) * hd, :])
        o_ref[0, h:h + 1, :] = (_col_to_row(jnp.sum(acc, axis=1, keepdims=True))
                                + p_new[h:h + 1, :] * _as_mxu_operand(vn_ref[0, h:h + 1, :]))


def dsa_sample_attn(page_table, q, k_new, v_new, mask, mask_new, bias_pages, bias_new, k_pool_t, v_pool_t):
    s, n_pages = page_table.shape
    pt = page_table.reshape(-1)
    per_s = lambda r, w: pl.BlockSpec((1, r, w), lambda i, pt_ref: (i, 0, 0))
    const = lambda shape: pl.BlockSpec(shape, lambda i, pt_ref: (0,) * len(shape))
    page_spec = lambda p: pl.BlockSpec((1, A_WIDTH, PAGE_SIZE), lambda i, pt_ref: (pt_ref[i * n_pages + p], 0, 0))
    hd = (A_HEADS, A_HEAD_DIM)
    grid_spec = pltpu.PrefetchScalarGridSpec(
        num_scalar_prefetch=1, grid=(s,),
        in_specs=[per_s(*hd), per_s(*hd), per_s(*hd), per_s(n_pages, PAGE_SIZE), per_s(1, LANES),
                  const(bias_pages.shape), const(bias_new.shape)]
                 + [page_spec(p) for p in range(n_pages)] * 2,
        out_specs=per_s(*hd))
    return pl.pallas_call(functools.partial(_dsa_s_attn_kernel, n_pages=n_pages), grid_spec=grid_spec,
                          out_shape=jax.ShapeDtypeStruct((s,) + hd, F32),
                          compiler_params=_cp("arbitrary"), name="dsa_sample_attn")(
                              pt, q, k_new, v_new, mask, mask_new, bias_pages, bias_new,
                              *([k_pool_t] * n_pages), *([v_pool_t] * n_pages))


def _rglru_gates(xc, wa, ba, wi, bi, lam):
    xb = xc.astype(wa.dtype)
    r = jax.nn.sigmoid(_mm(xb, wa) + ba)
    ig = jax.nn.sigmoid(_mm(xb, wi) + bi)
    log_a = -RG_C * r * jax.nn.softplus(-lam)
    a = jnp.exp(log_a)
    u = jnp.sqrt(1.0 - jnp.exp(2.0 * log_a)) * (ig * xc)
    return a, u


def _rglru_prompt_kernel(xr_ref, gr_ref, cw_ref, cb_ref, wa_ref, ba_ref, wi_ref, bi_ref, lam_ref,
                         y_ref, hl_ref, xs_sc, h_sc, *, tb):
    t = pl.program_id(1)
    pad = SUBLANES

    @pl.when(t == 0)
    def _():
        xs_sc[0:pad, :] = jnp.zeros((pad, xs_sc.shape[1]), F32)
        h_sc[...] = jnp.zeros(h_sc.shape, F32)

    xs_sc[pad:pad + tb, :] = xr_ref[0]
    xc = cb_ref[...] + cw_ref[RG_CONV - 1:RG_CONV, :] * xs_sc[pad:pad + tb, :]
    for j in range(1, RG_CONV):
        xc = xc + cw_ref[RG_CONV - 1 - j:RG_CONV - j, :] * xs_sc[pad - j:pad - j + tb, :]
    a, u = _rglru_gates(xc, wa_ref[...], ba_ref[...], wi_ref[...], bi_ref[...], lam_ref[...])
    rowi = lax.broadcasted_iota(I32, a.shape, 0)
    s = 1
    while s < tb:
        keep = rowi >= s
        a_sh = jnp.where(keep, pltpu.roll(a, s, 0), 1.0)
        u_sh = jnp.where(keep, pltpu.roll(u, s, 0), 0.0)
        u = a * u_sh + u
        a = a * a_sh
        s *= 2
    h = a * h_sc[0:1, :] + u
    y_ref[0] = (h * jax.nn.gelu(gr_ref[0])).astype(y_ref.dtype)
    h_sc[0:1, :] = h[tb - 1:tb, :]
    xs_sc[0:pad, :] = xs_sc[tb:tb + pad, :]

    @pl.when(t == pl.num_programs(1) - 1)
    def _():
        hl_ref[0] = h[tb - 1:tb, :]


def rglru_prompt(xr, gr, cw, cb, wa, ba, wi, bi, lam, tb_cap=256):
    bsz, t, w = xr.shape
    tb = _row_tile(t, tb_cap)
    blk = pl.BlockSpec((1, tb, w), lambda b, i: (b, i, 0))
    vec = _full((1, w))
    return pl.pallas_call(
        functools.partial(_rglru_prompt_kernel, tb=tb), grid=(bsz, t // tb),
        in_specs=[blk, blk, _full((RG_CONV, w)), vec, _full((w, w)), vec, _full((w, w)), vec, vec],
        out_specs=[blk, pl.BlockSpec((1, 1, w), lambda b, i: (b, 0, 0))],
        out_shape=[jax.ShapeDtypeStruct((bsz, t, w), BF16), jax.ShapeDtypeStruct((bsz, 1, w), F32)],
        scratch_shapes=[pltpu.VMEM((tb + SUBLANES, w), F32), pltpu.VMEM((SUBLANES, w), F32)],
        compiler_params=_cp("parallel", "arbitrary"), name="rglru_prompt")(
            xr, gr, cw, cb.reshape(1, w), wa, ba.reshape(1, w), wi, bi.reshape(1, w), lam.reshape(1, w))


def _rglru_step_kernel(xr_ref, gr_ref, b0_ref, b1_ref, b2_ref, h0_ref, cw_ref, cb_ref, wa_ref, ba_ref, wi_ref,
                       bi_ref, lam_ref, y_ref, h_ref):
    xc = (cb_ref[...] + cw_ref[0:1, :] * b0_ref[...] + cw_ref[1:2, :] * b1_ref[...]
          + cw_ref[2:3, :] * b2_ref[...] + cw_ref[3:4, :] * xr_ref[...])
    a, u = _rglru_gates(xc, wa_ref[...], ba_ref[...], wi_ref[...], bi_ref[...], lam_ref[...])
    h = a * h0_ref[...] + u
    h_ref[...] = h
    y_ref[...] = (h * jax.nn.gelu(gr_ref[...])).astype(y_ref.dtype)


def rglru_step(xr, gr, buf, h0, cw, cb, wa, ba, wi, bi, lam):
    s, w = xr.shape
    return pl.pallas_call(
        _rglru_step_kernel,
        out_shape=[jax.ShapeDtypeStruct((s, w), F32), jax.ShapeDtypeStruct((s, w), F32)],
        compiler_params=pltpu.CompilerParams(vmem_limit_bytes=VMEM_LIMIT), name="rglru_step")(
            xr, gr, buf[:, 0], buf[:, 1], buf[:, 2], h0, cw, cb.reshape(1, w), wa, ba.reshape(1, w),
            wi, bi.reshape(1, w), lam.reshape(1, w))


def _headnorm_gate(h, g, og):
    mu = jnp.mean(h, axis=-1, keepdims=True)
    hc = h - mu
    var = jnp.mean(hc * hc, axis=-1, keepdims=True)
    return hc * lax.rsqrt(var + LN_EPS) * g * jax.nn.sigmoid(og)


def _mlstm_prompt_kernel(q_ref, k_ref, v_ref, og_ref, gt_ref, bg_ref, ng_ref, tri_ref,
                         y_ref, c_ref, n_ref, m_ref, c_sc, n_sc, m_sc, *, cl, dh):
    ci = pl.program_id(1)

    @pl.when(ci == 0)
    def _():
        c_sc[...] = jnp.zeros(c_sc.shape, F32)
        n_sc[...] = jnp.zeros(n_sc.shape, F32)
        m_sc[...] = jnp.zeros(m_sc.shape, F32)

    pre = gt_ref[0] + bg_ref[...]
    lane = lax.broadcasted_iota(I32, pre.shape, 1)
    gv = jnp.where(lane < ML_HEADS, pre, jax.nn.log_sigmoid(pre))
    csum = jnp.dot(tri_ref[...], gv, preferred_element_type=F32, precision=HI)
    gv_t = gv.T
    csum_t = csum.T
    row = lax.broadcasted_iota(I32, (cl, cl), 0)
    col = lax.broadcasted_iota(I32, (cl, cl), 1)
    causal = col <= row
    kscale = dh ** -0.5
    for h in range(ML_HEADS):
        sl = slice(h * dh, (h + 1) * dh)
        qh = q_ref[0, :, sl]
        kh = k_ref[0, :, sl] * jnp.asarray(kscale, BF16)
        vh = v_ref[0, :, sl]
        b_col = csum[:, ML_HEADS + h:ML_HEADS + h + 1]
        i_col = gv[:, h:h + 1]
        b_row = csum_t[ML_HEADS + h:ML_HEADS + h + 1, :]
        i_row = gv_t[h:h + 1, :]
        m_prev = m_sc[h, 0:1, 0:1]
        c_prev = c_sc[h]
        n_prev = n_sc[h, 0:1, :]
        dmat = jnp.where(causal, b_col - b_row + i_row, NEG)
        inter = b_col + m_prev
        m_t = jnp.maximum(inter, jnp.max(dmat, axis=1, keepdims=True))
        g_inter = jnp.exp(inter - m_t)
        qk = lax.dot_general(qh, kh, NT, preferred_element_type=F32)
        ws = jnp.exp(dmat - m_t) * qk
        num = (g_inter * lax.dot_general(qh, c_prev.astype(BF16), NT, preferred_element_type=F32)
               + jnp.dot(ws.astype(BF16), vh, preferred_element_type=F32))
        qf = qh.astype(F32)
        den = g_inter * jnp.sum(qf * n_prev, axis=1, keepdims=True) + jnp.sum(ws, axis=1, keepdims=True)
        hout = num / jnp.maximum(jnp.abs(den), jnp.exp(-m_t))
        m_new = m_t[cl - 1:cl, :]
        b_last = b_col[cl - 1:cl, :]
        g_c = jnp.exp(b_last + m_prev - m_new)
        w_end = jnp.exp(b_last - b_col + i_col - m_new)
        kf = kh.astype(F32)
        vw = (vh.astype(F32) * w_end).astype(BF16)
        c_sc[h] = g_c * c_prev + lax.dot_general(vw, kh, TN, preferred_element_type=F32)
        n_sc[h, 0:1, :] = g_c * n_prev + jnp.sum(kf * w_end, axis=0, keepdims=True)
        m_sc[h, 0:1, :] = jnp.broadcast_to(m_new, (1, m_sc.shape[2]))
        y_ref[0, :, sl] = _headnorm_gate(hout, ng_ref[:, sl], og_ref[0, :, sl]).astype(y_ref.dtype)

    @pl.when(ci == pl.num_programs(1) - 1)
    def _():
        c_ref[0] = c_sc[...]
        n_ref[0] = jnp.zeros(n_ref.shape[1:], F32)
        m_ref[0] = jnp.zeros(m_ref.shape[1:], F32)
        for h in range(ML_HEADS):
            n_ref[0, h:h + 1, :] = n_sc[h, 0:1, :]
            m_ref[0, h:h + 1, :] = m_sc[h, 0:1, :]


def mlstm_prompt(q, k, v, og, gates, bgate, norm_g):
    bsz, t, width = q.shape
    dh = width // ML_HEADS
    cl = math.gcd(t, ML_CHUNK)
    tri = (lax.broadcasted_iota(I32, (cl, cl), 1) <= lax.broadcasted_iota(I32, (cl, cl), 0)).astype(F32)
    blk = lambda w: pl.BlockSpec((1, cl, w), lambda b, i: (b, i, 0))
    return pl.pallas_call(
        functools.partial(_mlstm_prompt_kernel, cl=cl, dh=dh), grid=(bsz, t // cl),
        in_specs=[blk(width), blk(width), blk(width), blk(width), blk(LANES), _full((1, LANES)), _full((1, width)),
                  _full((cl, cl))],
        out_specs=[blk(width), pl.BlockSpec((1, ML_HEADS, dh, dh), lambda b, i: (b, 0, 0, 0)),
                   pl.BlockSpec((1, SUBLANES, dh), lambda b, i: (b, 0, 0)),
                   pl.BlockSpec((1, SUBLANES, LANES), lambda b, i: (b, 0, 0))],
        out_shape=[jax.ShapeDtypeStruct((bsz, t, width), BF16), jax.ShapeDtypeStruct((bsz, ML_HEADS, dh, dh), F32),
                   jax.ShapeDtypeStruct((bsz, SUBLANES, dh), F32), jax.ShapeDtypeStruct((bsz, SUBLANES, LANES), F32)],
        scratch_shapes=[pltpu.VMEM((ML_HEADS, dh, dh), F32), pltpu.VMEM((ML_HEADS, SUBLANES, dh), F32),
                        pltpu.VMEM((ML_HEADS, SUBLANES, LANES), F32)],
        compiler_params=_cp("parallel", "arbitrary"), name="mlstm_prompt")(
            q, k, v, og, gates, bgate, norm_g.reshape(1, width), tri)


def _mlstm_step_kernel(q_ref, k_ref, v_ref, og_ref, gt_ref, bg_ref, ng_ref, c0_ref, n0_ref, m0_ref,
                       y_ref, c_ref, n_ref, m_ref, *, sb, dh):
    kscale = dh ** -0.5
    m_ref[...] = jnp.zeros(m_ref.shape, F32)

    def seq_body(s, carry):
        rs = pl.ds(s, 1)
        pre = gt_ref[rs, :] + bg_ref[...]
        for h in range(ML_HEADS):
            sl = slice(h * dh, (h + 1) * dh)
            qh = q_ref[rs, sl]
            kh = k_ref[rs, sl] * kscale
            vh = v_ref[rs, sl]
            i_pre = pre[:, h:h + 1]
            logf = jax.nn.log_sigmoid(pre[:, ML_HEADS + h:ML_HEADS + h + 1])
            m_prev = m0_ref[rs, h:h + 1]
            c_prev = c0_ref[s, h]
            n_prev = n0_ref[s, h:h + 1, :]
            inter = logf + m_prev
            m_t = jnp.maximum(inter, i_pre)
            g_inter = jnp.exp(inter - m_t)
            w_in = jnp.exp(i_pre - m_t)
            q16 = _as_mxu_operand(qh)
            qk = jnp.sum(q16 * _as_mxu_operand(kh), axis=1, keepdims=True)
            q8 = jnp.broadcast_to(qh, (SUBLANES, dh)).astype(BF16)
            cq = lax.dot_general(q8, c_prev.astype(BF16), NT, preferred_element_type=F32)[0:1]
            num = g_inter * cq + (w_in * qk) * vh
            den = g_inter * jnp.sum(_as_mxu_operand(n_prev) * q16, axis=1, keepdims=True) + w_in * qk
            hout = num / jnp.maximum(jnp.abs(den), jnp.exp(-m_t))
            c_ref[s, h] = g_inter * c_prev + _row_to_col(w_in * vh) * kh
            n_ref[s, h:h + 1, :] = g_inter * n_prev + w_in * kh
            m_ref[rs, h:h + 1] = m_t
            y_ref[rs, sl] = _headnorm_gate(hout, ng_ref[:, sl], og_ref[rs, sl]).astype(y_ref.dtype)
        return carry

    lax.fori_loop(0, sb, seq_body, 0)


def mlstm_step(q, k, v, og, gates, bgate, norm_g, c0, n0, m0, sb=SUBLANES):
    s, width = q.shape
    dh = width // ML_HEADS
    assert s % sb == 0
    row = lambda w: pl.BlockSpec((sb, w), lambda i: (i, 0))
    m0p = jnp.pad(m0, ((0, 0), (0, LANES - ML_HEADS)))
    outs = pl.pallas_call(
        functools.partial(_mlstm_step_kernel, sb=sb, dh=dh), grid=(s // sb,),
        in_specs=[row(width), row(width), row(width), row(width), row(LANES), _full((1, LANES)), _full((1, width)),
                  pl.BlockSpec((sb, ML_HEADS, dh, dh), lambda i: (i, 0, 0, 0)),
                  pl.BlockSpec((sb, ML_HEADS, dh), lambda i: (i, 0, 0)), row(LANES)],
        out_specs=[row(width), pl.BlockSpec((sb, ML_HEADS, dh, dh), lambda i: (i, 0, 0, 0)),
                   pl.BlockSpec((sb, ML_HEADS, dh), lambda i: (i, 0, 0)), row(LANES)],
        out_shape=[jax.ShapeDtypeStruct((s, width), F32), jax.ShapeDtypeStruct((s, ML_HEADS, dh, dh), F32),
                   jax.ShapeDtypeStruct((s, ML_HEADS, dh), F32), jax.ShapeDtypeStruct((s, LANES), F32)],
        compiler_params=_cp("parallel"), name="mlstm_step")(
            q, k, v, og, gates, bgate, norm_g.reshape(1, width), c0, n0, m0p)
    y, c, n, m = outs
    return y, c, n, m[:, :ML_HEADS]


def _xattn_prompt_kernel(q_ref, k_ref, v_ref, o_ref, *, dh):
    scale = dh ** -0.5
    for h in range(X_HEADS):
        sl = slice(h * dh, (h + 1) * dh)
        lg = lax.dot_general(q_ref[0, :, sl], k_ref[0, :, sl], NT, preferred_element_type=F32) * scale
        mx = jnp.max(lg, axis=-1, keepdims=True)
        e = jnp.exp(lg - mx)
        pr = e / jnp.sum(e, axis=-1, keepdims=True)
        o_ref[0, :, sl] = jnp.dot(pr.astype(BF16), v_ref[0, :, sl], preferred_element_type=F32).astype(o_ref.dtype)


def xattn_prompt(xq, mk, mv, tq_cap=512):
    bsz, t, d = xq.shape
    mt = mk.shape[1]
    tq = _row_tile(t, tq_cap)
    return pl.pallas_call(
        functools.partial(_xattn_prompt_kernel, dh=d // X_HEADS), grid=(bsz, t // tq),
        in_specs=[pl.BlockSpec((1, tq, d), lambda b, i: (b, i, 0)), pl.BlockSpec((1, mt, d), lambda b, i: (b, 0, 0)),
                  pl.BlockSpec((1, mt, d), lambda b, i: (b, 0, 0))],
        out_specs=pl.BlockSpec((1, tq, d), lambda b, i: (b, i, 0)),
        out_shape=jax.ShapeDtypeStruct((bsz, t, d), BF16),
        compiler_params=_cp("parallel", "parallel"), name="xattn_prompt")(xq, mk, mv)


def _xattn_step_kernel(q_ref, k_ref, v_ref, o_ref, *, sb, mt, dh):
    scale = dh ** -0.5
    parts = dh // LANES
    rows = X_HEADS * parts

    def seq_body(s, carry):
        rs = pl.ds(s, 1)
        qrow = _as_mxu_operand(q_ref[rs, :])
        pieces = []
        for h in range(X_HEADS):
            lg = jnp.zeros((mt, 1), F32)
            for c in range(parts):
                qp = qrow[:, h * dh + c * LANES:h * dh + (c + 1) * LANES]
                kp = _as_mxu_operand(k_ref[0, s, pl.ds(c * X_HEADS + h, mt, stride=rows), :])
                lg = lg + jnp.sum(kp * qp, axis=1, keepdims=True)
            lg = lg * scale
            e = jnp.exp(lg - jnp.max(lg, axis=0, keepdims=True))
            pr = _as_mxu_operand(e / jnp.sum(e, axis=0, keepdims=True))
            for c in range(parts):
                vp = _as_mxu_operand(v_ref[0, s, pl.ds(c * X_HEADS + h, mt, stride=rows), :])
                pieces.append(jnp.sum(pr * vp, axis=0, keepdims=True))
        o_ref[rs, :] = jnp.concatenate(pieces, axis=1)
        return carry

    lax.fori_loop(0, sb, seq_body, 0)


def _lane_chunk_view(mem):
    nl, s, mt, nh, dh = mem.shape
    v = mem.reshape(nl, s, mt, nh, dh // LANES, LANES)
    return jnp.transpose(v, (0, 1, 2, 4, 3, 5)).reshape(nl, s, mt * (dh // LANES) * nh, LANES)


def xattn_step(xq, mk_view, mv_view, layer, mt, sb=SUBLANES):
    s, d = xq.shape
    dh = d // X_HEADS
    assert s % sb == 0 and dh % LANES == 0
    nrows = mk_view.shape[2]
    mem_spec = pl.BlockSpec((1, sb, nrows, LANES), lambda i: (layer, i, 0, 0))
    return pl.pallas_call(
        functools.partial(_xattn_step_kernel, sb=sb, mt=mt, dh=dh), grid=(s // sb,),
        in_specs=[pl.BlockSpec((sb, d), lambda i: (i, 0)), mem_spec, mem_spec],
        out_specs=pl.BlockSpec((sb, d), lambda i: (i, 0)),
        out_shape=jax.ShapeDtypeStruct((s, d), F32),
        compiler_params=_cp("parallel"), name="xattn_step")(xq, mk_view, mv_view)


def _t5_bucket(dist):
    max_exact = REL_BUCKETS // 2
    d = jnp.maximum(dist.astype(F32), 1.0)
    large = max_exact + (jnp.log(d / max_exact) / math.log(REL_MAX_DIST / max_exact)
                         * (REL_BUCKETS - max_exact)).astype(I32)
    large = jnp.minimum(large, REL_BUCKETS - 1)
    return jnp.where(dist < max_exact, dist, large)


def _pad_cols(w, n):
    return jnp.pad(w, ((0, 0), (0, n - w.shape[1])))


def _block_diag(w):
    nb, di, do = w.shape
    eye = jnp.eye(nb, dtype=w.dtype)
    return jnp.einsum('nde,nm->ndme', w, eye).reshape(nb * di, nb * do)


def kernel(x_prompt, x_sample, cache_k, cache_v, cache_kidx, state_rglru_h, state_rglru_conv, state_mlstm_c,
           state_mlstm_n, state_mlstm_m, cache_mem_k, cache_mem_v, page_table, mem_prompt, ln_g, ln_b, rel_bias,
           w_in_even, w_out_even, rg_conv_w, rg_conv_b, rg_w_a, rg_b_a, rg_w_i, rg_b_i, rg_lambda, w_in_odd,
           b_gate_odd, ml_norm_g, w_out_odd, w_xq, w_xk, w_xv, w_xo, ffn_w_gate, ffn_w_up, ffn_w_down,
           moe_router, moe_w_gate, moe_w_up, moe_w_down):
    bsz, t, d = x_prompt.shape
    s = x_sample.shape[0]
    depth = ln_g.shape[0]
    n_pages = page_table.shape[1]
    past_len = n_pages * PAGE_SIZE
    mt = mem_prompt.shape[1]
    rg_w = rg_lambda.shape[1]
    ml_w = ml_norm_g.shape[1]
    n_experts = moe_router.shape[2]
    alpha = (2.0 * depth) ** 0.25
    bf = lambda a: a.astype(BF16)

    assert REL_MAX_DIST <= TQ
    rel32 = rel_bias.astype(F32)
    by_dist = rel32[_t5_bucket(jnp.arange(2 * TQ + 1, dtype=I32))]

    def spread(dist):
        out = jnp.broadcast_to(by_dist[2 * TQ], dist.shape + (A_HEADS,))
        for dd in range(2 * TQ):
            out = jnp.where((dist == dd)[..., None], by_dist[dd], out)
        return out

    kk_ = lax.broadcasted_iota(I32, (TQ, TQ), 0)
    qq_ = lax.broadcasted_iota(I32, (TQ, TQ), 1)
    bias_tab = jnp.stack([spread(jnp.maximum(qq_ - kk_, 0)), spread(TQ + qq_ - kk_),
                          spread(jnp.full((TQ, TQ), 2 * TQ, I32))])
    bias_tab = jnp.transpose(bias_tab, (0, 3, 1, 2))
    dist_s = (past_len - jnp.arange(past_len, dtype=I32)).reshape(n_pages, PAGE_SIZE)
    bias_s = jnp.transpose(rel32[_t5_bucket(dist_s)], (0, 2, 1))
    bias_s_new = jnp.broadcast_to(rel32[_t5_bucket(jnp.zeros((1,), I32))].reshape(A_HEADS, 1), (A_HEADS, LANES))
    mem_k_view, mem_v_view = _lane_chunk_view(cache_mem_k), _lane_chunk_view(cache_mem_v)

    mem_bf = bf(mem_prompt.reshape(bsz * mt, d))
    kv_w = [bf(w_xk[l]) for l in range(depth)] + [bf(w_xv[l]) for l in range(depth)]
    kv = multi_proj(mem_bf, kv_w, [(F32, BF16)] * (2 * depth))
    mem_k_f32 = [kv[2 * l] for l in range(depth)]
    mem_k_bf = [kv[2 * l + 1].reshape(bsz, mt, d) for l in range(depth)]
    mem_v_f32 = [kv[2 * depth + 2 * l] for l in range(depth)]
    mem_v_bf = [kv[2 * depth + 2 * l + 1].reshape(bsz, mt, d) for l in range(depth)]

    xp = x_prompt.reshape(bsz * t, d)
    xs = x_sample.reshape(s, d)
    xp_bf, xs_bf = bf(xp), bf(xs)
    outs = {}
    k_p, v_p, ki_p, k_s, v_s, ki_s = [], [], [], [], [], []
    rgh_p, rgc_p, rgh_s, rgc_s = [], [], [], []
    mlc_p, mln_p, mlm_p, mlc_s, mln_s, mlm_s = [], [], [], [], [], []

    for l in range(depth):
        wq, wo = bf(w_xq[l]), bf(w_xo[l])
        if l % 2 == 0:
            e = l // 2
            w_in = w_in_even[e]
            o = 0
            cols = {}
            for name, width in (("q", A_WIDTH), ("k", A_WIDTH), ("v", A_WIDTH), ("qi", IDX_HEADS * IDX_DIM),
                                ("ki", IDX_DIM), ("wi", IDX_HEADS), ("xr", rg_w), ("gr", rg_w)):
                cols[name] = bf(w_in[:, o:o + width])
                o += width
            cols["wi"] = _pad_cols(cols["wi"], LANES)
            names = ("q", "k", "v", "qi", "ki", "wi", "xr", "gr")
            wq_t, wv_t, wqi_t = cols["q"].T, cols["v"].T, cols["qi"].T
            wi_t = jnp.pad(cols["wi"][:, :IDX_HEADS].T, ((0, SUBLANES - IDX_HEADS), (0, 0)))
            wa = bf(_block_diag(rg_w_a[e]))
            wi_g = bf(_block_diag(rg_w_i[e]))
            w_out = bf(w_out_even[e])

            nq = t // TQ
            pnames = ("k", "v", "ki", "xr", "gr")
            kf, kb, vf, kif, kib, xr, gr, qt, vt, qit, wit = multi_proj(
                xp_bf, [cols[n] for n in pnames] + [wq_t, wv_t, wqi_t, wi_t],
                [(F32, BF16), (F32,), (F32, BF16), (F32,), (F32,), (BF16,), (BF16,), (BF16,), (F32,)],
                transposed=(False,) * len(pnames) + (True,) * 4)
            r3 = lambda a: a.reshape(bsz, t, a.shape[-1])
            r4 = lambda a: a.reshape(bsz, nq, a.shape[-2], TQ)
            att = dsa_prompt(r4(qt), r3(kb), r4(vt), r4(qit), r3(kib), r4(wit), bias_tab)
            rg_y, h_last = rglru_prompt(r3(xr), r3(gr), rg_conv_w[e], rg_conv_b[e], wa, rg_b_a[e], wi_g, rg_b_i[e],
                                        rg_lambda[e])
            xp, xp_bf, xq_p = mm_res_ln([att.reshape(bsz * t, A_WIDTH), rg_y.reshape(bsz * t, rg_w)],
                                        [w_out[:A_WIDTH], w_out[A_WIDTH:]], xp, ln_g[l, 0], ln_b[l, 0], alpha,
                                        next_ws=[wq])
            k_p.append(kf.reshape(bsz, t, A_HEADS, A_HEAD_DIM))
            v_p.append(vf.reshape(bsz, t, A_HEADS, A_HEAD_DIM))
            ki_p.append(kif.reshape(bsz, t, IDX_DIM))
            rgh_p.append(h_last.reshape(bsz, rg_w))
            rgc_p.append(r3(xr)[:, t - (RG_CONV - 1):, :])

            q, kf, vf, qi, kif, wi, xr, gr = multi_proj(xs_bf, [cols[n] for n in names], [(F32,)] * len(names))
            qi8 = jnp.pad(qi.reshape(s, IDX_HEADS, IDX_DIM), ((0, 0), (0, SUBLANES - IDX_HEADS), (0, 0)))
            pool = cache_k.shape[1]
            kidx_t = jnp.transpose(cache_kidx[e], (0, 2, 1))
            k_t = jnp.transpose(cache_k[e], (0, 2, 3, 1)).reshape(pool, A_WIDTH, PAGE_SIZE)
            v_t = jnp.transpose(cache_v[e], (0, 2, 3, 1)).reshape(pool, A_WIDTH, PAGE_SIZE)
            raw = dsa_sample_scores(page_table, qi8, wi.reshape(s, 1, LANES), kidx_t)
            mask, mask_new = dsa_sample_select(raw.reshape(s, past_len), qi, kif, wi)
            hd3 = lambda a: a.reshape(s, A_HEADS, A_HEAD_DIM)
            att = dsa_sample_attn(page_table, hd3(q), hd3(kf), hd3(vf), mask.reshape(s, n_pages, PAGE_SIZE),
                                  mask_new.reshape(s, 1, LANES), bias_s, bias_s_new, k_t, v_t)
            rg_y, h_new = rglru_step(xr, gr, state_rglru_conv[e], state_rglru_h[e], rg_conv_w[e], rg_conv_b[e],
                                     wa, rg_b_a[e], wi_g, rg_b_i[e], rg_lambda[e])
            xs, xs_bf = mm_res_ln([bf(att.reshape(s, A_WIDTH)), bf(rg_y)], [w_out[:A_WIDTH], w_out[A_WIDTH:]], xs,
                                  ln_g[l, 0], ln_b[l, 0], alpha)
            k_s.append(kf.reshape(s, 1, A_HEADS, A_HEAD_DIM))
            v_s.append(vf.reshape(s, 1, A_HEADS, A_HEAD_DIM))
            ki_s.append(kif.reshape(s, 1, IDX_DIM))
            rgh_s.append(h_new)
            rgc_s.append(jnp.concatenate([state_rglru_conv[e][:, 1:], xr[:, None, :]], axis=1))
        else:
            o_ = l // 2
            w_in = w_in_odd[o_]
            ws = [bf(w_in[:, i * ml_w:(i + 1) * ml_w]) for i in range(4)]
            ws.append(_pad_cols(bf(w_in[:, 4 * ml_w:]), LANES))
            bgate = jnp.pad(b_gate_odd[o_].astype(F32), (0, LANES - 2 * ML_HEADS)).reshape(1, LANES)
            w_out = bf(w_out_odd[o_])

            q, k, v, og, gates = multi_proj(xp_bf, ws, [(BF16,), (BF16,), (BF16,), (F32,), (F32,)])
            r3 = lambda a: a.reshape(bsz, t, a.shape[-1])
            hn, c, n, m = mlstm_prompt(r3(q), r3(k), r3(v), r3(og), r3(gates), bgate, ml_norm_g[o_])
            xp, xp_bf, xq_p = mm_res_ln([hn.reshape(bsz * t, ml_w)], [w_out], xp, ln_g[l, 0], ln_b[l, 0], alpha,
                                        next_ws=[wq])
            mlc_p.append(c)
            mln_p.append(n[:, :ML_HEADS])
            mlm_p.append(m[:, :ML_HEADS, 0])

            q, k, v, og, gates = multi_proj(xs_bf, ws, [(F32,)] * len(ws))
            hn, c, n, m = mlstm_step(q, k, v, og, gates, bgate, ml_norm_g[o_], state_mlstm_c[o_], state_mlstm_n[o_],
                                     state_mlstm_m[o_])
            xs, xs_bf = mm_res_ln([bf(hn)], [w_out], xs, ln_g[l, 0], ln_b[l, 0], alpha)
            mlc_s.append(c)
            mln_s.append(n)
            mlm_s.append(m)

        xo = xattn_prompt(xq_p.reshape(bsz, t, d), mem_k_bf[l], mem_v_bf[l])
        xp, xp_bf = mm_res_ln([xo.reshape(bsz * t, d)], [wo], xp, ln_g[l, 1], ln_b[l, 1], alpha)
        (xq,) = multi_proj(xs_bf, [wq], [(F32,)])
        xo = xattn_step(xq, mem_k_view, mem_v_view, l, mt)
        xs, xs_bf = mm_res_ln([bf(xo)], [wo], xs, ln_g[l, 1], ln_b[l, 1], alpha)

        if l % 2 == 0:
            e = l // 2
            wg, wu, wd = bf(ffn_w_gate[e]), bf(ffn_w_up[e]), bf(ffn_w_down[e])
            xp, xp_bf = ffn_ln(xp_bf, xp, wg, wu, wd, ln_g[l, 2], ln_b[l, 2], alpha)
            xs, xs_bf = ffn_ln(xs_bf, xs, wg, wu, wd, ln_g[l, 2], ln_b[l, 2], alpha)
        else:
            o_ = l // 2
            wr = _pad_cols(bf(moe_router[o_]), LANES)
            wg, wu, wd = bf(moe_w_gate[o_]), bf(moe_w_up[o_]), bf(moe_w_down[o_])
            xp, xp_bf = moe_top2_ln(xp_bf, xp, wr, wg, wu, wd, ln_g[l, 2], ln_b[l, 2], alpha)
            gs = router_gates(xs_bf, wr, n_experts)
            xs, xs_bf = moe_ln(xs_bf, xs, gs, wg, wu, wd, ln_g[l, 2], ln_b[l, 2], alpha)

    xh = mem_k_f32[0].shape[-1] // X_HEADS
    mem_shape = (bsz, mt, X_HEADS, xh)
    return (xp.reshape(bsz, t, d), xs.reshape(s, 1, d),
            jnp.stack(k_p), jnp.stack(v_p), jnp.stack(ki_p), jnp.stack(k_s), jnp.stack(v_s), jnp.stack(ki_s),
            jnp.stack(rgh_p), jnp.stack(rgc_p), jnp.stack(rgh_s), jnp.stack(rgc_s),
            jnp.stack(mlc_p), jnp.stack(mln_p), jnp.stack(mlm_p), jnp.stack(mlc_s), jnp.stack(mln_s), jnp.stack(mlm_s),
            jnp.stack([a.reshape(mem_shape) for a in mem_k_f32]), jnp.stack([a.reshape(mem_shape) for a in mem_v_f32]))
```
